```python
import jax, jax.numpy as jnp
from jax import lax
import numpy as np

D_MODEL = 2048
BATCH = 8
SEQ = 8192
DEPTH = 1

SB_HEADS = 8
SB_HEAD_DIM = 128
SB_WIDTH = SB_HEADS * SB_HEAD_DIM
SB_BLOCK = 128
HG_HEADS = 8
HG_KEY_DIM = 128
HG_VAL_DIM = 128
HG_K_WIDTH = HG_HEADS * HG_KEY_DIM
HG_V_WIDTH = HG_HEADS * HG_VAL_DIM
HG_CHUNK = 64
IN_COLS = 3 * SB_WIDTH + 2 * HG_K_WIDTH + 2 * HG_V_WIDTH + 2 * D_MODEL
D_FF = 5632
CONV_WIDTH = 3
EPS = 1e-6

kernel_name = "hybrid_stickbreaking_hgrn2_convffn"


def rmsnorm(x, g):
    xf = x.astype(jnp.float32)
    y = xf * lax.rsqrt(jnp.mean(xf * xf, axis=-1, keepdims=True) + EPS)
    return (y * g.astype(jnp.float32)).astype(x.dtype)


def split_heads(a, n_heads):
    b, s, w = a.shape
    return a.reshape(b, s, n_heads, w // n_heads).transpose(0, 2, 1, 3)


def merge_heads(a):
    b, h, s, d = a.shape
    return a.transpose(0, 2, 1, 3).reshape(b, s, h * d)


def stick_breaking_attention(q, k, v):
    seq = q.shape[2]
    scale = SB_HEAD_DIM ** -0.5
    outs = []
    for blk in range(seq // SB_BLOCK):
        t0, t1 = blk * SB_BLOCK, (blk + 1) * SB_BLOCK
        qb = q[:, :, t0:t1].astype(jnp.float32)
        kb = k[:, :, :t1].astype(jnp.float32)
        z = jnp.einsum('bhtd,bhsd->bhts', qb, kb) * scale
        t_idx = t0 + jnp.arange(SB_BLOCK)[:, None]
        s_idx = jnp.arange(t1)[None, :]
        strict = s_idx < t_idx
        log_keep = jnp.where(strict, jax.nn.log_sigmoid(-z), 0.0)
        later = lax.cumsum(log_keep, axis=3, reverse=True) - log_keep
        log_w = jnp.where(strict, jax.nn.log_sigmoid(z) + later, -jnp.inf)
        w = jnp.exp(log_w)
        outs.append(jnp.einsum('bhts,bhsd->bhtd', w, v[:, :, :t1].astype(jnp.float32)))
    return jnp.concatenate(outs, axis=2).astype(v.dtype)


def hgrn2_chunkwise(q, k, v, log_f):
    b_, h_, seq, dk = q.shape
    dv = v.shape[-1]
    n_chunks = seq // HG_CHUNK

    def to_chunks(a):
        return jnp.moveaxis(a.astype(jnp.float32).reshape(b_, h_, n_chunks, HG_CHUNK, a.shape[-1]), 2, 0)

    qc, kc, vc, gc = to_chunks(q), to_chunks(k), to_chunks(v), to_chunks(log_f)
    causal = jnp.tril(jnp.ones((HG_CHUNK, HG_CHUNK), dtype=bool))[:, :, None]

    def step(state, inp):
        qi, ki, vi, gi = inp
        b = jnp.cumsum(gi, axis=2)
        o_inter = jnp.einsum('bhtd,bhde->bhte', qi * jnp.exp(b), state)
        diff = b[:, :, :, None, :] - b[:, :, None, :, :]
        decay = jnp.exp(jnp.where(causal, diff, -jnp.inf))
        scores = jnp.einsum('bhtd,bhsd,bhtsd->bhts', qi, ki, decay)
        o_intra = jnp.einsum('bhts,bhse->bhte', scores, vi)
        b_last = b[:, :, -1:, :]
        k_dec = ki * jnp.exp(b_last - b)
        new_state = jnp.exp(b_last[:, :, 0, :])[..., None] * state + jnp.einsum('bhsd,bhse->bhde', k_dec, vi)
        return new_state, o_inter + o_intra

    state0 = jnp.zeros((b_, h_, dk, dv), jnp.float32)
    _, o = lax.scan(step, state0, (qc, kc, vc, gc))
    return jnp.moveaxis(o, 0, 2).reshape(b_, h_, seq, dv)


def causal_depthwise_conv(a, w, bias):
    seq = a.shape[1]
    a_pad = jnp.pad(a, ((0, 0), (CONV_WIDTH - 1, 0), (0, 0)))
    out = bias
    for j in range(CONV_WIDTH):
        out = out + w[j] * a_pad[:, j:j + seq]
    return out


def _fwd_setup_inputs(seed: int = 0) -> dict:
    key = jax.random.key(seed)
    ks = jax.random.split(key, 16)
    f32 = jnp.float32
    nrm = lambda k, shape, s: jax.random.normal(k, shape, f32) * s
    return {
        "x": nrm(ks[0], (BATCH, SEQ, D_MODEL), 1.0),
        "g_mix": 1.0 + nrm(ks[1], (DEPTH, D_MODEL), 0.02),
        "w_in": nrm(ks[2], (DEPTH, D_MODEL, IN_COLS), D_MODEL ** -0.5),
        "g_q": 1.0 + nrm(ks[3], (DEPTH, SB_HEAD_DIM), 0.02),
        "g_k": 1.0 + nrm(ks[4], (DEPTH, SB_HEAD_DIM), 0.02),
        "lb_logits": nrm(ks[5], (DEPTH + 1, HG_K_WIDTH), 0.5),
        "g_hg_out": 1.0 + nrm(ks[6], (DEPTH, HG_VAL_DIM), 0.02),
        "p_a": nrm(ks[7], (DEPTH, SB_WIDTH, D_MODEL), SB_WIDTH ** -0.5),
        "p_b": nrm(ks[8], (DEPTH, HG_V_WIDTH, D_MODEL), HG_V_WIDTH ** -0.5),
        "w_o": nrm(ks[9], (DEPTH, D_MODEL, D_MODEL), D_MODEL ** -0.5),
        "g_ffn": 1.0 + nrm(ks[10], (DEPTH, D_MODEL), 0.02),
        "w_up": nrm(ks[11], (DEPTH, D_MODEL, 2 * D_FF), D_MODEL ** -0.5),
        "conv_w": nrm(ks[12], (DEPTH, CONV_WIDTH, 2 * D_FF), CONV_WIDTH ** -0.5),
        "conv_b": nrm(ks[13], (DEPTH, 2 * D_FF), 0.02),
        "w_down": nrm(ks[14], (DEPTH, D_FF, D_MODEL), D_FF ** -0.5),
    }


def _fwd_reference(x, g_mix, w_in, g_q, g_k, lb_logits, g_hg_out, p_a, p_b, w_o,
              g_ffn, w_up, conv_w, conv_b, w_down):
    splits = [SB_WIDTH, 2 * SB_WIDTH, 3 * SB_WIDTH,
              3 * SB_WIDTH + HG_K_WIDTH, 3 * SB_WIDTH + 2 * HG_K_WIDTH,
              3 * SB_WIDTH + 2 * HG_K_WIDTH + HG_V_WIDTH,
              3 * SB_WIDTH + 2 * HG_K_WIDTH + 2 * HG_V_WIDTH,
              3 * SB_WIDTH + 2 * HG_K_WIDTH + 2 * HG_V_WIDTH + D_MODEL]
    lower_bounds = jnp.cumsum(jax.nn.softmax(lb_logits.astype(jnp.float32), axis=0), axis=0)
    h = x
    for layer in range(DEPTH):
        u = rmsnorm(h, g_mix[layer])
        proj = u @ w_in[layer]
        sb_q, sb_k, sb_v, hg_q, hg_f, hg_i, hg_og, gate_a, gate_b = jnp.split(proj, splits, axis=-1)

        q = rmsnorm(split_heads(sb_q, SB_HEADS), g_q[layer])
        k = rmsnorm(split_heads(sb_k, SB_HEADS), g_k[layer])
        v = split_heads(sb_v, SB_HEADS)
        y_a = merge_heads(stick_breaking_attention(q, k, v))

        lb = lower_bounds[layer].reshape(1, HG_HEADS, 1, HG_KEY_DIM)
        f = lb + (1.0 - lb) * jax.nn.sigmoid(split_heads(hg_f, HG_HEADS).astype(jnp.float32))
        o_b = hgrn2_chunkwise(jax.nn.silu(split_heads(hg_q, HG_HEADS)), 1.0 - f,
                              split_heads(hg_i, HG_HEADS), jnp.log(f))
        o_b = rmsnorm(o_b, g_hg_out[layer])
        y_b = (merge_heads(o_b) * jax.nn.silu(hg_og.astype(jnp.float32))).astype(h.dtype)

        m = jax.nn.sigmoid(gate_a) * (y_a @ p_a[layer]) + jax.nn.sigmoid(gate_b) * (y_b @ p_b[layer])
        h = h + m @ w_o[layer]

        up = rmsnorm(h, g_ffn[layer]) @ w_up[layer]
        up = causal_depthwise_conv(up, conv_w[layer], conv_b[layer])
        gate, val = jnp.split(up, 2, axis=-1)
        h = h + (jax.nn.silu(gate) * val) @ w_down[layer]
    return h


import jax as _jax
import jax.numpy as _jnp

TWIN_FORMAT = 'train_step'
FWD_PARAMS = ['x', 'g_mix', 'w_in', 'g_q', 'g_k', 'lb_logits', 'g_hg_out', 'p_a', 'p_b', 'w_o', 'g_ffn', 'w_up', 'conv_w', 'conv_b', 'w_down']
TWIN_WEIGHTS = ['g_mix', 'w_in', 'g_q', 'g_k', 'lb_logits', 'g_hg_out', 'p_a', 'p_b', 'w_o', 'g_ffn', 'w_up', 'conv_w', 'conv_b', 'w_down']
TWIN_DIFF_INPUT = 'x'
TWIN_INPUTS = ['x', 'g_mix', 'w_in', 'g_q', 'g_k', 'lb_logits', 'g_hg_out', 'p_a', 'p_b', 'w_o', 'g_ffn', 'w_up', 'conv_w', 'conv_b', 'w_down', 'loss_target', 'm_g_mix', 'm_w_in', 'm_g_q', 'm_g_k', 'm_lb_logits', 'm_g_hg_out', 'm_p_a', 'm_p_b', 'm_w_o', 'm_g_ffn', 'm_w_up', 'm_conv_w', 'm_conv_b', 'm_w_down', 'v_g_mix', 'v_w_in', 'v_g_q', 'v_g_k', 'v_lb_logits', 'v_g_hg_out', 'v_p_a', 'v_p_b', 'v_w_o', 'v_g_ffn', 'v_w_up', 'v_conv_w', 'v_conv_b', 'v_w_down']
TWIN_OUTPUTS = ['loss', 'grad_x', 'grad_g_mix', 'grad_w_in', 'grad_g_q', 'grad_g_k', 'grad_lb_logits', 'grad_g_hg_out', 'grad_p_a', 'grad_p_b', 'grad_w_o', 'grad_g_ffn', 'grad_w_up', 'grad_conv_w', 'grad_conv_b', 'grad_w_down', 'delta_g_mix', 'delta_w_in', 'delta_g_q', 'delta_g_k', 'delta_lb_logits', 'delta_g_hg_out', 'delta_p_a', 'delta_p_b', 'delta_w_o', 'delta_g_ffn', 'delta_w_up', 'delta_conv_w', 'delta_conv_b', 'delta_w_down', 'new_m_g_mix', 'new_m_w_in', 'new_m_g_q', 'new_m_g_k', 'new_m_lb_logits', 'new_m_g_hg_out', 'new_m_p_a', 'new_m_p_b', 'new_m_w_o', 'new_m_g_ffn', 'new_m_w_up', 'new_m_conv_w', 'new_m_conv_b', 'new_m_w_down', 'new_v_g_mix', 'new_v_w_in', 'new_v_g_q', 'new_v_g_k', 'new_v_lb_logits', 'new_v_g_hg_out', 'new_v_p_a', 'new_v_p_b', 'new_v_w_o', 'new_v_g_ffn', 'new_v_w_up', 'new_v_conv_w', 'new_v_conv_b', 'new_v_w_down']
TWIN_LEAF_KINDS = {'loss': 'loss', 'grad_x': 'grad_x', 'grad_g_mix': 'grad_w', 'grad_w_in': 'grad_w', 'grad_g_q': 'grad_w', 'grad_g_k': 'grad_w', 'grad_lb_logits': 'grad_w', 'grad_g_hg_out': 'grad_w', 'grad_p_a': 'grad_w', 'grad_p_b': 'grad_w', 'grad_w_o': 'grad_w', 'grad_g_ffn': 'grad_w', 'grad_w_up': 'grad_w', 'grad_conv_w': 'grad_w', 'grad_conv_b': 'grad_w', 'grad_w_down': 'grad_w', 'delta_g_mix': 'delta_w', 'delta_w_in': 'delta_w', 'delta_g_q': 'delta_w', 'delta_g_k': 'delta_w', 'delta_lb_logits': 'delta_w', 'delta_g_hg_out': 'delta_w', 'delta_p_a': 'delta_w', 'delta_p_b': 'delta_w', 'delta_w_o': 'delta_w', 'delta_g_ffn': 'delta_w', 'delta_w_up': 'delta_w', 'delta_conv_w': 'delta_w', 'delta_conv_b': 'delta_w', 'delta_w_down': 'delta_w', 'new_m_g_mix': 'new_m', 'new_m_w_in': 'new_m', 'new_m_g_q': 'new_m', 'new_m_g_k': 'new_m', 'new_m_lb_logits': 'new_m', 'new_m_g_hg_out': 'new_m', 'new_m_p_a': 'new_m', 'new_m_p_b': 'new_m', 'new_m_w_o': 'new_m', 'new_m_g_ffn': 'new_m', 'new_m_w_up': 'new_m', 'new_m_conv_w': 'new_m', 'new_m_conv_b': 'new_m', 'new_m_w_down': 'new_m', 'new_v_g_mix': 'new_v', 'new_v_w_in': 'new_v', 'new_v_g_q': 'new_v', 'new_v_g_k': 'new_v', 'new_v_lb_logits': 'new_v', 'new_v_g_hg_out': 'new_v', 'new_v_p_a': 'new_v', 'new_v_p_b': 'new_v', 'new_v_w_o': 'new_v', 'new_v_g_ffn': 'new_v', 'new_v_w_up': 'new_v', 'new_v_conv_w': 'new_v', 'new_v_conv_b': 'new_v', 'new_v_w_down': 'new_v'}


def _forward(args):
    return _fwd_reference(*[args[k] for k in FWD_PARAMS])


def _output_shape():
    def fwd():
        inp = _fwd_setup_inputs(0)
        return _fwd_reference(*[inp[k] for k in FWD_PARAMS])
    out = _jax.eval_shape(fwd)
    return out.shape, out.dtype

N_MICROBATCH = 1
ADAM_LR = 0.001
ADAM_B1 = 0.9
ADAM_B2 = 0.999
ADAM_EPS = 1e-08
ADAM_WD = 0.01
ADAM_STEP = 10
PER_EXAMPLE_BATCH_AXIS = {'x': 0, 'loss_target': 0}
SHARED_INPUTS = []
_WEIGHT_DTYPES = {'g_mix': _jnp.float32, 'w_in': _jnp.float32, 'g_q': _jnp.float32, 'g_k': _jnp.float32, 'lb_logits': _jnp.float32, 'g_hg_out': _jnp.float32, 'p_a': _jnp.float32, 'p_b': _jnp.float32, 'w_o': _jnp.float32, 'g_ffn': _jnp.float32, 'w_up': _jnp.float32, 'conv_w': _jnp.float32, 'conv_b': _jnp.float32, 'w_down': _jnp.float32}
MOMENT_SCALE = {'g_mix': 8.523549e+00, 'w_in': 1.145563e-01, 'g_q': 9.212157e+00, 'g_k': 9.219330e+00, 'lb_logits': 1.408052e-02, 'g_hg_out': 5.356912e+01, 'p_a': 1.617909e-01, 'p_b': 1.388404e-01, 'w_o': 2.131281e-01, 'g_ffn': 2.632107e+01, 'w_up': 1.633346e-01, 'conv_w': 3.665096e+00, 'conv_b': 3.230194e+00, 'w_down': 2.258222e-01}


def _to_microbatches(a, axis):
    t = _jnp.moveaxis(a, axis, 0)
    t = t.reshape((N_MICROBATCH, t.shape[0] // N_MICROBATCH) + t.shape[1:])
    return _jnp.moveaxis(t, 1, axis + 1)


def setup_inputs(seed: int = 0) -> dict:
    inp = _fwd_setup_inputs(seed)
    key = _jax.random.fold_in(_jax.random.key(seed), 7919)
    shape, _ = _output_shape()
    out = dict(inp)
    out["loss_target"] = _jax.random.normal(_jax.random.fold_in(key, 0), shape, _jnp.float32)
    for i, name in enumerate(TWIN_WEIGHTS):
        w = inp[name].astype(_jnp.float32)
        if MOMENT_SCALE is None:
            s = _jnp.sqrt(_jnp.mean(_jnp.square(w)) + 1e-30)
        else:
            s = MOMENT_SCALE[name]
        km, kv = _jax.random.split(_jax.random.fold_in(key, i + 1))
        out[name] = w
        out["m_" + name] = s * _jax.random.normal(km, w.shape, _jnp.float32)
        out["v_" + name] = (s * s) * _jax.random.uniform(kv, w.shape, _jnp.float32, 0.5, 1.5)
    if N_MICROBATCH > 1:
        for name, axis in PER_EXAMPLE_BATCH_AXIS.items():
            out[name] = _to_microbatches(out[name], axis)
    return {'x': out['x'], 'g_mix': out['g_mix'], 'w_in': out['w_in'], 'g_q': out['g_q'], 'g_k': out['g_k'], 'lb_logits': out['lb_logits'], 'g_hg_out': out['g_hg_out'], 'p_a': out['p_a'], 'p_b': out['p_b'], 'w_o': out['w_o'], 'g_ffn': out['g_ffn'], 'w_up': out['w_up'], 'conv_w': out['conv_w'], 'conv_b': out['conv_b'], 'w_down': out['w_down'], 'loss_target': out['loss_target'], 'm_g_mix': out['m_g_mix'], 'm_w_in': out['m_w_in'], 'm_g_q': out['m_g_q'], 'm_g_k': out['m_g_k'], 'm_lb_logits': out['m_lb_logits'], 'm_g_hg_out': out['m_g_hg_out'], 'm_p_a': out['m_p_a'], 'm_p_b': out['m_p_b'], 'm_w_o': out['m_w_o'], 'm_g_ffn': out['m_g_ffn'], 'm_w_up': out['m_w_up'], 'm_conv_w': out['m_conv_w'], 'm_conv_b': out['m_conv_b'], 'm_w_down': out['m_w_down'], 'v_g_mix': out['v_g_mix'], 'v_w_in': out['v_w_in'], 'v_g_q': out['v_g_q'], 'v_g_k': out['v_g_k'], 'v_lb_logits': out['v_lb_logits'], 'v_g_hg_out': out['v_g_hg_out'], 'v_p_a': out['v_p_a'], 'v_p_b': out['v_p_b'], 'v_w_o': out['v_w_o'], 'v_g_ffn': out['v_g_ffn'], 'v_w_up': out['v_w_up'], 'v_conv_w': out['v_conv_w'], 'v_conv_b': out['v_conv_b'], 'v_w_down': out['v_w_down']}


def _loss(weights, diff, rest, loss_target):
    with _jax.named_scope("forward"):
        args = {**rest, TWIN_DIFF_INPUT: diff, **{k: w.astype(_WEIGHT_DTYPES[k]) for k, w in weights.items()}}
        y = _forward(args)
    with _jax.named_scope("loss_head"):
        err = _jnp.square(y.astype(_jnp.float32) - loss_target)
        return 0.5 * _jnp.sum(_jnp.mean(err, axis=-1)) if err.ndim else 0.5 * err


def _adamw(w, g, m, v):
    m = ADAM_B1 * m + (1.0 - ADAM_B1) * g
    v = ADAM_B2 * v + (1.0 - ADAM_B2) * _jnp.square(g)
    m_hat = m / (1.0 - ADAM_B1 ** ADAM_STEP)
    v_hat = v / (1.0 - ADAM_B2 ** ADAM_STEP)
    delta = -ADAM_LR * (m_hat / (_jnp.sqrt(v_hat) + ADAM_EPS) + ADAM_WD * w)
    return delta, m, v


def reference(x, g_mix, w_in, g_q, g_k, lb_logits, g_hg_out, p_a, p_b, w_o, g_ffn, w_up, conv_w, conv_b, w_down, loss_target, m_g_mix, m_w_in, m_g_q, m_g_k, m_lb_logits, m_g_hg_out, m_p_a, m_p_b, m_w_o, m_g_ffn, m_w_up, m_conv_w, m_conv_b, m_w_down, v_g_mix, v_w_in, v_g_q, v_g_k, v_lb_logits, v_g_hg_out, v_p_a, v_p_b, v_w_o, v_g_ffn, v_w_up, v_conv_w, v_conv_b, v_w_down):
    given = dict(x=x, g_mix=g_mix, w_in=w_in, g_q=g_q, g_k=g_k, lb_logits=lb_logits, g_hg_out=g_hg_out, p_a=p_a, p_b=p_b, w_o=w_o, g_ffn=g_ffn, w_up=w_up, conv_w=conv_w, conv_b=conv_b, w_down=w_down, loss_target=loss_target, m_g_mix=m_g_mix, m_w_in=m_w_in, m_g_q=m_g_q, m_g_k=m_g_k, m_lb_logits=m_lb_logits, m_g_hg_out=m_g_hg_out, m_p_a=m_p_a, m_p_b=m_p_b, m_w_o=m_w_o, m_g_ffn=m_g_ffn, m_w_up=m_w_up, m_conv_w=m_conv_w, m_conv_b=m_conv_b, m_w_down=m_w_down, v_g_mix=v_g_mix, v_w_in=v_w_in, v_g_q=v_g_q, v_g_k=v_g_k, v_lb_logits=v_lb_logits, v_g_hg_out=v_g_hg_out, v_p_a=v_p_a, v_p_b=v_p_b, v_w_o=v_w_o, v_g_ffn=v_g_ffn, v_w_up=v_w_up, v_conv_w=v_conv_w, v_conv_b=v_conv_b, v_w_down=v_w_down)
    weights = {n: given[n] for n in TWIN_WEIGHTS}
    shared = {n: given[n] for n in SHARED_INPUTS}
    per_example = {n: given[n] for n in ['x']}
    grad_fn = _jax.value_and_grad(_loss, argnums=(0, 1))

    def one_microbatch(ex, loss_target):
        ex = dict(ex)
        diff = ex.pop(TWIN_DIFF_INPUT)
        return grad_fn(weights, diff, {**shared, **ex}, loss_target)

    if N_MICROBATCH == 1:
        loss, (grad_w, grad_x) = one_microbatch(per_example, given["loss_target"])
    else:
        def body(carry, xs):
            loss_sum, grad_sum = carry
            l_k, (gw_k, gx_k) = one_microbatch(xs[0], xs[1])
            with _jax.named_scope("update"):
                return (loss_sum + l_k, _jax.tree.map(_jnp.add, grad_sum, gw_k)), gx_k

        init = (_jnp.zeros((), _jnp.float32), _jax.tree.map(_jnp.zeros_like, weights))
        (loss, grad_w), grad_x = _jax.lax.scan(body, init, (per_example, given["loss_target"]))
    with _jax.named_scope("update"):
        delta_w, new_m, new_v = {}, {}, {}
        for n in TWIN_WEIGHTS:
            delta_w[n], new_m[n], new_v[n] = _adamw(weights[n], grad_w[n], given["m_" + n], given["v_" + n])
    return (loss, grad_x, *[grad_w[n] for n in TWIN_WEIGHTS], *[delta_w[n] for n in TWIN_WEIGHTS],
            *[new_m[n] for n in TWIN_WEIGHTS], *[new_v[n] for n in TWIN_WEIGHTS])
```

```python
import functools

import jax
import jax.numpy as jnp
from jax import lax
from jax.experimental import pallas as pl
from jax.experimental.pallas import tpu as pltpu

F32 = jnp.float32
BF16 = jnp.bfloat16
HEAD = 128
EPS = 1e-6
N_CHIPS = 4
HG_CHUNK = 64
SUB = 8
ADAM_LR, ADAM_B1, ADAM_B2, ADAM_EPS, ADAM_WD, ADAM_STEP = 0.001, 0.9, 0.999, 1e-08, 0.01, 10
VMEM_LIMIT = 56 * 1024 * 1024
MESH = pl.DeviceIdType.MESH

NN = (((1,), (0,)), ((), ()))
NT = (((1,), (1,)), ((), ()))
TN = (((0,), (0,)), ((), ()))


def _cparams(*sem):
    return pltpu.CompilerParams(dimension_semantics=sem if sem else None, vmem_limit_bytes=VMEM_LIMIT)


def _pick(n, cands):
    for c in cands:
        if c <= n and n % c == 0:
            return c
    return n


def _sig(x):
    return 1.0 / (1.0 + jnp.exp(-x))


def _dot(a, b, dims):
    return lax.dot_general(a, b, dims, preferred_element_type=F32)


def _split(x):
    hi = x.astype(BF16)
    lo = (x - hi.astype(F32)).astype(BF16)
    return hi, lo


def _tri(n, kind):
    r = lax.broadcasted_iota(jnp.int32, (n, n), 0)
    c = lax.broadcasted_iota(jnp.int32, (n, n), 1)
    m = {"ge": c >= r, "gt": c > r, "le": c <= r, "lt": c < r}[kind]
    return jnp.where(m, 1.0, 0.0).astype(BF16)


def _matmul(name, a, b, mode, out_dtype, add=None):
    if mode == "nn":
        m, k = a.shape
        g, _, ns = b.shape
        n = g * ns
        tm, tn, tk = _pick(m, (1024, 512, 256, 128)), _pick(ns, (1024, 512, 256, 128)), _pick(k, (512, 256, 128))
        nps = ns // tn
        grid = (m // tm, n // tn, k // tk)
        a_spec = pl.BlockSpec((tm, tk), lambda i, j, kk: (i, kk))
        b_spec = pl.BlockSpec((None, tk, tn), lambda i, j, kk: (j // nps, kk, j % nps))
        o_spec = pl.BlockSpec((tm, tn), lambda i, j, kk: (i, j))
        o_shape = jax.ShapeDtypeStruct((m, n), out_dtype)
        dims = NN
    elif mode == "nt":
        m, k = a.shape
        g, n, ks = b.shape
        tm, tn, tk = _pick(m, (1024, 512, 256, 128)), _pick(n, (1024, 512, 256, 128)), _pick(ks, (512, 256, 128))
        kps = ks // tk
        grid = (m // tm, n // tn, k // tk)
        a_spec = pl.BlockSpec((tm, tk), lambda i, j, kk: (i, kk))
        b_spec = pl.BlockSpec((None, tn, tk), lambda i, j, kk: (kk // kps, j, kk % kps))
        o_spec = pl.BlockSpec((tm, tn), lambda i, j, kk: (i, j))
        o_shape = jax.ShapeDtypeStruct((m, n), out_dtype)
        dims = NT
    else:
        raise ValueError(mode)
    nk = grid[2]

    def body(*refs):
        if add is None:
            a_ref, b_ref, o_ref, acc = refs
        else:
            a_ref, b_ref, add_ref, o_ref, acc = refs
        kk = pl.program_id(2)

        @pl.when(kk == 0)
        def _():
            acc[...] = jnp.zeros_like(acc)

        acc[...] += _dot(a_ref[...].astype(BF16), b_ref[...].astype(BF16), dims)

        @pl.when(kk == nk - 1)
        def _():
            r = acc[...]
            if add is not None:
                r = r + add_ref[...]
            o_ref[...] = r.astype(o_ref.dtype)

    in_specs = [a_spec, b_spec]
    args = [a, b]
    if add is not None:
        in_specs.append(o_spec)
        args.append(add)
    return pl.pallas_call(
        body, name=name, grid=grid, in_specs=in_specs, out_specs=o_spec, out_shape=o_shape,
        scratch_shapes=[pltpu.VMEM((tm, tn), F32)],
        compiler_params=_cparams("parallel", "parallel", "arbitrary"),
    )(*args)


def _matmul_tn(name, a, b, g, out_dtype):
    k, m = a.shape
    _, n = b.shape
    ns = n // g
    tm, tn, tk = _pick(m, (1024, 512, 256, 128)), _pick(ns, (1024, 512, 256, 128)), _pick(k, (512, 256, 128))
    nps = ns // tn
    nk = k // tk

    def body(a_ref, b_ref, o_ref, acc):
        kk = pl.program_id(2)

        @pl.when(kk == 0)
        def _():
            acc[...] = jnp.zeros_like(acc)

        acc[...] += _dot(a_ref[...].astype(BF16), b_ref[...].astype(BF16), TN)

        @pl.when(kk == nk - 1)
        def _():
            o_ref[...] = acc[...].astype(o_ref.dtype)

    return pl.pallas_call(
        body, name=name, grid=(m // tm, n // tn, nk),
        in_specs=[pl.BlockSpec((tk, tm), lambda i, j, kk: (kk, i)), pl.BlockSpec((tk, tn), lambda i, j, kk: (kk, j))],
        out_specs=pl.BlockSpec((None, tm, tn), lambda i, j, kk: (j // nps, i, j % nps)),
        out_shape=jax.ShapeDtypeStruct((g, m, ns), out_dtype),
        scratch_shapes=[pltpu.VMEM((tm, tn), F32)],
        compiler_params=_cparams("parallel", "parallel", "arbitrary"),
    )(a, b)


def _rmsnorm_fwd(name, x, g):
    s, d = x.shape
    tr = _pick(s, (256, 128, 64, 32, 16, 8))

    def body(x_ref, g_ref, u_ref):
        xv = x_ref[...]
        r = lax.rsqrt(jnp.mean(xv * xv, axis=-1, keepdims=True) + EPS)
        u_ref[...] = (xv * r * g_ref[...]).astype(u_ref.dtype)

    return pl.pallas_call(
        body, name=name, grid=(s // tr,),
        in_specs=[pl.BlockSpec((tr, d), lambda i: (i, 0)), pl.BlockSpec((1, d), lambda i: (0, 0))],
        out_specs=pl.BlockSpec((tr, d), lambda i: (i, 0)),
        out_shape=jax.ShapeDtypeStruct((s, d), BF16),
        compiler_params=_cparams("parallel"),
    )(x, g)


def _rmsnorm_bwd(name, x, g, du, extra):
    s, d = x.shape
    tr = _pick(s, (256, 128, 64, 32, 16, 8))

    def body(x_ref, g_ref, du_ref, e_ref, dx_ref, dg_ref):
        i = pl.program_id(0)
        xv = x_ref[...]
        r = lax.rsqrt(jnp.mean(xv * xv, axis=-1, keepdims=True) + EPS)
        n = xv * r
        dy = du_ref[...]
        a = dy * g_ref[...]
        dx = r * (a - n * jnp.mean(a * n, axis=-1, keepdims=True))
        dx_ref[...] = e_ref[...] + dx

        @pl.when(i == 0)
        def _():
            dg_ref[...] = jnp.zeros_like(dg_ref)

        dg_ref[...] += jnp.sum((dy * n).reshape(tr // SUB, SUB, d), axis=0)

    return pl.pallas_call(
        body, name=name, grid=(s // tr,),
        in_specs=[pl.BlockSpec((tr, d), lambda i: (i, 0)), pl.BlockSpec((1, d), lambda i: (0, 0)),
                  pl.BlockSpec((tr, d), lambda i: (i, 0)), pl.BlockSpec((tr, d), lambda i: (i, 0))],
        out_specs=[pl.BlockSpec((tr, d), lambda i: (i, 0)), pl.BlockSpec((SUB, d), lambda i: (0, 0))],
        out_shape=[jax.ShapeDtypeStruct((s, d), F32), jax.ShapeDtypeStruct((SUB, d), F32)],
        compiler_params=_cparams("arbitrary"),
    )(x, g, du, extra)


def _head_norm(x, g):
    r = lax.rsqrt(jnp.mean(x * x, axis=-1, keepdims=True) + EPS)
    return x * r * g


def _qk_norm_fwd(proj, g_q, g_k, nh):
    s = proj.shape[0]
    tr = _pick(s, (512, 256, 128, 64, 32, 16, 8))

    def body(q_ref, k_ref, v_ref, gq_ref, gk_ref, qn_ref, kn_ref, vb_ref):
        qn_ref[...] = _head_norm(q_ref[...], gq_ref[...]).astype(BF16)
        kn_ref[...] = _head_norm(k_ref[...], gk_ref[...]).astype(BF16)
        vb_ref[...] = v_ref[...].astype(BF16)

    col = lambda base: pl.BlockSpec((tr, HEAD), lambda i, h: (i, base + h))
    gs = pl.BlockSpec((1, HEAD), lambda i, h: (0, 0))
    o = pl.BlockSpec((tr, HEAD), lambda i, h: (i, h))
    sh = jax.ShapeDtypeStruct((s, nh * HEAD), BF16)
    return pl.pallas_call(
        body, name="qk_norm_fwd", grid=(s // tr, nh),
        in_specs=[col(0), col(nh), col(2 * nh), gs, gs], out_specs=[o, o, o], out_shape=[sh, sh, sh],
        compiler_params=_cparams("parallel", "parallel"),
    )(proj, proj, proj, g_q, g_k)


def _qk_norm_bwd(proj, g_q, g_k, dqn, dkn, nh):
    s = proj.shape[0]
    tr = _pick(s, (512, 256, 128, 64, 32, 16, 8))

    def one(x, g, dy):
        r = lax.rsqrt(jnp.mean(x * x, axis=-1, keepdims=True) + EPS)
        n = x * r
        a = dy * g
        dx = r * (a - n * jnp.mean(a * n, axis=-1, keepdims=True))
        return dx, jnp.sum((dy * n).reshape(tr // SUB, SUB, HEAD), axis=0)

    def body(q_ref, k_ref, gq_ref, gk_ref, dqn_ref, dkn_ref, dq_ref, dk_ref, dgq_ref, dgk_ref):
        first = (pl.program_id(0) == 0) & (pl.program_id(1) == 0)

        @pl.when(first)
        def _():
            dgq_ref[...] = jnp.zeros_like(dgq_ref)
            dgk_ref[...] = jnp.zeros_like(dgk_ref)

        dq, pq = one(q_ref[...], gq_ref[...], dqn_ref[...])
        dk, pk = one(k_ref[...], gk_ref[...], dkn_ref[...])
        dq_ref[...] = dq.astype(BF16)
        dk_ref[...] = dk.astype(BF16)
        dgq_ref[...] += pq
        dgk_ref[...] += pk

    col = lambda base: pl.BlockSpec((tr, HEAD), lambda i, h: (i, base + h))
    gs = pl.BlockSpec((1, HEAD), lambda i, h: (0, 0))
    o = pl.BlockSpec((tr, HEAD), lambda i, h: (i, h))
    part = pl.BlockSpec((SUB, HEAD), lambda i, h: (0, 0))
    sh = jax.ShapeDtypeStruct((s, nh * HEAD), BF16)
    psh = jax.ShapeDtypeStruct((SUB, HEAD), F32)
    return pl.pallas_call(
        body, name="qk_norm_bwd", grid=(s // tr, nh),
        in_specs=[col(0), col(nh), gs, gs, o, o], out_specs=[o, o, part, part], out_shape=[sh, sh, psh, psh],
        compiler_params=_cparams("arbitrary", "arbitrary"),
    )(proj, proj, g_q, g_k, dqn, dkn)


def _sb_tile(kb, q, scale, keep):
    zt = _dot(kb, q, NT) * scale
    l = -(jnp.maximum(zt, 0.0) + jnp.log1p(jnp.exp(-jnp.abs(zt))))
    return zt, jnp.where(keep, l, 0.0)


def _sb_attn_fwd(qn, kn, vb, nh, blk):
    s = qn.shape[0]
    nb = s // blk
    scale = HEAD ** -0.5

    def body(q_ref, k_ref, v_ref, y_ref, c_ref):
        qi = pl.program_id(1)
        q = q_ref[...]
        upper = _tri(blk, "ge")
        row = lax.broadcasted_iota(jnp.int32, (blk, blk), 0)
        colq = lax.broadcasted_iota(jnp.int32, (blk, blk), 1)
        strict = row < colq

        def step(i, carry):
            acc, cr = carry
            kb_i = qi - i
            off = pl.multiple_of(kb_i * blk, blk)
            kb = k_ref[pl.ds(off, blk), :]
            vv = v_ref[pl.ds(off, blk), :]
            keep = strict | (i > 0)
            zt, l = _sb_tile(kb, q, scale, keep)
            c_ref[kb_i] = cr
            hi, lo = _split(l)
            cum = _dot(upper, hi, NN) + _dot(upper, lo, NN)
            wt = jnp.where(keep, jnp.exp(zt + cum + cr), 0.0)
            acc = acc + _dot(wt.astype(BF16), vv, TN)
            return acc, cr + cum[0:1, :]

        acc, _ = lax.fori_loop(0, qi + 1, step, (jnp.zeros((blk, HEAD), F32), jnp.zeros((1, blk), F32)))
        y_ref[...] = acc.astype(y_ref.dtype)

    qs = pl.BlockSpec((blk, HEAD), lambda h, i: (i, h))
    full = pl.BlockSpec((s, HEAD), lambda h, i: (0, h))
    return pl.pallas_call(
        body, name="sb_attn_fwd", grid=(nh, nb),
        in_specs=[qs, full, full],
        out_specs=[qs, pl.BlockSpec((None, nb, 1, blk), lambda h, i: (h, 0, 0, i))],
        out_shape=[jax.ShapeDtypeStruct((s, nh * HEAD), BF16), jax.ShapeDtypeStruct((nh, nb, 1, s), F32)],
        compiler_params=_cparams("parallel", "arbitrary"),
    )(qn, kn, vb)


def _sb_attn_bwd(qn, kn, vb, dy, carries, nh, blk):
    s = qn.shape[0]
    nb = s // blk
    scale = HEAD ** -0.5

    def body(q_ref, k_ref, v_ref, dy_ref, c_ref, dq_ref, dk_ref, dv_ref):
        qi = pl.program_id(1)

        @pl.when(qi == 0)
        def _():
            dk_ref[...] = jnp.zeros_like(dk_ref)
            dv_ref[...] = jnp.zeros_like(dv_ref)

        q = q_ref[...]
        do = dy_ref[...].astype(BF16)
        upper = _tri(blk, "ge")
        lower = _tri(blk, "lt")
        row = lax.broadcasted_iota(jnp.int32, (blk, blk), 0)
        colq = lax.broadcasted_iota(jnp.int32, (blk, blk), 1)
        strict = row < colq

        def step(kb_i, carry):
            dq, ec = carry
            off = pl.multiple_of(kb_i * blk, blk)
            kb = k_ref[pl.ds(off, blk), :]
            vv = v_ref[pl.ds(off, blk), :]
            keep = strict | (kb_i < qi)
            zt, l = _sb_tile(kb, q, scale, keep)
            hi, lo = _split(l)
            cum = _dot(upper, hi, NN) + _dot(upper, lo, NN) + c_ref[kb_i]
            wt = jnp.where(keep, jnp.exp(zt + cum), 0.0)
            beta = jnp.where(keep, jnp.exp(zt + l), 0.0)
            et = wt * _dot(vv, do, NT)
            before = _dot(lower, et.astype(BF16), NN) + ec
            dzt = ((et * (1.0 - beta) - beta * before) * scale).astype(BF16)
            dv_ref[pl.ds(off, blk), :] += _dot(wt.astype(BF16), do, NN)
            dk_ref[pl.ds(off, blk), :] += _dot(dzt, q, NN)
            dq = dq + _dot(dzt, kb, TN)
            return dq, ec + jnp.sum(et, axis=0, keepdims=True)

        dq, _ = lax.fori_loop(0, qi + 1, step, (jnp.zeros((blk, HEAD), F32), jnp.zeros((1, blk), F32)))
        dq_ref[...] = dq

    qs = pl.BlockSpec((blk, HEAD), lambda h, i: (i, h))
    full = pl.BlockSpec((s, HEAD), lambda h, i: (0, h))
    sh = jax.ShapeDtypeStruct((s, nh * HEAD), F32)
    return pl.pallas_call(
        body, name="sb_attn_bwd", grid=(nh, nb),
        in_specs=[qs, full, full, qs, pl.BlockSpec((None, nb, 1, blk), lambda h, i: (h, 0, 0, i))],
        out_specs=[qs, full, full], out_shape=[sh, sh, sh],
        compiler_params=_cparams("parallel", "arbitrary"),
    )(qn, kn, vb, dy, carries)


def _lower_bound(lb_logits):
    def body(l_ref, o_ref):
        l = l_ref[...]
        m = jnp.max(l, axis=0, keepdims=True)
        e = jnp.exp(l - m)
        o_ref[...] = e[0:1, :] / jnp.sum(e, axis=0, keepdims=True)

    return pl.pallas_call(body, name="lower_bound", out_shape=jax.ShapeDtypeStruct((1, lb_logits.shape[1]), F32))(lb_logits)


def _hg_gates(hq, hf, lb):
    sq = _sig(hq)
    q = hq * sq
    sf = _sig(hf)
    f = lb + (1.0 - lb) * sf
    return q, sq, f, sf


def _hg_cum(g, c):
    hi, lo = _split(g)
    t = _tri(c, "le")
    return _dot(t, hi, NN) + _dot(t, lo, NN)


def _row_mask(r):
    return lax.broadcasted_iota(jnp.int32, (SUB, HEAD), 0) >= r


def _hg_intra_fwd(q, k, v, b, c):
    outs = []
    for bi in range(c // SUB):
        q_i, b_i = q[bi * SUB:(bi + 1) * SUB], b[bi * SUB:(bi + 1) * SUB]
        acc = jnp.zeros((SUB, HEAD), F32)
        for s in range((bi + 1) * SUB):
            d = b_i - b[s:s + 1]
            if s >= bi * SUB:
                d = jnp.where(_row_mask(s - bi * SUB), d, -jnp.inf)
            col = jnp.sum(q_i * k[s:s + 1] * jnp.exp(d), axis=1, keepdims=True)
            acc = acc + col * v[s:s + 1]
        outs.append(acc)
    return jnp.concatenate(outs, axis=0)


def _hg_intra_bwd(q, k, v, b, do, c, dq_scr, dk_scr, dv_scr):
    nblk = c // SUB
    dq_scr[...] = jnp.zeros_like(dq_scr)
    for s in range(c):
        bj = s // SUB
        ks, vs, bs = k[s:s + 1], v[s:s + 1], b[s:s + 1]
        acc_k = jnp.zeros((SUB, HEAD), F32)
        acc_v = jnp.zeros((SUB, HEAD), F32)
        for bi in range(bj, nblk):
            sl = slice(bi * SUB, (bi + 1) * SUB)
            d = b[sl] - bs
            if bi == bj:
                d = jnp.where(_row_mask(s - bj * SUB), d, -jnp.inf)
            dec = jnp.exp(d)
            qd = q[sl] * dec
            col = jnp.sum(qd * ks, axis=1, keepdims=True)
            dcol = jnp.sum(do[sl] * vs, axis=1, keepdims=True)
            dq_scr[sl, :] += dcol * (ks * dec)
            acc_k = acc_k + dcol * qd
            acc_v = acc_v + col * do[sl]
        dk_scr[s:s + 1, :] = jnp.sum(acc_k, axis=0, keepdims=True)
        dv_scr[s:s + 1, :] = jnp.sum(acc_v, axis=0, keepdims=True)


def _hgrn2_fwd(proj, lb, g_out, nh, base, c):
    s = proj.shape[0]
    nch = s // c

    def body(hq_ref, hf_ref, hi_ref, og_ref, lb_ref, g_ref, o_ref, y_ref, st_ref, st):
        @pl.when(pl.program_id(1) == 0)
        def _():
            st[...] = jnp.zeros_like(st)

        st_in = st[...]
        st_ref[...] = st_in
        q, _, f, _ = _hg_gates(hq_ref[...], hf_ref[...], lb_ref[...])
        k = 1.0 - f
        v = hi_ref[...]
        b = _hg_cum(jnp.log(f), c)
        bl = b[c - 1:c, :]
        o = _dot((q * jnp.exp(b)).astype(BF16), st_in.astype(BF16), NT) + _hg_intra_fwd(q, k, v, b, c)
        kd = k * jnp.exp(bl - b)
        st[...] = st_in * jnp.exp(bl) + _dot(v.astype(BF16), kd.astype(BF16), TN)
        o_ref[...] = o
        og = og_ref[...]
        y_ref[...] = (_head_norm(o, g_ref[...]) * (og * _sig(og))).astype(BF16)

    col = lambda j: pl.BlockSpec((c, HEAD), lambda h, i: (i, base + j * nh + h))
    row = pl.BlockSpec((1, HEAD), lambda h, i: (0, h))
    gs = pl.BlockSpec((1, HEAD), lambda h, i: (0, 0))
    o = pl.BlockSpec((c, HEAD), lambda h, i: (i, h))
    return pl.pallas_call(
        body, name="hgrn2_fwd", grid=(nh, nch),
        in_specs=[col(0), col(1), col(2), col(3), row, gs],
        out_specs=[o, o, pl.BlockSpec((None, None, HEAD, HEAD), lambda h, i: (h, i, 0, 0))],
        out_shape=[jax.ShapeDtypeStruct((s, nh * HEAD), F32), jax.ShapeDtypeStruct((s, nh * HEAD), BF16),
                   jax.ShapeDtypeStruct((nh, nch, HEAD, HEAD), F32)],
        scratch_shapes=[pltpu.VMEM((HEAD, HEAD), F32)],
        compiler_params=_cparams("parallel", "arbitrary"),
    )(proj, proj, proj, proj, lb, g_out)


def _hgrn2_bwd(proj, lb, g_out, o_pre, states, dy, nh, base, c):
    s = proj.shape[0]
    nch = s // c

    def body(hq_ref, hf_ref, hi_ref, og_ref, lb_ref, g_ref, o_ref, st_ref, se_ref, dy_ref,
             dhq_ref, dhf_ref, dhi_ref, dog_ref, dg_ref, dlb_ref, dst, dq_scr, dk_scr, dv_scr):
        h, i = pl.program_id(0), pl.program_id(1)

        @pl.when(i == 0)
        def _():
            dst[...] = jnp.zeros_like(dst)
            dlb_ref[...] = jnp.zeros_like(dlb_ref)

        @pl.when((i == 0) & (h == 0))
        def _():
            dg_ref[...] = jnp.zeros_like(dg_ref)

        lbv = lb_ref[...]
        hq, hf = hq_ref[...], hf_ref[...]
        q, sq, f, sf = _hg_gates(hq, hf, lbv)
        k = 1.0 - f
        v = hi_ref[...]
        b = _hg_cum(jnp.log(f), c)
        bl = b[c - 1:c, :]
        eb = jnp.exp(b)
        ebl = jnp.exp(bl - b)

        o = o_ref[...]
        gout = g_ref[...]
        og = og_ref[...]
        sg = _sig(og)
        r = lax.rsqrt(jnp.mean(o * o, axis=-1, keepdims=True) + EPS)
        n = o * r
        dyv = dy_ref[...]
        dn = dyv * (og * sg)
        dog_ref[...] = (dyv * n * gout * (sg * (1.0 + og * (1.0 - sg)))).astype(BF16)
        dg_ref[...] += jnp.sum((dn * n).reshape(c // SUB, SUB, HEAD), axis=0)
        a = dn * gout
        do = r * (a - n * jnp.mean(a * n, axis=-1, keepdims=True))

        st_in = st_ref[...]
        dstv = dst[...]
        dob = do.astype(BF16)
        dstb = dstv.astype(BF16)
        _hg_intra_bwd(q, k, v, b, do, c, dq_scr, dk_scr, dv_scr)
        dq = dq_scr[...] + eb * _dot(dob, st_in.astype(BF16), NN)
        dk = dk_scr[...] + ebl * _dot(v.astype(BF16), dstb, NN)
        dv = dv_scr[...] + _dot((k * ebl).astype(BF16), dstb, NT)
        dst[...] = dstv * jnp.exp(bl) + _dot(dob, (q * eb).astype(BF16), TN)

        hi_, lo_ = _split(q * dq - k * dk)
        rev = _tri(c, "ge")
        later = jnp.where(i > 0, jnp.sum(dstv * se_ref[...], axis=0, keepdims=True), 0.0)
        dg = _dot(rev, hi_, NN) + _dot(rev, lo_, NN) + later
        df = dg / f - dk
        dhq_ref[...] = (dq * (sq * (1.0 + hq * (1.0 - sq)))).astype(BF16)
        dhf_ref[...] = (df * (1.0 - lbv) * sf * (1.0 - sf)).astype(BF16)
        dhi_ref[...] = dv.astype(BF16)
        dlb_ref[...] += jnp.sum((df * (1.0 - sf)).reshape(c // SUB, SUB, HEAD), axis=0)

    rv = lambda i: nch - 1 - i
    col = lambda j: pl.BlockSpec((c, HEAD), lambda h, i: (rv(i), base + j * nh + h))
    row = pl.BlockSpec((1, HEAD), lambda h, i: (0, h))
    gs = pl.BlockSpec((1, HEAD), lambda h, i: (0, 0))
    o = pl.BlockSpec((c, HEAD), lambda h, i: (rv(i), h))
    st = pl.BlockSpec((None, None, HEAD, HEAD), lambda h, i: (h, rv(i), 0, 0))
    se = pl.BlockSpec((None, None, HEAD, HEAD), lambda h, i: (h, jnp.minimum(rv(i) + 1, nch - 1), 0, 0))
    sh = jax.ShapeDtypeStruct((s, nh * HEAD), BF16)
    return pl.pallas_call(
        body, name="hgrn2_bwd", grid=(nh, nch),
        in_specs=[col(0), col(1), col(2), col(3), row, gs, o, st, se, o],
        out_specs=[o, o, o, o, pl.BlockSpec((SUB, HEAD), lambda h, i: (0, 0)), pl.BlockSpec((SUB, HEAD), lambda h, i: (0, h))],
        out_shape=[sh, sh, sh, sh, jax.ShapeDtypeStruct((SUB, HEAD), F32), jax.ShapeDtypeStruct((SUB, nh * HEAD), F32)],
        scratch_shapes=[pltpu.VMEM((HEAD, HEAD), F32), pltpu.VMEM((c, HEAD), F32), pltpu.VMEM((c, HEAD), F32),
                        pltpu.VMEM((c, HEAD), F32)],
        compiler_params=_cparams("arbitrary", "arbitrary"),
    )(proj, proj, proj, proj, lb, g_out, o_pre, states, states, dy)


def _merge_tiles(s, d, gate_col):
    tr = _pick(s, (256, 128, 64, 32, 16, 8))
    tc = 128
    for cand in (512, 256):
        if d % cand == 0 and gate_col % cand == 0:
            tc = cand
            break
    return tr, tc


def _merge_fwd(proj, ya, yb, gate_col):
    s, d = ya.shape
    tr, tc = _merge_tiles(s, d, gate_col)
    ga0, gb0 = gate_col // tc, (gate_col + d) // tc

    def body(ga_ref, gb_ref, ya_ref, yb_ref, m_ref):
        m_ref[...] = (_sig(ga_ref[...]) * ya_ref[...] + _sig(gb_ref[...]) * yb_ref[...]).astype(BF16)

    o = pl.BlockSpec((tr, tc), lambda i, j: (i, j))
    return pl.pallas_call(
        body, name="merge_fwd", grid=(s // tr, d // tc),
        in_specs=[pl.BlockSpec((tr, tc), lambda i, j: (i, ga0 + j)), pl.BlockSpec((tr, tc), lambda i, j: (i, gb0 + j)), o, o],
        out_specs=o, out_shape=jax.ShapeDtypeStruct((s, d), BF16),
        compiler_params=_cparams("parallel", "parallel"),
    )(proj, proj, ya, yb)


def _merge_bwd(proj, ya, yb, dm, gate_col):
    s, d = ya.shape
    tr, tc = _merge_tiles(s, d, gate_col)
    ga0, gb0 = gate_col // tc, (gate_col + d) // tc

    def body(ga_ref, gb_ref, ya_ref, yb_ref, dm_ref, dya_ref, dyb_ref, dga_ref, dgb_ref):
        dmv = dm_ref[...]
        sa, sb = _sig(ga_ref[...]), _sig(gb_ref[...])
        dya_ref[...] = (dmv * sa).astype(BF16)
        dyb_ref[...] = (dmv * sb).astype(BF16)
        dga_ref[...] = (dmv * ya_ref[...] * sa * (1.0 - sa)).astype(BF16)
        dgb_ref[...] = (dmv * yb_ref[...] * sb * (1.0 - sb)).astype(BF16)

    o = pl.BlockSpec((tr, tc), lambda i, j: (i, j))
    sh = jax.ShapeDtypeStruct((s, d), BF16)
    return pl.pallas_call(
        body, name="merge_bwd", grid=(s // tr, d // tc),
        in_specs=[pl.BlockSpec((tr, tc), lambda i, j: (i, ga0 + j)), pl.BlockSpec((tr, tc), lambda i, j: (i, gb0 + j)), o, o, o],
        out_specs=[o, o, o, o], out_shape=[sh, sh, sh, sh],
        compiler_params=_cparams("parallel", "parallel"),
    )(proj, proj, ya, yb, dm)


CONV_ROWS = 512


def _conv_ext(ref, i, rows, s, before, after):
    parts = []
    if before:
        p = ref[pl.ds(pl.multiple_of(jnp.maximum(i * rows - before, 0), SUB), before), :]
        parts.append(jnp.where(i > 0, p, 0.0))
    parts.append(ref[pl.ds(pl.multiple_of(i * rows, SUB), rows), :])
    if after:
        nxt = ref[pl.ds(pl.multiple_of(jnp.minimum((i + 1) * rows, s - after), SUB), after), :]
        parts.append(jnp.where((i + 1) * rows < s, nxt, 0.0))
    return jnp.concatenate(parts, axis=0)


def _conv3(ext, w, bias):
    x1 = pltpu.roll(ext, 1, 0)
    x2 = pltpu.roll(ext, 2, 0)
    return bias + w[0:1, :] * x2 + w[1:2, :] * x1 + w[2:3, :] * ext, x1, x2


def _convffn_fwd(up, conv_w, conv_b, dff):
    s = up.shape[0]
    tc = HEAD
    nf = dff // tc
    rows = _pick(s, (CONV_ROWS, 256, 128, 64, 32, 16, 8))

    def body(ug_ref, uv_ref, wg_ref, wv_ref, bg_ref, bv_ref, a_ref):
        wg, wv, bg, bv = wg_ref[...], wv_ref[...], bg_ref[...], bv_ref[...]

        def step(i, _):
            g = _conv3(_conv_ext(ug_ref, i, rows, s, SUB, 0), wg, bg)[0][SUB:]
            v = _conv3(_conv_ext(uv_ref, i, rows, s, SUB, 0), wv, bv)[0][SUB:]
            a_ref[pl.ds(pl.multiple_of(i * rows, SUB), rows), :] = (g * _sig(g) * v).astype(BF16)
            return 0

        lax.fori_loop(0, s // rows, step, 0)

    cg = lambda r: pl.BlockSpec((r, tc), lambda j: (0, j))
    cv = lambda r: pl.BlockSpec((r, tc), lambda j: (0, nf + j))
    return pl.pallas_call(
        body, name="convffn_fwd", grid=(nf,),
        in_specs=[cg(s), cv(s), cg(3), cv(3), cg(1), cv(1)], out_specs=cg(s),
        out_shape=jax.ShapeDtypeStruct((s, dff), BF16),
        compiler_params=_cparams("parallel"),
    )(up, up, conv_w, conv_w, conv_b, conv_b)


def _convffn_bwd(up, conv_w, conv_b, dact, dff):
    s = up.shape[0]
    tc = HEAD
    nf = dff // tc
    rows = _pick(s, (CONV_ROWS, 256, 128, 64, 32, 16, 8))
    n_ext = rows + SUB

    def body(ug_ref, uv_ref, wg_ref, wv_ref, bg_ref, bv_ref, da_ref,
             dug_ref, duv_ref, dwg_ref, dwv_ref, dbg_ref, dbv_ref):
        wg, wv, bg, bv = wg_ref[...], wv_ref[...], bg_ref[...], bv_ref[...]

        def fold(x):
            return jnp.sum(x.reshape(rows // SUB, SUB, tc), axis=0)

        def one(ext, x1, x2, d_ext, w):
            d1 = pltpu.roll(d_ext, n_ext - 1, 0)[:rows]
            d2 = pltpu.roll(d_ext, n_ext - 2, 0)[:rows]
            dc = d_ext[:rows]
            du = w[2:3, :] * dc + w[1:2, :] * d1 + w[0:1, :] * d2
            sl = slice(SUB, SUB + rows)
            return du, (fold(dc * x2[sl]), fold(dc * x1[sl]), fold(dc * ext[sl]), fold(dc))

        def step(i, acc):
            eg = _conv_ext(ug_ref, i, rows, s, SUB, SUB)
            ev = _conv_ext(uv_ref, i, rows, s, SUB, SUB)
            g, g1, g2 = _conv3(eg, wg, bg)
            v, v1, v2 = _conv3(ev, wv, bv)
            g, v = g[SUB:], v[SUB:]
            da = _conv_ext(da_ref, i, rows, s, 0, SUB)
            sg = _sig(g)
            dg = da * v * (sg * (1.0 + g * (1.0 - sg)))
            dv = da * (g * sg)
            dug, pg = one(eg, g1, g2, dg, wg)
            duv, pv = one(ev, v1, v2, dv, wv)
            at = pl.ds(pl.multiple_of(i * rows, SUB), rows)
            dug_ref[at, :] = dug.astype(BF16)
            duv_ref[at, :] = duv.astype(BF16)
            return tuple(a + p for a, p in zip(acc, pg + pv))

        zero = jnp.zeros((SUB, tc), F32)
        acc = lax.fori_loop(0, s // rows, step, (zero,) * 8)
        red = [jnp.sum(a, axis=0, keepdims=True) for a in acc]
        for j in range(3):
            dwg_ref[j:j + 1, :] = red[j]
            dwv_ref[j:j + 1, :] = red[4 + j]
        dbg_ref[...] = red[3]
        dbv_ref[...] = red[7]

    cg = lambda r: pl.BlockSpec((r, tc), lambda j: (0, j))
    cv = lambda r: pl.BlockSpec((r, tc), lambda j: (0, nf + j))
    outs = pl.pallas_call(
        body, name="convffn_bwd", grid=(nf,),
        in_specs=[cg(s), cv(s), cg(3), cv(3), cg(1), cv(1), cg(s)],
        out_specs=[cg(s), cg(s), cg(3), cg(3), cg(1), cg(1)],
        out_shape=[jax.ShapeDtypeStruct((s, dff), BF16), jax.ShapeDtypeStruct((s, dff), BF16),
                   jax.ShapeDtypeStruct((3, dff), F32), jax.ShapeDtypeStruct((3, dff), F32),
                   jax.ShapeDtypeStruct((1, dff), F32), jax.ShapeDtypeStruct((1, dff), F32)],
        compiler_params=_cparams("parallel"),
    )(up, up, conv_w, conv_w, conv_b, conv_b, dact)
    return outs


def _loss_head(out, target):
    s, d = out.shape
    tr = _pick(s, (256, 128, 64, 32, 16, 8))

    def body(o_ref, t_ref, d_ref, l_ref):
        @pl.when(pl.program_id(0) == 0)
        def _():
            l_ref[...] = jnp.zeros_like(l_ref)

        err = o_ref[...] - t_ref[...]
        d_ref[...] = err * (1.0 / d)
        sq = jnp.sum((err * err).reshape(tr // SUB, SUB, d), axis=0)
        part = sq[:, 0:HEAD]
        for j in range(1, d // HEAD):
            part = part + sq[:, j * HEAD:(j + 1) * HEAD]
        l_ref[...] += part

    blk = pl.BlockSpec((tr, d), lambda i: (i, 0))
    return pl.pallas_call(
        body, name="loss_head", grid=(s // tr,), in_specs=[blk, blk],
        out_specs=[blk, pl.BlockSpec((SUB, HEAD), lambda i: (0, 0))],
        out_shape=[jax.ShapeDtypeStruct((s, d), F32), jax.ShapeDtypeStruct((SUB, HEAD), F32)],
        compiler_params=_cparams("arbitrary"),
    )(out, target)


def _sum_rows(name, parts):
    def body(p_ref, o_ref):
        o_ref[...] = jnp.sum(p_ref[...], axis=0, keepdims=True)

    return pl.pallas_call(body, name=name, out_shape=jax.ShapeDtypeStruct((1, parts.shape[1]), F32))(parts)


def _local_step(x, target, g_mix, g_q, g_k, lb_logits, g_hg_out, g_ffn, conv_w, conv_b, w_in, p_a, p_b, w_o, w_up, w_down):
    s, d = x.shape
    nh = p_a.shape[1] // HEAD
    wid = nh * HEAD
    dff = w_down.shape[1]
    blk = _pick(s, (256, 128))
    chunk = _pick(s, (HG_CHUNK,))
    gate_col = 7 * wid

    u = _rmsnorm_fwd("rmsnorm_mix", x, g_mix)
    proj = _matmul("in_proj", u, w_in, "nn", F32)
    qn, kn, vb = _qk_norm_fwd(proj, g_q, g_k, nh)
    y_a, carries = _sb_attn_fwd(qn, kn, vb, nh, blk)
    lb = _lower_bound(lb_logits)
    o_pre, y_b, states = _hgrn2_fwd(proj, lb, g_hg_out, nh, 3 * nh, chunk)
    ya_p = _matmul("proj_a", y_a, p_a, "nn", F32)
    yb_p = _matmul("proj_b", y_b, p_b, "nn", F32)
    m = _merge_fwd(proj, ya_p, yb_p, gate_col)
    h = _matmul("out_proj", m, w_o, "nn", F32, add=x)
    u2 = _rmsnorm_fwd("rmsnorm_ffn", h, g_ffn)
    up = _matmul("up_proj", u2, w_up, "nn", F32)
    act = _convffn_fwd(up, conv_w, conv_b, dff)
    out = _matmul("down_proj", act, w_down, "nn", F32, add=h)
    dout, sq = _loss_head(out, target)

    dact = _matmul("d_act", dout, w_down, "nt", F32)
    g_w_down = _matmul_tn("g_w_down", act, dout, 1, BF16)
    dup_g, dup_v, dcw_g, dcw_v, dcb_g, dcb_v = _convffn_bwd(up, conv_w, conv_b, dact, dff)
    dup = jnp.concatenate([dup_g, dup_v], axis=1)
    du2 = _matmul("d_u2", dup, w_up, "nt", F32)
    g_w_up = _matmul_tn("g_w_up", u2, dup, N_CHIPS, BF16)
    dh, pg_ffn = _rmsnorm_bwd("rmsnorm_ffn_bwd", h, g_ffn, du2, dout)
    dm = _matmul("d_m", dh, w_o, "nt", F32)
    g_w_o = _matmul_tn("g_w_o", m, dh, 1, BF16)
    dya_p, dyb_p, dga, dgb = _merge_bwd(proj, ya_p, yb_p, dm, gate_col)
    dy_a = _matmul("d_y_a", dya_p, p_a, "nt", F32)
    g_p_a = _matmul_tn("g_p_a", y_a, dya_p, N_CHIPS, BF16)
    dy_b = _matmul("d_y_b", dyb_p, p_b, "nt", F32)
    g_p_b = _matmul_tn("g_p_b", y_b, dyb_p, N_CHIPS, BF16)
    dhq, dhf, dhi, dog, pg_hg, p_lb = _hgrn2_bwd(proj, lb, g_hg_out, o_pre, states, dy_b, nh, 3 * nh, chunk)
    dqn, dkn, dv = _sb_attn_bwd(qn, kn, vb, dy_a, carries, nh, blk)
    dq, dk, pg_q, pg_k = _qk_norm_bwd(proj, g_q, g_k, dqn, dkn, nh)
    dproj = jnp.concatenate([dq, dk, dv.astype(BF16), dhq, dhf, dhi, dog, dga, dgb], axis=1)
    du = _matmul("d_u", dproj, w_in, "nt", F32)
    g_w_in = _matmul_tn("g_w_in", u, dproj, N_CHIPS, BF16)
    dx, pg_mix = _rmsnorm_bwd("rmsnorm_mix_bwd", x, g_mix, du, dh)

    small = dict(
        g_mix=_sum_rows("sum_g_mix", pg_mix), g_q=_sum_rows("sum_g_q", pg_q), g_k=_sum_rows("sum_g_k", pg_k),
        lb=_sum_rows("sum_lb", p_lb), g_hg_out=_sum_rows("sum_g_hg", pg_hg), g_ffn=_sum_rows("sum_g_ffn", pg_ffn),
        conv_w=jnp.concatenate([dcw_g, dcw_v], axis=1), conv_b=jnp.concatenate([dcb_g, dcb_v], axis=1),
        sq=_sum_rows("sum_sq", sq),
    )
    big = dict(w_in=g_w_in, p_a=g_p_a, p_b=g_p_b, w_o=g_w_o.reshape(N_CHIPS, d // N_CHIPS, d), w_up=g_w_up,
               w_down=g_w_down.reshape(N_CHIPS, dff // N_CHIPS, d))
    return dx, big, small, lb


ANY = pl.BlockSpec(memory_space=pl.ANY)


def _place():
    x, y, c = lax.axis_index("x"), lax.axis_index("y"), lax.axis_index("c")
    chips = [(1 - x, y), (x, 1 - y), (1 - x, 1 - y)]
    return x, y, c, chips


def _remote(src, dst, send_sem, recv_sem, to):
    return pltpu.make_async_remote_copy(src_ref=src, dst_ref=dst, send_sem=send_sem, recv_sem=recv_sem,
                                        device_id=to, device_id_type=MESH)


def _cast_bf16(name, w):
    r, c = w.shape
    tr = _pick(r, (256, 128, 64, 32, 16))

    def body(w_ref, o_ref):
        o_ref[...] = w_ref[...].astype(BF16)

    return pl.pallas_call(
        body, name=name, grid=(r // tr,), in_specs=[pl.BlockSpec((tr, c), lambda i: (i, 0))],
        out_specs=pl.BlockSpec((tr, c), lambda i: (i, 0)), out_shape=jax.ShapeDtypeStruct((r, c), BF16),
        compiler_params=_cparams("parallel"),
    )(w)


def _gather_weights(shards):
    n = len(shards)

    def body(*refs):
        ins, outs = refs[:n], refs[n:2 * n]
        send, recv, local = refs[2 * n:]
        x, y, c, chips = _place()
        mine = 2 * x + y
        sends, owns = [], []
        for k in range(n):
            half = ins[k].shape[0] // 2
            rows = pl.ds(c * half, half)
            own = pltpu.make_async_copy(ins[k], outs[k].at[mine], local.at[k])
            own.start()
            owns.append(own)
            for j, (px, py) in enumerate(chips):
                cp = _remote(ins[k].at[rows], outs[k].at[mine, rows], send.at[k, j], recv.at[k, j], (px, py, c))
                cp.start()
                sends.append(cp)
        for k in range(n):
            half = ins[k].shape[0] // 2
            rows = pl.ds(c * half, half)
            for j, (px, py) in enumerate(chips):
                part = outs[k].at[2 * px + py, rows]
                _remote(part, part, send.at[k, j], recv.at[k, j], (px, py, c)).wait_recv()
                fw = _remote(part, part, send.at[k, 3 + j], recv.at[k, 3 + j], (x, y, 1 - c))
                fw.start()
                sends.append(fw)
        for k in range(n):
            half = ins[k].shape[0] // 2
            other = pl.ds((1 - c) * half, half)
            for j, (px, py) in enumerate(chips):
                part = outs[k].at[2 * px + py, other]
                _remote(part, part, send.at[k, 3 + j], recv.at[k, 3 + j], (x, y, 1 - c)).wait_recv()
        for cp in sends:
            cp.wait_send()
        for cp in owns:
            cp.wait()

    return pl.pallas_call(
        body, name="gather_weights", in_specs=[ANY] * n, out_specs=[ANY] * n,
        out_shape=[jax.ShapeDtypeStruct((N_CHIPS,) + w.shape, w.dtype) for w in shards],
        scratch_shapes=[pltpu.SemaphoreType.DMA((n, 6)), pltpu.SemaphoreType.DMA((n, 6)), pltpu.SemaphoreType.DMA((n,))],
    )(*shards)


def _exchange_halves(name, srcs, src_half_other, out_shapes):
    n = len(srcs)

    def body(*refs):
        ins, outs = refs[:n], refs[n:2 * n]
        send, recv = refs[2 * n:]
        x, y, c, _ = _place()
        cps = []
        for k in range(n):
            if src_half_other:
                half = ins[k].shape[1] // 2
                src = ins[k].at[:, pl.ds((1 - c) * half, half)]
            else:
                src = ins[k]
            cp = _remote(src, outs[k], send.at[k], recv.at[k], (x, y, 1 - c))
            cp.start()
            cps.append(cp)
        for cp in cps:
            cp.wait_recv()
        for cp in cps:
            cp.wait_send()

    return pl.pallas_call(
        body, name=name, in_specs=[ANY] * n, out_specs=[ANY] * n, out_shape=out_shapes,
        scratch_shapes=[pltpu.SemaphoreType.DMA((n,)), pltpu.SemaphoreType.DMA((n,))],
    )(*srcs)


def _add_halves(name, g, got, c):
    _, r, cols = g.shape
    half = r // 2
    tr = _pick(half, (256, 128, 64, 32, 16))
    nt = half // tr

    def body(c_ref, g_ref, o_ref, out_ref):
        out_ref[...] = (g_ref[...].astype(F32) + o_ref[...].astype(F32)).astype(BF16)

    return pl.pallas_call(
        body, name=name,
        grid_spec=pltpu.PrefetchScalarGridSpec(
            num_scalar_prefetch=1, grid=(N_CHIPS, nt),
            in_specs=[pl.BlockSpec((None, tr, cols), lambda s, i, cr: (s, cr[0] * nt + i, 0)),
                      pl.BlockSpec((None, tr, cols), lambda s, i, cr: (s, i, 0))],
            out_specs=pl.BlockSpec((None, tr, cols), lambda s, i, cr: (s, i, 0))),
        out_shape=jax.ShapeDtypeStruct((N_CHIPS, half, cols), BF16),
        compiler_params=_cparams("parallel", "parallel"),
    )(c, g, got)


def _scatter_partials(parts):
    n = len(parts)

    def body(*refs):
        ins, outs = refs[:n], refs[n:2 * n]
        send, recv = refs[2 * n:]
        x, y, c, chips = _place()
        cps = []
        for k in range(n):
            for j, (px, py) in enumerate(chips):
                cp = _remote(ins[k].at[2 * px + py], outs[k].at[j], send.at[k, j], recv.at[k, j], (px, py, c))
                cp.start()
                cps.append(cp)
        for cp in cps:
            cp.wait_recv()
        for cp in cps:
            cp.wait_send()

    return pl.pallas_call(
        body, name="scatter_partials", in_specs=[ANY] * n, out_specs=[ANY] * n,
        out_shape=[jax.ShapeDtypeStruct((3,) + p.shape[1:], p.dtype) for p in parts],
        scratch_shapes=[pltpu.SemaphoreType.DMA((n, 3)), pltpu.SemaphoreType.DMA((n, 3))],
    )(*parts)


def _sum_partials(name, part, got, shard):
    _, half, cols = part.shape
    tr = _pick(half, (256, 128, 64, 32, 16))

    def body(s_ref, p_ref, g_ref, o_ref):
        acc = p_ref[...].astype(F32)
        for j in range(3):
            acc = acc + g_ref[j].astype(F32)
        o_ref[...] = acc

    return pl.pallas_call(
        body, name=name,
        grid_spec=pltpu.PrefetchScalarGridSpec(
            num_scalar_prefetch=1, grid=(half // tr,),
            in_specs=[pl.BlockSpec((None, tr, cols), lambda i, sr: (sr[0], i, 0)),
                      pl.BlockSpec((3, tr, cols), lambda i, sr: (0, i, 0))],
            out_specs=pl.BlockSpec((tr, cols), lambda i, sr: (i, 0))),
        out_shape=jax.ShapeDtypeStruct((half, cols), F32),
        compiler_params=_cparams("parallel"),
    )(shard, part, got)


def _join_halves(mine, got, c):
    half, cols = mine.shape
    tr = _pick(half, (256, 128, 64, 32, 16, 8))
    nt = half // tr

    def body(c_ref, a_ref, b_ref, o_ref):
        i = pl.program_id(0)
        own = (i // nt) == c_ref[0]

        @pl.when(own)
        def _():
            o_ref[...] = a_ref[...]

        @pl.when(jnp.logical_not(own))
        def _():
            o_ref[...] = b_ref[...]

    blk = pl.BlockSpec((tr, cols), lambda i, cr: (i % nt, 0))
    return pl.pallas_call(
        body, name="join_halves",
        grid_spec=pltpu.PrefetchScalarGridSpec(num_scalar_prefetch=1, grid=(2 * nt,), in_specs=[blk, blk],
                                               out_specs=pl.BlockSpec((tr, cols), lambda i, cr: (i, 0))),
        out_shape=jax.ShapeDtypeStruct((2 * half, cols), F32),
        compiler_params=_cparams("parallel"),
    )(c, mine, got)


def _all_gather_rows(name, row):
    p = row.shape[1]

    def body(in_ref, out_ref, send, recv, local):
        x, y, c, _ = _place()
        me = 4 * x + 2 * y + c
        own = pltpu.make_async_copy(in_ref, out_ref.at[me], local)
        own.start()
        cps = []
        for k in range(1, 8):
            px, py, pc = x ^ (k >> 2), y ^ ((k >> 1) & 1), c ^ (k & 1)
            cp = _remote(in_ref, out_ref.at[me], send.at[k - 1], recv.at[k - 1], (px, py, pc))
            cp.start()
            cps.append(cp)
        for cp in cps:
            cp.wait_recv()
        for cp in cps:
            cp.wait_send()
        own.wait()

    return pl.pallas_call(
        body, name=name, in_specs=[ANY], out_specs=ANY,
        out_shape=jax.ShapeDtypeStruct((8, 1, p), F32),
        scratch_shapes=[pltpu.SemaphoreType.DMA((7,)), pltpu.SemaphoreType.DMA((7,)), pltpu.SemaphoreType.DMA],
    )(row)


def _sum_devices(rows):
    def body(r_ref, o_ref):
        acc = r_ref[0]
        for k in range(1, 8):
            acc = acc + r_ref[k]
        o_ref[...] = acc

    return pl.pallas_call(body, name="sum_devices", out_shape=jax.ShapeDtypeStruct(rows.shape[1:], F32))(rows)


def _adamw(name, w, g, m, v):
    r, c = w.shape
    tr = _pick(r, (128, 64, 32, 16, 8))
    bc1 = 1.0 - ADAM_B1 ** ADAM_STEP
    bc2 = 1.0 - ADAM_B2 ** ADAM_STEP

    def body(w_ref, g_ref, m_ref, v_ref, d_ref, nm_ref, nv_ref):
        gv = g_ref[...]
        nm = ADAM_B1 * m_ref[...] + (1.0 - ADAM_B1) * gv
        nv = ADAM_B2 * v_ref[...] + (1.0 - ADAM_B2) * (gv * gv)
        d_ref[...] = -ADAM_LR * ((nm / bc1) / (jnp.sqrt(nv / bc2) + ADAM_EPS) + ADAM_WD * w_ref[...])
        nm_ref[...] = nm
        nv_ref[...] = nv

    blk = pl.BlockSpec((tr, c), lambda i: (i, 0))
    sh = jax.ShapeDtypeStruct((r, c), F32)
    return pl.pallas_call(
        body, name=name, grid=(r // tr,), in_specs=[blk] * 4, out_specs=[blk] * 3, out_shape=[sh] * 3,
        compiler_params=_cparams("parallel"),
    )(w, g, m, v)


def _lb_logits_grad(dlb, lb):
    def body(d_ref, lb_ref, o_ref):
        lbv = lb_ref[...]
        t = d_ref[...] * lbv * (1.0 - lbv)
        o_ref[0:1, :] = t
        o_ref[1:2, :] = -t

    return pl.pallas_call(body, name="lb_logits_grad", out_shape=jax.ShapeDtypeStruct((2, dlb.shape[1]), F32))(dlb, lb)


BIG = ("w_in", "p_a", "p_b", "w_o", "w_up", "w_down")
SMALL = ("g_mix", "g_q", "g_k", "lb_logits", "g_hg_out", "g_ffn", "conv_w", "conv_b")
ORDER = ("g_mix", "w_in", "g_q", "g_k", "lb_logits", "g_hg_out", "p_a", "p_b", "w_o", "g_ffn", "w_up", "conv_w", "conv_b", "w_down")


def kernel(x, g_mix, w_in, g_q, g_k, lb_logits, g_hg_out, p_a, p_b, w_o, g_ffn, w_up, conv_w, conv_b, w_down, loss_target, m_g_mix, m_w_in, m_g_q, m_g_k, m_lb_logits, m_g_hg_out, m_p_a, m_p_b, m_w_o, m_g_ffn, m_w_up, m_conv_w, m_conv_b, m_w_down, v_g_mix, v_w_in, v_g_q, v_g_k, v_lb_logits, v_g_hg_out, v_p_a, v_p_b, v_w_o, v_g_ffn, v_w_up, v_conv_w, v_conv_b, v_w_down):
    assert lb_logits.shape[0] == 2, "the lower bound is the first row of a two-row softmax"
    w = dict(g_mix=g_mix, w_in=w_in[0], g_q=g_q, g_k=g_k, lb_logits=lb_logits, g_hg_out=g_hg_out, p_a=p_a[0], p_b=p_b[0],
             w_o=w_o[0], g_ffn=g_ffn, w_up=w_up[0], conv_w=conv_w[0], conv_b=conv_b, w_down=w_down[0])
    mom = dict(g_mix=m_g_mix, w_in=m_w_in[0], g_q=m_g_q, g_k=m_g_k, lb_logits=m_lb_logits, g_hg_out=m_g_hg_out, p_a=m_p_a[0],
               p_b=m_p_b[0], w_o=m_w_o[0], g_ffn=m_g_ffn, w_up=m_w_up[0], conv_w=m_conv_w[0], conv_b=m_conv_b, w_down=m_w_down[0])
    var = dict(g_mix=v_g_mix, w_in=v_w_in[0], g_q=v_g_q, g_k=v_g_k, lb_logits=v_lb_logits, g_hg_out=v_g_hg_out, p_a=v_p_a[0],
               p_b=v_p_b[0], w_o=v_w_o[0], g_ffn=v_g_ffn, w_up=v_w_up[0], conv_w=v_conv_w[0], conv_b=v_conv_b, w_down=v_w_down[0])
    d = x.shape[2]
    cx, cy, cc = lax.axis_index("x"), lax.axis_index("y"), lax.axis_index("c")
    shard = (2 * cx + cy).astype(jnp.int32).reshape(1)
    core = cc.astype(jnp.int32).reshape(1)

    full = dict(zip(BIG, _gather_weights([_cast_bf16("cast_" + n, w[n]) for n in BIG])))
    cw = conv_w.shape[2]
    rows = _all_gather_rows("gather_conv_w", w["conv_w"].reshape(1, 3 * cw))
    conv_full = jnp.concatenate([rows[2 * s, 0].reshape(3, cw) for s in range(N_CHIPS)], axis=1)
    f_w_o = full["w_o"].reshape(1, d, d)
    f_w_down = full["w_down"].reshape(1, -1, d)

    dx, big, small, lb = _local_step(x[0], loss_target[0], g_mix, g_q, g_k, lb_logits, g_hg_out, g_ffn, conv_full, conv_b,
                                     full["w_in"], full["p_a"], full["p_b"], f_w_o, full["w_up"], f_w_down)

    gs = [big[n] for n in BIG]
    got = _exchange_halves("halves_to_sibling", gs, True,
                           [jax.ShapeDtypeStruct((N_CHIPS, g.shape[1] // 2, g.shape[2]), BF16) for g in gs])
    parts = [_add_halves("chip_sum_" + n, g, o, core) for n, g, o in zip(BIG, gs, got)]
    recv = _scatter_partials(parts)
    mine = [_sum_partials("shard_sum_" + n, p, r, shard) for n, p, r in zip(BIG, parts, recv)]
    theirs = _exchange_halves("reduced_to_sibling", mine, False, [jax.ShapeDtypeStruct(a.shape, F32) for a in mine])
    grads = {n: _join_halves(a, b, core) for n, a, b in zip(BIG, mine, theirs)}

    names = ("g_mix", "g_q", "g_k", "lb", "g_hg_out", "g_ffn", "conv_b", "sq")
    packed = jnp.concatenate([small[n] for n in names] + [small["conv_w"].reshape(1, -1)], axis=1)
    total = _sum_devices(_all_gather_rows("gather_small_grads", packed))
    off = 0
    red = {}
    for n in names:
        ln = small[n].shape[1]
        red[n] = total[:, off:off + ln]
        off += ln
    conv_all = total[:, off:].reshape(3, -1)
    loss = 0.5 * jnp.sum(red["sq"]) / d
    grads["conv_w"] = lax.dynamic_slice_in_dim(conv_all, (2 * cx + cy) * cw, cw, axis=1)
    grads["lb_logits"] = _lb_logits_grad(red["lb"], lb)
    for n in ("g_mix", "g_q", "g_k", "g_hg_out", "g_ffn", "conv_b"):
        grads[n] = red[n]

    delta, new_m, new_v = {}, {}, {}
    for n in ORDER:
        delta[n], new_m[n], new_v[n] = _adamw("adamw_" + n, w[n], grads[n], mom[n], var[n])

    def shaped(a, like):
        return a.reshape(like.shape)

    ref_w = dict(g_mix=g_mix, w_in=w_in, g_q=g_q, g_k=g_k, lb_logits=lb_logits, g_hg_out=g_hg_out, p_a=p_a, p_b=p_b, w_o=w_o,
                 g_ffn=g_ffn, w_up=w_up, conv_w=conv_w, conv_b=conv_b, w_down=w_down)
    outs = [loss, dx[None]]
    for group in (grads, delta, new_m, new_v):
        outs += [shaped(group[n], ref_w[n]) for n in ORDER]
    return tuple(outs)
```

```python
import functools

import jax
import jax.numpy as jnp
from jax import lax
from jax.experimental import pallas as pl
from jax.experimental.pallas import tpu as pltpu

F32 = jnp.float32
BF16 = jnp.bfloat16
HEAD = 128
EPS = 1e-6
N_CHIPS = 4
HG_CHUNK = 64
SUB = 8
ADAM_LR, ADAM_B1, ADAM_B2, ADAM_EPS, ADAM_WD, ADAM_STEP = 0.001, 0.9, 0.999, 1e-08, 0.01, 10
VMEM_LIMIT = 56 * 1024 * 1024
MESH = pl.DeviceIdType.MESH

NN = (((1,), (0,)), ((), ()))
NT = (((1,), (1,)), ((), ()))
TN = (((0,), (0,)), ((), ()))


def _cparams(*sem):
    return pltpu.CompilerParams(dimension_semantics=sem if sem else None, vmem_limit_bytes=VMEM_LIMIT)


def _pick(n, cands):
    for c in cands:
        if c <= n and n % c == 0:
            return c
    return n


def _sig(x):
    return 1.0 / (1.0 + jnp.exp(-x))


def _dot(a, b, dims):
    return lax.dot_general(a, b, dims, preferred_element_type=F32)


def _split(x):
    hi = x.astype(BF16)
    lo = (x - hi.astype(F32)).astype(BF16)
    return hi, lo


def _tri(n, kind):
    r = lax.broadcasted_iota(jnp.int32, (n, n), 0)
    c = lax.broadcasted_iota(jnp.int32, (n, n), 1)
    m = {"ge": c >= r, "gt": c > r, "le": c <= r, "lt": c < r}[kind]
    return jnp.where(m, 1.0, 0.0).astype(BF16)


TILE_M = (1024, 512, 256, 128)
TILE_N = (1408, 1024, 512, 256, 128)
TILE_K = (2048, 1408, 1024, 512, 256, 128)


def _matmul(name, a, b, mode, out_dtype, add=None):
    if mode == "nn":
        m, k = a.shape
        g, _, ns = b.shape
        n = g * ns
        tm, tn, tk = _pick(m, TILE_M), _pick(ns, TILE_N), _pick(k, TILE_K)
        nps = ns // tn
        grid = (m // tm, n // tn, k // tk)
        a_spec = pl.BlockSpec((tm, tk), lambda i, j, kk: (i, kk))
        b_spec = pl.BlockSpec((None, tk, tn), lambda i, j, kk: (j // nps, kk, j % nps))
        o_spec = pl.BlockSpec((tm, tn), lambda i, j, kk: (i, j))
        o_shape = jax.ShapeDtypeStruct((m, n), out_dtype)
        dims = NN
    elif mode == "nt":
        m, k = a.shape
        g, n, ks = b.shape
        tm, tn, tk = _pick(m, TILE_M), _pick(n, TILE_N), _pick(ks, TILE_K)
        kps = ks // tk
        grid = (m // tm, n // tn, k // tk)
        a_spec = pl.BlockSpec((tm, tk), lambda i, j, kk: (i, kk))
        b_spec = pl.BlockSpec((None, tn, tk), lambda i, j, kk: (kk // kps, j, kk % kps))
        o_spec = pl.BlockSpec((tm, tn), lambda i, j, kk: (i, j))
        o_shape = jax.ShapeDtypeStruct((m, n), out_dtype)
        dims = NT
    else:
        raise ValueError(mode)
    nk = grid[2]

    def body(*refs):
        a_ref, b_ref = refs[0], refs[1]
        add_ref = refs[2] if add is not None else None
        o_ref = refs[2 + (add is not None)]

        def finish(r):
            if add is not None:
                r = r + add_ref[...]
            o_ref[...] = r.astype(o_ref.dtype)

        part = _dot(a_ref[...].astype(BF16), b_ref[...].astype(BF16), dims)
        if nk == 1:
            finish(part)
            return
        acc = refs[-1]
        kk = pl.program_id(2)

        @pl.when(kk == 0)
        def _():
            acc[...] = part

        @pl.when(kk > 0)
        def _():
            acc[...] += part

        @pl.when(kk == nk - 1)
        def _():
            finish(acc[...])

    in_specs = [a_spec, b_spec]
    args = [a, b]
    if add is not None:
        in_specs.append(o_spec)
        args.append(add)
    return pl.pallas_call(
        body, name=name, grid=grid, in_specs=in_specs, out_specs=o_spec, out_shape=o_shape,
        scratch_shapes=[pltpu.VMEM((tm, tn), F32)] if nk > 1 else [],
        compiler_params=_cparams("parallel", "parallel", "arbitrary"),
    )(*args)


def _matmul_tn(name, a, b, g, out_dtype):
    k, m = a.shape
    _, n = b.shape
    ns = n // g
    tm, tn, tk = _pick(m, (2048, 1408) + TILE_M), _pick(ns, TILE_N), _pick(k, (1024, 512, 256, 128))
    nps = ns // tn
    nk = k // tk

    def body(a_ref, b_ref, o_ref, acc):
        kk = pl.program_id(2)
        part = _dot(a_ref[...].astype(BF16), b_ref[...].astype(BF16), TN)

        @pl.when(kk == 0)
        def _():
            acc[...] = part

        @pl.when(kk > 0)
        def _():
            acc[...] += part

        @pl.when(kk == nk - 1)
        def _():
            o_ref[...] = acc[...].astype(o_ref.dtype)

    return pl.pallas_call(
        body, name=name, grid=(m // tm, n // tn, nk),
        in_specs=[pl.BlockSpec((tk, tm), lambda i, j, kk: (kk, i)), pl.BlockSpec((tk, tn), lambda i, j, kk: (kk, j))],
        out_specs=pl.BlockSpec((None, tm, tn), lambda i, j, kk: (j // nps, i, j % nps)),
        out_shape=jax.ShapeDtypeStruct((g, m, ns), out_dtype),
        scratch_shapes=[pltpu.VMEM((tm, tn), F32)],
        compiler_params=_cparams("parallel", "parallel", "arbitrary"),
    )(a, b)


def _rmsnorm_fwd(name, x, g):
    s, d = x.shape
    tr = _pick(s, (256, 128, 64, 32, 16, 8))

    def body(x_ref, g_ref, u_ref):
        xv = x_ref[...]
        r = lax.rsqrt(jnp.mean(xv * xv, axis=-1, keepdims=True) + EPS)
        u_ref[...] = (xv * r * g_ref[...]).astype(u_ref.dtype)

    return pl.pallas_call(
        body, name=name, grid=(s // tr,),
        in_specs=[pl.BlockSpec((tr, d), lambda i: (i, 0)), pl.BlockSpec((1, d), lambda i: (0, 0))],
        out_specs=pl.BlockSpec((tr, d), lambda i: (i, 0)),
        out_shape=jax.ShapeDtypeStruct((s, d), BF16),
        compiler_params=_cparams("parallel"),
    )(x, g)


def _rmsnorm_bwd(name, x, g, du, extra):
    s, d = x.shape
    tr = _pick(s, (256, 128, 64, 32, 16, 8))

    def body(x_ref, g_ref, du_ref, e_ref, dx_ref, dg_ref):
        i = pl.program_id(0)
        xv = x_ref[...]
        r = lax.rsqrt(jnp.mean(xv * xv, axis=-1, keepdims=True) + EPS)
        n = xv * r
        dy = du_ref[...]
        a = dy * g_ref[...]
        dx = r * (a - n * jnp.mean(a * n, axis=-1, keepdims=True))
        dx_ref[...] = e_ref[...] + dx

        @pl.when(i == 0)
        def _():
            dg_ref[...] = jnp.zeros_like(dg_ref)

        dg_ref[...] += jnp.sum((dy * n).reshape(tr // SUB, SUB, d), axis=0)

    return pl.pallas_call(
        body, name=name, grid=(s // tr,),
        in_specs=[pl.BlockSpec((tr, d), lambda i: (i, 0)), pl.BlockSpec((1, d), lambda i: (0, 0)),
                  pl.BlockSpec((tr, d), lambda i: (i, 0)), pl.BlockSpec((tr, d), lambda i: (i, 0))],
        out_specs=[pl.BlockSpec((tr, d), lambda i: (i, 0)), pl.BlockSpec((SUB, d), lambda i: (0, 0))],
        out_shape=[jax.ShapeDtypeStruct((s, d), F32), jax.ShapeDtypeStruct((SUB, d), F32)],
        compiler_params=_cparams("arbitrary"),
    )(x, g, du, extra)


def _head_norm(x, g):
    r = lax.rsqrt(jnp.mean(x * x, axis=-1, keepdims=True) + EPS)
    return x * r * g


def _qk_norm_fwd(proj, g_q, g_k, nh):
    s = proj.shape[0]
    tr = _pick(s, (512, 256, 128, 64, 32, 16, 8))

    def body(q_ref, k_ref, v_ref, gq_ref, gk_ref, qn_ref, kn_ref, vb_ref):
        qn_ref[...] = _head_norm(q_ref[...], gq_ref[...]).astype(BF16)
        kn_ref[...] = _head_norm(k_ref[...], gk_ref[...]).astype(BF16)
        vb_ref[...] = v_ref[...].astype(BF16)

    col = lambda base: pl.BlockSpec((tr, HEAD), lambda i, h: (i, base + h))
    gs = pl.BlockSpec((1, HEAD), lambda i, h: (0, 0))
    o = pl.BlockSpec((tr, HEAD), lambda i, h: (i, h))
    sh = jax.ShapeDtypeStruct((s, nh * HEAD), BF16)
    return pl.pallas_call(
        body, name="qk_norm_fwd", grid=(s // tr, nh),
        in_specs=[col(0), col(nh), col(2 * nh), gs, gs], out_specs=[o, o, o], out_shape=[sh, sh, sh],
        compiler_params=_cparams("parallel", "parallel"),
    )(proj, proj, proj, g_q, g_k)


def _qk_norm_bwd(proj, g_q, g_k, dqn, dkn, nh):
    s = proj.shape[0]
    tr = _pick(s, (512, 256, 128, 64, 32, 16, 8))

    def one(x, g, dy):
        r = lax.rsqrt(jnp.mean(x * x, axis=-1, keepdims=True) + EPS)
        n = x * r
        a = dy * g
        dx = r * (a - n * jnp.mean(a * n, axis=-1, keepdims=True))
        return dx, jnp.sum((dy * n).reshape(tr // SUB, SUB, HEAD), axis=0)

    def body(q_ref, k_ref, gq_ref, gk_ref, dqn_ref, dkn_ref, dq_ref, dk_ref, dgq_ref, dgk_ref):
        first = (pl.program_id(0) == 0) & (pl.program_id(1) == 0)

        @pl.when(first)
        def _():
            dgq_ref[...] = jnp.zeros_like(dgq_ref)
            dgk_ref[...] = jnp.zeros_like(dgk_ref)

        dq, pq = one(q_ref[...], gq_ref[...], dqn_ref[...])
        dk, pk = one(k_ref[...], gk_ref[...], dkn_ref[...])
        dq_ref[...] = dq.astype(BF16)
        dk_ref[...] = dk.astype(BF16)
        dgq_ref[...] += pq
        dgk_ref[...] += pk

    col = lambda base: pl.BlockSpec((tr, HEAD), lambda i, h: (i, base + h))
    gs = pl.BlockSpec((1, HEAD), lambda i, h: (0, 0))
    o = pl.BlockSpec((tr, HEAD), lambda i, h: (i, h))
    part = pl.BlockSpec((SUB, HEAD), lambda i, h: (0, 0))
    sh = jax.ShapeDtypeStruct((s, nh * HEAD), BF16)
    psh = jax.ShapeDtypeStruct((SUB, HEAD), F32)
    return pl.pallas_call(
        body, name="qk_norm_bwd", grid=(s // tr, nh),
        in_specs=[col(0), col(nh), gs, gs, o, o], out_specs=[o, o, part, part], out_shape=[sh, sh, psh, psh],
        compiler_params=_cparams("arbitrary", "arbitrary"),
    )(proj, proj, g_q, g_k, dqn, dkn)


def _sb_consts(blk, hp):
    upper = _tri(blk, "ge")
    row = lax.broadcasted_iota(jnp.int32, (blk, hp * blk), 0)
    col = lax.broadcasted_iota(jnp.int32, (blk, hp * blk), 1)
    strict = row < col
    for h in range(1, hp):
        strict = strict & ((col < h * blk) | (row < col - h * blk))
    return jnp.concatenate([upper, upper], axis=1), strict


def _sb_log_keep(zt, strict):
    l = jnp.minimum(-zt, 0.0) - jnp.log(1.0 + jnp.exp(-jnp.abs(zt)))
    return l if strict is None else jnp.where(strict, l, 0.0)


def _sb_heads(nh):
    return 2 if nh % 2 == 0 else 1


def _sb_attn_fwd(qn, kn, vb, nh, blk):
    s = qn.shape[0]
    nb = s // blk
    scale = HEAD ** -0.5
    hp = _sb_heads(nh)

    def body(q_ref, k_ref, v_ref, y_ref, c_ref):
        qi = pl.program_id(1)
        suffix, strict = _sb_consts(blk, hp)
        qs = [q_ref[:, h * HEAD:(h + 1) * HEAD] for h in range(hp)]

        def logits(kb_i):
            off = pl.multiple_of(kb_i * blk, blk)
            return jnp.concatenate(
                [_dot(k_ref[pl.ds(off, blk), h * HEAD:(h + 1) * HEAD], qs[h], NT) for h in range(hp)], axis=1) * scale

        def tile(kb_i, zt, accs, cr, mask):
            off = pl.multiple_of(kb_i * blk, blk)
            l = _sb_log_keep(zt, mask)
            for h in range(hp):
                c_ref[h, kb_i] = cr[:, h * blk:(h + 1) * blk]
            hi, lo = _split(l)
            cum = _dot(suffix, jnp.concatenate([hi, lo], axis=0), NN)
            wt = jnp.exp(zt + cum + cr)
            if mask is not None:
                wt = jnp.where(mask, wt, 0.0)
            wt = wt.astype(BF16)
            accs = tuple(
                accs[h] + _dot(wt[:, h * blk:(h + 1) * blk], v_ref[pl.ds(off, blk), h * HEAD:(h + 1) * HEAD], TN)
                for h in range(hp))
            return accs, cr + cum[0:1, :]

        zeros = tuple(jnp.zeros((blk, HEAD), F32) for _ in range(hp))
        accs, cr = tile(qi, logits(qi), zeros, jnp.zeros((1, hp * blk), F32), strict)

        def step(i, st):
            accs, cr, zt = st
            ahead = logits(jnp.maximum(qi - i - 1, 0))
            accs, cr = tile(qi - i, zt, accs, cr, None)
            return accs, cr, ahead

        accs, _, _ = lax.fori_loop(1, qi + 1, step, (accs, cr, logits(jnp.maximum(qi - 1, 0))))
        for h in range(hp):
            y_ref[:, h * HEAD:(h + 1) * HEAD] = accs[h].astype(y_ref.dtype)

    qs_ = pl.BlockSpec((blk, hp * HEAD), lambda h, i: (i, h))
    full = pl.BlockSpec((s, hp * HEAD), lambda h, i: (0, h))
    return pl.pallas_call(
        body, name="sb_attn_fwd", grid=(nh // hp, nb),
        in_specs=[qs_, full, full],
        out_specs=[qs_, pl.BlockSpec((hp, nb, 1, blk), lambda h, i: (h, 0, 0, i))],
        out_shape=[jax.ShapeDtypeStruct((s, nh * HEAD), BF16), jax.ShapeDtypeStruct((nh, nb, 1, s), F32)],
        compiler_params=_cparams("parallel", "arbitrary"),
    )(qn, kn, vb)


def _sb_attn_bwd(qn, kn, vb, dy, carries, nh, blk):
    s = qn.shape[0]
    nb = s // blk
    scale = HEAD ** -0.5
    hp = _sb_heads(nh)

    def body(q_ref, k_ref, v_ref, dy_ref, c_ref, dq_ref, dk_ref, dv_ref):
        qi = pl.program_id(1)

        @pl.when(qi == 0)
        def _():
            dk_ref[...] = jnp.zeros_like(dk_ref)
            dv_ref[...] = jnp.zeros_like(dv_ref)

        suffix, strict = _sb_consts(blk, hp)
        prefix = _tri(blk, "lt")
        qs = [q_ref[:, h * HEAD:(h + 1) * HEAD] for h in range(hp)]
        dos = [dy_ref[:, h * HEAD:(h + 1) * HEAD].astype(BF16) for h in range(hp)]

        def logits(kb_i):
            off = pl.multiple_of(kb_i * blk, blk)
            return jnp.concatenate(
                [_dot(k_ref[pl.ds(off, blk), h * HEAD:(h + 1) * HEAD], qs[h], NT) for h in range(hp)], axis=1) * scale

        def tile(kb_i, zt, dqs, ec, mask):
            off = pl.multiple_of(kb_i * blk, blk)
            rows = pl.ds(off, blk)
            l = _sb_log_keep(zt, mask)
            hi, lo = _split(l)
            cr = jnp.concatenate([c_ref[h, kb_i] for h in range(hp)], axis=1)
            wt = jnp.exp(zt + _dot(suffix, jnp.concatenate([hi, lo], axis=0), NN) + cr)
            if mask is not None:
                wt = jnp.where(mask, wt, 0.0)
            dw = jnp.concatenate([_dot(v_ref[rows, h * HEAD:(h + 1) * HEAD], dos[h], NT) for h in range(hp)], axis=1)
            et = wt * dw
            before = _dot(prefix, et.astype(BF16), NN) + ec
            dzt = ((jnp.exp(l) * (et + before) - before) * scale).astype(BF16)
            wtb = wt.astype(BF16)
            out = []
            for h in range(hp):
                cols, part = slice(h * HEAD, (h + 1) * HEAD), slice(h * blk, (h + 1) * blk)
                dv_ref[rows, cols] += _dot(wtb[:, part], dos[h], NN)
                dk_ref[rows, cols] += _dot(dzt[:, part], qs[h], NN)
                out.append(dqs[h] + _dot(dzt[:, part], k_ref[rows, cols], TN))
            return tuple(out), ec + jnp.sum(et, axis=0, keepdims=True)

        def step(kb_i, st):
            dqs, ec, zt = st
            ahead = logits(kb_i + 1)
            dqs, ec = tile(kb_i, zt, dqs, ec, None)
            return dqs, ec, ahead

        zeros = tuple(jnp.zeros((blk, HEAD), F32) for _ in range(hp))
        dqs, ec, zt = lax.fori_loop(0, qi, step, (zeros, jnp.zeros((1, hp * blk), F32), logits(0)))
        dqs, _ = tile(qi, zt, dqs, ec, strict)
        for h in range(hp):
            dq_ref[:, h * HEAD:(h + 1) * HEAD] = dqs[h]

    qs_ = pl.BlockSpec((blk, hp * HEAD), lambda h, i: (i, h))
    full = pl.BlockSpec((s, hp * HEAD), lambda h, i: (0, h))
    sh = jax.ShapeDtypeStruct((s, nh * HEAD), F32)
    return pl.pallas_call(
        body, name="sb_attn_bwd", grid=(nh // hp, nb),
        in_specs=[qs_, full, full, qs_, pl.BlockSpec((hp, nb, 1, blk), lambda h, i: (h, 0, 0, i))],
        out_specs=[qs_, full, full], out_shape=[sh, sh, sh],
        compiler_params=_cparams("parallel", "arbitrary"),
    )(qn, kn, vb, dy, carries)


def _lower_bound(lb_logits):
    def body(l_ref, o_ref):
        l = l_ref[...]
        m = jnp.max(l, axis=0, keepdims=True)
        e = jnp.exp(l - m)
        o_ref[...] = e[0:1, :] / jnp.sum(e, axis=0, keepdims=True)

    return pl.pallas_call(body, name="lower_bound", out_shape=jax.ShapeDtypeStruct((1, lb_logits.shape[1]), F32))(lb_logits)


def _hg_gates(hq, hf, lb):
    sq = _sig(hq)
    q = hq * sq
    sf = _sig(hf)
    f = lb + (1.0 - lb) * sf
    return q, sq, f, sf


def _hg_cum(g, c):
    hi, lo = _split(g)
    t = _tri(c, "le")
    return _dot(t, hi, NN) + _dot(t, lo, NN)


def _row_mask(r):
    return lax.broadcasted_iota(jnp.int32, (SUB, HEAD), 0) >= r


def _hg_intra_fwd(q, k, v, b, c):
    outs = []
    for bi in range(c // SUB):
        q_i, b_i = q[bi * SUB:(bi + 1) * SUB], b[bi * SUB:(bi + 1) * SUB]
        acc = jnp.zeros((SUB, HEAD), F32)
        for s in range((bi + 1) * SUB):
            d = b_i - b[s:s + 1]
            if s >= bi * SUB:
                d = jnp.where(_row_mask(s - bi * SUB), d, -jnp.inf)
            col = jnp.sum(q_i * k[s:s + 1] * jnp.exp(d), axis=1, keepdims=True)
            acc = acc + col * v[s:s + 1]
        outs.append(acc)
    return jnp.concatenate(outs, axis=0)


def _hg_intra_bwd(q, k, v, b, do, c, dq_scr, dk_scr, dv_scr):
    nblk = c // SUB
    dq_scr[...] = jnp.zeros_like(dq_scr)
    for s in range(c):
        bj = s // SUB
        ks, vs, bs = k[s:s + 1], v[s:s + 1], b[s:s + 1]
        acc_k = jnp.zeros((SUB, HEAD), F32)
        acc_v = jnp.zeros((SUB, HEAD), F32)
        for bi in range(bj, nblk):
            sl = slice(bi * SUB, (bi + 1) * SUB)
            d = b[sl] - bs
            if bi == bj:
                d = jnp.where(_row_mask(s - bj * SUB), d, -jnp.inf)
            dec = jnp.exp(d)
            qd = q[sl] * dec
            col = jnp.sum(qd * ks, axis=1, keepdims=True)
            dcol = jnp.sum(do[sl] * vs, axis=1, keepdims=True)
            dq_scr[sl, :] += dcol * (ks * dec)
            acc_k = acc_k + dcol * qd
            acc_v = acc_v + col * do[sl]
        dk_scr[s:s + 1, :] = jnp.sum(acc_k, axis=0, keepdims=True)
        dv_scr[s:s + 1, :] = jnp.sum(acc_v, axis=0, keepdims=True)


def _hgrn2_fwd(proj, lb, g_out, nh, base, c):
    s = proj.shape[0]
    nch = s // c

    def body(hq_ref, hf_ref, hi_ref, og_ref, lb_ref, g_ref, o_ref, y_ref, st_ref, st):
        @pl.when(pl.program_id(1) == 0)
        def _():
            st[...] = jnp.zeros_like(st)

        st_in = st[...]
        st_ref[...] = st_in
        q, _, f, _ = _hg_gates(hq_ref[...], hf_ref[...], lb_ref[...])
        k = 1.0 - f
        v = hi_ref[...]
        b = _hg_cum(jnp.log(f), c)
        bl = b[c - 1:c, :]
        o = _dot((q * jnp.exp(b)).astype(BF16), st_in.astype(BF16), NT) + _hg_intra_fwd(q, k, v, b, c)
        kd = k * jnp.exp(bl - b)
        st[...] = st_in * jnp.exp(bl) + _dot(v.astype(BF16), kd.astype(BF16), TN)
        o_ref[...] = o
        og = og_ref[...]
        y_ref[...] = (_head_norm(o, g_ref[...]) * (og * _sig(og))).astype(BF16)

    col = lambda j: pl.BlockSpec((c, HEAD), lambda h, i: (i, base + j * nh + h))
    row = pl.BlockSpec((1, HEAD), lambda h, i: (0, h))
    gs = pl.BlockSpec((1, HEAD), lambda h, i: (0, 0))
    o = pl.BlockSpec((c, HEAD), lambda h, i: (i, h))
    return pl.pallas_call(
        body, name="hgrn2_fwd", grid=(nh, nch),
        in_specs=[col(0), col(1), col(2), col(3), row, gs],
        out_specs=[o, o, pl.BlockSpec((None, None, HEAD, HEAD), lambda h, i: (h, i, 0, 0))],
        out_shape=[jax.ShapeDtypeStruct((s, nh * HEAD), F32), jax.ShapeDtypeStruct((s, nh * HEAD), BF16),
                   jax.ShapeDtypeStruct((nh, nch, HEAD, HEAD), F32)],
        scratch_shapes=[pltpu.VMEM((HEAD, HEAD), F32)],
        compiler_params=_cparams("parallel", "arbitrary"),
    )(proj, proj, proj, proj, lb, g_out)


def _hgrn2_bwd(proj, lb, g_out, o_pre, states, dy, nh, base, c):
    s = proj.shape[0]
    nch = s // c

    def body(hq_ref, hf_ref, hi_ref, og_ref, lb_ref, g_ref, o_ref, st_ref, se_ref, dy_ref,
             dhq_ref, dhf_ref, dhi_ref, dog_ref, dg_ref, dlb_ref, dst, dq_scr, dk_scr, dv_scr):
        h, i = pl.program_id(0), pl.program_id(1)

        @pl.when(i == 0)
        def _():
            dst[...] = jnp.zeros_like(dst)
            dlb_ref[...] = jnp.zeros_like(dlb_ref)

        @pl.when((i == 0) & (h == 0))
        def _():
            dg_ref[...] = jnp.zeros_like(dg_ref)

        lbv = lb_ref[...]
        hq, hf = hq_ref[...], hf_ref[...]
        q, sq, f, sf = _hg_gates(hq, hf, lbv)
        k = 1.0 - f
        v = hi_ref[...]
        b = _hg_cum(jnp.log(f), c)
        bl = b[c - 1:c, :]
        eb = jnp.exp(b)
        ebl = jnp.exp(bl - b)

        o = o_ref[...]
        gout = g_ref[...]
        og = og_ref[...]
        sg = _sig(og)
        r = lax.rsqrt(jnp.mean(o * o, axis=-1, keepdims=True) + EPS)
        n = o * r
        dyv = dy_ref[...]
        dn = dyv * (og * sg)
        dog_ref[...] = (dyv * n * gout * (sg * (1.0 + og * (1.0 - sg)))).astype(BF16)
        dg_ref[...] += jnp.sum((dn * n).reshape(c // SUB, SUB, HEAD), axis=0)
        a = dn * gout
        do = r * (a - n * jnp.mean(a * n, axis=-1, keepdims=True))

        st_in = st_ref[...]
        dstv = dst[...]
        dob = do.astype(BF16)
        dstb = dstv.astype(BF16)
        _hg_intra_bwd(q, k, v, b, do, c, dq_scr, dk_scr, dv_scr)
        dq = dq_scr[...] + eb * _dot(dob, st_in.astype(BF16), NN)
        dk = dk_scr[...] + ebl * _dot(v.astype(BF16), dstb, NN)
        dv = dv_scr[...] + _dot((k * ebl).astype(BF16), dstb, NT)
        dst[...] = dstv * jnp.exp(bl) + _dot(dob, (q * eb).astype(BF16), TN)

        hi_, lo_ = _split(q * dq - k * dk)
        rev = _tri(c, "ge")
        later = jnp.where(i > 0, jnp.sum(dstv * se_ref[...], axis=0, keepdims=True), 0.0)
        dg = _dot(rev, hi_, NN) + _dot(rev, lo_, NN) + later
        df = dg / f - dk
        dhq_ref[...] = (dq * (sq * (1.0 + hq * (1.0 - sq)))).astype(BF16)
        dhf_ref[...] = (df * (1.0 - lbv) * sf * (1.0 - sf)).astype(BF16)
        dhi_ref[...] = dv.astype(BF16)
        dlb_ref[...] += jnp.sum((df * (1.0 - sf)).reshape(c // SUB, SUB, HEAD), axis=0)

    rv = lambda i: nch - 1 - i
    col = lambda j: pl.BlockSpec((c, HEAD), lambda h, i: (rv(i), base + j * nh + h))
    row = pl.BlockSpec((1, HEAD), lambda h, i: (0, h))
    gs = pl.BlockSpec((1, HEAD), lambda h, i: (0, 0))
    o = pl.BlockSpec((c, HEAD), lambda h, i: (rv(i), h))
    st = pl.BlockSpec((None, None, HEAD, HEAD), lambda h, i: (h, rv(i), 0, 0))
    se = pl.BlockSpec((None, None, HEAD, HEAD), lambda h, i: (h, jnp.minimum(rv(i) + 1, nch - 1), 0, 0))
    sh = jax.ShapeDtypeStruct((s, nh * HEAD), BF16)
    return pl.pallas_call(
        body, name="hgrn2_bwd", grid=(nh, nch),
        in_specs=[col(0), col(1), col(2), col(3), row, gs, o, st, se, o],
        out_specs=[o, o, o, o, pl.BlockSpec((SUB, HEAD), lambda h, i: (0, 0)), pl.BlockSpec((SUB, HEAD), lambda h, i: (0, h))],
        out_shape=[sh, sh, sh, sh, jax.ShapeDtypeStruct((SUB, HEAD), F32), jax.ShapeDtypeStruct((SUB, nh * HEAD), F32)],
        scratch_shapes=[pltpu.VMEM((HEAD, HEAD), F32), pltpu.VMEM((c, HEAD), F32), pltpu.VMEM((c, HEAD), F32),
                        pltpu.VMEM((c, HEAD), F32)],
        compiler_params=_cparams("arbitrary", "arbitrary"),
    )(proj, proj, proj, proj, lb, g_out, o_pre, states, states, dy)


def _merge_tiles(s, d, gate_col):
    tr = _pick(s, (256, 128, 64, 32, 16, 8))
    tc = 128
    for cand in (512, 256):
        if d % cand == 0 and gate_col % cand == 0:
            tc = cand
            break
    return tr, tc


def _merge_fwd(proj, ya, yb, gate_col):
    s, d = ya.shape
    tr, tc = _merge_tiles(s, d, gate_col)
    ga0, gb0 = gate_col // tc, (gate_col + d) // tc

    def body(ga_ref, gb_ref, ya_ref, yb_ref, m_ref):
        m_ref[...] = (_sig(ga_ref[...]) * ya_ref[...] + _sig(gb_ref[...]) * yb_ref[...]).astype(BF16)

    o = pl.BlockSpec((tr, tc), lambda i, j: (i, j))
    return pl.pallas_call(
        body, name="merge_fwd", grid=(s // tr, d // tc),
        in_specs=[pl.BlockSpec((tr, tc), lambda i, j: (i, ga0 + j)), pl.BlockSpec((tr, tc), lambda i, j: (i, gb0 + j)), o, o],
        out_specs=o, out_shape=jax.ShapeDtypeStruct((s, d), BF16),
        compiler_params=_cparams("parallel", "parallel"),
    )(proj, proj, ya, yb)


def _merge_bwd(proj, ya, yb, dm, gate_col):
    s, d = ya.shape
    tr, tc = _merge_tiles(s, d, gate_col)
    ga0, gb0 = gate_col // tc, (gate_col + d) // tc

    def body(ga_ref, gb_ref, ya_ref, yb_ref, dm_ref, dya_ref, dyb_ref, dga_ref, dgb_ref):
        dmv = dm_ref[...]
        sa, sb = _sig(ga_ref[...]), _sig(gb_ref[...])
        dya_ref[...] = (dmv * sa).astype(BF16)
        dyb_ref[...] = (dmv * sb).astype(BF16)
        dga_ref[...] = (dmv * ya_ref[...] * sa * (1.0 - sa)).astype(BF16)
        dgb_ref[...] = (dmv * yb_ref[...] * sb * (1.0 - sb)).astype(BF16)

    o = pl.BlockSpec((tr, tc), lambda i, j: (i, j))
    sh = jax.ShapeDtypeStruct((s, d), BF16)
    return pl.pallas_call(
        body, name="merge_bwd", grid=(s // tr, d // tc),
        in_specs=[pl.BlockSpec((tr, tc), lambda i, j: (i, ga0 + j)), pl.BlockSpec((tr, tc), lambda i, j: (i, gb0 + j)), o, o, o],
        out_specs=[o, o, o, o], out_shape=[sh, sh, sh, sh],
        compiler_params=_cparams("parallel", "parallel"),
    )(proj, proj, ya, yb, dm)


CONV_ROWS = 512


def _conv_ext(ref, i, rows, s, before, after):
    parts = []
    if before:
        p = ref[pl.ds(pl.multiple_of(jnp.maximum(i * rows - before, 0), SUB), before), :]
        parts.append(jnp.where(i > 0, p, 0.0))
    parts.append(ref[pl.ds(pl.multiple_of(i * rows, SUB), rows), :])
    if after:
        nxt = ref[pl.ds(pl.multiple_of(jnp.minimum((i + 1) * rows, s - after), SUB), after), :]
        parts.append(jnp.where((i + 1) * rows < s, nxt, 0.0))
    return jnp.concatenate(parts, axis=0)


def _conv3(ext, w, bias):
    x1 = pltpu.roll(ext, 1, 0)
    x2 = pltpu.roll(ext, 2, 0)
    return bias + w[0:1, :] * x2 + w[1:2, :] * x1 + w[2:3, :] * ext, x1, x2


def _convffn_fwd(up, conv_w, conv_b, dff):
    s = up.shape[0]
    tc = HEAD
    nf = dff // tc
    rows = _pick(s, (CONV_ROWS, 256, 128, 64, 32, 16, 8))

    def body(ug_ref, uv_ref, wg_ref, wv_ref, bg_ref, bv_ref, a_ref):
        wg, wv, bg, bv = wg_ref[...], wv_ref[...], bg_ref[...], bv_ref[...]

        def step(i, _):
            g = _conv3(_conv_ext(ug_ref, i, rows, s, SUB, 0), wg, bg)[0][SUB:]
            v = _conv3(_conv_ext(uv_ref, i, rows, s, SUB, 0), wv, bv)[0][SUB:]
            a_ref[pl.ds(pl.multiple_of(i * rows, SUB), rows), :] = (g * _sig(g) * v).astype(BF16)
            return 0

        lax.fori_loop(0, s // rows, step, 0)

    cg = lambda r: pl.BlockSpec((r, tc), lambda j: (0, j))
    cv = lambda r: pl.BlockSpec((r, tc), lambda j: (0, nf + j))
    return pl.pallas_call(
        body, name="convffn_fwd", grid=(nf,),
        in_specs=[cg(s), cv(s), cg(3), cv(3), cg(1), cv(1)], out_specs=cg(s),
        out_shape=jax.ShapeDtypeStruct((s, dff), BF16),
        compiler_params=_cparams("parallel"),
    )(up, up, conv_w, conv_w, conv_b, conv_b)


def _convffn_bwd(up, conv_w, conv_b, dact, dff):
    s = up.shape[0]
    tc = HEAD
    nf = dff // tc
    rows = _pick(s, (CONV_ROWS, 256, 128, 64, 32, 16, 8))
    n_ext = rows + SUB

    def body(ug_ref, uv_ref, wg_ref, wv_ref, bg_ref, bv_ref, da_ref,
             dug_ref, duv_ref, dwg_ref, dwv_ref, dbg_ref, dbv_ref):
        wg, wv, bg, bv = wg_ref[...], wv_ref[...], bg_ref[...], bv_ref[...]

        def fold(x):
            return jnp.sum(x.reshape(rows // SUB, SUB, tc), axis=0)

        def one(ext, x1, x2, d_ext, w):
            d1 = pltpu.roll(d_ext, n_ext - 1, 0)[:rows]
            d2 = pltpu.roll(d_ext, n_ext - 2, 0)[:rows]
            dc = d_ext[:rows]
            du = w[2:3, :] * dc + w[1:2, :] * d1 + w[0:1, :] * d2
            sl = slice(SUB, SUB + rows)
            return du, (fold(dc * x2[sl]), fold(dc * x1[sl]), fold(dc * ext[sl]), fold(dc))

        def step(i, acc):
            eg = _conv_ext(ug_ref, i, rows, s, SUB, SUB)
            ev = _conv_ext(uv_ref, i, rows, s, SUB, SUB)
            g, g1, g2 = _conv3(eg, wg, bg)
            v, v1, v2 = _conv3(ev, wv, bv)
            g, v = g[SUB:], v[SUB:]
            da = _conv_ext(da_ref, i, rows, s, 0, SUB)
            sg = _sig(g)
            dg = da * v * (sg * (1.0 + g * (1.0 - sg)))
            dv = da * (g * sg)
            dug, pg = one(eg, g1, g2, dg, wg)
            duv, pv = one(ev, v1, v2, dv, wv)
            at = pl.ds(pl.multiple_of(i * rows, SUB), rows)
            dug_ref[at, :] = dug.astype(BF16)
            duv_ref[at, :] = duv.astype(BF16)
            return tuple(a + p for a, p in zip(acc, pg + pv))

        zero = jnp.zeros((SUB, tc), F32)
        acc = lax.fori_loop(0, s // rows, step, (zero,) * 8)
        red = [jnp.sum(a, axis=0, keepdims=True) for a in acc]
        for j in range(3):
            dwg_ref[j:j + 1, :] = red[j]
            dwv_ref[j:j + 1, :] = red[4 + j]
        dbg_ref[...] = red[3]
        dbv_ref[...] = red[7]

    cg = lambda r: pl.BlockSpec((r, tc), lambda j: (0, j))
    cv = lambda r: pl.BlockSpec((r, tc), lambda j: (0, nf + j))
    outs = pl.pallas_call(
        body, name="convffn_bwd", grid=(nf,),
        in_specs=[cg(s), cv(s), cg(3), cv(3), cg(1), cv(1), cg(s)],
        out_specs=[cg(s), cg(s), cg(3), cg(3), cg(1), cg(1)],
        out_shape=[jax.ShapeDtypeStruct((s, dff), BF16), jax.ShapeDtypeStruct((s, dff), BF16),
                   jax.ShapeDtypeStruct((3, dff), F32), jax.ShapeDtypeStruct((3, dff), F32),
                   jax.ShapeDtypeStruct((1, dff), F32), jax.ShapeDtypeStruct((1, dff), F32)],
        compiler_params=_cparams("parallel"),
    )(up, up, conv_w, conv_w, conv_b, conv_b, dact)
    return outs


def _loss_head(out, target):
    s, d = out.shape
    tr = _pick(s, (256, 128, 64, 32, 16, 8))

    def body(o_ref, t_ref, d_ref, l_ref):
        @pl.when(pl.program_id(0) == 0)
        def _():
            l_ref[...] = jnp.zeros_like(l_ref)

        err = o_ref[...] - t_ref[...]
        d_ref[...] = err * (1.0 / d)
        sq = jnp.sum((err * err).reshape(tr // SUB, SUB, d), axis=0)
        part = sq[:, 0:HEAD]
        for j in range(1, d // HEAD):
            part = part + sq[:, j * HEAD:(j + 1) * HEAD]
        l_ref[...] += part

    blk = pl.BlockSpec((tr, d), lambda i: (i, 0))
    return pl.pallas_call(
        body, name="loss_head", grid=(s // tr,), in_specs=[blk, blk],
        out_specs=[blk, pl.BlockSpec((SUB, HEAD), lambda i: (0, 0))],
        out_shape=[jax.ShapeDtypeStruct((s, d), F32), jax.ShapeDtypeStruct((SUB, HEAD), F32)],
        compiler_params=_cparams("arbitrary"),
    )(out, target)


def _sum_rows(name, parts):
    def body(p_ref, o_ref):
        o_ref[...] = jnp.sum(p_ref[...], axis=0, keepdims=True)

    return pl.pallas_call(body, name=name, out_shape=jax.ShapeDtypeStruct((1, parts.shape[1]), F32))(parts)


def _local_step(x, target, g_mix, g_q, g_k, lb_logits, g_hg_out, g_ffn, conv_w, conv_b, w_in, p_a, p_b, w_o, w_up, w_down):
    s, d = x.shape
    nh = p_a.shape[1] // HEAD
    wid = nh * HEAD
    dff = w_down.shape[1]
    blk = _pick(s, (256, 128))
    chunk = _pick(s, (HG_CHUNK,))
    gate_col = 7 * wid

    u = _rmsnorm_fwd("rmsnorm_mix", x, g_mix)
    proj = _matmul("in_proj", u, w_in, "nn", F32)
    qn, kn, vb = _qk_norm_fwd(proj, g_q, g_k, nh)
    y_a, carries = _sb_attn_fwd(qn, kn, vb, nh, blk)
    lb = _lower_bound(lb_logits)
    o_pre, y_b, states = _hgrn2_fwd(proj, lb, g_hg_out, nh, 3 * nh, chunk)
    ya_p = _matmul("proj_a", y_a, p_a, "nn", F32)
    yb_p = _matmul("proj_b", y_b, p_b, "nn", F32)
    m = _merge_fwd(proj, ya_p, yb_p, gate_col)
    h = _matmul("out_proj", m, w_o, "nn", F32, add=x)
    u2 = _rmsnorm_fwd("rmsnorm_ffn", h, g_ffn)
    up = _matmul("up_proj", u2, w_up, "nn", F32)
    act = _convffn_fwd(up, conv_w, conv_b, dff)
    out = _matmul("down_proj", act, w_down, "nn", F32, add=h)
    dout, sq = _loss_head(out, target)

    dact = _matmul("d_act", dout, w_down, "nt", F32)
    g_w_down = _matmul_tn("g_w_down", act, dout, 1, BF16)
    dup_g, dup_v, dcw_g, dcw_v, dcb_g, dcb_v = _convffn_bwd(up, conv_w, conv_b, dact, dff)
    dup = jnp.concatenate([dup_g, dup_v], axis=1)
    du2 = _matmul("d_u2", dup, w_up, "nt", F32)
    g_w_up = _matmul_tn("g_w_up", u2, dup, N_CHIPS, BF16)
    dh, pg_ffn = _rmsnorm_bwd("rmsnorm_ffn_bwd", h, g_ffn, du2, dout)
    dm = _matmul("d_m", dh, w_o, "nt", F32)
    g_w_o = _matmul_tn("g_w_o", m, dh, 1, BF16)
    dya_p, dyb_p, dga, dgb = _merge_bwd(proj, ya_p, yb_p, dm, gate_col)
    dy_a = _matmul("d_y_a", dya_p, p_a, "nt", F32)
    g_p_a = _matmul_tn("g_p_a", y_a, dya_p, N_CHIPS, BF16)
    dy_b = _matmul("d_y_b", dyb_p, p_b, "nt", F32)
    g_p_b = _matmul_tn("g_p_b", y_b, dyb_p, N_CHIPS, BF16)
    dhq, dhf, dhi, dog, pg_hg, p_lb = _hgrn2_bwd(proj, lb, g_hg_out, o_pre, states, dy_b, nh, 3 * nh, chunk)
    dqn, dkn, dv = _sb_attn_bwd(qn, kn, vb, dy_a, carries, nh, blk)
    dq, dk, pg_q, pg_k = _qk_norm_bwd(proj, g_q, g_k, dqn, dkn, nh)
    dproj = jnp.concatenate([dq, dk, dv.astype(BF16), dhq, dhf, dhi, dog, dga, dgb], axis=1)
    du = _matmul("d_u", dproj, w_in, "nt", F32)
    g_w_in = _matmul_tn("g_w_in", u, dproj, N_CHIPS, BF16)
    dx, pg_mix = _rmsnorm_bwd("rmsnorm_mix_bwd", x, g_mix, du, dh)

    small = dict(
        g_mix=_sum_rows("sum_g_mix", pg_mix), g_q=_sum_rows("sum_g_q", pg_q), g_k=_sum_rows("sum_g_k", pg_k),
        lb=_sum_rows("sum_lb", p_lb), g_hg_out=_sum_rows("sum_g_hg", pg_hg), g_ffn=_sum_rows("sum_g_ffn", pg_ffn),
        conv_w=jnp.concatenate([dcw_g, dcw_v], axis=1), conv_b=jnp.concatenate([dcb_g, dcb_v], axis=1),
        sq=_sum_rows("sum_sq", sq),
    )
    big = dict(w_in=g_w_in, p_a=g_p_a, p_b=g_p_b, w_o=g_w_o.reshape(N_CHIPS, d // N_CHIPS, d), w_up=g_w_up,
               w_down=g_w_down.reshape(N_CHIPS, dff // N_CHIPS, d))
    return dx, big, small, lb


ANY = pl.BlockSpec(memory_space=pl.ANY)


def _place():
    x, y, c = lax.axis_index("x"), lax.axis_index("y"), lax.axis_index("c")
    chips = [(1 - x, y), (x, 1 - y), (1 - x, 1 - y)]
    return x, y, c, chips


def _remote(src, dst, send_sem, recv_sem, to):
    return pltpu.make_async_remote_copy(src_ref=src, dst_ref=dst, send_sem=send_sem, recv_sem=recv_sem,
                                        device_id=to, device_id_type=MESH)


def _cast_bf16(name, w):
    r, c = w.shape
    tr = _pick(r, (256, 128, 64, 32, 16))

    def body(w_ref, o_ref):
        o_ref[...] = w_ref[...].astype(BF16)

    return pl.pallas_call(
        body, name=name, grid=(r // tr,), in_specs=[pl.BlockSpec((tr, c), lambda i: (i, 0))],
        out_specs=pl.BlockSpec((tr, c), lambda i: (i, 0)), out_shape=jax.ShapeDtypeStruct((r, c), BF16),
        compiler_params=_cparams("parallel"),
    )(w)


def _gather_weights(shards):
    n = len(shards)

    def body(*refs):
        ins, outs = refs[:n], refs[n:2 * n]
        send, recv, local = refs[2 * n:]
        x, y, c, chips = _place()
        mine = 2 * x + y
        sends, owns = [], []
        for k in range(n):
            half = ins[k].shape[0] // 2
            rows = pl.ds(c * half, half)
            own = pltpu.make_async_copy(ins[k], outs[k].at[mine], local.at[k])
            own.start()
            owns.append(own)
            for j, (px, py) in enumerate(chips):
                cp = _remote(ins[k].at[rows], outs[k].at[mine, rows], send.at[k, j], recv.at[k, j], (px, py, c))
                cp.start()
                sends.append(cp)
        for k in range(n):
            half = ins[k].shape[0] // 2
            rows = pl.ds(c * half, half)
            for j, (px, py) in enumerate(chips):
                part = outs[k].at[2 * px + py, rows]
                _remote(part, part, send.at[k, j], recv.at[k, j], (px, py, c)).wait_recv()
                fw = _remote(part, part, send.at[k, 3 + j], recv.at[k, 3 + j], (x, y, 1 - c))
                fw.start()
                sends.append(fw)
        for k in range(n):
            half = ins[k].shape[0] // 2
            other = pl.ds((1 - c) * half, half)
            for j, (px, py) in enumerate(chips):
                part = outs[k].at[2 * px + py, other]
                _remote(part, part, send.at[k, 3 + j], recv.at[k, 3 + j], (x, y, 1 - c)).wait_recv()
        for cp in sends:
            cp.wait_send()
        for cp in owns:
            cp.wait()

    return pl.pallas_call(
        body, name="gather_weights", in_specs=[ANY] * n, out_specs=[ANY] * n,
        out_shape=[jax.ShapeDtypeStruct((N_CHIPS,) + w.shape, w.dtype) for w in shards],
        scratch_shapes=[pltpu.SemaphoreType.DMA((n, 6)), pltpu.SemaphoreType.DMA((n, 6)), pltpu.SemaphoreType.DMA((n,))],
    )(*shards)


def _exchange_halves(name, srcs, src_half_other, out_shapes):
    n = len(srcs)

    def body(*refs):
        ins, outs = refs[:n], refs[n:2 * n]
        send, recv = refs[2 * n:]
        x, y, c, _ = _place()
        cps = []
        for k in range(n):
            if src_half_other:
                half = ins[k].shape[1] // 2
                src = ins[k].at[:, pl.ds((1 - c) * half, half)]
            else:
                src = ins[k]
            cp = _remote(src, outs[k], send.at[k], recv.at[k], (x, y, 1 - c))
            cp.start()
            cps.append(cp)
        for cp in cps:
            cp.wait_recv()
        for cp in cps:
            cp.wait_send()

    return pl.pallas_call(
        body, name=name, in_specs=[ANY] * n, out_specs=[ANY] * n, out_shape=out_shapes,
        scratch_shapes=[pltpu.SemaphoreType.DMA((n,)), pltpu.SemaphoreType.DMA((n,))],
    )(*srcs)


def _add_halves(name, g, got, c):
    _, r, cols = g.shape
    half = r // 2
    tr = _pick(half, (256, 128, 64, 32, 16))
    nt = half // tr

    def body(c_ref, g_ref, o_ref, out_ref):
        out_ref[...] = (g_ref[...].astype(F32) + o_ref[...].astype(F32)).astype(BF16)

    return pl.pallas_call(
        body, name=name,
        grid_spec=pltpu.PrefetchScalarGridSpec(
            num_scalar_prefetch=1, grid=(N_CHIPS, nt),
            in_specs=[pl.BlockSpec((None, tr, cols), lambda s, i, cr: (s, cr[0] * nt + i, 0)),
                      pl.BlockSpec((None, tr, cols), lambda s, i, cr: (s, i, 0))],
            out_specs=pl.BlockSpec((None, tr, cols), lambda s, i, cr: (s, i, 0))),
        out_shape=jax.ShapeDtypeStruct((N_CHIPS, half, cols), BF16),
        compiler_params=_cparams("parallel", "parallel"),
    )(c, g, got)


def _scatter_partials(parts):
    n = len(parts)

    def body(*refs):
        ins, outs = refs[:n], refs[n:2 * n]
        send, recv = refs[2 * n:]
        x, y, c, chips = _place()
        cps = []
        for k in range(n):
            for j, (px, py) in enumerate(chips):
                cp = _remote(ins[k].at[2 * px + py], outs[k].at[j], send.at[k, j], recv.at[k, j], (px, py, c))
                cp.start()
                cps.append(cp)
        for cp in cps:
            cp.wait_recv()
        for cp in cps:
            cp.wait_send()

    return pl.pallas_call(
        body, name="scatter_partials", in_specs=[ANY] * n, out_specs=[ANY] * n,
        out_shape=[jax.ShapeDtypeStruct((3,) + p.shape[1:], p.dtype) for p in parts],
        scratch_shapes=[pltpu.SemaphoreType.DMA((n, 3)), pltpu.SemaphoreType.DMA((n, 3))],
    )(*parts)


def _sum_partials(name, part, got, shard):
    _, half, cols = part.shape
    tr = _pick(half, (256, 128, 64, 32, 16))

    def body(s_ref, p_ref, g_ref, o_ref):
        acc = p_ref[...].astype(F32)
        for j in range(3):
            acc = acc + g_ref[j].astype(F32)
        o_ref[...] = acc

    return pl.pallas_call(
        body, name=name,
        grid_spec=pltpu.PrefetchScalarGridSpec(
            num_scalar_prefetch=1, grid=(half // tr,),
            in_specs=[pl.BlockSpec((None, tr, cols), lambda i, sr: (sr[0], i, 0)),
                      pl.BlockSpec((3, tr, cols), lambda i, sr: (0, i, 0))],
            out_specs=pl.BlockSpec((tr, cols), lambda i, sr: (i, 0))),
        out_shape=jax.ShapeDtypeStruct((half, cols), F32),
        compiler_params=_cparams("parallel"),
    )(shard, part, got)


def _join_halves(mine, got, c):
    half, cols = mine.shape
    tr = _pick(half, (256, 128, 64, 32, 16, 8))
    nt = half // tr

    def body(c_ref, a_ref, b_ref, o_ref):
        i = pl.program_id(0)
        own = (i // nt) == c_ref[0]

        @pl.when(own)
        def _():
            o_ref[...] = a_ref[...]

        @pl.when(jnp.logical_not(own))
        def _():
            o_ref[...] = b_ref[...]

    blk = pl.BlockSpec((tr, cols), lambda i, cr: (i % nt, 0))
    return pl.pallas_call(
        body, name="join_halves",
        grid_spec=pltpu.PrefetchScalarGridSpec(num_scalar_prefetch=1, grid=(2 * nt,), in_specs=[blk, blk],
                                               out_specs=pl.BlockSpec((tr, cols), lambda i, cr: (i, 0))),
        out_shape=jax.ShapeDtypeStruct((2 * half, cols), F32),
        compiler_params=_cparams("parallel"),
    )(c, mine, got)


def _all_gather_rows(name, row):
    p = row.shape[1]

    def body(in_ref, out_ref, send, recv, local):
        x, y, c, _ = _place()
        me = 4 * x + 2 * y + c
        own = pltpu.make_async_copy(in_ref, out_ref.at[me], local)
        own.start()
        cps = []
        for k in range(1, 8):
            px, py, pc = x ^ (k >> 2), y ^ ((k >> 1) & 1), c ^ (k & 1)
            cp = _remote(in_ref, out_ref.at[me], send.at[k - 1], recv.at[k - 1], (px, py, pc))
            cp.start()
            cps.append(cp)
        for cp in cps:
            cp.wait_recv()
        for cp in cps:
            cp.wait_send()
        own.wait()

    return pl.pallas_call(
        body, name=name, in_specs=[ANY], out_specs=ANY,
        out_shape=jax.ShapeDtypeStruct((8, 1, p), F32),
        scratch_shapes=[pltpu.SemaphoreType.DMA((7,)), pltpu.SemaphoreType.DMA((7,)), pltpu.SemaphoreType.DMA],
    )(row)


def _sum_devices(rows):
    def body(r_ref, o_ref):
        acc = r_ref[0]
        for k in range(1, 8):
            acc = acc + r_ref[k]
        o_ref[...] = acc

    return pl.pallas_call(body, name="sum_devices", out_shape=jax.ShapeDtypeStruct(rows.shape[1:], F32))(rows)


def _adamw(name, w, g, m, v):
    r, c = w.shape
    tr = _pick(r, (128, 64, 32, 16, 8))
    bc1 = 1.0 - ADAM_B1 ** ADAM_STEP
    bc2 = 1.0 - ADAM_B2 ** ADAM_STEP

    def body(w_ref, g_ref, m_ref, v_ref, d_ref, nm_ref, nv_ref):
        gv = g_ref[...]
        nm = ADAM_B1 * m_ref[...] + (1.0 - ADAM_B1) * gv
        nv = ADAM_B2 * v_ref[...] + (1.0 - ADAM_B2) * (gv * gv)
        d_ref[...] = -ADAM_LR * ((nm / bc1) / (jnp.sqrt(nv / bc2) + ADAM_EPS) + ADAM_WD * w_ref[...])
        nm_ref[...] = nm
        nv_ref[...] = nv

    blk = pl.BlockSpec((tr, c), lambda i: (i, 0))
    sh = jax.ShapeDtypeStruct((r, c), F32)
    return pl.pallas_call(
        body, name=name, grid=(r // tr,), in_specs=[blk] * 4, out_specs=[blk] * 3, out_shape=[sh] * 3,
        compiler_params=_cparams("parallel"),
    )(w, g, m, v)


def _lb_logits_grad(dlb, lb):
    def body(d_ref, lb_ref, o_ref):
        lbv = lb_ref[...]
        t = d_ref[...] * lbv * (1.0 - lbv)
        o_ref[0:1, :] = t
        o_ref[1:2, :] = -t

    return pl.pallas_call(body, name="lb_logits_grad", out_shape=jax.ShapeDtypeStruct((2, dlb.shape[1]), F32))(dlb, lb)


BIG = ("w_in", "p_a", "p_b", "w_o", "w_up", "w_down")
SMALL = ("g_mix", "g_q", "g_k", "lb_logits", "g_hg_out", "g_ffn", "conv_w", "conv_b")
ORDER = ("g_mix", "w_in", "g_q", "g_k", "lb_logits", "g_hg_out", "p_a", "p_b", "w_o", "g_ffn", "w_up", "conv_w", "conv_b", "w_down")


def kernel(x, g_mix, w_in, g_q, g_k, lb_logits, g_hg_out, p_a, p_b, w_o, g_ffn, w_up, conv_w, conv_b, w_down, loss_target, m_g_mix, m_w_in, m_g_q, m_g_k, m_lb_logits, m_g_hg_out, m_p_a, m_p_b, m_w_o, m_g_ffn, m_w_up, m_conv_w, m_conv_b, m_w_down, v_g_mix, v_w_in, v_g_q, v_g_k, v_lb_logits, v_g_hg_out, v_p_a, v_p_b, v_w_o, v_g_ffn, v_w_up, v_conv_w, v_conv_b, v_w_down):
    assert lb_logits.shape[0] == 2, "the lower bound is the first row of a two-row softmax"
    w = dict(g_mix=g_mix, w_in=w_in[0], g_q=g_q, g_k=g_k, lb_logits=lb_logits, g_hg_out=g_hg_out, p_a=p_a[0], p_b=p_b[0],
             w_o=w_o[0], g_ffn=g_ffn, w_up=w_up[0], conv_w=conv_w[0], conv_b=conv_b, w_down=w_down[0])
    mom = dict(g_mix=m_g_mix, w_in=m_w_in[0], g_q=m_g_q, g_k=m_g_k, lb_logits=m_lb_logits, g_hg_out=m_g_hg_out, p_a=m_p_a[0],
               p_b=m_p_b[0], w_o=m_w_o[0], g_ffn=m_g_ffn, w_up=m_w_up[0], conv_w=m_conv_w[0], conv_b=m_conv_b, w_down=m_w_down[0])
    var = dict(g_mix=v_g_mix, w_in=v_w_in[0], g_q=v_g_q, g_k=v_g_k, lb_logits=v_lb_logits, g_hg_out=v_g_hg_out, p_a=v_p_a[0],
               p_b=v_p_b[0], w_o=v_w_o[0], g_ffn=v_g_ffn, w_up=v_w_up[0], conv_w=v_conv_w[0], conv_b=v_conv_b, w_down=v_w_down[0])
    d = x.shape[2]
    cx, cy, cc = lax.axis_index("x"), lax.axis_index("y"), lax.axis_index("c")
    shard = (2 * cx + cy).astype(jnp.int32).reshape(1)
    core = cc.astype(jnp.int32).reshape(1)

    full = dict(zip(BIG, _gather_weights([_cast_bf16("cast_" + n, w[n]) for n in BIG])))
    cw = conv_w.shape[2]
    rows = _all_gather_rows("gather_conv_w", w["conv_w"].reshape(1, 3 * cw))
    conv_full = jnp.concatenate([rows[2 * s, 0].reshape(3, cw) for s in range(N_CHIPS)], axis=1)
    f_w_o = full["w_o"].reshape(1, d, d)
    f_w_down = full["w_down"].reshape(1, -1, d)

    dx, big, small, lb = _local_step(x[0], loss_target[0], g_mix, g_q, g_k, lb_logits, g_hg_out, g_ffn, conv_full, conv_b,
                                     full["w_in"], full["p_a"], full["p_b"], f_w_o, full["w_up"], f_w_down)

    gs = [big[n] for n in BIG]
    got = _exchange_halves("halves_to_sibling", gs, True,
                           [jax.ShapeDtypeStruct((N_CHIPS, g.shape[1] // 2, g.shape[2]), BF16) for g in gs])
    parts = [_add_halves("chip_sum_" + n, g, o, core) for n, g, o in zip(BIG, gs, got)]
    recv = _scatter_partials(parts)
    mine = [_sum_partials("shard_sum_" + n, p, r, shard) for n, p, r in zip(BIG, parts, recv)]
    theirs = _exchange_halves("reduced_to_sibling", mine, False, [jax.ShapeDtypeStruct(a.shape, F32) for a in mine])
    grads = {n: _join_halves(a, b, core) for n, a, b in zip(BIG, mine, theirs)}

    names = ("g_mix", "g_q", "g_k", "lb", "g_hg_out", "g_ffn", "conv_b", "sq")
    packed = jnp.concatenate([small[n] for n in names] + [small["conv_w"].reshape(1, -1)], axis=1)
    total = _sum_devices(_all_gather_rows("gather_small_grads", packed))
    off = 0
    red = {}
    for n in names:
        ln = small[n].shape[1]
        red[n] = total[:, off:off + ln]
        off += ln
    conv_all = total[:, off:].reshape(3, -1)
    loss = 0.5 * jnp.sum(red["sq"]) / d
    grads["conv_w"] = lax.dynamic_slice_in_dim(conv_all, (2 * cx + cy) * cw, cw, axis=1)
    grads["lb_logits"] = _lb_logits_grad(red["lb"], lb)
    for n in ("g_mix", "g_q", "g_k", "g_hg_out", "g_ffn", "conv_b"):
        grads[n] = red[n]

    delta, new_m, new_v = {}, {}, {}
    for n in ORDER:
        delta[n], new_m[n], new_v[n] = _adamw("adamw_" + n, w[n], grads[n], mom[n], var[n])

    def shaped(a, like):
        return a.reshape(like.shape)

    ref_w = dict(g_mix=g_mix, w_in=w_in, g_q=g_q, g_k=g_k, lb_logits=lb_logits, g_hg_out=g_hg_out, p_a=p_a, p_b=p_b, w_o=w_o,
                 g_ffn=g_ffn, w_up=w_up, conv_w=conv_w, conv_b=conv_b, w_down=w_down)
    outs = [loss, dx[None]]
    for group in (grads, delta, new_m, new_v):
        outs += [shaped(group[n], ref_w[n]) for n in ORDER]
    return tuple(outs)
```

```python
import functools

import jax
import jax.numpy as jnp
from jax import lax
from jax.experimental import pallas as pl
from jax.experimental.pallas import tpu as pltpu

F32 = jnp.float32
BF16 = jnp.bfloat16
HEAD = 128
EPS = 1e-6
N_CHIPS = 4
HG_CHUNK = 64
SUB = 8
ADAM_LR, ADAM_B1, ADAM_B2, ADAM_EPS, ADAM_WD, ADAM_STEP = 0.001, 0.9, 0.999, 1e-08, 0.01, 10
VMEM_LIMIT = 56 * 1024 * 1024
MESH = pl.DeviceIdType.MESH

NN = (((1,), (0,)), ((), ()))
NT = (((1,), (1,)), ((), ()))
TN = (((0,), (0,)), ((), ()))


def _cparams(*sem):
    return pltpu.CompilerParams(dimension_semantics=sem if sem else None, vmem_limit_bytes=VMEM_LIMIT)


def _pick(n, cands):
    for c in cands:
        if c <= n and n % c == 0:
            return c
    return n


def _sig(x):
    return 1.0 / (1.0 + jnp.exp(-x))


def _dot(a, b, dims):
    return lax.dot_general(a, b, dims, preferred_element_type=F32)


def _split(x):
    hi = x.astype(BF16)
    lo = (x - hi.astype(F32)).astype(BF16)
    return hi, lo


def _tri(n, kind):
    r = lax.broadcasted_iota(jnp.int32, (n, n), 0)
    c = lax.broadcasted_iota(jnp.int32, (n, n), 1)
    m = {"ge": c >= r, "gt": c > r, "le": c <= r, "lt": c < r}[kind]
    return jnp.where(m, 1.0, 0.0).astype(BF16)


TILE_M = (1024, 512, 256, 128)
TILE_N = (1408, 1024, 512, 256, 128)
TILE_K = (2048, 1408, 1024, 512, 256, 128)


def _matmul(name, a, b, mode, out_dtype, add=None):
    if mode == "nn":
        m, k = a.shape
        g, _, ns = b.shape
        n = g * ns
        tm, tn, tk = _pick(m, TILE_M), _pick(ns, TILE_N), _pick(k, TILE_K)
        nps = ns // tn
        grid = (m // tm, n // tn, k // tk)
        a_spec = pl.BlockSpec((tm, tk), lambda i, j, kk: (i, kk))
        b_spec = pl.BlockSpec((None, tk, tn), lambda i, j, kk: (j // nps, kk, j % nps))
        o_spec = pl.BlockSpec((tm, tn), lambda i, j, kk: (i, j))
        o_shape = jax.ShapeDtypeStruct((m, n), out_dtype)
        dims = NN
    elif mode == "nt":
        m, k = a.shape
        g, n, ks = b.shape
        tm, tn, tk = _pick(m, TILE_M), _pick(n, TILE_N), _pick(ks, TILE_K)
        kps = ks // tk
        grid = (m // tm, n // tn, k // tk)
        a_spec = pl.BlockSpec((tm, tk), lambda i, j, kk: (i, kk))
        b_spec = pl.BlockSpec((None, tn, tk), lambda i, j, kk: (kk // kps, j, kk % kps))
        o_spec = pl.BlockSpec((tm, tn), lambda i, j, kk: (i, j))
        o_shape = jax.ShapeDtypeStruct((m, n), out_dtype)
        dims = NT
    else:
        raise ValueError(mode)
    nk = grid[2]

    def body(*refs):
        a_ref, b_ref = refs[0], refs[1]
        add_ref = refs[2] if add is not None else None
        o_ref = refs[2 + (add is not None)]

        def finish(r):
            if add is not None:
                r = r + add_ref[...]
            o_ref[...] = r.astype(o_ref.dtype)

        part = _dot(a_ref[...].astype(BF16), b_ref[...].astype(BF16), dims)
        if nk == 1:
            finish(part)
            return
        acc = refs[-1]
        kk = pl.program_id(2)

        @pl.when(kk == 0)
        def _():
            acc[...] = part

        @pl.when(kk > 0)
        def _():
            acc[...] += part

        @pl.when(kk == nk - 1)
        def _():
            finish(acc[...])

    in_specs = [a_spec, b_spec]
    args = [a, b]
    if add is not None:
        in_specs.append(o_spec)
        args.append(add)
    return pl.pallas_call(
        body, name=name, grid=grid, in_specs=in_specs, out_specs=o_spec, out_shape=o_shape,
        scratch_shapes=[pltpu.VMEM((tm, tn), F32)] if nk > 1 else [],
        compiler_params=_cparams("parallel", "parallel", "arbitrary"),
    )(*args)


def _matmul_tn(name, a, b, g, out_dtype):
    k, m = a.shape
    _, n = b.shape
    ns = n // g
    tm, tn, tk = _pick(m, (2048, 1408) + TILE_M), _pick(ns, TILE_N), _pick(k, (1024, 512, 256, 128))
    nps = ns // tn
    nk = k // tk

    def body(a_ref, b_ref, o_ref, acc):
        kk = pl.program_id(2)
        part = _dot(a_ref[...].astype(BF16), b_ref[...].astype(BF16), TN)

        @pl.when(kk == 0)
        def _():
            acc[...] = part

        @pl.when(kk > 0)
        def _():
            acc[...] += part

        @pl.when(kk == nk - 1)
        def _():
            o_ref[...] = acc[...].astype(o_ref.dtype)

    return pl.pallas_call(
        body, name=name, grid=(m // tm, n // tn, nk),
        in_specs=[pl.BlockSpec((tk, tm), lambda i, j, kk: (kk, i)), pl.BlockSpec((tk, tn), lambda i, j, kk: (kk, j))],
        out_specs=pl.BlockSpec((None, tm, tn), lambda i, j, kk: (j // nps, i, j % nps)),
        out_shape=jax.ShapeDtypeStruct((g, m, ns), out_dtype),
        scratch_shapes=[pltpu.VMEM((tm, tn), F32)],
        compiler_params=_cparams("parallel", "parallel", "arbitrary"),
    )(a, b)


def _rmsnorm_fwd(name, x, g):
    s, d = x.shape
    tr = _pick(s, (256, 128, 64, 32, 16, 8))

    def body(x_ref, g_ref, u_ref):
        xv = x_ref[...]
        r = lax.rsqrt(jnp.mean(xv * xv, axis=-1, keepdims=True) + EPS)
        u_ref[...] = (xv * r * g_ref[...]).astype(u_ref.dtype)

    return pl.pallas_call(
        body, name=name, grid=(s // tr,),
        in_specs=[pl.BlockSpec((tr, d), lambda i: (i, 0)), pl.BlockSpec((1, d), lambda i: (0, 0))],
        out_specs=pl.BlockSpec((tr, d), lambda i: (i, 0)),
        out_shape=jax.ShapeDtypeStruct((s, d), BF16),
        compiler_params=_cparams("parallel"),
    )(x, g)


def _rmsnorm_bwd(name, x, g, du, extra):
    s, d = x.shape
    tr = _pick(s, (256, 128, 64, 32, 16, 8))

    def body(x_ref, g_ref, du_ref, e_ref, dx_ref, dg_ref):
        i = pl.program_id(0)
        xv = x_ref[...]
        r = lax.rsqrt(jnp.mean(xv * xv, axis=-1, keepdims=True) + EPS)
        n = xv * r
        dy = du_ref[...]
        a = dy * g_ref[...]
        dx = r * (a - n * jnp.mean(a * n, axis=-1, keepdims=True))
        dx_ref[...] = e_ref[...] + dx

        @pl.when(i == 0)
        def _():
            dg_ref[...] = jnp.zeros_like(dg_ref)

        dg_ref[...] += jnp.sum((dy * n).reshape(tr // SUB, SUB, d), axis=0)

    return pl.pallas_call(
        body, name=name, grid=(s // tr,),
        in_specs=[pl.BlockSpec((tr, d), lambda i: (i, 0)), pl.BlockSpec((1, d), lambda i: (0, 0)),
                  pl.BlockSpec((tr, d), lambda i: (i, 0)), pl.BlockSpec((tr, d), lambda i: (i, 0))],
        out_specs=[pl.BlockSpec((tr, d), lambda i: (i, 0)), pl.BlockSpec((SUB, d), lambda i: (0, 0))],
        out_shape=[jax.ShapeDtypeStruct((s, d), F32), jax.ShapeDtypeStruct((SUB, d), F32)],
        compiler_params=_cparams("arbitrary"),
    )(x, g, du, extra)


def _head_norm(x, g):
    r = lax.rsqrt(jnp.mean(x * x, axis=-1, keepdims=True) + EPS)
    return x * r * g


def _qk_norm_fwd(proj, g_q, g_k, nh, blk):
    s = proj.shape[0]

    def body(q_ref, k_ref, v_ref, gq_ref, gk_ref, qn_ref, kn_ref, vb_ref, kt_ref, vt_ref):
        qn_ref[...] = _head_norm(q_ref[...], gq_ref[...]).astype(BF16)
        kn = _head_norm(k_ref[...], gk_ref[...])
        kn_ref[...] = kn.astype(BF16)
        kt_ref[...] = kn.T.astype(BF16)
        v = v_ref[...]
        vb_ref[...] = v.astype(BF16)
        vt_ref[...] = v.T.astype(BF16)

    col = lambda base: pl.BlockSpec((blk, HEAD), lambda i, h: (i, base + h))
    gs = pl.BlockSpec((1, HEAD), lambda i, h: (0, 0))
    o = pl.BlockSpec((blk, HEAD), lambda i, h: (i, h))
    t = pl.BlockSpec((None, HEAD, blk), lambda i, h: (i, h, 0))
    sh = jax.ShapeDtypeStruct((s, nh * HEAD), BF16)
    tsh = jax.ShapeDtypeStruct((s // blk, nh * HEAD, blk), BF16)
    return pl.pallas_call(
        body, name="qk_norm_fwd", grid=(s // blk, nh),
        in_specs=[col(0), col(nh), col(2 * nh), gs, gs], out_specs=[o, o, o, t, t], out_shape=[sh, sh, sh, tsh, tsh],
        compiler_params=_cparams("parallel", "parallel"),
    )(proj, proj, proj, g_q, g_k)


def _qk_norm_bwd(proj, g_q, g_k, dqn, dkn, nh):
    s = proj.shape[0]
    tr = _pick(s, (512, 256, 128, 64, 32, 16, 8))

    def one(x, g, dy):
        r = lax.rsqrt(jnp.mean(x * x, axis=-1, keepdims=True) + EPS)
        n = x * r
        a = dy * g
        dx = r * (a - n * jnp.mean(a * n, axis=-1, keepdims=True))
        return dx, jnp.sum((dy * n).reshape(tr // SUB, SUB, HEAD), axis=0)

    def body(q_ref, k_ref, gq_ref, gk_ref, dqn_ref, dkn_ref, dq_ref, dk_ref, dgq_ref, dgk_ref):
        first = (pl.program_id(0) == 0) & (pl.program_id(1) == 0)

        @pl.when(first)
        def _():
            dgq_ref[...] = jnp.zeros_like(dgq_ref)
            dgk_ref[...] = jnp.zeros_like(dgk_ref)

        dq, pq = one(q_ref[...], gq_ref[...], dqn_ref[...])
        dk, pk = one(k_ref[...], gk_ref[...], dkn_ref[...])
        dq_ref[...] = dq.astype(BF16)
        dk_ref[...] = dk.astype(BF16)
        dgq_ref[...] += pq
        dgk_ref[...] += pk

    col = lambda base: pl.BlockSpec((tr, HEAD), lambda i, h: (i, base + h))
    gs = pl.BlockSpec((1, HEAD), lambda i, h: (0, 0))
    o = pl.BlockSpec((tr, HEAD), lambda i, h: (i, h))
    part = pl.BlockSpec((SUB, HEAD), lambda i, h: (0, 0))
    sh = jax.ShapeDtypeStruct((s, nh * HEAD), BF16)
    psh = jax.ShapeDtypeStruct((SUB, HEAD), F32)
    return pl.pallas_call(
        body, name="qk_norm_bwd", grid=(s // tr, nh),
        in_specs=[col(0), col(nh), gs, gs, o, o], out_specs=[o, o, part, part], out_shape=[sh, sh, psh, psh],
        compiler_params=_cparams("arbitrary", "arbitrary"),
    )(proj, proj, g_q, g_k, dqn, dkn)


def _sb_consts(blk, hp):
    upper = _tri(blk, "ge")
    row = lax.broadcasted_iota(jnp.int32, (blk, hp * blk), 0)
    col = lax.broadcasted_iota(jnp.int32, (blk, hp * blk), 1)
    strict = row < col
    for h in range(1, hp):
        strict = strict & ((col < h * blk) | (row < col - h * blk))
    return jnp.concatenate([upper, upper], axis=1), strict


def _sb_log_keep(zt, strict):
    l = jnp.minimum(-zt, 0.0) - jnp.log(1.0 + jnp.exp(-jnp.abs(zt)))
    return l if strict is None else jnp.where(strict, l, 0.0)


SB_GROUP = 4
SB_GROUP_BWD = 2


def _sb_heads(nh):
    return 2 if nh % 2 == 0 else 1


def _sb_attn_fwd(qn, kn, vt, nh, blk):
    s = qn.shape[0]
    nb = s // blk
    scale = HEAD ** -0.5
    hp = _sb_heads(nh)

    def body(q_ref, k_ref, vt_ref, y_ref, c_ref):
        qi = pl.program_id(1)
        suffix, strict = _sb_consts(blk, hp)
        qs = [q_ref[:, h * HEAD:(h + 1) * HEAD] for h in range(hp)]

        def logits(kb_i):
            off = pl.multiple_of(kb_i * blk, blk)
            return jnp.concatenate(
                [_dot(k_ref[pl.ds(off, blk), h * HEAD:(h + 1) * HEAD], qs[h], NT) for h in range(hp)], axis=1) * scale

        def sums(zt, mask):
            l = _sb_log_keep(zt, mask)
            parts = []
            for h in range(hp):
                hi, lo = _split(l[:, h * blk:(h + 1) * blk])
                parts.append(_dot(suffix, jnp.concatenate([hi, lo], axis=0), NN))
            return jnp.concatenate(parts, axis=1)

        def weights(kb_i, zt, cum, cr, mask):
            for h in range(hp):
                c_ref[h, kb_i] = cr[:, h * blk:(h + 1) * blk]
            wt = jnp.exp(zt + cum + cr)
            if mask is not None:
                wt = jnp.where(mask, wt, 0.0)
            return wt.astype(BF16), cr + cum[0:1, :]

        def add_values(kb_i, wt, accs):
            return tuple(
                accs[h] + _dot(vt_ref[kb_i, h * HEAD:(h + 1) * HEAD, :], wt[:, h * blk:(h + 1) * blk], NN)
                for h in range(hp))

        def group(kbs, masks, accs, cr):
            zts = [logits(k) for k in kbs]
            cums = [sums(zt, m) for zt, m in zip(zts, masks)]
            for k, zt, cum, m in zip(kbs, zts, cums, masks):
                wt, cr = weights(k, zt, cum, cr, m)
                accs = add_values(k, wt, accs)
            return accs, cr

        accs = tuple(jnp.zeros((HEAD, blk), F32) for _ in range(hp))
        accs, cr = group([qi], [strict], accs, jnp.zeros((1, hp * blk), F32))
        n_groups = qi // SB_GROUP

        def many(g, st):
            top = qi - 1 - g * SB_GROUP
            return group([top - j for j in range(SB_GROUP)], [None] * SB_GROUP, *st)

        def one(r, st):
            return group([qi - 1 - n_groups * SB_GROUP - r], [None], *st)

        st = lax.fori_loop(0, n_groups, many, (accs, cr))
        accs, _ = lax.fori_loop(0, qi - n_groups * SB_GROUP, one, st)
        for h in range(hp):
            y_ref[:, h * HEAD:(h + 1) * HEAD] = accs[h].T.astype(y_ref.dtype)

    qs_ = pl.BlockSpec((blk, hp * HEAD), lambda h, i: (i, h))
    full = pl.BlockSpec((s, hp * HEAD), lambda h, i: (0, h))
    return pl.pallas_call(
        body, name="sb_attn_fwd", grid=(nh // hp, nb),
        in_specs=[qs_, full, pl.BlockSpec((nb, hp * HEAD, blk), lambda h, i: (0, h, 0))],
        out_specs=[qs_, pl.BlockSpec((hp, nb, 1, blk), lambda h, i: (h, 0, 0, i))],
        out_shape=[jax.ShapeDtypeStruct((s, nh * HEAD), BF16), jax.ShapeDtypeStruct((nh, nb, 1, s), F32)],
        compiler_params=_cparams("parallel", "arbitrary"),
    )(qn, kn, vt)


def _sb_attn_bwd(qn, kn, kt, vb, dy, carries, nh, blk):
    s = qn.shape[0]
    nb = s // blk
    scale = HEAD ** -0.5
    hp = _sb_heads(nh)

    def body(q_ref, k_ref, kt_ref, v_ref, dy_ref, c_ref, dq_ref, dk_ref, dv_ref):
        qi = pl.program_id(1)

        @pl.when(qi == 0)
        def _():
            dk_ref[...] = jnp.zeros_like(dk_ref)
            dv_ref[...] = jnp.zeros_like(dv_ref)

        suffix, strict = _sb_consts(blk, hp)
        prefix = _tri(blk, "lt")
        qs = [q_ref[:, h * HEAD:(h + 1) * HEAD] for h in range(hp)]
        dos = [dy_ref[:, h * HEAD:(h + 1) * HEAD].astype(BF16) for h in range(hp)]

        def logits(kb_i):
            off = pl.multiple_of(kb_i * blk, blk)
            return jnp.concatenate(
                [_dot(k_ref[pl.ds(off, blk), h * HEAD:(h + 1) * HEAD], qs[h], NT) for h in range(hp)], axis=1) * scale

        def group(kbs, masks, dqs, ec):
            rows = [pl.ds(pl.multiple_of(k * blk, blk), blk) for k in kbs]
            zts = [logits(k) for k in kbs]
            dws = [jnp.concatenate([_dot(v_ref[r, h * HEAD:(h + 1) * HEAD], dos[h], NT) for h in range(hp)], axis=1)
                   for r in rows]
            ls = [_sb_log_keep(zt, m) for zt, m in zip(zts, masks)]
            cums = []
            for l in ls:
                parts = []
                for h in range(hp):
                    hi, lo = _split(l[:, h * blk:(h + 1) * blk])
                    parts.append(_dot(suffix, jnp.concatenate([hi, lo], axis=0), NN))
                cums.append(jnp.concatenate(parts, axis=1))
            wts, ets, befores = [], [], []
            for k, zt, cum, dw, m in zip(kbs, zts, cums, dws, masks):
                cr = jnp.concatenate([c_ref[h, k] for h in range(hp)], axis=1)
                wt = jnp.exp(zt + cum + cr)
                if m is not None:
                    wt = jnp.where(m, wt, 0.0)
                et = wt * dw
                befores.append(_dot(prefix, et.astype(BF16), NN) + ec)
                ec = ec + jnp.sum(et, axis=0, keepdims=True)
                wts.append(wt.astype(BF16))
                ets.append(et)
            for k, r, l, et, before, wtb in zip(kbs, rows, ls, ets, befores, wts):
                dzt = ((jnp.exp(l) * (et + before) - before) * scale).astype(BF16)
                out = []
                for h in range(hp):
                    cols, part = slice(h * HEAD, (h + 1) * HEAD), slice(h * blk, (h + 1) * blk)
                    dv_ref[r, cols] += _dot(wtb[:, part], dos[h], NN)
                    dk_ref[r, cols] += _dot(dzt[:, part], qs[h], NN)
                    out.append(dqs[h] + _dot(kt_ref[k, cols, :], dzt[:, part], NN))
                dqs = tuple(out)
            return dqs, ec

        n_groups = qi // SB_GROUP_BWD

        def many(g, st):
            return group([g * SB_GROUP_BWD + j for j in range(SB_GROUP_BWD)], [None] * SB_GROUP_BWD, *st)

        def one(r, st):
            return group([n_groups * SB_GROUP_BWD + r], [None], *st)

        st = (tuple(jnp.zeros((HEAD, blk), F32) for _ in range(hp)), jnp.zeros((1, hp * blk), F32))
        st = lax.fori_loop(0, n_groups, many, st)
        st = lax.fori_loop(0, qi - n_groups * SB_GROUP_BWD, one, st)
        dqs, _ = group([qi], [strict], *st)
        for h in range(hp):
            dq_ref[:, h * HEAD:(h + 1) * HEAD] = dqs[h].T

    qs_ = pl.BlockSpec((blk, hp * HEAD), lambda h, i: (i, h))
    full = pl.BlockSpec((s, hp * HEAD), lambda h, i: (0, h), pipeline_mode=pl.Buffered(1))
    sh = jax.ShapeDtypeStruct((s, nh * HEAD), F32)
    return pl.pallas_call(
        body, name="sb_attn_bwd", grid=(nh // hp, nb),
        in_specs=[qs_, full, pl.BlockSpec((nb, hp * HEAD, blk), lambda h, i: (0, h, 0), pipeline_mode=pl.Buffered(1)),
                  full, qs_, pl.BlockSpec((hp, nb, 1, blk), lambda h, i: (h, 0, 0, i))],
        out_specs=[qs_, full, full], out_shape=[sh, sh, sh],
        compiler_params=_cparams("parallel", "arbitrary"),
    )(qn, kn, kt, vb, dy, carries)


def _lower_bound(lb_logits):
    def body(l_ref, o_ref):
        l = l_ref[...]
        m = jnp.max(l, axis=0, keepdims=True)
        e = jnp.exp(l - m)
        o_ref[...] = e[0:1, :] / jnp.sum(e, axis=0, keepdims=True)

    return pl.pallas_call(body, name="lower_bound", out_shape=jax.ShapeDtypeStruct((1, lb_logits.shape[1]), F32))(lb_logits)


def _hg_gates(hq, hf, lb):
    sq = _sig(hq)
    q = hq * sq
    sf = _sig(hf)
    f = lb + (1.0 - lb) * sf
    return q, sq, f, sf


def _hg_cum(g, c):
    hi, lo = _split(g)
    t = _tri(c, "le")
    return _dot(t, hi, NN) + _dot(t, lo, NN)


def _row_mask(r):
    return lax.broadcasted_iota(jnp.int32, (SUB, HEAD), 0) >= r


def _hg_intra_fwd(q, k, v, b, c):
    outs = []
    for bi in range(c // SUB):
        q_i, b_i = q[bi * SUB:(bi + 1) * SUB], b[bi * SUB:(bi + 1) * SUB]
        acc = jnp.zeros((SUB, HEAD), F32)
        for s in range((bi + 1) * SUB):
            d = b_i - b[s:s + 1]
            if s >= bi * SUB:
                d = jnp.where(_row_mask(s - bi * SUB), d, -jnp.inf)
            col = jnp.sum(q_i * k[s:s + 1] * jnp.exp(d), axis=1, keepdims=True)
            acc = acc + col * v[s:s + 1]
        outs.append(acc)
    return jnp.concatenate(outs, axis=0)


def _hg_intra_bwd(q, k, v, b, do, c, dq_scr, dk_scr, dv_scr):
    nblk = c // SUB
    dq_scr[...] = jnp.zeros_like(dq_scr)
    for s in range(c):
        bj = s // SUB
        ks, vs, bs = k[s:s + 1], v[s:s + 1], b[s:s + 1]
        acc_k = jnp.zeros((SUB, HEAD), F32)
        acc_v = jnp.zeros((SUB, HEAD), F32)
        for bi in range(bj, nblk):
            sl = slice(bi * SUB, (bi + 1) * SUB)
            d = b[sl] - bs
            if bi == bj:
                d = jnp.where(_row_mask(s - bj * SUB), d, -jnp.inf)
            dec = jnp.exp(d)
            qd = q[sl] * dec
            col = jnp.sum(qd * ks, axis=1, keepdims=True)
            dcol = jnp.sum(do[sl] * vs, axis=1, keepdims=True)
            dq_scr[sl, :] += dcol * (ks * dec)
            acc_k = acc_k + dcol * qd
            acc_v = acc_v + col * do[sl]
        dk_scr[s:s + 1, :] = jnp.sum(acc_k, axis=0, keepdims=True)
        dv_scr[s:s + 1, :] = jnp.sum(acc_v, axis=0, keepdims=True)


def _hgrn2_fwd(proj, lb, g_out, nh, base, c):
    s = proj.shape[0]
    nch = s // c

    def body(hq_ref, hf_ref, hi_ref, og_ref, lb_ref, g_ref, o_ref, y_ref, st_ref, st):
        @pl.when(pl.program_id(1) == 0)
        def _():
            st[...] = jnp.zeros_like(st)

        st_in = st[...]
        st_ref[...] = st_in
        q, _, f, _ = _hg_gates(hq_ref[...], hf_ref[...], lb_ref[...])
        k = 1.0 - f
        v = hi_ref[...]
        b = _hg_cum(jnp.log(f), c)
        bl = b[c - 1:c, :]
        o = _dot((q * jnp.exp(b)).astype(BF16), st_in.astype(BF16), NT) + _hg_intra_fwd(q, k, v, b, c)
        kd = k * jnp.exp(bl - b)
        st[...] = st_in * jnp.exp(bl) + _dot(v.astype(BF16), kd.astype(BF16), TN)
        o_ref[...] = o
        og = og_ref[...]
        y_ref[...] = (_head_norm(o, g_ref[...]) * (og * _sig(og))).astype(BF16)

    col = lambda j: pl.BlockSpec((c, HEAD), lambda h, i: (i, base + j * nh + h))
    row = pl.BlockSpec((1, HEAD), lambda h, i: (0, h))
    gs = pl.BlockSpec((1, HEAD), lambda h, i: (0, 0))
    o = pl.BlockSpec((c, HEAD), lambda h, i: (i, h))
    return pl.pallas_call(
        body, name="hgrn2_fwd", grid=(nh, nch),
        in_specs=[col(0), col(1), col(2), col(3), row, gs],
        out_specs=[o, o, pl.BlockSpec((None, None, HEAD, HEAD), lambda h, i: (h, i, 0, 0))],
        out_shape=[jax.ShapeDtypeStruct((s, nh * HEAD), F32), jax.ShapeDtypeStruct((s, nh * HEAD), BF16),
                   jax.ShapeDtypeStruct((nh, nch, HEAD, HEAD), F32)],
        scratch_shapes=[pltpu.VMEM((HEAD, HEAD), F32)],
        compiler_params=_cparams("parallel", "arbitrary"),
    )(proj, proj, proj, proj, lb, g_out)


def _hgrn2_bwd(proj, lb, g_out, o_pre, states, dy, nh, base, c):
    s = proj.shape[0]
    nch = s // c

    def body(hq_ref, hf_ref, hi_ref, og_ref, lb_ref, g_ref, o_ref, st_ref, se_ref, dy_ref,
             dhq_ref, dhf_ref, dhi_ref, dog_ref, dg_ref, dlb_ref, dst, dq_scr, dk_scr, dv_scr):
        h, i = pl.program_id(0), pl.program_id(1)

        @pl.when(i == 0)
        def _():
            dst[...] = jnp.zeros_like(dst)
            dlb_ref[...] = jnp.zeros_like(dlb_ref)

        @pl.when((i == 0) & (h == 0))
        def _():
            dg_ref[...] = jnp.zeros_like(dg_ref)

        lbv = lb_ref[...]
        hq, hf = hq_ref[...], hf_ref[...]
        q, sq, f, sf = _hg_gates(hq, hf, lbv)
        k = 1.0 - f
        v = hi_ref[...]
        b = _hg_cum(jnp.log(f), c)
        bl = b[c - 1:c, :]
        eb = jnp.exp(b)
        ebl = jnp.exp(bl - b)

        o = o_ref[...]
        gout = g_ref[...]
        og = og_ref[...]
        sg = _sig(og)
        r = lax.rsqrt(jnp.mean(o * o, axis=-1, keepdims=True) + EPS)
        n = o * r
        dyv = dy_ref[...]
        dn = dyv * (og * sg)
        dog_ref[...] = (dyv * n * gout * (sg * (1.0 + og * (1.0 - sg)))).astype(BF16)
        dg_ref[...] += jnp.sum((dn * n).reshape(c // SUB, SUB, HEAD), axis=0)
        a = dn * gout
        do = r * (a - n * jnp.mean(a * n, axis=-1, keepdims=True))

        st_in = st_ref[...]
        dstv = dst[...]
        dob = do.astype(BF16)
        dstb = dstv.astype(BF16)
        _hg_intra_bwd(q, k, v, b, do, c, dq_scr, dk_scr, dv_scr)
        dq = dq_scr[...] + eb * _dot(dob, st_in.astype(BF16), NN)
        dk = dk_scr[...] + ebl * _dot(v.astype(BF16), dstb, NN)
        dv = dv_scr[...] + _dot((k * ebl).astype(BF16), dstb, NT)
        dst[...] = dstv * jnp.exp(bl) + _dot(dob, (q * eb).astype(BF16), TN)

        hi_, lo_ = _split(q * dq - k * dk)
        rev = _tri(c, "ge")
        later = jnp.where(i > 0, jnp.sum(dstv * se_ref[...], axis=0, keepdims=True), 0.0)
        dg = _dot(rev, hi_, NN) + _dot(rev, lo_, NN) + later
        df = dg / f - dk
        dhq_ref[...] = (dq * (sq * (1.0 + hq * (1.0 - sq)))).astype(BF16)
        dhf_ref[...] = (df * (1.0 - lbv) * sf * (1.0 - sf)).astype(BF16)
        dhi_ref[...] = dv.astype(BF16)
        dlb_ref[...] += jnp.sum((df * (1.0 - sf)).reshape(c // SUB, SUB, HEAD), axis=0)

    rv = lambda i: nch - 1 - i
    col = lambda j: pl.BlockSpec((c, HEAD), lambda h, i: (rv(i), base + j * nh + h))
    row = pl.BlockSpec((1, HEAD), lambda h, i: (0, h))
    gs = pl.BlockSpec((1, HEAD), lambda h, i: (0, 0))
    o = pl.BlockSpec((c, HEAD), lambda h, i: (rv(i), h))
    st = pl.BlockSpec((None, None, HEAD, HEAD), lambda h, i: (h, rv(i), 0, 0))
    se = pl.BlockSpec((None, None, HEAD, HEAD), lambda h, i: (h, jnp.minimum(rv(i) + 1, nch - 1), 0, 0))
    sh = jax.ShapeDtypeStruct((s, nh * HEAD), BF16)
    return pl.pallas_call(
        body, name="hgrn2_bwd", grid=(nh, nch),
        in_specs=[col(0), col(1), col(2), col(3), row, gs, o, st, se, o],
        out_specs=[o, o, o, o, pl.BlockSpec((SUB, HEAD), lambda h, i: (0, 0)), pl.BlockSpec((SUB, HEAD), lambda h, i: (0, h))],
        out_shape=[sh, sh, sh, sh, jax.ShapeDtypeStruct((SUB, HEAD), F32), jax.ShapeDtypeStruct((SUB, nh * HEAD), F32)],
        scratch_shapes=[pltpu.VMEM((HEAD, HEAD), F32), pltpu.VMEM((c, HEAD), F32), pltpu.VMEM((c, HEAD), F32),
                        pltpu.VMEM((c, HEAD), F32)],
        compiler_params=_cparams("arbitrary", "arbitrary"),
    )(proj, proj, proj, proj, lb, g_out, o_pre, states, states, dy)


def _merge_tiles(s, d, gate_col):
    tr = _pick(s, (256, 128, 64, 32, 16, 8))
    tc = 128
    for cand in (512, 256):
        if d % cand == 0 and gate_col % cand == 0:
            tc = cand
            break
    return tr, tc


def _merge_fwd(proj, ya, yb, gate_col):
    s, d = ya.shape
    tr, tc = _merge_tiles(s, d, gate_col)
    ga0, gb0 = gate_col // tc, (gate_col + d) // tc

    def body(ga_ref, gb_ref, ya_ref, yb_ref, m_ref):
        m_ref[...] = (_sig(ga_ref[...]) * ya_ref[...] + _sig(gb_ref[...]) * yb_ref[...]).astype(BF16)

    o = pl.BlockSpec((tr, tc), lambda i, j: (i, j))
    return pl.pallas_call(
        body, name="merge_fwd", grid=(s // tr, d // tc),
        in_specs=[pl.BlockSpec((tr, tc), lambda i, j: (i, ga0 + j)), pl.BlockSpec((tr, tc), lambda i, j: (i, gb0 + j)), o, o],
        out_specs=o, out_shape=jax.ShapeDtypeStruct((s, d), BF16),
        compiler_params=_cparams("parallel", "parallel"),
    )(proj, proj, ya, yb)


def _merge_bwd(proj, ya, yb, dm, gate_col):
    s, d = ya.shape
    tr, tc = _merge_tiles(s, d, gate_col)
    ga0, gb0 = gate_col // tc, (gate_col + d) // tc

    def body(ga_ref, gb_ref, ya_ref, yb_ref, dm_ref, dya_ref, dyb_ref, dga_ref, dgb_ref):
        dmv = dm_ref[...]
        sa, sb = _sig(ga_ref[...]), _sig(gb_ref[...])
        dya_ref[...] = (dmv * sa).astype(BF16)
        dyb_ref[...] = (dmv * sb).astype(BF16)
        dga_ref[...] = (dmv * ya_ref[...] * sa * (1.0 - sa)).astype(BF16)
        dgb_ref[...] = (dmv * yb_ref[...] * sb * (1.0 - sb)).astype(BF16)

    o = pl.BlockSpec((tr, tc), lambda i, j: (i, j))
    sh = jax.ShapeDtypeStruct((s, d), BF16)
    return pl.pallas_call(
        body, name="merge_bwd", grid=(s // tr, d // tc),
        in_specs=[pl.BlockSpec((tr, tc), lambda i, j: (i, ga0 + j)), pl.BlockSpec((tr, tc), lambda i, j: (i, gb0 + j)), o, o, o],
        out_specs=[o, o, o, o], out_shape=[sh, sh, sh, sh],
        compiler_params=_cparams("parallel", "parallel"),
    )(proj, proj, ya, yb, dm)


CONV_ROWS = 512


def _conv_ext(ref, i, rows, s, before, after):
    parts = []
    if before:
        p = ref[pl.ds(pl.multiple_of(jnp.maximum(i * rows - before, 0), SUB), before), :]
        parts.append(jnp.where(i > 0, p, 0.0))
    parts.append(ref[pl.ds(pl.multiple_of(i * rows, SUB), rows), :])
    if after:
        nxt = ref[pl.ds(pl.multiple_of(jnp.minimum((i + 1) * rows, s - after), SUB), after), :]
        parts.append(jnp.where((i + 1) * rows < s, nxt, 0.0))
    return jnp.concatenate(parts, axis=0)


def _conv3(ext, w, bias):
    x1 = pltpu.roll(ext, 1, 0)
    x2 = pltpu.roll(ext, 2, 0)
    return bias + w[0:1, :] * x2 + w[1:2, :] * x1 + w[2:3, :] * ext, x1, x2


def _convffn_fwd(up, conv_w, conv_b, dff):
    s = up.shape[0]
    tc = HEAD
    nf = dff // tc
    rows = _pick(s, (CONV_ROWS, 256, 128, 64, 32, 16, 8))

    def body(ug_ref, uv_ref, wg_ref, wv_ref, bg_ref, bv_ref, a_ref):
        wg, wv, bg, bv = wg_ref[...], wv_ref[...], bg_ref[...], bv_ref[...]

        def step(i, _):
            g = _conv3(_conv_ext(ug_ref, i, rows, s, SUB, 0), wg, bg)[0][SUB:]
            v = _conv3(_conv_ext(uv_ref, i, rows, s, SUB, 0), wv, bv)[0][SUB:]
            a_ref[pl.ds(pl.multiple_of(i * rows, SUB), rows), :] = (g * _sig(g) * v).astype(BF16)
            return 0

        lax.fori_loop(0, s // rows, step, 0)

    cg = lambda r: pl.BlockSpec((r, tc), lambda j: (0, j))
    cv = lambda r: pl.BlockSpec((r, tc), lambda j: (0, nf + j))
    return pl.pallas_call(
        body, name="convffn_fwd", grid=(nf,),
        in_specs=[cg(s), cv(s), cg(3), cv(3), cg(1), cv(1)], out_specs=cg(s),
        out_shape=jax.ShapeDtypeStruct((s, dff), BF16),
        compiler_params=_cparams("parallel"),
    )(up, up, conv_w, conv_w, conv_b, conv_b)


def _convffn_bwd(up, conv_w, conv_b, dact, dff):
    s = up.shape[0]
    tc = HEAD
    nf = dff // tc
    rows = _pick(s, (CONV_ROWS, 256, 128, 64, 32, 16, 8))
    n_ext = rows + SUB

    def body(ug_ref, uv_ref, wg_ref, wv_ref, bg_ref, bv_ref, da_ref,
             dug_ref, duv_ref, dwg_ref, dwv_ref, dbg_ref, dbv_ref):
        wg, wv, bg, bv = wg_ref[...], wv_ref[...], bg_ref[...], bv_ref[...]

        def fold(x):
            return jnp.sum(x.reshape(rows // SUB, SUB, tc), axis=0)

        def one(ext, x1, x2, d_ext, w):
            d1 = pltpu.roll(d_ext, n_ext - 1, 0)[:rows]
            d2 = pltpu.roll(d_ext, n_ext - 2, 0)[:rows]
            dc = d_ext[:rows]
            du = w[2:3, :] * dc + w[1:2, :] * d1 + w[0:1, :] * d2
            sl = slice(SUB, SUB + rows)
            return du, (fold(dc * x2[sl]), fold(dc * x1[sl]), fold(dc * ext[sl]), fold(dc))

        def step(i, acc):
            eg = _conv_ext(ug_ref, i, rows, s, SUB, SUB)
            ev = _conv_ext(uv_ref, i, rows, s, SUB, SUB)
            g, g1, g2 = _conv3(eg, wg, bg)
            v, v1, v2 = _conv3(ev, wv, bv)
            g, v = g[SUB:], v[SUB:]
            da = _conv_ext(da_ref, i, rows, s, 0, SUB)
            sg = _sig(g)
            dg = da * v * (sg * (1.0 + g * (1.0 - sg)))
            dv = da * (g * sg)
            dug, pg = one(eg, g1, g2, dg, wg)
            duv, pv = one(ev, v1, v2, dv, wv)
            at = pl.ds(pl.multiple_of(i * rows, SUB), rows)
            dug_ref[at, :] = dug.astype(BF16)
            duv_ref[at, :] = duv.astype(BF16)
            return tuple(a + p for a, p in zip(acc, pg + pv))

        zero = jnp.zeros((SUB, tc), F32)
        acc = lax.fori_loop(0, s // rows, step, (zero,) * 8)
        red = [jnp.sum(a, axis=0, keepdims=True) for a in acc]
        for j in range(3):
            dwg_ref[j:j + 1, :] = red[j]
            dwv_ref[j:j + 1, :] = red[4 + j]
        dbg_ref[...] = red[3]
        dbv_ref[...] = red[7]

    cg = lambda r: pl.BlockSpec((r, tc), lambda j: (0, j))
    cv = lambda r: pl.BlockSpec((r, tc), lambda j: (0, nf + j))
    outs = pl.pallas_call(
        body, name="convffn_bwd", grid=(nf,),
        in_specs=[cg(s), cv(s), cg(3), cv(3), cg(1), cv(1), cg(s)],
        out_specs=[cg(s), cg(s), cg(3), cg(3), cg(1), cg(1)],
        out_shape=[jax.ShapeDtypeStruct((s, dff), BF16), jax.ShapeDtypeStruct((s, dff), BF16),
                   jax.ShapeDtypeStruct((3, dff), F32), jax.ShapeDtypeStruct((3, dff), F32),
                   jax.ShapeDtypeStruct((1, dff), F32), jax.ShapeDtypeStruct((1, dff), F32)],
        compiler_params=_cparams("parallel"),
    )(up, up, conv_w, conv_w, conv_b, conv_b, dact)
    return outs


def _loss_head(out, target):
    s, d = out.shape
    tr = _pick(s, (256, 128, 64, 32, 16, 8))

    def body(o_ref, t_ref, d_ref, l_ref):
        @pl.when(pl.program_id(0) == 0)
        def _():
            l_ref[...] = jnp.zeros_like(l_ref)

        err = o_ref[...] - t_ref[...]
        d_ref[...] = err * (1.0 / d)
        sq = jnp.sum((err * err).reshape(tr // SUB, SUB, d), axis=0)
        part = sq[:, 0:HEAD]
        for j in range(1, d // HEAD):
            part = part + sq[:, j * HEAD:(j + 1) * HEAD]
        l_ref[...] += part

    blk = pl.BlockSpec((tr, d), lambda i: (i, 0))
    return pl.pallas_call(
        body, name="loss_head", grid=(s // tr,), in_specs=[blk, blk],
        out_specs=[blk, pl.BlockSpec((SUB, HEAD), lambda i: (0, 0))],
        out_shape=[jax.ShapeDtypeStruct((s, d), F32), jax.ShapeDtypeStruct((SUB, HEAD), F32)],
        compiler_params=_cparams("arbitrary"),
    )(out, target)


def _sum_rows(name, parts):
    def body(p_ref, o_ref):
        o_ref[...] = jnp.sum(p_ref[...], axis=0, keepdims=True)

    return pl.pallas_call(body, name=name, out_shape=jax.ShapeDtypeStruct((1, parts.shape[1]), F32))(parts)


def _local_step(x, target, g_mix, g_q, g_k, lb_logits, g_hg_out, g_ffn, conv_w, conv_b, w_in, p_a, p_b, w_o, w_up, w_down):
    s, d = x.shape
    nh = p_a.shape[1] // HEAD
    wid = nh * HEAD
    dff = w_down.shape[1]
    blk = _pick(s, (256, 128))
    chunk = _pick(s, (HG_CHUNK,))
    gate_col = 7 * wid

    u = _rmsnorm_fwd("rmsnorm_mix", x, g_mix)
    proj = _matmul("in_proj", u, w_in, "nn", F32)
    qn, kn, vb, kt, vt = _qk_norm_fwd(proj, g_q, g_k, nh, blk)
    y_a, carries = _sb_attn_fwd(qn, kn, vt, nh, blk)
    lb = _lower_bound(lb_logits)
    o_pre, y_b, states = _hgrn2_fwd(proj, lb, g_hg_out, nh, 3 * nh, chunk)
    ya_p = _matmul("proj_a", y_a, p_a, "nn", F32)
    yb_p = _matmul("proj_b", y_b, p_b, "nn", F32)
    m = _merge_fwd(proj, ya_p, yb_p, gate_col)
    h = _matmul("out_proj", m, w_o, "nn", F32, add=x)
    u2 = _rmsnorm_fwd("rmsnorm_ffn", h, g_ffn)
    up = _matmul("up_proj", u2, w_up, "nn", F32)
    act = _convffn_fwd(up, conv_w, conv_b, dff)
    out = _matmul("down_proj", act, w_down, "nn", F32, add=h)
    dout, sq = _loss_head(out, target)

    dact = _matmul("d_act", dout, w_down, "nt", F32)
    g_w_down = _matmul_tn("g_w_down", act, dout, 1, BF16)
    dup_g, dup_v, dcw_g, dcw_v, dcb_g, dcb_v = _convffn_bwd(up, conv_w, conv_b, dact, dff)
    dup = jnp.concatenate([dup_g, dup_v], axis=1)
    du2 = _matmul("d_u2", dup, w_up, "nt", F32)
    g_w_up = _matmul_tn("g_w_up", u2, dup, N_CHIPS, BF16)
    dh, pg_ffn = _rmsnorm_bwd("rmsnorm_ffn_bwd", h, g_ffn, du2, dout)
    dm = _matmul("d_m", dh, w_o, "nt", F32)
    g_w_o = _matmul_tn("g_w_o", m, dh, 1, BF16)
    dya_p, dyb_p, dga, dgb = _merge_bwd(proj, ya_p, yb_p, dm, gate_col)
    dy_a = _matmul("d_y_a", dya_p, p_a, "nt", F32)
    g_p_a = _matmul_tn("g_p_a", y_a, dya_p, N_CHIPS, BF16)
    dy_b = _matmul("d_y_b", dyb_p, p_b, "nt", F32)
    g_p_b = _matmul_tn("g_p_b", y_b, dyb_p, N_CHIPS, BF16)
    dhq, dhf, dhi, dog, pg_hg, p_lb = _hgrn2_bwd(proj, lb, g_hg_out, o_pre, states, dy_b, nh, 3 * nh, chunk)
    dqn, dkn, dv = _sb_attn_bwd(qn, kn, kt, vb, dy_a, carries, nh, blk)
    dq, dk, pg_q, pg_k = _qk_norm_bwd(proj, g_q, g_k, dqn, dkn, nh)
    dproj = jnp.concatenate([dq, dk, dv.astype(BF16), dhq, dhf, dhi, dog, dga, dgb], axis=1)
    du = _matmul("d_u", dproj, w_in, "nt", F32)
    g_w_in = _matmul_tn("g_w_in", u, dproj, N_CHIPS, BF16)
    dx, pg_mix = _rmsnorm_bwd("rmsnorm_mix_bwd", x, g_mix, du, dh)

    small = dict(
        g_mix=_sum_rows("sum_g_mix", pg_mix), g_q=_sum_rows("sum_g_q", pg_q), g_k=_sum_rows("sum_g_k", pg_k),
        lb=_sum_rows("sum_lb", p_lb), g_hg_out=_sum_rows("sum_g_hg", pg_hg), g_ffn=_sum_rows("sum_g_ffn", pg_ffn),
        conv_w=jnp.concatenate([dcw_g, dcw_v], axis=1), conv_b=jnp.concatenate([dcb_g, dcb_v], axis=1),
        sq=_sum_rows("sum_sq", sq),
    )
    big = dict(w_in=g_w_in, p_a=g_p_a, p_b=g_p_b, w_o=g_w_o.reshape(N_CHIPS, d // N_CHIPS, d), w_up=g_w_up,
               w_down=g_w_down.reshape(N_CHIPS, dff // N_CHIPS, d))
    return dx, big, small, lb


ANY = pl.BlockSpec(memory_space=pl.ANY)


def _place():
    x, y, c = lax.axis_index("x"), lax.axis_index("y"), lax.axis_index("c")
    chips = [(1 - x, y), (x, 1 - y), (1 - x, 1 - y)]
    return x, y, c, chips


def _remote(src, dst, send_sem, recv_sem, to):
    return pltpu.make_async_remote_copy(src_ref=src, dst_ref=dst, send_sem=send_sem, recv_sem=recv_sem,
                                        device_id=to, device_id_type=MESH)


def _cast_bf16(name, w):
    r, c = w.shape
    tr = _pick(r, (256, 128, 64, 32, 16))

    def body(w_ref, o_ref):
        o_ref[...] = w_ref[...].astype(BF16)

    return pl.pallas_call(
        body, name=name, grid=(r // tr,), in_specs=[pl.BlockSpec((tr, c), lambda i: (i, 0))],
        out_specs=pl.BlockSpec((tr, c), lambda i: (i, 0)), out_shape=jax.ShapeDtypeStruct((r, c), BF16),
        compiler_params=_cparams("parallel"),
    )(w)


def _gather_weights(shards):
    n = len(shards)

    def body(*refs):
        ins, outs = refs[:n], refs[n:2 * n]
        send, recv, local = refs[2 * n:]
        x, y, c, chips = _place()
        mine = 2 * x + y
        sends, owns = [], []
        for k in range(n):
            half = ins[k].shape[0] // 2
            rows = pl.ds(c * half, half)
            own = pltpu.make_async_copy(ins[k], outs[k].at[mine], local.at[k])
            own.start()
            owns.append(own)
            for j, (px, py) in enumerate(chips):
                cp = _remote(ins[k].at[rows], outs[k].at[mine, rows], send.at[k, j], recv.at[k, j], (px, py, c))
                cp.start()
                sends.append(cp)
        for k in range(n):
            half = ins[k].shape[0] // 2
            rows = pl.ds(c * half, half)
            for j, (px, py) in enumerate(chips):
                part = outs[k].at[2 * px + py, rows]
                _remote(part, part, send.at[k, j], recv.at[k, j], (px, py, c)).wait_recv()
                fw = _remote(part, part, send.at[k, 3 + j], recv.at[k, 3 + j], (x, y, 1 - c))
                fw.start()
                sends.append(fw)
        for k in range(n):
            half = ins[k].shape[0] // 2
            other = pl.ds((1 - c) * half, half)
            for j, (px, py) in enumerate(chips):
                part = outs[k].at[2 * px + py, other]
                _remote(part, part, send.at[k, 3 + j], recv.at[k, 3 + j], (x, y, 1 - c)).wait_recv()
        for cp in sends:
            cp.wait_send()
        for cp in owns:
            cp.wait()

    return pl.pallas_call(
        body, name="gather_weights", in_specs=[ANY] * n, out_specs=[ANY] * n,
        out_shape=[jax.ShapeDtypeStruct((N_CHIPS,) + w.shape, w.dtype) for w in shards],
        scratch_shapes=[pltpu.SemaphoreType.DMA((n, 6)), pltpu.SemaphoreType.DMA((n, 6)), pltpu.SemaphoreType.DMA((n,))],
    )(*shards)


def _exchange_halves(name, srcs, src_half_other, out_shapes):
    n = len(srcs)

    def body(*refs):
        ins, outs = refs[:n], refs[n:2 * n]
        send, recv = refs[2 * n:]
        x, y, c, _ = _place()
        cps = []
        for k in range(n):
            if src_half_other:
                half = ins[k].shape[1] // 2
                src = ins[k].at[:, pl.ds((1 - c) * half, half)]
            else:
                src = ins[k]
            cp = _remote(src, outs[k], send.at[k], recv.at[k], (x, y, 1 - c))
            cp.start()
            cps.append(cp)
        for cp in cps:
            cp.wait_recv()
        for cp in cps:
            cp.wait_send()

    return pl.pallas_call(
        body, name=name, in_specs=[ANY] * n, out_specs=[ANY] * n, out_shape=out_shapes,
        scratch_shapes=[pltpu.SemaphoreType.DMA((n,)), pltpu.SemaphoreType.DMA((n,))],
    )(*srcs)


def _add_halves(name, g, got, c):
    _, r, cols = g.shape
    half = r // 2
    tr = _pick(half, (256, 128, 64, 32, 16))
    nt = half // tr

    def body(c_ref, g_ref, o_ref, out_ref):
        out_ref[...] = (g_ref[...].astype(F32) + o_ref[...].astype(F32)).astype(BF16)

    return pl.pallas_call(
        body, name=name,
        grid_spec=pltpu.PrefetchScalarGridSpec(
            num_scalar_prefetch=1, grid=(N_CHIPS, nt),
            in_specs=[pl.BlockSpec((None, tr, cols), lambda s, i, cr: (s, cr[0] * nt + i, 0)),
                      pl.BlockSpec((None, tr, cols), lambda s, i, cr: (s, i, 0))],
            out_specs=pl.BlockSpec((None, tr, cols), lambda s, i, cr: (s, i, 0))),
        out_shape=jax.ShapeDtypeStruct((N_CHIPS, half, cols), BF16),
        compiler_params=_cparams("parallel", "parallel"),
    )(c, g, got)


def _scatter_partials(parts):
    n = len(parts)

    def body(*refs):
        ins, outs = refs[:n], refs[n:2 * n]
        send, recv = refs[2 * n:]
        x, y, c, chips = _place()
        cps = []
        for k in range(n):
            for j, (px, py) in enumerate(chips):
                cp = _remote(ins[k].at[2 * px + py], outs[k].at[j], send.at[k, j], recv.at[k, j], (px, py, c))
                cp.start()
                cps.append(cp)
        for cp in cps:
            cp.wait_recv()
        for cp in cps:
            cp.wait_send()

    return pl.pallas_call(
        body, name="scatter_partials", in_specs=[ANY] * n, out_specs=[ANY] * n,
        out_shape=[jax.ShapeDtypeStruct((3,) + p.shape[1:], p.dtype) for p in parts],
        scratch_shapes=[pltpu.SemaphoreType.DMA((n, 3)), pltpu.SemaphoreType.DMA((n, 3))],
    )(*parts)


def _sum_partials(name, part, got, shard):
    _, half, cols = part.shape
    tr = _pick(half, (256, 128, 64, 32, 16))

    def body(s_ref, p_ref, g_ref, o_ref):
        acc = p_ref[...].astype(F32)
        for j in range(3):
            acc = acc + g_ref[j].astype(F32)
        o_ref[...] = acc

    return pl.pallas_call(
        body, name=name,
        grid_spec=pltpu.PrefetchScalarGridSpec(
            num_scalar_prefetch=1, grid=(half // tr,),
            in_specs=[pl.BlockSpec((None, tr, cols), lambda i, sr: (sr[0], i, 0)),
                      pl.BlockSpec((3, tr, cols), lambda i, sr: (0, i, 0))],
            out_specs=pl.BlockSpec((tr, cols), lambda i, sr: (i, 0))),
        out_shape=jax.ShapeDtypeStruct((half, cols), F32),
        compiler_params=_cparams("parallel"),
    )(shard, part, got)


def _join_halves(mine, got, c):
    half, cols = mine.shape
    tr = _pick(half, (256, 128, 64, 32, 16, 8))
    nt = half // tr

    def body(c_ref, a_ref, b_ref, o_ref):
        i = pl.program_id(0)
        own = (i // nt) == c_ref[0]

        @pl.when(own)
        def _():
            o_ref[...] = a_ref[...]

        @pl.when(jnp.logical_not(own))
        def _():
            o_ref[...] = b_ref[...]

    blk = pl.BlockSpec((tr, cols), lambda i, cr: (i % nt, 0))
    return pl.pallas_call(
        body, name="join_halves",
        grid_spec=pltpu.PrefetchScalarGridSpec(num_scalar_prefetch=1, grid=(2 * nt,), in_specs=[blk, blk],
                                               out_specs=pl.BlockSpec((tr, cols), lambda i, cr: (i, 0))),
        out_shape=jax.ShapeDtypeStruct((2 * half, cols), F32),
        compiler_params=_cparams("parallel"),
    )(c, mine, got)


def _all_gather_rows(name, row):
    p = row.shape[1]

    def body(in_ref, out_ref, send, recv, local):
        x, y, c, _ = _place()
        me = 4 * x + 2 * y + c
        own = pltpu.make_async_copy(in_ref, out_ref.at[me], local)
        own.start()
        cps = []
        for k in range(1, 8):
            px, py, pc = x ^ (k >> 2), y ^ ((k >> 1) & 1), c ^ (k & 1)
            cp = _remote(in_ref, out_ref.at[me], send.at[k - 1], recv.at[k - 1], (px, py, pc))
            cp.start()
            cps.append(cp)
        for cp in cps:
            cp.wait_recv()
        for cp in cps:
            cp.wait_send()
        own.wait()

    return pl.pallas_call(
        body, name=name, in_specs=[ANY], out_specs=ANY,
        out_shape=jax.ShapeDtypeStruct((8, 1, p), F32),
        scratch_shapes=[pltpu.SemaphoreType.DMA((7,)), pltpu.SemaphoreType.DMA((7,)), pltpu.SemaphoreType.DMA],
    )(row)


def _sum_devices(rows):
    def body(r_ref, o_ref):
        acc = r_ref[0]
        for k in range(1, 8):
            acc = acc + r_ref[k]
        o_ref[...] = acc

    return pl.pallas_call(body, name="sum_devices", out_shape=jax.ShapeDtypeStruct(rows.shape[1:], F32))(rows)


def _adamw(name, w, g, m, v):
    r, c = w.shape
    tr = _pick(r, (128, 64, 32, 16, 8))
    bc1 = 1.0 - ADAM_B1 ** ADAM_STEP
    bc2 = 1.0 - ADAM_B2 ** ADAM_STEP

    def body(w_ref, g_ref, m_ref, v_ref, d_ref, nm_ref, nv_ref):
        gv = g_ref[...]
        nm = ADAM_B1 * m_ref[...] + (1.0 - ADAM_B1) * gv
        nv = ADAM_B2 * v_ref[...] + (1.0 - ADAM_B2) * (gv * gv)
        d_ref[...] = -ADAM_LR * ((nm / bc1) / (jnp.sqrt(nv / bc2) + ADAM_EPS) + ADAM_WD * w_ref[...])
        nm_ref[...] = nm
        nv_ref[...] = nv

    blk = pl.BlockSpec((tr, c), lambda i: (i, 0))
    sh = jax.ShapeDtypeStruct((r, c), F32)
    return pl.pallas_call(
        body, name=name, grid=(r // tr,), in_specs=[blk] * 4, out_specs=[blk] * 3, out_shape=[sh] * 3,
        compiler_params=_cparams("parallel"),
    )(w, g, m, v)


def _lb_logits_grad(dlb, lb):
    def body(d_ref, lb_ref, o_ref):
        lbv = lb_ref[...]
        t = d_ref[...] * lbv * (1.0 - lbv)
        o_ref[0:1, :] = t
        o_ref[1:2, :] = -t

    return pl.pallas_call(body, name="lb_logits_grad", out_shape=jax.ShapeDtypeStruct((2, dlb.shape[1]), F32))(dlb, lb)


BIG = ("w_in", "p_a", "p_b", "w_o", "w_up", "w_down")
SMALL = ("g_mix", "g_q", "g_k", "lb_logits", "g_hg_out", "g_ffn", "conv_w", "conv_b")
ORDER = ("g_mix", "w_in", "g_q", "g_k", "lb_logits", "g_hg_out", "p_a", "p_b", "w_o", "g_ffn", "w_up", "conv_w", "conv_b", "w_down")


def kernel(x, g_mix, w_in, g_q, g_k, lb_logits, g_hg_out, p_a, p_b, w_o, g_ffn, w_up, conv_w, conv_b, w_down, loss_target, m_g_mix, m_w_in, m_g_q, m_g_k, m_lb_logits, m_g_hg_out, m_p_a, m_p_b, m_w_o, m_g_ffn, m_w_up, m_conv_w, m_conv_b, m_w_down, v_g_mix, v_w_in, v_g_q, v_g_k, v_lb_logits, v_g_hg_out, v_p_a, v_p_b, v_w_o, v_g_ffn, v_w_up, v_conv_w, v_conv_b, v_w_down):
    assert lb_logits.shape[0] == 2, "the lower bound is the first row of a two-row softmax"
    w = dict(g_mix=g_mix, w_in=w_in[0], g_q=g_q, g_k=g_k, lb_logits=lb_logits, g_hg_out=g_hg_out, p_a=p_a[0], p_b=p_b[0],
             w_o=w_o[0], g_ffn=g_ffn, w_up=w_up[0], conv_w=conv_w[0], conv_b=conv_b, w_down=w_down[0])
    mom = dict(g_mix=m_g_mix, w_in=m_w_in[0], g_q=m_g_q, g_k=m_g_k, lb_logits=m_lb_logits, g_hg_out=m_g_hg_out, p_a=m_p_a[0],
               p_b=m_p_b[0], w_o=m_w_o[0], g_ffn=m_g_ffn, w_up=m_w_up[0], conv_w=m_conv_w[0], conv_b=m_conv_b, w_down=m_w_down[0])
    var = dict(g_mix=v_g_mix, w_in=v_w_in[0], g_q=v_g_q, g_k=v_g_k, lb_logits=v_lb_logits, g_hg_out=v_g_hg_out, p_a=v_p_a[0],
               p_b=v_p_b[0], w_o=v_w_o[0], g_ffn=v_g_ffn, w_up=v_w_up[0], conv_w=v_conv_w[0], conv_b=v_conv_b, w_down=v_w_down[0])
    d = x.shape[2]
    cx, cy, cc = lax.axis_index("x"), lax.axis_index("y"), lax.axis_index("c")
    shard = (2 * cx + cy).astype(jnp.int32).reshape(1)
    core = cc.astype(jnp.int32).reshape(1)

    full = dict(zip(BIG, _gather_weights([_cast_bf16("cast_" + n, w[n]) for n in BIG])))
    cw = conv_w.shape[2]
    rows = _all_gather_rows("gather_conv_w", w["conv_w"].reshape(1, 3 * cw))
    conv_full = jnp.concatenate([rows[2 * s, 0].reshape(3, cw) for s in range(N_CHIPS)], axis=1)
    f_w_o = full["w_o"].reshape(1, d, d)
    f_w_down = full["w_down"].reshape(1, -1, d)

    dx, big, small, lb = _local_step(x[0], loss_target[0], g_mix, g_q, g_k, lb_logits, g_hg_out, g_ffn, conv_full, conv_b,
                                     full["w_in"], full["p_a"], full["p_b"], f_w_o, full["w_up"], f_w_down)

    gs = [big[n] for n in BIG]
    got = _exchange_halves("halves_to_sibling", gs, True,
                           [jax.ShapeDtypeStruct((N_CHIPS, g.shape[1] // 2, g.shape[2]), BF16) for g in gs])
    parts = [_add_halves("chip_sum_" + n, g, o, core) for n, g, o in zip(BIG, gs, got)]
    recv = _scatter_partials(parts)
    mine = [_sum_partials("shard_sum_" + n, p, r, shard) for n, p, r in zip(BIG, parts, recv)]
    theirs = _exchange_halves("reduced_to_sibling", mine, False, [jax.ShapeDtypeStruct(a.shape, F32) for a in mine])
    grads = {n: _join_halves(a, b, core) for n, a, b in zip(BIG, mine, theirs)}

    names = ("g_mix", "g_q", "g_k", "lb", "g_hg_out", "g_ffn", "conv_b", "sq")
    packed = jnp.concatenate([small[n] for n in names] + [small["conv_w"].reshape(1, -1)], axis=1)
    total = _sum_devices(_all_gather_rows("gather_small_grads", packed))
    off = 0
    red = {}
    for n in names:
        ln = small[n].shape[1]
        red[n] = total[:, off:off + ln]
        off += ln
    conv_all = total[:, off:].reshape(3, -1)
    loss = 0.5 * jnp.sum(red["sq"]) / d
    grads["conv_w"] = lax.dynamic_slice_in_dim(conv_all, (2 * cx + cy) * cw, cw, axis=1)
    grads["lb_logits"] = _lb_logits_grad(red["lb"], lb)
    for n in ("g_mix", "g_q", "g_k", "g_hg_out", "g_ffn", "conv_b"):
        grads[n] = red[n]

    delta, new_m, new_v = {}, {}, {}
    for n in ORDER:
        delta[n], new_m[n], new_v[n] = _adamw("adamw_" + n, w[n], grads[n], mom[n], var[n])

    def shaped(a, like):
        return a.reshape(like.shape)

    ref_w = dict(g_mix=g_mix, w_in=w_in, g_q=g_q, g_k=g_k, lb_logits=lb_logits, g_hg_out=g_hg_out, p_a=p_a, p_b=p_b, w_o=w_o,
                 g_ffn=g_ffn, w_up=w_up, conv_w=conv_w, conv_b=conv_b, w_down=w_down)
    outs = [loss, dx[None]]
    for group in (grads, delta, new_m, new_v):
        outs += [shaped(group[n], ref_w[n]) for n in ORDER]
    return tuple(outs)
```

```python
import functools

import jax
import jax.numpy as jnp
from jax import lax
from jax.experimental import pallas as pl
from jax.experimental.pallas import tpu as pltpu

F32 = jnp.float32
BF16 = jnp.bfloat16
HEAD = 128
EPS = 1e-6
N_CHIPS = 4
HG_CHUNK = 64
SUB = 8
ADAM_LR, ADAM_B1, ADAM_B2, ADAM_EPS, ADAM_WD, ADAM_STEP = 0.001, 0.9, 0.999, 1e-08, 0.01, 10
VMEM_LIMIT = 56 * 1024 * 1024
MESH = pl.DeviceIdType.MESH

NN = (((1,), (0,)), ((), ()))
NT = (((1,), (1,)), ((), ()))
TN = (((0,), (0,)), ((), ()))


def _cparams(*sem):
    return pltpu.CompilerParams(dimension_semantics=sem if sem else None, vmem_limit_bytes=VMEM_LIMIT)


def _pick(n, cands):
    for c in cands:
        if c <= n and n % c == 0:
            return c
    return n


def _sig(x):
    return 1.0 / (1.0 + jnp.exp(-x))


def _dot(a, b, dims):
    return lax.dot_general(a, b, dims, preferred_element_type=F32)


def _split(x):
    hi = x.astype(BF16)
    lo = (x - hi.astype(F32)).astype(BF16)
    return hi, lo


def _tri(n, kind):
    r = lax.broadcasted_iota(jnp.int32, (n, n), 0)
    c = lax.broadcasted_iota(jnp.int32, (n, n), 1)
    m = {"ge": c >= r, "gt": c > r, "le": c <= r, "lt": c < r}[kind]
    return jnp.where(m, 1.0, 0.0).astype(BF16)


TILE_M = (1024, 512, 256, 128)
TILE_N = (1408, 1024, 512, 256, 128)
TILE_K = (2048, 1408, 1024, 512, 256, 128)


def _matmul(name, a, b, mode, out_dtype, add=None):
    if mode == "nn":
        m, k = a.shape
        g, _, ns = b.shape
        n = g * ns
        tm, tn, tk = _pick(m, TILE_M), _pick(ns, TILE_N), _pick(k, TILE_K)
        nps = ns // tn
        grid = (m // tm, n // tn, k // tk)
        a_spec = pl.BlockSpec((tm, tk), lambda i, j, kk: (i, kk))
        b_spec = pl.BlockSpec((None, tk, tn), lambda i, j, kk: (j // nps, kk, j % nps))
        o_spec = pl.BlockSpec((tm, tn), lambda i, j, kk: (i, j))
        o_shape = jax.ShapeDtypeStruct((m, n), out_dtype)
        dims = NN
    elif mode == "nt":
        m, k = a.shape
        g, n, ks = b.shape
        tm, tn, tk = _pick(m, TILE_M), _pick(n, TILE_N), _pick(ks, TILE_K)
        kps = ks // tk
        grid = (m // tm, n // tn, k // tk)
        a_spec = pl.BlockSpec((tm, tk), lambda i, j, kk: (i, kk))
        b_spec = pl.BlockSpec((None, tn, tk), lambda i, j, kk: (kk // kps, j, kk % kps))
        o_spec = pl.BlockSpec((tm, tn), lambda i, j, kk: (i, j))
        o_shape = jax.ShapeDtypeStruct((m, n), out_dtype)
        dims = NT
    else:
        raise ValueError(mode)
    nk = grid[2]

    def body(*refs):
        a_ref, b_ref = refs[0], refs[1]
        add_ref = refs[2] if add is not None else None
        o_ref = refs[2 + (add is not None)]

        def finish(r):
            if add is not None:
                r = r + add_ref[...]
            o_ref[...] = r.astype(o_ref.dtype)

        part = _dot(a_ref[...].astype(BF16), b_ref[...].astype(BF16), dims)
        if nk == 1:
            finish(part)
            return
        acc = refs[-1]
        kk = pl.program_id(2)

        @pl.when(kk == 0)
        def _():
            acc[...] = part

        @pl.when(kk > 0)
        def _():
            acc[...] += part

        @pl.when(kk == nk - 1)
        def _():
            finish(acc[...])

    in_specs = [a_spec, b_spec]
    args = [a, b]
    if add is not None:
        in_specs.append(o_spec)
        args.append(add)
    return pl.pallas_call(
        body, name=name, grid=grid, in_specs=in_specs, out_specs=o_spec, out_shape=o_shape,
        scratch_shapes=[pltpu.VMEM((tm, tn), F32)] if nk > 1 else [],
        compiler_params=_cparams("parallel", "parallel", "arbitrary"),
    )(*args)


def _matmul_tn(name, a, b, g, out_dtype):
    k, m = a.shape
    _, n = b.shape
    ns = n // g
    tm, tn, tk = _pick(m, (2048, 1408) + TILE_M), _pick(ns, TILE_N), _pick(k, (1024, 512, 256, 128))
    nps = ns // tn
    nk = k // tk

    def body(a_ref, b_ref, o_ref, acc):
        kk = pl.program_id(2)
        part = _dot(a_ref[...].astype(BF16), b_ref[...].astype(BF16), TN)

        @pl.when(kk == 0)
        def _():
            acc[...] = part

        @pl.when(kk > 0)
        def _():
            acc[...] += part

        @pl.when(kk == nk - 1)
        def _():
            o_ref[...] = acc[...].astype(o_ref.dtype)

    return pl.pallas_call(
        body, name=name, grid=(m // tm, n // tn, nk),
        in_specs=[pl.BlockSpec((tk, tm), lambda i, j, kk: (kk, i)), pl.BlockSpec((tk, tn), lambda i, j, kk: (kk, j))],
        out_specs=pl.BlockSpec((None, tm, tn), lambda i, j, kk: (j // nps, i, j % nps)),
        out_shape=jax.ShapeDtypeStruct((g, m, ns), out_dtype),
        scratch_shapes=[pltpu.VMEM((tm, tn), F32)],
        compiler_params=_cparams("parallel", "parallel", "arbitrary"),
    )(a, b)


def _rmsnorm_fwd(name, x, g):
    s, d = x.shape
    tr = _pick(s, (256, 128, 64, 32, 16, 8))

    def body(x_ref, g_ref, u_ref):
        xv = x_ref[...]
        r = lax.rsqrt(jnp.mean(xv * xv, axis=-1, keepdims=True) + EPS)
        u_ref[...] = (xv * r * g_ref[...]).astype(u_ref.dtype)

    return pl.pallas_call(
        body, name=name, grid=(s // tr,),
        in_specs=[pl.BlockSpec((tr, d), lambda i: (i, 0)), pl.BlockSpec((1, d), lambda i: (0, 0))],
        out_specs=pl.BlockSpec((tr, d), lambda i: (i, 0)),
        out_shape=jax.ShapeDtypeStruct((s, d), BF16),
        compiler_params=_cparams("parallel"),
    )(x, g)


def _rmsnorm_bwd(name, x, g, du, extra):
    s, d = x.shape
    tr = _pick(s, (256, 128, 64, 32, 16, 8))

    def body(x_ref, g_ref, du_ref, e_ref, dx_ref, dg_ref):
        i = pl.program_id(0)
        xv = x_ref[...]
        r = lax.rsqrt(jnp.mean(xv * xv, axis=-1, keepdims=True) + EPS)
        n = xv * r
        dy = du_ref[...]
        a = dy * g_ref[...]
        dx = r * (a - n * jnp.mean(a * n, axis=-1, keepdims=True))
        dx_ref[...] = e_ref[...] + dx

        @pl.when(i == 0)
        def _():
            dg_ref[...] = jnp.zeros_like(dg_ref)

        dg_ref[...] += jnp.sum((dy * n).reshape(tr // SUB, SUB, d), axis=0)

    return pl.pallas_call(
        body, name=name, grid=(s // tr,),
        in_specs=[pl.BlockSpec((tr, d), lambda i: (i, 0)), pl.BlockSpec((1, d), lambda i: (0, 0)),
                  pl.BlockSpec((tr, d), lambda i: (i, 0)), pl.BlockSpec((tr, d), lambda i: (i, 0))],
        out_specs=[pl.BlockSpec((tr, d), lambda i: (i, 0)), pl.BlockSpec((SUB, d), lambda i: (0, 0))],
        out_shape=[jax.ShapeDtypeStruct((s, d), F32), jax.ShapeDtypeStruct((SUB, d), F32)],
        compiler_params=_cparams("arbitrary"),
    )(x, g, du, extra)


def _head_norm(x, g):
    r = lax.rsqrt(jnp.mean(x * x, axis=-1, keepdims=True) + EPS)
    return x * r * g


def _qk_norm_fwd(proj, g_q, g_k, nh, blk):
    s = proj.shape[0]

    def body(q_ref, k_ref, v_ref, gq_ref, gk_ref, qn_ref, kn_ref, vb_ref, kt_ref, vt_ref):
        qn_ref[...] = _head_norm(q_ref[...], gq_ref[...]).astype(BF16)
        kn = _head_norm(k_ref[...], gk_ref[...])
        kn_ref[...] = kn.astype(BF16)
        kt_ref[...] = kn.T.astype(BF16)
        v = v_ref[...]
        vb_ref[...] = v.astype(BF16)
        vt_ref[...] = v.T.astype(BF16)

    col = lambda base: pl.BlockSpec((blk, HEAD), lambda i, h: (i, base + h))
    gs = pl.BlockSpec((1, HEAD), lambda i, h: (0, 0))
    o = pl.BlockSpec((blk, HEAD), lambda i, h: (i, h))
    t = pl.BlockSpec((None, HEAD, blk), lambda i, h: (i, h, 0))
    sh = jax.ShapeDtypeStruct((s, nh * HEAD), BF16)
    tsh = jax.ShapeDtypeStruct((s // blk, nh * HEAD, blk), BF16)
    return pl.pallas_call(
        body, name="qk_norm_fwd", grid=(s // blk, nh),
        in_specs=[col(0), col(nh), col(2 * nh), gs, gs], out_specs=[o, o, o, t, t], out_shape=[sh, sh, sh, tsh, tsh],
        compiler_params=_cparams("parallel", "parallel"),
    )(proj, proj, proj, g_q, g_k)


def _qk_norm_bwd(proj, g_q, g_k, dqn, dkn, nh):
    s = proj.shape[0]
    tr = _pick(s, (512, 256, 128, 64, 32, 16, 8))

    def one(x, g, dy):
        r = lax.rsqrt(jnp.mean(x * x, axis=-1, keepdims=True) + EPS)
        n = x * r
        a = dy * g
        dx = r * (a - n * jnp.mean(a * n, axis=-1, keepdims=True))
        return dx, jnp.sum((dy * n).reshape(tr // SUB, SUB, HEAD), axis=0)

    def body(q_ref, k_ref, gq_ref, gk_ref, dqn_ref, dkn_ref, dq_ref, dk_ref, dgq_ref, dgk_ref):
        first = (pl.program_id(0) == 0) & (pl.program_id(1) == 0)

        @pl.when(first)
        def _():
            dgq_ref[...] = jnp.zeros_like(dgq_ref)
            dgk_ref[...] = jnp.zeros_like(dgk_ref)

        dq, pq = one(q_ref[...], gq_ref[...], dqn_ref[...])
        dk, pk = one(k_ref[...], gk_ref[...], dkn_ref[...])
        dq_ref[...] = dq.astype(BF16)
        dk_ref[...] = dk.astype(BF16)
        dgq_ref[...] += pq
        dgk_ref[...] += pk

    col = lambda base: pl.BlockSpec((tr, HEAD), lambda i, h: (i, base + h))
    gs = pl.BlockSpec((1, HEAD), lambda i, h: (0, 0))
    o = pl.BlockSpec((tr, HEAD), lambda i, h: (i, h))
    part = pl.BlockSpec((SUB, HEAD), lambda i, h: (0, 0))
    sh = jax.ShapeDtypeStruct((s, nh * HEAD), BF16)
    psh = jax.ShapeDtypeStruct((SUB, HEAD), F32)
    return pl.pallas_call(
        body, name="qk_norm_bwd", grid=(s // tr, nh),
        in_specs=[col(0), col(nh), gs, gs, o, o], out_specs=[o, o, part, part], out_shape=[sh, sh, psh, psh],
        compiler_params=_cparams("arbitrary", "arbitrary"),
    )(proj, proj, g_q, g_k, dqn, dkn)


def _sb_consts(blk, hp):
    upper = _tri(blk, "ge")
    row = lax.broadcasted_iota(jnp.int32, (blk, hp * blk), 0)
    col = lax.broadcasted_iota(jnp.int32, (blk, hp * blk), 1)
    strict = row < col
    for h in range(1, hp):
        strict = strict & ((col < h * blk) | (row < col - h * blk))
    return jnp.concatenate([upper, upper], axis=1), strict


def _sb_log_keep(zt, strict):
    l = jnp.minimum(-zt, 0.0) - jnp.log(1.0 + jnp.exp(-jnp.abs(zt)))
    return l if strict is None else jnp.where(strict, l, 0.0)


SB_GROUP = 4
SB_GROUP_BWD = 2


def _sb_heads(nh):
    return 2 if nh % 2 == 0 else 1


def _sb_attn_fwd(qn, kn, vt, nh, blk):
    s = qn.shape[0]
    nb = s // blk
    scale = HEAD ** -0.5
    hp = _sb_heads(nh)

    def body(q_ref, k_ref, vt_ref, y_ref, c_ref):
        qi = pl.program_id(1)
        suffix, strict = _sb_consts(blk, hp)
        qs = [q_ref[:, h * HEAD:(h + 1) * HEAD] for h in range(hp)]

        def logits(kb_i):
            off = pl.multiple_of(kb_i * blk, blk)
            return jnp.concatenate(
                [_dot(k_ref[pl.ds(off, blk), h * HEAD:(h + 1) * HEAD], qs[h], NT) for h in range(hp)], axis=1) * scale

        def sums(zt, mask):
            l = _sb_log_keep(zt, mask)
            parts = []
            for h in range(hp):
                hi, lo = _split(l[:, h * blk:(h + 1) * blk])
                parts.append(_dot(suffix, jnp.concatenate([hi, lo], axis=0), NN))
            return jnp.concatenate(parts, axis=1)

        def weights(kb_i, zt, cum, cr, mask):
            for h in range(hp):
                c_ref[h, kb_i] = cr[:, h * blk:(h + 1) * blk]
            wt = jnp.exp(zt + cum + cr)
            if mask is not None:
                wt = jnp.where(mask, wt, 0.0)
            return wt.astype(BF16), cr + cum[0:1, :]

        def add_values(kb_i, wt, accs):
            return tuple(
                accs[h] + _dot(vt_ref[kb_i, h * HEAD:(h + 1) * HEAD, :], wt[:, h * blk:(h + 1) * blk], NN)
                for h in range(hp))

        def group(kbs, masks, accs, cr):
            zts = [logits(k) for k in kbs]
            cums = [sums(zt, m) for zt, m in zip(zts, masks)]
            for k, zt, cum, m in zip(kbs, zts, cums, masks):
                wt, cr = weights(k, zt, cum, cr, m)
                accs = add_values(k, wt, accs)
            return accs, cr

        accs = tuple(jnp.zeros((HEAD, blk), F32) for _ in range(hp))
        accs, cr = group([qi], [strict], accs, jnp.zeros((1, hp * blk), F32))
        n_groups = qi // SB_GROUP

        def many(g, st):
            top = qi - 1 - g * SB_GROUP
            return group([top - j for j in range(SB_GROUP)], [None] * SB_GROUP, *st)

        def one(r, st):
            return group([qi - 1 - n_groups * SB_GROUP - r], [None], *st)

        st = lax.fori_loop(0, n_groups, many, (accs, cr))
        accs, _ = lax.fori_loop(0, qi - n_groups * SB_GROUP, one, st)
        for h in range(hp):
            y_ref[:, h * HEAD:(h + 1) * HEAD] = accs[h].T.astype(y_ref.dtype)

    qs_ = pl.BlockSpec((blk, hp * HEAD), lambda h, i: (i, h))
    full = pl.BlockSpec((s, hp * HEAD), lambda h, i: (0, h))
    return pl.pallas_call(
        body, name="sb_attn_fwd", grid=(nh // hp, nb),
        in_specs=[qs_, full, pl.BlockSpec((nb, hp * HEAD, blk), lambda h, i: (0, h, 0))],
        out_specs=[qs_, pl.BlockSpec((hp, nb, 1, blk), lambda h, i: (h, 0, 0, i))],
        out_shape=[jax.ShapeDtypeStruct((s, nh * HEAD), BF16), jax.ShapeDtypeStruct((nh, nb, 1, s), F32)],
        compiler_params=_cparams("parallel", "arbitrary"),
    )(qn, kn, vt)


def _sb_attn_bwd(qn, kn, kt, vb, dy, carries, nh, blk):
    s = qn.shape[0]
    nb = s // blk
    scale = HEAD ** -0.5
    hp = _sb_heads(nh)

    def body(q_ref, k_ref, kt_ref, v_ref, dy_ref, c_ref, dq_ref, dk_ref, dv_ref):
        qi = pl.program_id(1)

        @pl.when(qi == 0)
        def _():
            dk_ref[...] = jnp.zeros_like(dk_ref)
            dv_ref[...] = jnp.zeros_like(dv_ref)

        suffix, strict = _sb_consts(blk, hp)
        prefix = _tri(blk, "lt")
        qs = [q_ref[:, h * HEAD:(h + 1) * HEAD] for h in range(hp)]
        dos = [dy_ref[:, h * HEAD:(h + 1) * HEAD].astype(BF16) for h in range(hp)]

        def logits(kb_i):
            off = pl.multiple_of(kb_i * blk, blk)
            return jnp.concatenate(
                [_dot(k_ref[pl.ds(off, blk), h * HEAD:(h + 1) * HEAD], qs[h], NT) for h in range(hp)], axis=1) * scale

        def group(kbs, masks, dqs, ec):
            rows = [pl.ds(pl.multiple_of(k * blk, blk), blk) for k in kbs]
            zts = [logits(k) for k in kbs]
            dws = [jnp.concatenate([_dot(v_ref[r, h * HEAD:(h + 1) * HEAD], dos[h], NT) for h in range(hp)], axis=1)
                   for r in rows]
            ls = [_sb_log_keep(zt, m) for zt, m in zip(zts, masks)]
            cums = []
            for l in ls:
                parts = []
                for h in range(hp):
                    hi, lo = _split(l[:, h * blk:(h + 1) * blk])
                    parts.append(_dot(suffix, jnp.concatenate([hi, lo], axis=0), NN))
                cums.append(jnp.concatenate(parts, axis=1))
            wts, ets, befores = [], [], []
            for k, zt, cum, dw, m in zip(kbs, zts, cums, dws, masks):
                cr = jnp.concatenate([c_ref[h, k] for h in range(hp)], axis=1)
                wt = jnp.exp(zt + cum + cr)
                if m is not None:
                    wt = jnp.where(m, wt, 0.0)
                et = wt * dw
                befores.append(_dot(prefix, et.astype(BF16), NN) + ec)
                ec = ec + jnp.sum(et, axis=0, keepdims=True)
                wts.append(wt.astype(BF16))
                ets.append(et)
            for k, r, l, et, before, wtb in zip(kbs, rows, ls, ets, befores, wts):
                dzt = ((jnp.exp(l) * (et + before) - before) * scale).astype(BF16)
                out = []
                for h in range(hp):
                    cols, part = slice(h * HEAD, (h + 1) * HEAD), slice(h * blk, (h + 1) * blk)
                    dv_ref[r, cols] += _dot(wtb[:, part], dos[h], NN)
                    dk_ref[r, cols] += _dot(dzt[:, part], qs[h], NN)
                    out.append(dqs[h] + _dot(kt_ref[k, cols, :], dzt[:, part], NN))
                dqs = tuple(out)
            return dqs, ec

        n_groups = qi // SB_GROUP_BWD

        def many(g, st):
            return group([g * SB_GROUP_BWD + j for j in range(SB_GROUP_BWD)], [None] * SB_GROUP_BWD, *st)

        def one(r, st):
            return group([n_groups * SB_GROUP_BWD + r], [None], *st)

        st = (tuple(jnp.zeros((HEAD, blk), F32) for _ in range(hp)), jnp.zeros((1, hp * blk), F32))
        st = lax.fori_loop(0, n_groups, many, st)
        st = lax.fori_loop(0, qi - n_groups * SB_GROUP_BWD, one, st)
        dqs, _ = group([qi], [strict], *st)
        for h in range(hp):
            dq_ref[:, h * HEAD:(h + 1) * HEAD] = dqs[h].T

    qs_ = pl.BlockSpec((blk, hp * HEAD), lambda h, i: (i, h))
    full = pl.BlockSpec((s, hp * HEAD), lambda h, i: (0, h), pipeline_mode=pl.Buffered(1))
    sh = jax.ShapeDtypeStruct((s, nh * HEAD), F32)
    return pl.pallas_call(
        body, name="sb_attn_bwd", grid=(nh // hp, nb),
        in_specs=[qs_, full, pl.BlockSpec((nb, hp * HEAD, blk), lambda h, i: (0, h, 0), pipeline_mode=pl.Buffered(1)),
                  full, qs_, pl.BlockSpec((hp, nb, 1, blk), lambda h, i: (h, 0, 0, i))],
        out_specs=[qs_, full, full], out_shape=[sh, sh, sh],
        compiler_params=_cparams("parallel", "arbitrary"),
    )(qn, kn, kt, vb, dy, carries)


def _lower_bound(lb_logits):
    def body(l_ref, o_ref):
        l = l_ref[...]
        m = jnp.max(l, axis=0, keepdims=True)
        e = jnp.exp(l - m)
        o_ref[...] = e[0:1, :] / jnp.sum(e, axis=0, keepdims=True)

    return pl.pallas_call(body, name="lower_bound", out_shape=jax.ShapeDtypeStruct((1, lb_logits.shape[1]), F32))(lb_logits)


def _hg_gates(hq, hf, lb):
    sq = _sig(hq)
    q = hq * sq
    sf = _sig(hf)
    f = lb + (1.0 - lb) * sf
    return q, sq, f, sf


def _hg_cum(g, c):
    hi, lo = _split(g)
    t = _tri(c, "le")
    return _dot(t, hi, NN) + _dot(t, lo, NN)


def _row_mask(r):
    return lax.broadcasted_iota(jnp.int32, (SUB, HEAD), 0) >= r


def _hg_intra_fwd(q, k, v, b, c):
    outs = []
    for bi in range(c // SUB):
        q_i, b_i = q[bi * SUB:(bi + 1) * SUB], b[bi * SUB:(bi + 1) * SUB]
        acc = jnp.zeros((SUB, HEAD), F32)
        for s in range((bi + 1) * SUB):
            d = b_i - b[s:s + 1]
            if s >= bi * SUB:
                d = jnp.where(_row_mask(s - bi * SUB), d, -jnp.inf)
            col = jnp.sum(q_i * k[s:s + 1] * jnp.exp(d), axis=1, keepdims=True)
            acc = acc + col * v[s:s + 1]
        outs.append(acc)
    return jnp.concatenate(outs, axis=0)


def _hg_intra_bwd(q, k, v, b, do, c, dq_scr, dk_scr, dv_scr):
    nblk = c // SUB
    dq_scr[...] = jnp.zeros_like(dq_scr)
    for s in range(c):
        bj = s // SUB
        ks, vs, bs = k[s:s + 1], v[s:s + 1], b[s:s + 1]
        acc_k = jnp.zeros((SUB, HEAD), F32)
        acc_v = jnp.zeros((SUB, HEAD), F32)
        for bi in range(bj, nblk):
            sl = slice(bi * SUB, (bi + 1) * SUB)
            d = b[sl] - bs
            if bi == bj:
                d = jnp.where(_row_mask(s - bj * SUB), d, -jnp.inf)
            dec = jnp.exp(d)
            qd = q[sl] * dec
            col = jnp.sum(qd * ks, axis=1, keepdims=True)
            dcol = jnp.sum(do[sl] * vs, axis=1, keepdims=True)
            dq_scr[sl, :] += dcol * (ks * dec)
            acc_k = acc_k + dcol * qd
            acc_v = acc_v + col * do[sl]
        dk_scr[s:s + 1, :] = jnp.sum(acc_k, axis=0, keepdims=True)
        dv_scr[s:s + 1, :] = jnp.sum(acc_v, axis=0, keepdims=True)


def _hgrn2_fwd(proj, lb, g_out, nh, base, c):
    s = proj.shape[0]
    nch = s // c

    def body(hq_ref, hf_ref, hi_ref, og_ref, lb_ref, g_ref, o_ref, y_ref, st_ref, st):
        @pl.when(pl.program_id(1) == 0)
        def _():
            st[...] = jnp.zeros_like(st)

        st_in = st[...]
        st_ref[...] = st_in
        q, _, f, _ = _hg_gates(hq_ref[...], hf_ref[...], lb_ref[...])
        k = 1.0 - f
        v = hi_ref[...]
        b = _hg_cum(jnp.log(f), c)
        bl = b[c - 1:c, :]
        o = _dot((q * jnp.exp(b)).astype(BF16), st_in.astype(BF16), NT) + _hg_intra_fwd(q, k, v, b, c)
        kd = k * jnp.exp(bl - b)
        st[...] = st_in * jnp.exp(bl) + _dot(v.astype(BF16), kd.astype(BF16), TN)
        o_ref[...] = o
        og = og_ref[...]
        y_ref[...] = (_head_norm(o, g_ref[...]) * (og * _sig(og))).astype(BF16)

    col = lambda j: pl.BlockSpec((c, HEAD), lambda h, i: (i, base + j * nh + h))
    row = pl.BlockSpec((1, HEAD), lambda h, i: (0, h))
    gs = pl.BlockSpec((1, HEAD), lambda h, i: (0, 0))
    o = pl.BlockSpec((c, HEAD), lambda h, i: (i, h))
    return pl.pallas_call(
        body, name="hgrn2_fwd", grid=(nh, nch),
        in_specs=[col(0), col(1), col(2), col(3), row, gs],
        out_specs=[o, o, pl.BlockSpec((None, None, HEAD, HEAD), lambda h, i: (h, i, 0, 0))],
        out_shape=[jax.ShapeDtypeStruct((s, nh * HEAD), F32), jax.ShapeDtypeStruct((s, nh * HEAD), BF16),
                   jax.ShapeDtypeStruct((nh, nch, HEAD, HEAD), F32)],
        scratch_shapes=[pltpu.VMEM((HEAD, HEAD), F32)],
        compiler_params=_cparams("parallel", "arbitrary"),
    )(proj, proj, proj, proj, lb, g_out)


def _hgrn2_bwd(proj, lb, g_out, o_pre, states, dy, nh, base, c):
    s = proj.shape[0]
    nch = s // c

    def body(hq_ref, hf_ref, hi_ref, og_ref, lb_ref, g_ref, o_ref, st_ref, se_ref, dy_ref,
             dhq_ref, dhf_ref, dhi_ref, dog_ref, dg_ref, dlb_ref, dst, dq_scr, dk_scr, dv_scr):
        h, i = pl.program_id(0), pl.program_id(1)

        @pl.when(i == 0)
        def _():
            dst[...] = jnp.zeros_like(dst)
            dlb_ref[...] = jnp.zeros_like(dlb_ref)

        @pl.when((i == 0) & (h == 0))
        def _():
            dg_ref[...] = jnp.zeros_like(dg_ref)

        lbv = lb_ref[...]
        hq, hf = hq_ref[...], hf_ref[...]
        q, sq, f, sf = _hg_gates(hq, hf, lbv)
        k = 1.0 - f
        v = hi_ref[...]
        b = _hg_cum(jnp.log(f), c)
        bl = b[c - 1:c, :]
        eb = jnp.exp(b)
        ebl = jnp.exp(bl - b)

        o = o_ref[...]
        gout = g_ref[...]
        og = og_ref[...]
        sg = _sig(og)
        r = lax.rsqrt(jnp.mean(o * o, axis=-1, keepdims=True) + EPS)
        n = o * r
        dyv = dy_ref[...]
        dn = dyv * (og * sg)
        dog_ref[...] = (dyv * n * gout * (sg * (1.0 + og * (1.0 - sg)))).astype(BF16)
        dg_ref[...] += jnp.sum((dn * n).reshape(c // SUB, SUB, HEAD), axis=0)
        a = dn * gout
        do = r * (a - n * jnp.mean(a * n, axis=-1, keepdims=True))

        st_in = st_ref[...]
        dstv = dst[...]
        dob = do.astype(BF16)
        dstb = dstv.astype(BF16)
        _hg_intra_bwd(q, k, v, b, do, c, dq_scr, dk_scr, dv_scr)
        dq = dq_scr[...] + eb * _dot(dob, st_in.astype(BF16), NN)
        dk = dk_scr[...] + ebl * _dot(v.astype(BF16), dstb, NN)
        dv = dv_scr[...] + _dot((k * ebl).astype(BF16), dstb, NT)
        dst[...] = dstv * jnp.exp(bl) + _dot(dob, (q * eb).astype(BF16), TN)

        hi_, lo_ = _split(q * dq - k * dk)
        rev = _tri(c, "ge")
        later = jnp.where(i > 0, jnp.sum(dstv * se_ref[...], axis=0, keepdims=True), 0.0)
        dg = _dot(rev, hi_, NN) + _dot(rev, lo_, NN) + later
        df = dg / f - dk
        dhq_ref[...] = (dq * (sq * (1.0 + hq * (1.0 - sq)))).astype(BF16)
        dhf_ref[...] = (df * (1.0 - lbv) * sf * (1.0 - sf)).astype(BF16)
        dhi_ref[...] = dv.astype(BF16)
        dlb_ref[...] += jnp.sum((df * (1.0 - sf)).reshape(c // SUB, SUB, HEAD), axis=0)

    rv = lambda i: nch - 1 - i
    col = lambda j: pl.BlockSpec((c, HEAD), lambda h, i: (rv(i), base + j * nh + h))
    row = pl.BlockSpec((1, HEAD), lambda h, i: (0, h))
    gs = pl.BlockSpec((1, HEAD), lambda h, i: (0, 0))
    o = pl.BlockSpec((c, HEAD), lambda h, i: (rv(i), h))
    st = pl.BlockSpec((None, None, HEAD, HEAD), lambda h, i: (h, rv(i), 0, 0))
    se = pl.BlockSpec((None, None, HEAD, HEAD), lambda h, i: (h, jnp.minimum(rv(i) + 1, nch - 1), 0, 0))
    sh = jax.ShapeDtypeStruct((s, nh * HEAD), BF16)
    return pl.pallas_call(
        body, name="hgrn2_bwd", grid=(nh, nch),
        in_specs=[col(0), col(1), col(2), col(3), row, gs, o, st, se, o],
        out_specs=[o, o, o, o, pl.BlockSpec((SUB, HEAD), lambda h, i: (0, 0)), pl.BlockSpec((SUB, HEAD), lambda h, i: (0, h))],
        out_shape=[sh, sh, sh, sh, jax.ShapeDtypeStruct((SUB, HEAD), F32), jax.ShapeDtypeStruct((SUB, nh * HEAD), F32)],
        scratch_shapes=[pltpu.VMEM((HEAD, HEAD), F32), pltpu.VMEM((c, HEAD), F32), pltpu.VMEM((c, HEAD), F32),
                        pltpu.VMEM((c, HEAD), F32)],
        compiler_params=_cparams("arbitrary", "arbitrary"),
    )(proj, proj, proj, proj, lb, g_out, o_pre, states, states, dy)


def _merge_tiles(s, d, gate_col):
    tr = _pick(s, (256, 128, 64, 32, 16, 8))
    tc = 128
    for cand in (512, 256):
        if d % cand == 0 and gate_col % cand == 0:
            tc = cand
            break
    return tr, tc


def _merge_fwd(proj, ya, yb, gate_col):
    s, d = ya.shape
    tr, tc = _merge_tiles(s, d, gate_col)
    ga0, gb0 = gate_col // tc, (gate_col + d) // tc

    def body(ga_ref, gb_ref, ya_ref, yb_ref, m_ref):
        m_ref[...] = (_sig(ga_ref[...]) * ya_ref[...] + _sig(gb_ref[...]) * yb_ref[...]).astype(BF16)

    o = pl.BlockSpec((tr, tc), lambda i, j: (i, j))
    return pl.pallas_call(
        body, name="merge_fwd", grid=(s // tr, d // tc),
        in_specs=[pl.BlockSpec((tr, tc), lambda i, j: (i, ga0 + j)), pl.BlockSpec((tr, tc), lambda i, j: (i, gb0 + j)), o, o],
        out_specs=o, out_shape=jax.ShapeDtypeStruct((s, d), BF16),
        compiler_params=_cparams("parallel", "parallel"),
    )(proj, proj, ya, yb)


def _merge_bwd(proj, ya, yb, dm, gate_col):
    s, d = ya.shape
    tr, tc = _merge_tiles(s, d, gate_col)
    ga0, gb0 = gate_col // tc, (gate_col + d) // tc

    def body(ga_ref, gb_ref, ya_ref, yb_ref, dm_ref, dya_ref, dyb_ref, dga_ref, dgb_ref):
        dmv = dm_ref[...]
        sa, sb = _sig(ga_ref[...]), _sig(gb_ref[...])
        dya_ref[...] = (dmv * sa).astype(BF16)
        dyb_ref[...] = (dmv * sb).astype(BF16)
        dga_ref[...] = (dmv * ya_ref[...] * sa * (1.0 - sa)).astype(BF16)
        dgb_ref[...] = (dmv * yb_ref[...] * sb * (1.0 - sb)).astype(BF16)

    o = pl.BlockSpec((tr, tc), lambda i, j: (i, j))
    sh = jax.ShapeDtypeStruct((s, d), BF16)
    return pl.pallas_call(
        body, name="merge_bwd", grid=(s // tr, d // tc),
        in_specs=[pl.BlockSpec((tr, tc), lambda i, j: (i, ga0 + j)), pl.BlockSpec((tr, tc), lambda i, j: (i, gb0 + j)), o, o, o],
        out_specs=[o, o, o, o], out_shape=[sh, sh, sh, sh],
        compiler_params=_cparams("parallel", "parallel"),
    )(proj, proj, ya, yb, dm)


CONV_ROWS = 512


def _conv_ext(ref, i, rows, s, before, after):
    parts = []
    if before:
        p = ref[pl.ds(pl.multiple_of(jnp.maximum(i * rows - before, 0), SUB), before), :]
        parts.append(jnp.where(i > 0, p, 0.0))
    parts.append(ref[pl.ds(pl.multiple_of(i * rows, SUB), rows), :])
    if after:
        nxt = ref[pl.ds(pl.multiple_of(jnp.minimum((i + 1) * rows, s - after), SUB), after), :]
        parts.append(jnp.where((i + 1) * rows < s, nxt, 0.0))
    return jnp.concatenate(parts, axis=0)


def _conv3(ext, w, bias):
    x1 = pltpu.roll(ext, 1, 0)
    x2 = pltpu.roll(ext, 2, 0)
    return bias + w[0:1, :] * x2 + w[1:2, :] * x1 + w[2:3, :] * ext, x1, x2


def _convffn_fwd(up, conv_w, conv_b, dff):
    s = up.shape[0]
    tc = HEAD
    nf = dff // tc
    rows = _pick(s, (CONV_ROWS, 256, 128, 64, 32, 16, 8))

    def body(ug_ref, uv_ref, wg_ref, wv_ref, bg_ref, bv_ref, a_ref):
        wg, wv, bg, bv = wg_ref[...], wv_ref[...], bg_ref[...], bv_ref[...]

        def step(i, _):
            g = _conv3(_conv_ext(ug_ref, i, rows, s, SUB, 0), wg, bg)[0][SUB:]
            v = _conv3(_conv_ext(uv_ref, i, rows, s, SUB, 0), wv, bv)[0][SUB:]
            a_ref[pl.ds(pl.multiple_of(i * rows, SUB), rows), :] = (g * _sig(g) * v).astype(BF16)
            return 0

        lax.fori_loop(0, s // rows, step, 0)

    cg = lambda r: pl.BlockSpec((r, tc), lambda j: (0, j))
    cv = lambda r: pl.BlockSpec((r, tc), lambda j: (0, nf + j))
    return pl.pallas_call(
        body, name="convffn_fwd", grid=(nf,),
        in_specs=[cg(s), cv(s), cg(3), cv(3), cg(1), cv(1)], out_specs=cg(s),
        out_shape=jax.ShapeDtypeStruct((s, dff), BF16),
        compiler_params=_cparams("parallel"),
    )(up, up, conv_w, conv_w, conv_b, conv_b)


def _convffn_bwd(up, conv_w, conv_b, dact, dff):
    s = up.shape[0]
    tc = HEAD
    nf = dff // tc
    rows = _pick(s, (CONV_ROWS, 256, 128, 64, 32, 16, 8))
    n_ext = rows + SUB

    def body(ug_ref, uv_ref, wg_ref, wv_ref, bg_ref, bv_ref, da_ref,
             dug_ref, duv_ref, dwg_ref, dwv_ref, dbg_ref, dbv_ref):
        wg, wv, bg, bv = wg_ref[...], wv_ref[...], bg_ref[...], bv_ref[...]

        def fold(x):
            return jnp.sum(x.reshape(rows // SUB, SUB, tc), axis=0)

        def one(ext, x1, x2, d_ext, w):
            d1 = pltpu.roll(d_ext, n_ext - 1, 0)[:rows]
            d2 = pltpu.roll(d_ext, n_ext - 2, 0)[:rows]
            dc = d_ext[:rows]
            du = w[2:3, :] * dc + w[1:2, :] * d1 + w[0:1, :] * d2
            sl = slice(SUB, SUB + rows)
            return du, (fold(dc * x2[sl]), fold(dc * x1[sl]), fold(dc * ext[sl]), fold(dc))

        def step(i, acc):
            eg = _conv_ext(ug_ref, i, rows, s, SUB, SUB)
            ev = _conv_ext(uv_ref, i, rows, s, SUB, SUB)
            g, g1, g2 = _conv3(eg, wg, bg)
            v, v1, v2 = _conv3(ev, wv, bv)
            g, v = g[SUB:], v[SUB:]
            da = _conv_ext(da_ref, i, rows, s, 0, SUB)
            sg = _sig(g)
            dg = da * v * (sg * (1.0 + g * (1.0 - sg)))
            dv = da * (g * sg)
            dug, pg = one(eg, g1, g2, dg, wg)
            duv, pv = one(ev, v1, v2, dv, wv)
            at = pl.ds(pl.multiple_of(i * rows, SUB), rows)
            dug_ref[at, :] = dug.astype(BF16)
            duv_ref[at, :] = duv.astype(BF16)
            return tuple(a + p for a, p in zip(acc, pg + pv))

        zero = jnp.zeros((SUB, tc), F32)
        acc = lax.fori_loop(0, s // rows, step, (zero,) * 8)
        red = [jnp.sum(a, axis=0, keepdims=True) for a in acc]
        for j in range(3):
            dwg_ref[j:j + 1, :] = red[j]
            dwv_ref[j:j + 1, :] = red[4 + j]
        dbg_ref[...] = red[3]
        dbv_ref[...] = red[7]

    cg = lambda r: pl.BlockSpec((r, tc), lambda j: (0, j))
    cv = lambda r: pl.BlockSpec((r, tc), lambda j: (0, nf + j))
    outs = pl.pallas_call(
        body, name="convffn_bwd", grid=(nf,),
        in_specs=[cg(s), cv(s), cg(3), cv(3), cg(1), cv(1), cg(s)],
        out_specs=[cg(s), cg(s), cg(3), cg(3), cg(1), cg(1)],
        out_shape=[jax.ShapeDtypeStruct((s, dff), BF16), jax.ShapeDtypeStruct((s, dff), BF16),
                   jax.ShapeDtypeStruct((3, dff), F32), jax.ShapeDtypeStruct((3, dff), F32),
                   jax.ShapeDtypeStruct((1, dff), F32), jax.ShapeDtypeStruct((1, dff), F32)],
        compiler_params=_cparams("parallel"),
    )(up, up, conv_w, conv_w, conv_b, conv_b, dact)
    return outs


def _loss_head(out, target):
    s, d = out.shape
    tr = _pick(s, (256, 128, 64, 32, 16, 8))

    def body(o_ref, t_ref, d_ref, l_ref):
        @pl.when(pl.program_id(0) == 0)
        def _():
            l_ref[...] = jnp.zeros_like(l_ref)

        err = o_ref[...] - t_ref[...]
        d_ref[...] = err * (1.0 / d)
        sq = jnp.sum((err * err).reshape(tr // SUB, SUB, d), axis=0)
        part = sq[:, 0:HEAD]
        for j in range(1, d // HEAD):
            part = part + sq[:, j * HEAD:(j + 1) * HEAD]
        l_ref[...] += part

    blk = pl.BlockSpec((tr, d), lambda i: (i, 0))
    return pl.pallas_call(
        body, name="loss_head", grid=(s // tr,), in_specs=[blk, blk],
        out_specs=[blk, pl.BlockSpec((SUB, HEAD), lambda i: (0, 0))],
        out_shape=[jax.ShapeDtypeStruct((s, d), F32), jax.ShapeDtypeStruct((SUB, HEAD), F32)],
        compiler_params=_cparams("arbitrary"),
    )(out, target)


def _sum_rows(name, parts):
    def body(p_ref, o_ref):
        o_ref[...] = jnp.sum(p_ref[...], axis=0, keepdims=True)

    return pl.pallas_call(body, name=name, out_shape=jax.ShapeDtypeStruct((1, parts.shape[1]), F32))(parts)


def _local_step(x, target, g_mix, g_q, g_k, lb_logits, g_hg_out, g_ffn, conv_w, conv_b, w_in, later_weights, grads_ready):
    s, d = x.shape
    nh = lb_logits.shape[1] // HEAD
    wid = nh * HEAD
    blk = _pick(s, (256, 128))
    chunk = _pick(s, (HG_CHUNK,))
    gate_col = 7 * wid

    u = _rmsnorm_fwd("rmsnorm_mix", x, g_mix)
    proj = _matmul("in_proj", u, w_in, "nn", F32)
    qn, kn, vb, kt, vt = _qk_norm_fwd(proj, g_q, g_k, nh, blk)
    y_a, carries = _sb_attn_fwd(qn, kn, vt, nh, blk)
    lb = _lower_bound(lb_logits)
    o_pre, y_b, states = _hgrn2_fwd(proj, lb, g_hg_out, nh, 3 * nh, chunk)
    later = later_weights(o_pre)
    p_a, p_b, w_up = later["p_a"], later["p_b"], later["w_up"]
    w_o = later["w_o"].reshape(1, d, d)
    dff = later["w_down"].shape[1] * N_CHIPS
    w_down = later["w_down"].reshape(1, dff, d)
    ya_p = _matmul("proj_a", y_a, p_a, "nn", F32)
    yb_p = _matmul("proj_b", y_b, p_b, "nn", F32)
    m = _merge_fwd(proj, ya_p, yb_p, gate_col)
    h = _matmul("out_proj", m, w_o, "nn", F32, add=x)
    u2 = _rmsnorm_fwd("rmsnorm_ffn", h, g_ffn)
    up = _matmul("up_proj", u2, w_up, "nn", F32)
    act = _convffn_fwd(up, conv_w, conv_b, dff)
    out = _matmul("down_proj", act, w_down, "nn", F32, add=h)
    dout, sq = _loss_head(out, target)

    dact = _matmul("d_act", dout, w_down, "nt", F32)
    g_w_down = _matmul_tn("g_w_down", act, dout, 1, BF16).reshape(N_CHIPS, dff // N_CHIPS, d)
    dup_g, dup_v, dcw_g, dcw_v, dcb_g, dcb_v = _convffn_bwd(up, conv_w, conv_b, dact, dff)
    dup = jnp.concatenate([dup_g, dup_v], axis=1)
    g_w_up = _matmul_tn("g_w_up", u2, dup, N_CHIPS, BF16)
    sent = grads_ready(("w_down", "w_up"), [g_w_down, g_w_up])
    du2 = _matmul("d_u2", dup, w_up, "nt", F32)
    dh, pg_ffn = _rmsnorm_bwd("rmsnorm_ffn_bwd", h, g_ffn + sent, du2, dout)
    dm = _matmul("d_m", dh, w_o, "nt", F32)
    g_w_o = _matmul_tn("g_w_o", m, dh, 1, BF16).reshape(N_CHIPS, d // N_CHIPS, d)
    dya_p, dyb_p, dga, dgb = _merge_bwd(proj, ya_p, yb_p, dm, gate_col)
    g_p_a = _matmul_tn("g_p_a", y_a, dya_p, N_CHIPS, BF16)
    g_p_b = _matmul_tn("g_p_b", y_b, dyb_p, N_CHIPS, BF16)
    sent = grads_ready(("w_o", "p_a", "p_b"), [g_w_o, g_p_a, g_p_b])
    dy_a = _matmul("d_y_a", dya_p, p_a, "nt", F32)
    dy_b = _matmul("d_y_b", dyb_p, p_b, "nt", F32)
    dhq, dhf, dhi, dog, pg_hg, p_lb = _hgrn2_bwd(proj, lb, g_hg_out + sent, o_pre, states, dy_b, nh, 3 * nh, chunk)
    dqn, dkn, dv = _sb_attn_bwd(qn, kn, kt, vb, dy_a, carries, nh, blk)
    dq, dk, pg_q, pg_k = _qk_norm_bwd(proj, g_q, g_k, dqn, dkn, nh)
    dproj = jnp.concatenate([dq, dk, dv.astype(BF16), dhq, dhf, dhi, dog, dga, dgb], axis=1)
    g_w_in = _matmul_tn("g_w_in", u, dproj, N_CHIPS, BF16)
    sent = grads_ready(("w_in",), [g_w_in])
    du = _matmul("d_u", dproj, w_in, "nt", F32)
    dx, pg_mix = _rmsnorm_bwd("rmsnorm_mix_bwd", x, g_mix + sent, du, dh)

    small = dict(
        g_mix=_sum_rows("sum_g_mix", pg_mix), g_q=_sum_rows("sum_g_q", pg_q), g_k=_sum_rows("sum_g_k", pg_k),
        lb=_sum_rows("sum_lb", p_lb), g_hg_out=_sum_rows("sum_g_hg", pg_hg), g_ffn=_sum_rows("sum_g_ffn", pg_ffn),
        conv_w=jnp.concatenate([dcw_g, dcw_v], axis=1), conv_b=jnp.concatenate([dcb_g, dcb_v], axis=1),
        sq=_sum_rows("sum_sq", sq),
    )
    return dx, small, lb


ANY = pl.BlockSpec(memory_space=pl.ANY)


def _place():
    x, y, c = lax.axis_index("x"), lax.axis_index("y"), lax.axis_index("c")
    chips = [(1 - x, y), (x, 1 - y), (1 - x, 1 - y)]
    return x, y, c, chips


def _remote(src, dst, send_sem, recv_sem, to):
    return pltpu.make_async_remote_copy(src_ref=src, dst_ref=dst, send_sem=send_sem, recv_sem=recv_sem,
                                        device_id=to, device_id_type=MESH)


def _cast_bf16(name, w):
    r, c = w.shape
    tr = _pick(r, (256, 128, 64, 32, 16))

    def body(w_ref, o_ref):
        o_ref[...] = w_ref[...].astype(BF16)

    return pl.pallas_call(
        body, name=name, grid=(r // tr,), in_specs=[pl.BlockSpec((tr, c), lambda i: (i, 0))],
        out_specs=pl.BlockSpec((tr, c), lambda i: (i, 0)), out_shape=jax.ShapeDtypeStruct((r, c), BF16),
        compiler_params=_cparams("parallel"),
    )(w)


def _gather_weights(shards):
    n = len(shards)

    def body(*refs):
        ins, outs = refs[:n], refs[n:2 * n]
        send, recv, local = refs[2 * n:]
        x, y, c, chips = _place()
        mine = 2 * x + y
        sends, owns = [], []
        for k in range(n):
            half = ins[k].shape[0] // 2
            rows = pl.ds(c * half, half)
            own = pltpu.make_async_copy(ins[k], outs[k].at[mine], local.at[k])
            own.start()
            owns.append(own)
            for j, (px, py) in enumerate(chips):
                cp = _remote(ins[k].at[rows], outs[k].at[mine, rows], send.at[k, j], recv.at[k, j], (px, py, c))
                cp.start()
                sends.append(cp)
        for k in range(n):
            half = ins[k].shape[0] // 2
            rows = pl.ds(c * half, half)
            for j, (px, py) in enumerate(chips):
                part = outs[k].at[2 * px + py, rows]
                _remote(part, part, send.at[k, j], recv.at[k, j], (px, py, c)).wait_recv()
                fw = _remote(part, part, send.at[k, 3 + j], recv.at[k, 3 + j], (x, y, 1 - c))
                fw.start()
                sends.append(fw)
        for k in range(n):
            half = ins[k].shape[0] // 2
            other = pl.ds((1 - c) * half, half)
            for j, (px, py) in enumerate(chips):
                part = outs[k].at[2 * px + py, other]
                _remote(part, part, send.at[k, 3 + j], recv.at[k, 3 + j], (x, y, 1 - c)).wait_recv()
        for cp in sends:
            cp.wait_send()
        for cp in owns:
            cp.wait()

    return pl.pallas_call(
        body, name="gather_weights", in_specs=[ANY] * n, out_specs=[ANY] * n,
        out_shape=[jax.ShapeDtypeStruct((N_CHIPS,) + w.shape, w.dtype) for w in shards],
        scratch_shapes=[pltpu.SemaphoreType.DMA((n, 6)), pltpu.SemaphoreType.DMA((n, 6)), pltpu.SemaphoreType.DMA((n,))],
    )(*shards)


def _to_sibling(name, srcs):
    n = len(srcs)

    def body(*refs):
        ins, outs = refs[:n], refs[n:2 * n]
        send, recv = refs[2 * n:]
        x, y, c, _ = _place()
        cps = []
        for k in range(n):
            cp = _remote(ins[k], outs[k], send.at[k], recv.at[k], (x, y, 1 - c))
            cp.start()
            cps.append(cp)
        for cp in cps:
            cp.wait_recv()
        for cp in cps:
            cp.wait_send()

    return pl.pallas_call(
        body, name=name, in_specs=[ANY] * n, out_specs=[ANY] * n,
        out_shape=[jax.ShapeDtypeStruct(a.shape, a.dtype) for a in srcs],
        scratch_shapes=[pltpu.SemaphoreType.DMA((n,)), pltpu.SemaphoreType.DMA((n,))],
    )(*srcs)


HBM = pl.BlockSpec(memory_space=pltpu.HBM)
SEM = pl.BlockSpec(memory_space=pltpu.SEMAPHORE)
SIDE = pltpu.SideEffectType.DATAFLOW_SIDE_EFFECTING
N_PEERS = 7


def _peer(r):
    x, y, c = lax.axis_index("x"), lax.axis_index("y"), lax.axis_index("c")
    return (1 - x if r & 4 else x), (1 - y if r & 2 else y), (1 - c if r & 1 else c)


def _partial_copy(src, land, send, recv, k, r):
    px, py, pc = _peer(r)
    half = src.shape[1] // 2
    sem = k * N_PEERS + r - 1
    return _remote(src.at[2 * px + py, pl.ds(pc * half, half)], land.at[r - 1], send.at[sem], recv.at[sem], (px, py, pc))


def _shard_copy(full, send, recv, k, r):
    x, y, c = lax.axis_index("x"), lax.axis_index("y"), lax.axis_index("c")
    half = full.shape[1] // 2
    part = full.at[2 * x + y, pl.ds(c * half, half)]
    sem = k * (N_PEERS - 1) + r - 2
    return _remote(part, part, send.at[sem], recv.at[sem], _peer(r))


def _start_copies(name, arrays, n_sems, copies):
    n = len(arrays)

    def body(*refs):
        send, recv, token = refs[n], refs[n + 1], refs[-1]
        for cp in copies(refs[:n], send, recv):
            cp.start()
        token[...] = jnp.zeros_like(token)

    sem = pltpu.SemaphoreType.DMA((n_sems,))
    outs = pl.pallas_call(
        body, name=name, in_specs=[HBM] * n,
        out_specs=[SEM, SEM] + [HBM] * n + [pl.BlockSpec(memory_space=pltpu.VMEM)],
        out_shape=[sem, sem] + [pltpu.HBM(a.shape, a.dtype) for a in arrays] + [jax.ShapeDtypeStruct((SUB, HEAD), F32)],
        input_output_aliases={i: 2 + i for i in range(n)},
        compiler_params=pltpu.CompilerParams(has_side_effects=SIDE),
    )(*[pltpu.with_memory_space_constraint(a, pltpu.HBM) for a in arrays])
    return outs[0], outs[1], list(outs[2:2 + n]), outs[-1]


def _wait_copies(name, send, recv, arrays, after, copies):
    n = len(arrays)

    def body(*refs):
        for cp in copies(refs[:n], refs[n], refs[n + 1]):
            cp.wait_send()
            cp.wait_recv()

    return list(pl.pallas_call(
        body, name=name, in_specs=[HBM] * n + [SEM, SEM, ANY], out_specs=[HBM] * n,
        out_shape=[pltpu.HBM(a.shape, a.dtype) for a in arrays],
        input_output_aliases={i: i for i in range(n)},
        compiler_params=pltpu.CompilerParams(has_side_effects=SIDE),
    )(*arrays, send, recv, after))


def _partial_copies(n):
    def copies(refs, send, recv):
        return [_partial_copy(refs[k], refs[n + k], send, recv, k, r) for k in range(n) for r in range(1, N_PEERS + 1)]
    return copies


def _shard_copies(n):
    def copies(refs, send, recv):
        return [_shard_copy(refs[k], send, recv, k, r) for k in range(n) for r in range(2, N_PEERS + 1)]
    return copies


def _cast_place(name, w, shard):
    r, c = w.shape
    tr = _pick(r, (256, 128, 64, 32, 16))

    def body(s_ref, w_ref, o_ref):
        o_ref[...] = w_ref[...].astype(BF16)

    return pl.pallas_call(
        body, name=name,
        grid_spec=pltpu.PrefetchScalarGridSpec(
            num_scalar_prefetch=1, grid=(r // tr,), in_specs=[pl.BlockSpec((tr, c), lambda i, sr: (i, 0))],
            out_specs=pl.BlockSpec((None, tr, c), lambda i, sr: (sr[0], i, 0))),
        out_shape=jax.ShapeDtypeStruct((N_CHIPS, r, c), BF16),
        compiler_params=_cparams("parallel"),
    )(shard, w)


def _sum_peers(name, g, land, shard, core):
    _, r, cols = g.shape
    half = r // 2
    tr = _pick(half, (128, 64, 32, 16))
    nt = half // tr

    def body(s_ref, c_ref, g_ref, l_ref, o_ref):
        acc = g_ref[...].astype(F32)
        for j in range(N_PEERS):
            acc = acc + l_ref[j].astype(F32)
        o_ref[...] = acc

    return pl.pallas_call(
        body, name=name,
        grid_spec=pltpu.PrefetchScalarGridSpec(
            num_scalar_prefetch=2, grid=(nt,),
            in_specs=[pl.BlockSpec((None, tr, cols), lambda i, sr, cr: (sr[0], cr[0] * nt + i, 0)),
                      pl.BlockSpec((N_PEERS, tr, cols), lambda i, sr, cr: (0, i, 0))],
            out_specs=pl.BlockSpec((tr, cols), lambda i, sr, cr: (i, 0))),
        out_shape=jax.ShapeDtypeStruct((half, cols), F32),
        compiler_params=_cparams("parallel"),
    )(shard, core, g, land)


def _join_halves(mine, got, c):
    half, cols = mine.shape
    tr = _pick(half, (256, 128, 64, 32, 16, 8))
    nt = half // tr

    def body(c_ref, a_ref, b_ref, o_ref):
        i = pl.program_id(0)
        own = (i // nt) == c_ref[0]

        @pl.when(own)
        def _():
            o_ref[...] = a_ref[...]

        @pl.when(jnp.logical_not(own))
        def _():
            o_ref[...] = b_ref[...]

    blk = pl.BlockSpec((tr, cols), lambda i, cr: (i % nt, 0))
    return pl.pallas_call(
        body, name="join_halves",
        grid_spec=pltpu.PrefetchScalarGridSpec(num_scalar_prefetch=1, grid=(2 * nt,), in_specs=[blk, blk],
                                               out_specs=pl.BlockSpec((tr, cols), lambda i, cr: (i, 0))),
        out_shape=jax.ShapeDtypeStruct((2 * half, cols), F32),
        compiler_params=_cparams("parallel"),
    )(c, mine, got)


def _all_gather_rows(name, row):
    p = row.shape[1]

    def body(in_ref, out_ref, send, recv, local):
        x, y, c, _ = _place()
        me = 4 * x + 2 * y + c
        own = pltpu.make_async_copy(in_ref, out_ref.at[me], local)
        own.start()
        cps = []
        for k in range(1, 8):
            px, py, pc = x ^ (k >> 2), y ^ ((k >> 1) & 1), c ^ (k & 1)
            cp = _remote(in_ref, out_ref.at[me], send.at[k - 1], recv.at[k - 1], (px, py, pc))
            cp.start()
            cps.append(cp)
        for cp in cps:
            cp.wait_recv()
        for cp in cps:
            cp.wait_send()
        own.wait()

    return pl.pallas_call(
        body, name=name, in_specs=[ANY], out_specs=ANY,
        out_shape=jax.ShapeDtypeStruct((8, 1, p), F32),
        scratch_shapes=[pltpu.SemaphoreType.DMA((7,)), pltpu.SemaphoreType.DMA((7,)), pltpu.SemaphoreType.DMA],
    )(row)


def _sum_devices(rows):
    def body(r_ref, o_ref):
        acc = r_ref[0]
        for k in range(1, 8):
            acc = acc + r_ref[k]
        o_ref[...] = acc

    return pl.pallas_call(body, name="sum_devices", out_shape=jax.ShapeDtypeStruct(rows.shape[1:], F32))(rows)


def _adamw(name, w, g, m, v):
    r, c = w.shape
    tr = _pick(r, (128, 64, 32, 16, 8))
    bc1 = 1.0 - ADAM_B1 ** ADAM_STEP
    bc2 = 1.0 - ADAM_B2 ** ADAM_STEP

    def body(w_ref, g_ref, m_ref, v_ref, d_ref, nm_ref, nv_ref):
        gv = g_ref[...]
        nm = ADAM_B1 * m_ref[...] + (1.0 - ADAM_B1) * gv
        nv = ADAM_B2 * v_ref[...] + (1.0 - ADAM_B2) * (gv * gv)
        d_ref[...] = -ADAM_LR * ((nm / bc1) / (jnp.sqrt(nv / bc2) + ADAM_EPS) + ADAM_WD * w_ref[...])
        nm_ref[...] = nm
        nv_ref[...] = nv

    blk = pl.BlockSpec((tr, c), lambda i: (i, 0))
    sh = jax.ShapeDtypeStruct((r, c), F32)
    return pl.pallas_call(
        body, name=name, grid=(r // tr,), in_specs=[blk] * 4, out_specs=[blk] * 3, out_shape=[sh] * 3,
        compiler_params=_cparams("parallel"),
    )(w, g, m, v)


def _lb_logits_grad(dlb, lb):
    def body(d_ref, lb_ref, o_ref):
        lbv = lb_ref[...]
        t = d_ref[...] * lbv * (1.0 - lbv)
        o_ref[0:1, :] = t
        o_ref[1:2, :] = -t

    return pl.pallas_call(body, name="lb_logits_grad", out_shape=jax.ShapeDtypeStruct((2, dlb.shape[1]), F32))(dlb, lb)


BIG = ("w_in", "p_a", "p_b", "w_o", "w_up", "w_down")
SMALL = ("g_mix", "g_q", "g_k", "lb_logits", "g_hg_out", "g_ffn", "conv_w", "conv_b")
ORDER = ("g_mix", "w_in", "g_q", "g_k", "lb_logits", "g_hg_out", "p_a", "p_b", "w_o", "g_ffn", "w_up", "conv_w", "conv_b", "w_down")


def kernel(x, g_mix, w_in, g_q, g_k, lb_logits, g_hg_out, p_a, p_b, w_o, g_ffn, w_up, conv_w, conv_b, w_down, loss_target, m_g_mix, m_w_in, m_g_q, m_g_k, m_lb_logits, m_g_hg_out, m_p_a, m_p_b, m_w_o, m_g_ffn, m_w_up, m_conv_w, m_conv_b, m_w_down, v_g_mix, v_w_in, v_g_q, v_g_k, v_lb_logits, v_g_hg_out, v_p_a, v_p_b, v_w_o, v_g_ffn, v_w_up, v_conv_w, v_conv_b, v_w_down):
    assert lb_logits.shape[0] == 2, "the lower bound is the first row of a two-row softmax"
    w = dict(g_mix=g_mix, w_in=w_in[0], g_q=g_q, g_k=g_k, lb_logits=lb_logits, g_hg_out=g_hg_out, p_a=p_a[0], p_b=p_b[0],
             w_o=w_o[0], g_ffn=g_ffn, w_up=w_up[0], conv_w=conv_w[0], conv_b=conv_b, w_down=w_down[0])
    mom = dict(g_mix=m_g_mix, w_in=m_w_in[0], g_q=m_g_q, g_k=m_g_k, lb_logits=m_lb_logits, g_hg_out=m_g_hg_out, p_a=m_p_a[0],
               p_b=m_p_b[0], w_o=m_w_o[0], g_ffn=m_g_ffn, w_up=m_w_up[0], conv_w=m_conv_w[0], conv_b=m_conv_b, w_down=m_w_down[0])
    var = dict(g_mix=v_g_mix, w_in=v_w_in[0], g_q=v_g_q, g_k=v_g_k, lb_logits=v_lb_logits, g_hg_out=v_g_hg_out, p_a=v_p_a[0],
               p_b=v_p_b[0], w_o=v_w_o[0], g_ffn=v_g_ffn, w_up=v_w_up[0], conv_w=v_conv_w[0], conv_b=v_conv_b, w_down=v_w_down[0])
    d = x.shape[2]
    cx, cy, cc = lax.axis_index("x"), lax.axis_index("y"), lax.axis_index("c")
    shard = (2 * cx + cy).astype(jnp.int32).reshape(1)
    core = cc.astype(jnp.int32).reshape(1)

    w_in_full = _gather_weights([_cast_bf16("cast_w_in", w["w_in"])])[0]
    later = ("p_a", "p_b", "w_o", "w_up", "w_down")
    n_later = len(later)
    fulls = [_cast_place("cast_" + n, w[n], shard) for n in later]
    w_send, w_recv, fulls, w_token = _start_copies("weights_start", fulls, n_later * (N_PEERS - 1), _shard_copies(n_later))

    def later_weights(after):
        return dict(zip(later, _wait_copies("weights_wait", w_send, w_recv, fulls, after, _shard_copies(n_later))))

    cw = conv_w.shape[2]
    rows = _all_gather_rows("gather_conv_w", w["conv_w"].reshape(1, 3 * cw))
    conv_full = jnp.concatenate([rows[2 * s, 0].reshape(3, cw) for s in range(N_CHIPS)], axis=1)

    pending = []

    def grads_ready(names, gs):
        n = len(gs)
        lands = [lax.empty((N_PEERS, g.shape[1] // 2, g.shape[2]), BF16) for g in gs]
        send, recv, arrays, token = _start_copies("partials_start_" + names[0], list(gs) + lands, n * N_PEERS, _partial_copies(n))
        pending.append((names, send, recv, arrays))
        return token[0:1, 0:1]

    dx, small, lb = _local_step(x[0], loss_target[0], g_mix + w_token[0:1, 0:1], g_q, g_k, lb_logits, g_hg_out, g_ffn,
                                conv_full, conv_b, w_in_full, later_weights, grads_ready)

    mine = {}
    after = dx
    for names, send, recv, arrays in pending:
        n = len(names)
        arrays = _wait_copies("partials_wait_" + names[0], send, recv, arrays, after, _partial_copies(n))
        for k, name in enumerate(names):
            mine[name] = _sum_peers("sum_" + name, arrays[k], arrays[n + k], shard, core)
        after = mine[names[-1]]
    theirs = _to_sibling("reduced_to_sibling", [mine[n] for n in BIG])
    grads = {n: _join_halves(mine[n], b, core) for n, b in zip(BIG, theirs)}

    names = ("g_mix", "g_q", "g_k", "lb", "g_hg_out", "g_ffn", "conv_b", "sq")
    packed = jnp.concatenate([small[n] for n in names] + [small["conv_w"].reshape(1, -1)], axis=1)
    total = _sum_devices(_all_gather_rows("gather_small_grads", packed))
    off = 0
    red = {}
    for n in names:
        ln = small[n].shape[1]
        red[n] = total[:, off:off + ln]
        off += ln
    conv_all = total[:, off:].reshape(3, -1)
    loss = 0.5 * jnp.sum(red["sq"]) / d
    grads["conv_w"] = lax.dynamic_slice_in_dim(conv_all, (2 * cx + cy) * cw, cw, axis=1)
    grads["lb_logits"] = _lb_logits_grad(red["lb"], lb)
    for n in ("g_mix", "g_q", "g_k", "g_hg_out", "g_ffn", "conv_b"):
        grads[n] = red[n]

    delta, new_m, new_v = {}, {}, {}
    for n in ORDER:
        delta[n], new_m[n], new_v[n] = _adamw("adamw_" + n, w[n], grads[n], mom[n], var[n])

    def shaped(a, like):
        return a.reshape(like.shape)

    ref_w = dict(g_mix=g_mix, w_in=w_in, g_q=g_q, g_k=g_k, lb_logits=lb_logits, g_hg_out=g_hg_out, p_a=p_a, p_b=p_b, w_o=w_o,
                 g_ffn=g_ffn, w_up=w_up, conv_w=conv_w, conv_b=conv_b, w_down=w_down)
    outs = [loss, dx[None]]
    for group in (grads, delta, new_m, new_v):
        outs += [shaped(group[n], ref_w[n]) for n in ORDER]
    return tuple(outs)
```

```python
import functools

import jax
import jax.numpy as jnp
from jax import lax
from jax.experimental import pallas as pl
from jax.experimental.pallas import tpu as pltpu

F32 = jnp.float32
BF16 = jnp.bfloat16
HEAD = 128
EPS = 1e-6
N_CHIPS = 4
HG_CHUNK = 64
SUB = 8
ADAM_LR, ADAM_B1, ADAM_B2, ADAM_EPS, ADAM_WD, ADAM_STEP = 0.001, 0.9, 0.999, 1e-08, 0.01, 10
VMEM_LIMIT = 56 * 1024 * 1024
MESH = pl.DeviceIdType.MESH

NN = (((1,), (0,)), ((), ()))
NT = (((1,), (1,)), ((), ()))
TN = (((0,), (0,)), ((), ()))


def _cparams(*sem):
    return pltpu.CompilerParams(dimension_semantics=sem if sem else None, vmem_limit_bytes=VMEM_LIMIT)


def _pick(n, cands):
    for c in cands:
        if c <= n and n % c == 0:
            return c
    return n


def _sig(x):
    return 1.0 / (1.0 + jnp.exp(-x))


def _dot(a, b, dims):
    return lax.dot_general(a, b, dims, preferred_element_type=F32)


def _split(x):
    hi = x.astype(BF16)
    lo = (x - hi.astype(F32)).astype(BF16)
    return hi, lo


def _tri(n, kind):
    r = lax.broadcasted_iota(jnp.int32, (n, n), 0)
    c = lax.broadcasted_iota(jnp.int32, (n, n), 1)
    m = {"ge": c >= r, "gt": c > r, "le": c <= r, "lt": c < r}[kind]
    return jnp.where(m, 1.0, 0.0).astype(BF16)


TILE_M = (1024, 512, 256, 128)
TILE_N = (1408, 1024, 512, 256, 128)
TILE_K = (2048, 1408, 1024, 512, 256, 128)


def _matmul(name, a, b, mode, out_dtype, add=None):
    if mode == "nn":
        m, k = a.shape
        g, _, ns = b.shape
        n = g * ns
        tm, tn, tk = _pick(m, TILE_M), _pick(ns, TILE_N), _pick(k, TILE_K)
        nps = ns // tn
        grid = (m // tm, n // tn, k // tk)
        a_spec = pl.BlockSpec((tm, tk), lambda i, j, kk: (i, kk))
        b_spec = pl.BlockSpec((None, tk, tn), lambda i, j, kk: (j // nps, kk, j % nps))
        o_spec = pl.BlockSpec((tm, tn), lambda i, j, kk: (i, j))
        o_shape = jax.ShapeDtypeStruct((m, n), out_dtype)
        dims = NN
    elif mode == "nt":
        m, k = a.shape
        g, n, ks = b.shape
        tm, tn, tk = _pick(m, TILE_M), _pick(n, TILE_N), _pick(ks, TILE_K)
        kps = ks // tk
        grid = (m // tm, n // tn, k // tk)
        a_spec = pl.BlockSpec((tm, tk), lambda i, j, kk: (i, kk))
        b_spec = pl.BlockSpec((None, tn, tk), lambda i, j, kk: (kk // kps, j, kk % kps))
        o_spec = pl.BlockSpec((tm, tn), lambda i, j, kk: (i, j))
        o_shape = jax.ShapeDtypeStruct((m, n), out_dtype)
        dims = NT
    else:
        raise ValueError(mode)
    nk = grid[2]

    def body(*refs):
        a_ref, b_ref = refs[0], refs[1]
        add_ref = refs[2] if add is not None else None
        o_ref = refs[2 + (add is not None)]

        def finish(r):
            if add is not None:
                r = r + add_ref[...]
            o_ref[...] = r.astype(o_ref.dtype)

        part = _dot(a_ref[...].astype(BF16), b_ref[...].astype(BF16), dims)
        if nk == 1:
            finish(part)
            return
        acc = refs[-1]
        kk = pl.program_id(2)

        @pl.when(kk == 0)
        def _():
            acc[...] = part

        @pl.when(kk > 0)
        def _():
            acc[...] += part

        @pl.when(kk == nk - 1)
        def _():
            finish(acc[...])

    in_specs = [a_spec, b_spec]
    args = [a, b]
    if add is not None:
        in_specs.append(o_spec)
        args.append(add)
    return pl.pallas_call(
        body, name=name, grid=grid, in_specs=in_specs, out_specs=o_spec, out_shape=o_shape,
        scratch_shapes=[pltpu.VMEM((tm, tn), F32)] if nk > 1 else [],
        compiler_params=_cparams("parallel", "parallel", "arbitrary"),
    )(*args)


def _matmul_tn(name, a, b, g, out_dtype):
    k, m = a.shape
    _, n = b.shape
    ns = n // g
    tm, tn, tk = _pick(m, (2048, 1408) + TILE_M), _pick(ns, TILE_N), _pick(k, (1024, 512, 256, 128))
    nps = ns // tn
    nk = k // tk

    def body(a_ref, b_ref, o_ref, acc):
        kk = pl.program_id(2)
        part = _dot(a_ref[...].astype(BF16), b_ref[...].astype(BF16), TN)

        @pl.when(kk == 0)
        def _():
            acc[...] = part

        @pl.when(kk > 0)
        def _():
            acc[...] += part

        @pl.when(kk == nk - 1)
        def _():
            o_ref[...] = acc[...].astype(o_ref.dtype)

    return pl.pallas_call(
        body, name=name, grid=(m // tm, n // tn, nk),
        in_specs=[pl.BlockSpec((tk, tm), lambda i, j, kk: (kk, i)), pl.BlockSpec((tk, tn), lambda i, j, kk: (kk, j))],
        out_specs=pl.BlockSpec((None, tm, tn), lambda i, j, kk: (j // nps, i, j % nps)),
        out_shape=jax.ShapeDtypeStruct((g, m, ns), out_dtype),
        scratch_shapes=[pltpu.VMEM((tm, tn), F32)],
        compiler_params=_cparams("parallel", "parallel", "arbitrary"),
    )(a, b)


def _rmsnorm_fwd(name, x, g):
    s, d = x.shape
    tr = _pick(s, (256, 128, 64, 32, 16, 8))

    def body(x_ref, g_ref, u_ref):
        xv = x_ref[...]
        r = lax.rsqrt(jnp.mean(xv * xv, axis=-1, keepdims=True) + EPS)
        u_ref[...] = (xv * r * g_ref[...]).astype(u_ref.dtype)

    return pl.pallas_call(
        body, name=name, grid=(s // tr,),
        in_specs=[pl.BlockSpec((tr, d), lambda i: (i, 0)), pl.BlockSpec((1, d), lambda i: (0, 0))],
        out_specs=pl.BlockSpec((tr, d), lambda i: (i, 0)),
        out_shape=jax.ShapeDtypeStruct((s, d), BF16),
        compiler_params=_cparams("parallel"),
    )(x, g)


def _rmsnorm_bwd(name, x, g, du, extra):
    s, d = x.shape
    tr = _pick(s, (256, 128, 64, 32, 16, 8))

    def body(x_ref, g_ref, du_ref, e_ref, dx_ref, dg_ref):
        i = pl.program_id(0)
        xv = x_ref[...]
        r = lax.rsqrt(jnp.mean(xv * xv, axis=-1, keepdims=True) + EPS)
        n = xv * r
        dy = du_ref[...]
        a = dy * g_ref[...]
        dx = r * (a - n * jnp.mean(a * n, axis=-1, keepdims=True))
        dx_ref[...] = e_ref[...] + dx

        @pl.when(i == 0)
        def _():
            dg_ref[...] = jnp.zeros_like(dg_ref)

        dg_ref[...] += jnp.sum((dy * n).reshape(tr // SUB, SUB, d), axis=0)

    return pl.pallas_call(
        body, name=name, grid=(s // tr,),
        in_specs=[pl.BlockSpec((tr, d), lambda i: (i, 0)), pl.BlockSpec((1, d), lambda i: (0, 0)),
                  pl.BlockSpec((tr, d), lambda i: (i, 0)), pl.BlockSpec((tr, d), lambda i: (i, 0))],
        out_specs=[pl.BlockSpec((tr, d), lambda i: (i, 0)), pl.BlockSpec((SUB, d), lambda i: (0, 0))],
        out_shape=[jax.ShapeDtypeStruct((s, d), F32), jax.ShapeDtypeStruct((SUB, d), F32)],
        compiler_params=_cparams("arbitrary"),
    )(x, g, du, extra)


def _head_norm(x, g):
    r = lax.rsqrt(jnp.mean(x * x, axis=-1, keepdims=True) + EPS)
    return x * r * g


def _qk_norm_fwd(proj, g_q, g_k, nh, blk):
    s = proj.shape[0]

    def body(q_ref, k_ref, v_ref, gq_ref, gk_ref, qn_ref, kn_ref, vb_ref, kt_ref, vt_ref):
        qn_ref[...] = _head_norm(q_ref[...], gq_ref[...]).astype(BF16)
        kn = _head_norm(k_ref[...], gk_ref[...])
        kn_ref[...] = kn.astype(BF16)
        kt_ref[...] = kn.T.astype(BF16)
        v = v_ref[...]
        vb_ref[...] = v.astype(BF16)
        vt_ref[...] = v.T.astype(BF16)

    col = lambda base: pl.BlockSpec((blk, HEAD), lambda i, h: (i, base + h))
    gs = pl.BlockSpec((1, HEAD), lambda i, h: (0, 0))
    o = pl.BlockSpec((blk, HEAD), lambda i, h: (i, h))
    t = pl.BlockSpec((None, HEAD, blk), lambda i, h: (i, h, 0))
    sh = jax.ShapeDtypeStruct((s, nh * HEAD), BF16)
    tsh = jax.ShapeDtypeStruct((s // blk, nh * HEAD, blk), BF16)
    return pl.pallas_call(
        body, name="qk_norm_fwd", grid=(s // blk, nh),
        in_specs=[col(0), col(nh), col(2 * nh), gs, gs], out_specs=[o, o, o, t, t], out_shape=[sh, sh, sh, tsh, tsh],
        compiler_params=_cparams("parallel", "parallel"),
    )(proj, proj, proj, g_q, g_k)


def _qk_norm_bwd(proj, g_q, g_k, dqn, dkn, nh):
    s = proj.shape[0]
    tr = _pick(s, (512, 256, 128, 64, 32, 16, 8))

    def one(x, g, dy):
        r = lax.rsqrt(jnp.mean(x * x, axis=-1, keepdims=True) + EPS)
        n = x * r
        a = dy * g
        dx = r * (a - n * jnp.mean(a * n, axis=-1, keepdims=True))
        return dx, jnp.sum((dy * n).reshape(tr // SUB, SUB, HEAD), axis=0)

    def body(q_ref, k_ref, gq_ref, gk_ref, dqn_ref, dkn_ref, dq_ref, dk_ref, dgq_ref, dgk_ref):
        first = (pl.program_id(0) == 0) & (pl.program_id(1) == 0)

        @pl.when(first)
        def _():
            dgq_ref[...] = jnp.zeros_like(dgq_ref)
            dgk_ref[...] = jnp.zeros_like(dgk_ref)

        dq, pq = one(q_ref[...], gq_ref[...], dqn_ref[...])
        dk, pk = one(k_ref[...], gk_ref[...], dkn_ref[...])
        dq_ref[...] = dq.astype(BF16)
        dk_ref[...] = dk.astype(BF16)
        dgq_ref[...] += pq
        dgk_ref[...] += pk

    col = lambda base: pl.BlockSpec((tr, HEAD), lambda i, h: (i, base + h))
    gs = pl.BlockSpec((1, HEAD), lambda i, h: (0, 0))
    o = pl.BlockSpec((tr, HEAD), lambda i, h: (i, h))
    part = pl.BlockSpec((SUB, HEAD), lambda i, h: (0, 0))
    sh = jax.ShapeDtypeStruct((s, nh * HEAD), BF16)
    psh = jax.ShapeDtypeStruct((SUB, HEAD), F32)
    return pl.pallas_call(
        body, name="qk_norm_bwd", grid=(s // tr, nh),
        in_specs=[col(0), col(nh), gs, gs, o, o], out_specs=[o, o, part, part], out_shape=[sh, sh, psh, psh],
        compiler_params=_cparams("arbitrary", "arbitrary"),
    )(proj, proj, g_q, g_k, dqn, dkn)


def _sb_consts(blk, hp):
    upper = _tri(blk, "ge")
    row = lax.broadcasted_iota(jnp.int32, (blk, hp * blk), 0)
    col = lax.broadcasted_iota(jnp.int32, (blk, hp * blk), 1)
    strict = row < col
    for h in range(1, hp):
        strict = strict & ((col < h * blk) | (row < col - h * blk))
    return jnp.concatenate([upper, upper], axis=1), strict


def _sb_log_keep(zt, strict):
    l = jnp.minimum(-zt, 0.0) - jnp.log(1.0 + jnp.exp(-jnp.abs(zt)))
    return l if strict is None else jnp.where(strict, l, 0.0)


SB_GROUP = 4
SB_GROUP_BWD = 2


def _sb_heads(nh):
    return 2 if nh % 2 == 0 else 1


def _sb_attn_fwd(qn, kn, vt, nh, blk):
    s = qn.shape[0]
    nb = s // blk
    scale = HEAD ** -0.5
    hp = _sb_heads(nh)

    def body(q_ref, k_ref, vt_ref, y_ref, c_ref):
        qi = pl.program_id(1)
        suffix, strict = _sb_consts(blk, hp)
        qs = [q_ref[:, h * HEAD:(h + 1) * HEAD] for h in range(hp)]

        def logits(kb_i):
            off = pl.multiple_of(kb_i * blk, blk)
            return jnp.concatenate(
                [_dot(k_ref[pl.ds(off, blk), h * HEAD:(h + 1) * HEAD], qs[h], NT) for h in range(hp)], axis=1) * scale

        def sums(zt, mask):
            l = _sb_log_keep(zt, mask)
            parts = []
            for h in range(hp):
                hi, lo = _split(l[:, h * blk:(h + 1) * blk])
                parts.append(_dot(suffix, jnp.concatenate([hi, lo], axis=0), NN))
            return jnp.concatenate(parts, axis=1)

        def weights(kb_i, zt, cum, cr, mask):
            for h in range(hp):
                c_ref[h, kb_i] = cr[:, h * blk:(h + 1) * blk]
            wt = jnp.exp(zt + cum + cr)
            if mask is not None:
                wt = jnp.where(mask, wt, 0.0)
            return wt.astype(BF16), cr + cum[0:1, :]

        def add_values(kb_i, wt, accs):
            return tuple(
                accs[h] + _dot(vt_ref[kb_i, h * HEAD:(h + 1) * HEAD, :], wt[:, h * blk:(h + 1) * blk], NN)
                for h in range(hp))

        def group(kbs, masks, accs, cr):
            zts = [logits(k) for k in kbs]
            cums = [sums(zt, m) for zt, m in zip(zts, masks)]
            for k, zt, cum, m in zip(kbs, zts, cums, masks):
                wt, cr = weights(k, zt, cum, cr, m)
                accs = add_values(k, wt, accs)
            return accs, cr

        accs = tuple(jnp.zeros((HEAD, blk), F32) for _ in range(hp))
        accs, cr = group([qi], [strict], accs, jnp.zeros((1, hp * blk), F32))
        n_groups = qi // SB_GROUP

        def many(g, st):
            top = qi - 1 - g * SB_GROUP
            return group([top - j for j in range(SB_GROUP)], [None] * SB_GROUP, *st)

        def one(r, st):
            return group([qi - 1 - n_groups * SB_GROUP - r], [None], *st)

        st = lax.fori_loop(0, n_groups, many, (accs, cr))
        accs, _ = lax.fori_loop(0, qi - n_groups * SB_GROUP, one, st)
        for h in range(hp):
            y_ref[:, h * HEAD:(h + 1) * HEAD] = accs[h].T.astype(y_ref.dtype)

    qs_ = pl.BlockSpec((blk, hp * HEAD), lambda h, i: (i, h))
    full = pl.BlockSpec((s, hp * HEAD), lambda h, i: (0, h))
    return pl.pallas_call(
        body, name="sb_attn_fwd", grid=(nh // hp, nb),
        in_specs=[qs_, full, pl.BlockSpec((nb, hp * HEAD, blk), lambda h, i: (0, h, 0))],
        out_specs=[qs_, pl.BlockSpec((hp, nb, 1, blk), lambda h, i: (h, 0, 0, i))],
        out_shape=[jax.ShapeDtypeStruct((s, nh * HEAD), BF16), jax.ShapeDtypeStruct((nh, nb, 1, s), F32)],
        compiler_params=_cparams("parallel", "arbitrary"),
    )(qn, kn, vt)


def _sb_attn_bwd(qn, kn, kt, vb, dy, carries, nh, blk):
    s = qn.shape[0]
    nb = s // blk
    scale = HEAD ** -0.5
    hp = _sb_heads(nh)

    def body(q_ref, k_ref, kt_ref, v_ref, dy_ref, c_ref, dq_ref, dk_ref, dv_ref):
        qi = pl.program_id(1)

        @pl.when(qi == 0)
        def _():
            dk_ref[...] = jnp.zeros_like(dk_ref)
            dv_ref[...] = jnp.zeros_like(dv_ref)

        suffix, strict = _sb_consts(blk, hp)
        prefix = _tri(blk, "lt")
        qs = [q_ref[:, h * HEAD:(h + 1) * HEAD] for h in range(hp)]
        dos = [dy_ref[:, h * HEAD:(h + 1) * HEAD].astype(BF16) for h in range(hp)]

        def logits(kb_i):
            off = pl.multiple_of(kb_i * blk, blk)
            return jnp.concatenate(
                [_dot(k_ref[pl.ds(off, blk), h * HEAD:(h + 1) * HEAD], qs[h], NT) for h in range(hp)], axis=1) * scale

        def group(kbs, masks, dqs, ec):
            rows = [pl.ds(pl.multiple_of(k * blk, blk), blk) for k in kbs]
            zts = [logits(k) for k in kbs]
            dws = [jnp.concatenate([_dot(v_ref[r, h * HEAD:(h + 1) * HEAD], dos[h], NT) for h in range(hp)], axis=1)
                   for r in rows]
            ls = [_sb_log_keep(zt, m) for zt, m in zip(zts, masks)]
            cums = []
            for l in ls:
                parts = []
                for h in range(hp):
                    hi, lo = _split(l[:, h * blk:(h + 1) * blk])
                    parts.append(_dot(suffix, jnp.concatenate([hi, lo], axis=0), NN))
                cums.append(jnp.concatenate(parts, axis=1))
            wts, ets, befores = [], [], []
            for k, zt, cum, dw, m in zip(kbs, zts, cums, dws, masks):
                cr = jnp.concatenate([c_ref[h, k] for h in range(hp)], axis=1)
                wt = jnp.exp(zt + cum + cr)
                if m is not None:
                    wt = jnp.where(m, wt, 0.0)
                et = wt * dw
                befores.append(_dot(prefix, et.astype(BF16), NN) + ec)
                ec = ec + jnp.sum(et, axis=0, keepdims=True)
                wts.append(wt.astype(BF16))
                ets.append(et)
            for k, r, l, et, before, wtb in zip(kbs, rows, ls, ets, befores, wts):
                dzt = ((jnp.exp(l) * (et + before) - before) * scale).astype(BF16)
                out = []
                for h in range(hp):
                    cols, part = slice(h * HEAD, (h + 1) * HEAD), slice(h * blk, (h + 1) * blk)
                    dv_ref[r, cols] += _dot(wtb[:, part], dos[h], NN)
                    dk_ref[r, cols] += _dot(dzt[:, part], qs[h], NN)
                    out.append(dqs[h] + _dot(kt_ref[k, cols, :], dzt[:, part], NN))
                dqs = tuple(out)
            return dqs, ec

        n_groups = qi // SB_GROUP_BWD

        def many(g, st):
            return group([g * SB_GROUP_BWD + j for j in range(SB_GROUP_BWD)], [None] * SB_GROUP_BWD, *st)

        def one(r, st):
            return group([n_groups * SB_GROUP_BWD + r], [None], *st)

        st = (tuple(jnp.zeros((HEAD, blk), F32) for _ in range(hp)), jnp.zeros((1, hp * blk), F32))
        st = lax.fori_loop(0, n_groups, many, st)
        st = lax.fori_loop(0, qi - n_groups * SB_GROUP_BWD, one, st)
        dqs, _ = group([qi], [strict], *st)
        for h in range(hp):
            dq_ref[:, h * HEAD:(h + 1) * HEAD] = dqs[h].T

    qs_ = pl.BlockSpec((blk, hp * HEAD), lambda h, i: (i, h))
    full = pl.BlockSpec((s, hp * HEAD), lambda h, i: (0, h), pipeline_mode=pl.Buffered(1))
    sh = jax.ShapeDtypeStruct((s, nh * HEAD), F32)
    return pl.pallas_call(
        body, name="sb_attn_bwd", grid=(nh // hp, nb),
        in_specs=[qs_, full, pl.BlockSpec((nb, hp * HEAD, blk), lambda h, i: (0, h, 0), pipeline_mode=pl.Buffered(1)),
                  full, qs_, pl.BlockSpec((hp, nb, 1, blk), lambda h, i: (h, 0, 0, i))],
        out_specs=[qs_, full, full], out_shape=[sh, sh, sh],
        compiler_params=_cparams("parallel", "arbitrary"),
    )(qn, kn, kt, vb, dy, carries)


def _lower_bound(lb_logits):
    def body(l_ref, o_ref):
        l = l_ref[...]
        m = jnp.max(l, axis=0, keepdims=True)
        e = jnp.exp(l - m)
        o_ref[...] = e[0:1, :] / jnp.sum(e, axis=0, keepdims=True)

    return pl.pallas_call(body, name="lower_bound", out_shape=jax.ShapeDtypeStruct((1, lb_logits.shape[1]), F32))(lb_logits)


def _hg_gates(hq, hf, lb):
    sq = _sig(hq)
    q = hq * sq
    sf = _sig(hf)
    f = lb + (1.0 - lb) * sf
    return q, sq, f, sf


def _hg_cum(g, c):
    hi, lo = _split(g)
    t = _tri(c, "le")
    return _dot(t, hi, NN) + _dot(t, lo, NN)


def _hg_heads(nh):
    return 4 if nh % 4 == 0 else 2 if nh % 2 == 0 else 1


def _head_cols(x, h):
    return x[:, h * HEAD:(h + 1) * HEAD]


def _per_head(x, hp, fn):
    return jnp.concatenate([fn(_head_cols(x, h), h) for h in range(hp)], axis=1)


def _head_sums(x, hp):
    return [jnp.sum(_head_cols(x, h), axis=1, keepdims=True) for h in range(hp)]


def _head_scale(cols, x, hp):
    return jnp.concatenate([cols[h] * _head_cols(x, h) for h in range(hp)], axis=1)


def _row_mask(r, width):
    return lax.broadcasted_iota(jnp.int32, (SUB, width), 0) >= r


def _hg_intra_fwd(q, k, v, b, c, hp):
    outs = []
    for bi in range(c // SUB):
        q_i, b_i = q[bi * SUB:(bi + 1) * SUB], b[bi * SUB:(bi + 1) * SUB]
        acc = jnp.zeros((SUB, hp * HEAD), F32)
        for s in range((bi + 1) * SUB):
            d = b_i - b[s:s + 1]
            if s >= bi * SUB:
                d = jnp.where(_row_mask(s - bi * SUB, hp * HEAD), d, -jnp.inf)
            acc = acc + _head_scale(_head_sums(q_i * k[s:s + 1] * jnp.exp(d), hp), v[s:s + 1], hp)
        outs.append(acc)
    return jnp.concatenate(outs, axis=0)


def _hg_intra_bwd(q, k, v, b, do, c, hp, dq_scr, dk_scr, dv_scr):
    nblk = c // SUB
    dq_scr[...] = jnp.zeros_like(dq_scr)
    for s in range(c):
        bj = s // SUB
        ks, vs, bs = k[s:s + 1], v[s:s + 1], b[s:s + 1]
        acc_k = jnp.zeros((SUB, hp * HEAD), F32)
        acc_v = jnp.zeros((SUB, hp * HEAD), F32)
        for bi in range(bj, nblk):
            sl = slice(bi * SUB, (bi + 1) * SUB)
            d = b[sl] - bs
            if bi == bj:
                d = jnp.where(_row_mask(s - bj * SUB, hp * HEAD), d, -jnp.inf)
            dec = jnp.exp(d)
            qd = q[sl] * dec
            col = _head_sums(qd * ks, hp)
            dcol = _head_sums(do[sl] * vs, hp)
            dq_scr[sl, :] += _head_scale(dcol, ks * dec, hp)
            acc_k = acc_k + _head_scale(dcol, qd, hp)
            acc_v = acc_v + _head_scale(col, do[sl], hp)
        dk_scr[s:s + 1, :] = jnp.sum(acc_k, axis=0, keepdims=True)
        dv_scr[s:s + 1, :] = jnp.sum(acc_v, axis=0, keepdims=True)


def _hgrn2_fwd(proj, lb, g_out, nh, base, c):
    s = proj.shape[0]
    nch = s // c
    hp = _hg_heads(nh)
    wide = hp * HEAD

    def body(hq_ref, hf_ref, hi_ref, og_ref, lb_ref, g_ref, o_ref, y_ref, st_ref, st):
        @pl.when(pl.program_id(1) == 0)
        def _():
            st[...] = jnp.zeros_like(st)

        st_in = [st[h] for h in range(hp)]
        for h in range(hp):
            st_ref[h] = st_in[h]
        q, _, f, _ = _hg_gates(hq_ref[...], hf_ref[...], lb_ref[...])
        k = 1.0 - f
        v = hi_ref[...]
        b = _hg_cum(jnp.log(f), c)
        bl = b[c - 1:c, :]
        qe = (q * jnp.exp(b)).astype(BF16)
        o = _per_head(qe, hp, lambda x, h: _dot(x, st_in[h].astype(BF16), NT)) + _hg_intra_fwd(q, k, v, b, c, hp)
        kd = (k * jnp.exp(bl - b)).astype(BF16)
        vb = v.astype(BF16)
        keep = jnp.exp(bl)
        for h in range(hp):
            st[h] = st_in[h] * _head_cols(keep, h) + _dot(_head_cols(vb, h), _head_cols(kd, h), TN)
        o_ref[...] = o
        og = og_ref[...]
        gout = g_ref[...]
        y_ref[...] = (_per_head(o, hp, lambda x, h: _head_norm(x, gout)) * (og * _sig(og))).astype(BF16)

    col = lambda j: pl.BlockSpec((c, wide), lambda g, i: (i, (base + j * nh) // hp + g))
    row = pl.BlockSpec((1, wide), lambda g, i: (0, g))
    gs = pl.BlockSpec((1, HEAD), lambda g, i: (0, 0))
    o = pl.BlockSpec((c, wide), lambda g, i: (i, g))
    return pl.pallas_call(
        body, name="hgrn2_fwd", grid=(nh // hp, nch),
        in_specs=[col(0), col(1), col(2), col(3), row, gs],
        out_specs=[o, o, pl.BlockSpec((hp, None, HEAD, HEAD), lambda g, i: (g, i, 0, 0))],
        out_shape=[jax.ShapeDtypeStruct((s, nh * HEAD), F32), jax.ShapeDtypeStruct((s, nh * HEAD), BF16),
                   jax.ShapeDtypeStruct((nh, nch, HEAD, HEAD), F32)],
        scratch_shapes=[pltpu.VMEM((hp, HEAD, HEAD), F32)],
        compiler_params=_cparams("parallel", "arbitrary"),
    )(proj, proj, proj, proj, lb, g_out)


def _hgrn2_bwd(proj, lb, g_out, o_pre, states, dy, nh, base, c):
    s = proj.shape[0]
    nch = s // c
    hp = _hg_heads(nh)
    wide = hp * HEAD

    def fold(x):
        return jnp.sum(x.reshape(c // SUB, SUB, x.shape[1]), axis=0)

    def body(hq_ref, hf_ref, hi_ref, og_ref, lb_ref, g_ref, o_ref, st_ref, se_ref, dy_ref,
             dhq_ref, dhf_ref, dhi_ref, dog_ref, dg_ref, dlb_ref, dst, dq_scr, dk_scr, dv_scr):
        g, i = pl.program_id(0), pl.program_id(1)

        @pl.when(i == 0)
        def _():
            dst[...] = jnp.zeros_like(dst)
            dlb_ref[...] = jnp.zeros_like(dlb_ref)

        @pl.when((i == 0) & (g == 0))
        def _():
            dg_ref[...] = jnp.zeros_like(dg_ref)

        lbv = lb_ref[...]
        hq, hf = hq_ref[...], hf_ref[...]
        q, sq, f, sf = _hg_gates(hq, hf, lbv)
        k = 1.0 - f
        v = hi_ref[...]
        b = _hg_cum(jnp.log(f), c)
        bl = b[c - 1:c, :]
        eb = jnp.exp(b)
        ebl = jnp.exp(bl - b)

        o = o_ref[...]
        gout = g_ref[...]
        og = og_ref[...]
        sg = _sig(og)
        r = _per_head(o, hp, lambda x, h: jnp.broadcast_to(
            lax.rsqrt(jnp.mean(x * x, axis=-1, keepdims=True) + EPS), x.shape))
        gw = jnp.concatenate([gout] * hp, axis=1)
        n = o * r
        dyv = dy_ref[...]
        dn = dyv * (og * sg)
        dog_ref[...] = (dyv * n * gw * (sg * (1.0 + og * (1.0 - sg)))).astype(BF16)
        dnn = fold(dn * n)
        part = _head_cols(dnn, 0)
        for h in range(1, hp):
            part = part + _head_cols(dnn, h)
        dg_ref[...] += part
        a = dn * gw
        an = a * n
        do = r * (a - n * _per_head(an, hp, lambda x, h: jnp.broadcast_to(jnp.mean(x, axis=-1, keepdims=True), x.shape)))

        st_in = [st_ref[h].astype(BF16) for h in range(hp)]
        dstv = [dst[h] for h in range(hp)]
        dstb = [d.astype(BF16) for d in dstv]
        dob = do.astype(BF16)
        vb = v.astype(BF16)
        kdb = (k * ebl).astype(BF16)
        qeb = (q * eb).astype(BF16)
        _hg_intra_bwd(q, k, v, b, do, c, hp, dq_scr, dk_scr, dv_scr)
        dq = dq_scr[...] + eb * _per_head(dob, hp, lambda x, h: _dot(x, st_in[h], NN))
        dk = dk_scr[...] + ebl * _per_head(vb, hp, lambda x, h: _dot(x, dstb[h], NN))
        dv = dv_scr[...] + _per_head(kdb, hp, lambda x, h: _dot(x, dstb[h], NT))
        keep = jnp.exp(bl)
        for h in range(hp):
            dst[h] = dstv[h] * _head_cols(keep, h) + _dot(_head_cols(dob, h), _head_cols(qeb, h), TN)

        hi_, lo_ = _split(q * dq - k * dk)
        rev = _tri(c, "ge")
        later = jnp.concatenate([jnp.sum(dstv[h] * se_ref[h], axis=0, keepdims=True) for h in range(hp)], axis=1)
        dg = _dot(rev, hi_, NN) + _dot(rev, lo_, NN) + jnp.where(i > 0, later, 0.0)
        df = dg / f - dk
        dhq_ref[...] = (dq * (sq * (1.0 + hq * (1.0 - sq)))).astype(BF16)
        dhf_ref[...] = (df * (1.0 - lbv) * sf * (1.0 - sf)).astype(BF16)
        dhi_ref[...] = dv.astype(BF16)
        dlb_ref[...] += fold(df * (1.0 - sf))

    rv = lambda i: nch - 1 - i
    col = lambda j: pl.BlockSpec((c, wide), lambda g, i: (rv(i), (base + j * nh) // hp + g))
    row = pl.BlockSpec((1, wide), lambda g, i: (0, g))
    gs = pl.BlockSpec((1, HEAD), lambda g, i: (0, 0))
    o = pl.BlockSpec((c, wide), lambda g, i: (rv(i), g))
    st = pl.BlockSpec((hp, None, HEAD, HEAD), lambda g, i: (g, rv(i), 0, 0))
    se = pl.BlockSpec((hp, None, HEAD, HEAD), lambda g, i: (g, jnp.minimum(rv(i) + 1, nch - 1), 0, 0))
    sh = jax.ShapeDtypeStruct((s, nh * HEAD), BF16)
    return pl.pallas_call(
        body, name="hgrn2_bwd", grid=(nh // hp, nch),
        in_specs=[col(0), col(1), col(2), col(3), row, gs, o, st, se, o],
        out_specs=[o, o, o, o, pl.BlockSpec((SUB, HEAD), lambda g, i: (0, 0)), pl.BlockSpec((SUB, wide), lambda g, i: (0, g))],
        out_shape=[sh, sh, sh, sh, jax.ShapeDtypeStruct((SUB, HEAD), F32), jax.ShapeDtypeStruct((SUB, nh * HEAD), F32)],
        scratch_shapes=[pltpu.VMEM((hp, HEAD, HEAD), F32), pltpu.VMEM((c, wide), F32), pltpu.VMEM((c, wide), F32),
                        pltpu.VMEM((c, wide), F32)],
        compiler_params=_cparams("arbitrary", "arbitrary"),
    )(proj, proj, proj, proj, lb, g_out, o_pre, states, states, dy)


def _merge_tiles(s, d, gate_col):
    tr = _pick(s, (256, 128, 64, 32, 16, 8))
    tc = 128
    for cand in (512, 256):
        if d % cand == 0 and gate_col % cand == 0:
            tc = cand
            break
    return tr, tc


def _merge_fwd(proj, ya, yb, gate_col):
    s, d = ya.shape
    tr, tc = _merge_tiles(s, d, gate_col)
    ga0, gb0 = gate_col // tc, (gate_col + d) // tc

    def body(ga_ref, gb_ref, ya_ref, yb_ref, m_ref):
        m_ref[...] = (_sig(ga_ref[...]) * ya_ref[...] + _sig(gb_ref[...]) * yb_ref[...]).astype(BF16)

    o = pl.BlockSpec((tr, tc), lambda i, j: (i, j))
    return pl.pallas_call(
        body, name="merge_fwd", grid=(s // tr, d // tc),
        in_specs=[pl.BlockSpec((tr, tc), lambda i, j: (i, ga0 + j)), pl.BlockSpec((tr, tc), lambda i, j: (i, gb0 + j)), o, o],
        out_specs=o, out_shape=jax.ShapeDtypeStruct((s, d), BF16),
        compiler_params=_cparams("parallel", "parallel"),
    )(proj, proj, ya, yb)


def _merge_bwd(proj, ya, yb, dm, gate_col):
    s, d = ya.shape
    tr, tc = _merge_tiles(s, d, gate_col)
    ga0, gb0 = gate_col // tc, (gate_col + d) // tc

    def body(ga_ref, gb_ref, ya_ref, yb_ref, dm_ref, dya_ref, dyb_ref, dga_ref, dgb_ref):
        dmv = dm_ref[...]
        sa, sb = _sig(ga_ref[...]), _sig(gb_ref[...])
        dya_ref[...] = (dmv * sa).astype(BF16)
        dyb_ref[...] = (dmv * sb).astype(BF16)
        dga_ref[...] = (dmv * ya_ref[...] * sa * (1.0 - sa)).astype(BF16)
        dgb_ref[...] = (dmv * yb_ref[...] * sb * (1.0 - sb)).astype(BF16)

    o = pl.BlockSpec((tr, tc), lambda i, j: (i, j))
    sh = jax.ShapeDtypeStruct((s, d), BF16)
    return pl.pallas_call(
        body, name="merge_bwd", grid=(s // tr, d // tc),
        in_specs=[pl.BlockSpec((tr, tc), lambda i, j: (i, ga0 + j)), pl.BlockSpec((tr, tc), lambda i, j: (i, gb0 + j)), o, o, o],
        out_specs=[o, o, o, o], out_shape=[sh, sh, sh, sh],
        compiler_params=_cparams("parallel", "parallel"),
    )(proj, proj, ya, yb, dm)


CONV_ROWS = 512


def _conv_ext(ref, i, rows, s, before, after):
    parts = []
    if before:
        p = ref[pl.ds(pl.multiple_of(jnp.maximum(i * rows - before, 0), SUB), before), :]
        parts.append(jnp.where(i > 0, p, 0.0))
    parts.append(ref[pl.ds(pl.multiple_of(i * rows, SUB), rows), :])
    if after:
        nxt = ref[pl.ds(pl.multiple_of(jnp.minimum((i + 1) * rows, s - after), SUB), after), :]
        parts.append(jnp.where((i + 1) * rows < s, nxt, 0.0))
    return jnp.concatenate(parts, axis=0)


def _conv3(ext, w, bias):
    x1 = pltpu.roll(ext, 1, 0)
    x2 = pltpu.roll(ext, 2, 0)
    return bias + w[0:1, :] * x2 + w[1:2, :] * x1 + w[2:3, :] * ext, x1, x2


def _convffn_fwd(up, conv_w, conv_b, dff):
    s = up.shape[0]
    tc = HEAD
    nf = dff // tc
    rows = _pick(s, (CONV_ROWS, 256, 128, 64, 32, 16, 8))

    def body(ug_ref, uv_ref, wg_ref, wv_ref, bg_ref, bv_ref, a_ref):
        wg, wv, bg, bv = wg_ref[...], wv_ref[...], bg_ref[...], bv_ref[...]

        def step(i, _):
            g = _conv3(_conv_ext(ug_ref, i, rows, s, SUB, 0), wg, bg)[0][SUB:]
            v = _conv3(_conv_ext(uv_ref, i, rows, s, SUB, 0), wv, bv)[0][SUB:]
            a_ref[pl.ds(pl.multiple_of(i * rows, SUB), rows), :] = (g * _sig(g) * v).astype(BF16)
            return 0

        lax.fori_loop(0, s // rows, step, 0)

    cg = lambda r: pl.BlockSpec((r, tc), lambda j: (0, j))
    cv = lambda r: pl.BlockSpec((r, tc), lambda j: (0, nf + j))
    return pl.pallas_call(
        body, name="convffn_fwd", grid=(nf,),
        in_specs=[cg(s), cv(s), cg(3), cv(3), cg(1), cv(1)], out_specs=cg(s),
        out_shape=jax.ShapeDtypeStruct((s, dff), BF16),
        compiler_params=_cparams("parallel"),
    )(up, up, conv_w, conv_w, conv_b, conv_b)


def _convffn_bwd(up, conv_w, conv_b, dact, dff):
    s = up.shape[0]
    tc = HEAD
    nf = dff // tc
    rows = _pick(s, (CONV_ROWS, 256, 128, 64, 32, 16, 8))
    n_ext = rows + SUB

    def body(ug_ref, uv_ref, wg_ref, wv_ref, bg_ref, bv_ref, da_ref,
             dug_ref, duv_ref, dwg_ref, dwv_ref, dbg_ref, dbv_ref):
        wg, wv, bg, bv = wg_ref[...], wv_ref[...], bg_ref[...], bv_ref[...]

        def fold(x):
            return jnp.sum(x.reshape(rows // SUB, SUB, tc), axis=0)

        def one(ext, x1, x2, d_ext, w):
            d1 = pltpu.roll(d_ext, n_ext - 1, 0)[:rows]
            d2 = pltpu.roll(d_ext, n_ext - 2, 0)[:rows]
            dc = d_ext[:rows]
            du = w[2:3, :] * dc + w[1:2, :] * d1 + w[0:1, :] * d2
            sl = slice(SUB, SUB + rows)
            return du, (fold(dc * x2[sl]), fold(dc * x1[sl]), fold(dc * ext[sl]), fold(dc))

        def step(i, acc):
            eg = _conv_ext(ug_ref, i, rows, s, SUB, SUB)
            ev = _conv_ext(uv_ref, i, rows, s, SUB, SUB)
            g, g1, g2 = _conv3(eg, wg, bg)
            v, v1, v2 = _conv3(ev, wv, bv)
            g, v = g[SUB:], v[SUB:]
            da = _conv_ext(da_ref, i, rows, s, 0, SUB)
            sg = _sig(g)
            dg = da * v * (sg * (1.0 + g * (1.0 - sg)))
            dv = da * (g * sg)
            dug, pg = one(eg, g1, g2, dg, wg)
            duv, pv = one(ev, v1, v2, dv, wv)
            at = pl.ds(pl.multiple_of(i * rows, SUB), rows)
            dug_ref[at, :] = dug.astype(BF16)
            duv_ref[at, :] = duv.astype(BF16)
            return tuple(a + p for a, p in zip(acc, pg + pv))

        zero = jnp.zeros((SUB, tc), F32)
        acc = lax.fori_loop(0, s // rows, step, (zero,) * 8)
        red = [jnp.sum(a, axis=0, keepdims=True) for a in acc]
        for j in range(3):
            dwg_ref[j:j + 1, :] = red[j]
            dwv_ref[j:j + 1, :] = red[4 + j]
        dbg_ref[...] = red[3]
        dbv_ref[...] = red[7]

    cg = lambda r: pl.BlockSpec((r, tc), lambda j: (0, j))
    cv = lambda r: pl.BlockSpec((r, tc), lambda j: (0, nf + j))
    outs = pl.pallas_call(
        body, name="convffn_bwd", grid=(nf,),
        in_specs=[cg(s), cv(s), cg(3), cv(3), cg(1), cv(1), cg(s)],
        out_specs=[cg(s), cg(s), cg(3), cg(3), cg(1), cg(1)],
        out_shape=[jax.ShapeDtypeStruct((s, dff), BF16), jax.ShapeDtypeStruct((s, dff), BF16),
                   jax.ShapeDtypeStruct((3, dff), F32), jax.ShapeDtypeStruct((3, dff), F32),
                   jax.ShapeDtypeStruct((1, dff), F32), jax.ShapeDtypeStruct((1, dff), F32)],
        compiler_params=_cparams("parallel"),
    )(up, up, conv_w, conv_w, conv_b, conv_b, dact)
    return outs


def _loss_head(out, target):
    s, d = out.shape
    tr = _pick(s, (256, 128, 64, 32, 16, 8))

    def body(o_ref, t_ref, d_ref, l_ref):
        @pl.when(pl.program_id(0) == 0)
        def _():
            l_ref[...] = jnp.zeros_like(l_ref)

        err = o_ref[...] - t_ref[...]
        d_ref[...] = err * (1.0 / d)
        sq = jnp.sum((err * err).reshape(tr // SUB, SUB, d), axis=0)
        part = sq[:, 0:HEAD]
        for j in range(1, d // HEAD):
            part = part + sq[:, j * HEAD:(j + 1) * HEAD]
        l_ref[...] += part

    blk = pl.BlockSpec((tr, d), lambda i: (i, 0))
    return pl.pallas_call(
        body, name="loss_head", grid=(s // tr,), in_specs=[blk, blk],
        out_specs=[blk, pl.BlockSpec((SUB, HEAD), lambda i: (0, 0))],
        out_shape=[jax.ShapeDtypeStruct((s, d), F32), jax.ShapeDtypeStruct((SUB, HEAD), F32)],
        compiler_params=_cparams("arbitrary"),
    )(out, target)


def _sum_rows(name, parts):
    def body(p_ref, o_ref):
        o_ref[...] = jnp.sum(p_ref[...], axis=0, keepdims=True)

    return pl.pallas_call(body, name=name, out_shape=jax.ShapeDtypeStruct((1, parts.shape[1]), F32))(parts)


def _local_step(x, target, g_mix, g_q, g_k, lb_logits, g_hg_out, g_ffn, conv_w, conv_b, w_in, later_weights, grads_ready):
    s, d = x.shape
    nh = lb_logits.shape[1] // HEAD
    wid = nh * HEAD
    blk = _pick(s, (256, 128))
    chunk = _pick(s, (HG_CHUNK,))
    gate_col = 7 * wid

    u = _rmsnorm_fwd("rmsnorm_mix", x, g_mix)
    proj = _matmul("in_proj", u, w_in, "nn", F32)
    qn, kn, vb, kt, vt = _qk_norm_fwd(proj, g_q, g_k, nh, blk)
    y_a, carries = _sb_attn_fwd(qn, kn, vt, nh, blk)
    lb = _lower_bound(lb_logits)
    o_pre, y_b, states = _hgrn2_fwd(proj, lb, g_hg_out, nh, 3 * nh, chunk)
    later = later_weights(o_pre)
    p_a, p_b, w_up = later["p_a"], later["p_b"], later["w_up"]
    w_o = later["w_o"].reshape(1, d, d)
    dff = later["w_down"].shape[1] * N_CHIPS
    w_down = later["w_down"].reshape(1, dff, d)
    ya_p = _matmul("proj_a", y_a, p_a, "nn", F32)
    yb_p = _matmul("proj_b", y_b, p_b, "nn", F32)
    m = _merge_fwd(proj, ya_p, yb_p, gate_col)
    h = _matmul("out_proj", m, w_o, "nn", F32, add=x)
    u2 = _rmsnorm_fwd("rmsnorm_ffn", h, g_ffn)
    up = _matmul("up_proj", u2, w_up, "nn", F32)
    act = _convffn_fwd(up, conv_w, conv_b, dff)
    out = _matmul("down_proj", act, w_down, "nn", F32, add=h)
    dout, sq = _loss_head(out, target)

    dact = _matmul("d_act", dout, w_down, "nt", F32)
    g_w_down = _matmul_tn("g_w_down", act, dout, 1, BF16).reshape(N_CHIPS, dff // N_CHIPS, d)
    dup_g, dup_v, dcw_g, dcw_v, dcb_g, dcb_v = _convffn_bwd(up, conv_w, conv_b, dact, dff)
    dup = jnp.concatenate([dup_g, dup_v], axis=1)
    g_w_up = _matmul_tn("g_w_up", u2, dup, N_CHIPS, BF16)
    sent = grads_ready(("w_down", "w_up"), [g_w_down, g_w_up])
    du2 = _matmul("d_u2", dup, w_up, "nt", F32)
    dh, pg_ffn = _rmsnorm_bwd("rmsnorm_ffn_bwd", h, g_ffn + sent, du2, dout)
    dm = _matmul("d_m", dh, w_o, "nt", F32)
    g_w_o = _matmul_tn("g_w_o", m, dh, 1, BF16).reshape(N_CHIPS, d // N_CHIPS, d)
    dya_p, dyb_p, dga, dgb = _merge_bwd(proj, ya_p, yb_p, dm, gate_col)
    g_p_a = _matmul_tn("g_p_a", y_a, dya_p, N_CHIPS, BF16)
    g_p_b = _matmul_tn("g_p_b", y_b, dyb_p, N_CHIPS, BF16)
    sent = grads_ready(("w_o", "p_a", "p_b"), [g_w_o, g_p_a, g_p_b])
    dy_a = _matmul("d_y_a", dya_p, p_a, "nt", F32)
    dy_b = _matmul("d_y_b", dyb_p, p_b, "nt", F32)
    dhq, dhf, dhi, dog, pg_hg, p_lb = _hgrn2_bwd(proj, lb, g_hg_out + sent, o_pre, states, dy_b, nh, 3 * nh, chunk)
    dqn, dkn, dv = _sb_attn_bwd(qn, kn, kt, vb, dy_a, carries, nh, blk)
    dq, dk, pg_q, pg_k = _qk_norm_bwd(proj, g_q, g_k, dqn, dkn, nh)
    dproj = jnp.concatenate([dq, dk, dv.astype(BF16), dhq, dhf, dhi, dog, dga, dgb], axis=1)
    g_w_in = _matmul_tn("g_w_in", u, dproj, N_CHIPS, BF16)
    sent = grads_ready(("w_in",), [g_w_in])
    du = _matmul("d_u", dproj, w_in, "nt", F32)
    dx, pg_mix = _rmsnorm_bwd("rmsnorm_mix_bwd", x, g_mix + sent, du, dh)

    small = dict(
        g_mix=_sum_rows("sum_g_mix", pg_mix), g_q=_sum_rows("sum_g_q", pg_q), g_k=_sum_rows("sum_g_k", pg_k),
        lb=_sum_rows("sum_lb", p_lb), g_hg_out=_sum_rows("sum_g_hg", pg_hg), g_ffn=_sum_rows("sum_g_ffn", pg_ffn),
        conv_w=jnp.concatenate([dcw_g, dcw_v], axis=1), conv_b=jnp.concatenate([dcb_g, dcb_v], axis=1),
        sq=_sum_rows("sum_sq", sq),
    )
    return dx, small, lb


ANY = pl.BlockSpec(memory_space=pl.ANY)


def _place():
    x, y, c = lax.axis_index("x"), lax.axis_index("y"), lax.axis_index("c")
    chips = [(1 - x, y), (x, 1 - y), (1 - x, 1 - y)]
    return x, y, c, chips


def _remote(src, dst, send_sem, recv_sem, to):
    return pltpu.make_async_remote_copy(src_ref=src, dst_ref=dst, send_sem=send_sem, recv_sem=recv_sem,
                                        device_id=to, device_id_type=MESH)


def _cast_bf16(name, w):
    r, c = w.shape
    tr = _pick(r, (256, 128, 64, 32, 16))

    def body(w_ref, o_ref):
        o_ref[...] = w_ref[...].astype(BF16)

    return pl.pallas_call(
        body, name=name, grid=(r // tr,), in_specs=[pl.BlockSpec((tr, c), lambda i: (i, 0))],
        out_specs=pl.BlockSpec((tr, c), lambda i: (i, 0)), out_shape=jax.ShapeDtypeStruct((r, c), BF16),
        compiler_params=_cparams("parallel"),
    )(w)


def _gather_weights(shards):
    n = len(shards)

    def body(*refs):
        ins, outs = refs[:n], refs[n:2 * n]
        send, recv, local = refs[2 * n:]
        x, y, c, chips = _place()
        mine = 2 * x + y
        sends, owns = [], []
        for k in range(n):
            half = ins[k].shape[0] // 2
            rows = pl.ds(c * half, half)
            own = pltpu.make_async_copy(ins[k], outs[k].at[mine], local.at[k])
            own.start()
            owns.append(own)
            for j, (px, py) in enumerate(chips):
                cp = _remote(ins[k].at[rows], outs[k].at[mine, rows], send.at[k, j], recv.at[k, j], (px, py, c))
                cp.start()
                sends.append(cp)
        for k in range(n):
            half = ins[k].shape[0] // 2
            rows = pl.ds(c * half, half)
            for j, (px, py) in enumerate(chips):
                part = outs[k].at[2 * px + py, rows]
                _remote(part, part, send.at[k, j], recv.at[k, j], (px, py, c)).wait_recv()
                fw = _remote(part, part, send.at[k, 3 + j], recv.at[k, 3 + j], (x, y, 1 - c))
                fw.start()
                sends.append(fw)
        for k in range(n):
            half = ins[k].shape[0] // 2
            other = pl.ds((1 - c) * half, half)
            for j, (px, py) in enumerate(chips):
                part = outs[k].at[2 * px + py, other]
                _remote(part, part, send.at[k, 3 + j], recv.at[k, 3 + j], (x, y, 1 - c)).wait_recv()
        for cp in sends:
            cp.wait_send()
        for cp in owns:
            cp.wait()

    return pl.pallas_call(
        body, name="gather_weights", in_specs=[ANY] * n, out_specs=[ANY] * n,
        out_shape=[jax.ShapeDtypeStruct((N_CHIPS,) + w.shape, w.dtype) for w in shards],
        scratch_shapes=[pltpu.SemaphoreType.DMA((n, 6)), pltpu.SemaphoreType.DMA((n, 6)), pltpu.SemaphoreType.DMA((n,))],
    )(*shards)


def _to_sibling(name, srcs):
    n = len(srcs)

    def body(*refs):
        ins, outs = refs[:n], refs[n:2 * n]
        send, recv = refs[2 * n:]
        x, y, c, _ = _place()
        cps = []
        for k in range(n):
            cp = _remote(ins[k], outs[k], send.at[k], recv.at[k], (x, y, 1 - c))
            cp.start()
            cps.append(cp)
        for cp in cps:
            cp.wait_recv()
        for cp in cps:
            cp.wait_send()

    return pl.pallas_call(
        body, name=name, in_specs=[ANY] * n, out_specs=[ANY] * n,
        out_shape=[jax.ShapeDtypeStruct(a.shape, a.dtype) for a in srcs],
        scratch_shapes=[pltpu.SemaphoreType.DMA((n,)), pltpu.SemaphoreType.DMA((n,))],
    )(*srcs)


HBM = pl.BlockSpec(memory_space=pltpu.HBM)
SEM = pl.BlockSpec(memory_space=pltpu.SEMAPHORE)
SIDE = pltpu.SideEffectType.DATAFLOW_SIDE_EFFECTING
N_PEERS = 7


def _peer(r):
    x, y, c = lax.axis_index("x"), lax.axis_index("y"), lax.axis_index("c")
    return (1 - x if r & 4 else x), (1 - y if r & 2 else y), (1 - c if r & 1 else c)


def _partial_copy(src, land, send, recv, k, r):
    px, py, pc = _peer(r)
    half = src.shape[1] // 2
    sem = k * N_PEERS + r - 1
    return _remote(src.at[2 * px + py, pl.ds(pc * half, half)], land.at[r - 1], send.at[sem], recv.at[sem], (px, py, pc))


def _shard_copy(full, send, recv, k, r):
    x, y, c = lax.axis_index("x"), lax.axis_index("y"), lax.axis_index("c")
    half = full.shape[1] // 2
    part = full.at[2 * x + y, pl.ds(c * half, half)]
    sem = k * (N_PEERS - 1) + r - 2
    return _remote(part, part, send.at[sem], recv.at[sem], _peer(r))


def _start_copies(name, arrays, n_sems, copies):
    n = len(arrays)

    def body(*refs):
        send, recv, token = refs[n], refs[n + 1], refs[-1]
        for cp in copies(refs[:n], send, recv):
            cp.start()
        token[...] = jnp.zeros_like(token)

    sem = pltpu.SemaphoreType.DMA((n_sems,))
    outs = pl.pallas_call(
        body, name=name, in_specs=[HBM] * n,
        out_specs=[SEM, SEM] + [HBM] * n + [pl.BlockSpec(memory_space=pltpu.VMEM)],
        out_shape=[sem, sem] + [pltpu.HBM(a.shape, a.dtype) for a in arrays] + [jax.ShapeDtypeStruct((SUB, HEAD), F32)],
        input_output_aliases={i: 2 + i for i in range(n)},
        compiler_params=pltpu.CompilerParams(has_side_effects=SIDE),
    )(*[pltpu.with_memory_space_constraint(a, pltpu.HBM) for a in arrays])
    return outs[0], outs[1], list(outs[2:2 + n]), outs[-1]


def _wait_copies(name, send, recv, arrays, after, copies):
    n = len(arrays)

    def body(*refs):
        for cp in copies(refs[:n], refs[n], refs[n + 1]):
            cp.wait_send()
            cp.wait_recv()

    return list(pl.pallas_call(
        body, name=name, in_specs=[HBM] * n + [SEM, SEM, ANY], out_specs=[HBM] * n,
        out_shape=[pltpu.HBM(a.shape, a.dtype) for a in arrays],
        input_output_aliases={i: i for i in range(n)},
        compiler_params=pltpu.CompilerParams(has_side_effects=SIDE),
    )(*arrays, send, recv, after))


def _partial_copies(n):
    def copies(refs, send, recv):
        return [_partial_copy(refs[k], refs[n + k], send, recv, k, r) for k in range(n) for r in range(1, N_PEERS + 1)]
    return copies


def _shard_copies(n):
    def copies(refs, send, recv):
        return [_shard_copy(refs[k], send, recv, k, r) for k in range(n) for r in range(2, N_PEERS + 1)]
    return copies


def _cast_place(name, w, shard):
    r, c = w.shape
    tr = _pick(r, (256, 128, 64, 32, 16))

    def body(s_ref, w_ref, o_ref):
        o_ref[...] = w_ref[...].astype(BF16)

    return pl.pallas_call(
        body, name=name,
        grid_spec=pltpu.PrefetchScalarGridSpec(
            num_scalar_prefetch=1, grid=(r // tr,), in_specs=[pl.BlockSpec((tr, c), lambda i, sr: (i, 0))],
            out_specs=pl.BlockSpec((None, tr, c), lambda i, sr: (sr[0], i, 0))),
        out_shape=jax.ShapeDtypeStruct((N_CHIPS, r, c), BF16),
        compiler_params=_cparams("parallel"),
    )(shard, w)


def _sum_peers(name, g, land, shard, core):
    _, r, cols = g.shape
    half = r // 2
    tr = _pick(half, (128, 64, 32, 16))
    nt = half // tr

    def body(s_ref, c_ref, g_ref, l_ref, o_ref):
        acc = g_ref[...].astype(F32)
        for j in range(N_PEERS):
            acc = acc + l_ref[j].astype(F32)
        o_ref[...] = acc

    return pl.pallas_call(
        body, name=name,
        grid_spec=pltpu.PrefetchScalarGridSpec(
            num_scalar_prefetch=2, grid=(nt,),
            in_specs=[pl.BlockSpec((None, tr, cols), lambda i, sr, cr: (sr[0], cr[0] * nt + i, 0)),
                      pl.BlockSpec((N_PEERS, tr, cols), lambda i, sr, cr: (0, i, 0))],
            out_specs=pl.BlockSpec((tr, cols), lambda i, sr, cr: (i, 0))),
        out_shape=jax.ShapeDtypeStruct((half, cols), F32),
        compiler_params=_cparams("parallel"),
    )(shard, core, g, land)


def _join_halves(mine, got, c):
    half, cols = mine.shape
    tr = _pick(half, (256, 128, 64, 32, 16, 8))
    nt = half // tr

    def body(c_ref, a_ref, b_ref, o_ref):
        i = pl.program_id(0)
        own = (i // nt) == c_ref[0]

        @pl.when(own)
        def _():
            o_ref[...] = a_ref[...]

        @pl.when(jnp.logical_not(own))
        def _():
            o_ref[...] = b_ref[...]

    blk = pl.BlockSpec((tr, cols), lambda i, cr: (i % nt, 0))
    return pl.pallas_call(
        body, name="join_halves",
        grid_spec=pltpu.PrefetchScalarGridSpec(num_scalar_prefetch=1, grid=(2 * nt,), in_specs=[blk, blk],
                                               out_specs=pl.BlockSpec((tr, cols), lambda i, cr: (i, 0))),
        out_shape=jax.ShapeDtypeStruct((2 * half, cols), F32),
        compiler_params=_cparams("parallel"),
    )(c, mine, got)


def _all_gather_rows(name, row):
    p = row.shape[1]

    def body(in_ref, out_ref, send, recv, local):
        x, y, c, _ = _place()
        me = 4 * x + 2 * y + c
        own = pltpu.make_async_copy(in_ref, out_ref.at[me], local)
        own.start()
        cps = []
        for k in range(1, 8):
            px, py, pc = x ^ (k >> 2), y ^ ((k >> 1) & 1), c ^ (k & 1)
            cp = _remote(in_ref, out_ref.at[me], send.at[k - 1], recv.at[k - 1], (px, py, pc))
            cp.start()
            cps.append(cp)
        for cp in cps:
            cp.wait_recv()
        for cp in cps:
            cp.wait_send()
        own.wait()

    return pl.pallas_call(
        body, name=name, in_specs=[ANY], out_specs=ANY,
        out_shape=jax.ShapeDtypeStruct((8, 1, p), F32),
        scratch_shapes=[pltpu.SemaphoreType.DMA((7,)), pltpu.SemaphoreType.DMA((7,)), pltpu.SemaphoreType.DMA],
    )(row)


def _sum_devices(rows):
    def body(r_ref, o_ref):
        acc = r_ref[0]
        for k in range(1, 8):
            acc = acc + r_ref[k]
        o_ref[...] = acc

    return pl.pallas_call(body, name="sum_devices", out_shape=jax.ShapeDtypeStruct(rows.shape[1:], F32))(rows)


def _adamw(name, w, g, m, v):
    r, c = w.shape
    tr = _pick(r, (128, 64, 32, 16, 8))
    bc1 = 1.0 - ADAM_B1 ** ADAM_STEP
    bc2 = 1.0 - ADAM_B2 ** ADAM_STEP

    def body(w_ref, g_ref, m_ref, v_ref, d_ref, nm_ref, nv_ref):
        gv = g_ref[...]
        nm = ADAM_B1 * m_ref[...] + (1.0 - ADAM_B1) * gv
        nv = ADAM_B2 * v_ref[...] + (1.0 - ADAM_B2) * (gv * gv)
        d_ref[...] = -ADAM_LR * ((nm / bc1) / (jnp.sqrt(nv / bc2) + ADAM_EPS) + ADAM_WD * w_ref[...])
        nm_ref[...] = nm
        nv_ref[...] = nv

    blk = pl.BlockSpec((tr, c), lambda i: (i, 0))
    sh = jax.ShapeDtypeStruct((r, c), F32)
    return pl.pallas_call(
        body, name=name, grid=(r // tr,), in_specs=[blk] * 4, out_specs=[blk] * 3, out_shape=[sh] * 3,
        compiler_params=_cparams("parallel"),
    )(w, g, m, v)


def _lb_logits_grad(dlb, lb):
    def body(d_ref, lb_ref, o_ref):
        lbv = lb_ref[...]
        t = d_ref[...] * lbv * (1.0 - lbv)
        o_ref[0:1, :] = t
        o_ref[1:2, :] = -t

    return pl.pallas_call(body, name="lb_logits_grad", out_shape=jax.ShapeDtypeStruct((2, dlb.shape[1]), F32))(dlb, lb)


BIG = ("w_in", "p_a", "p_b", "w_o", "w_up", "w_down")
SMALL = ("g_mix", "g_q", "g_k", "lb_logits", "g_hg_out", "g_ffn", "conv_w", "conv_b")
ORDER = ("g_mix", "w_in", "g_q", "g_k", "lb_logits", "g_hg_out", "p_a", "p_b", "w_o", "g_ffn", "w_up", "conv_w", "conv_b", "w_down")


def kernel(x, g_mix, w_in, g_q, g_k, lb_logits, g_hg_out, p_a, p_b, w_o, g_ffn, w_up, conv_w, conv_b, w_down, loss_target, m_g_mix, m_w_in, m_g_q, m_g_k, m_lb_logits, m_g_hg_out, m_p_a, m_p_b, m_w_o, m_g_ffn, m_w_up, m_conv_w, m_conv_b, m_w_down, v_g_mix, v_w_in, v_g_q, v_g_k, v_lb_logits, v_g_hg_out, v_p_a, v_p_b, v_w_o, v_g_ffn, v_w_up, v_conv_w, v_conv_b, v_w_down):
    assert lb_logits.shape[0] == 2, "the lower bound is the first row of a two-row softmax"
    w = dict(g_mix=g_mix, w_in=w_in[0], g_q=g_q, g_k=g_k, lb_logits=lb_logits, g_hg_out=g_hg_out, p_a=p_a[0], p_b=p_b[0],
             w_o=w_o[0], g_ffn=g_ffn, w_up=w_up[0], conv_w=conv_w[0], conv_b=conv_b, w_down=w_down[0])
    mom = dict(g_mix=m_g_mix, w_in=m_w_in[0], g_q=m_g_q, g_k=m_g_k, lb_logits=m_lb_logits, g_hg_out=m_g_hg_out, p_a=m_p_a[0],
               p_b=m_p_b[0], w_o=m_w_o[0], g_ffn=m_g_ffn, w_up=m_w_up[0], conv_w=m_conv_w[0], conv_b=m_conv_b, w_down=m_w_down[0])
    var = dict(g_mix=v_g_mix, w_in=v_w_in[0], g_q=v_g_q, g_k=v_g_k, lb_logits=v_lb_logits, g_hg_out=v_g_hg_out, p_a=v_p_a[0],
               p_b=v_p_b[0], w_o=v_w_o[0], g_ffn=v_g_ffn, w_up=v_w_up[0], conv_w=v_conv_w[0], conv_b=v_conv_b, w_down=v_w_down[0])
    d = x.shape[2]
    cx, cy, cc = lax.axis_index("x"), lax.axis_index("y"), lax.axis_index("c")
    shard = (2 * cx + cy).astype(jnp.int32).reshape(1)
    core = cc.astype(jnp.int32).reshape(1)

    w_in_full = _gather_weights([_cast_bf16("cast_w_in", w["w_in"])])[0]
    later = ("p_a", "p_b", "w_o", "w_up", "w_down")
    n_later = len(later)
    fulls = [_cast_place("cast_" + n, w[n], shard) for n in later]
    w_send, w_recv, fulls, w_token = _start_copies("weights_start", fulls + [w_in_full], n_later * (N_PEERS - 1),
                                                   _shard_copies(n_later))
    w_in_full = fulls.pop()

    def later_weights(after):
        return dict(zip(later, _wait_copies("weights_wait", w_send, w_recv, fulls, after, _shard_copies(n_later))))

    cw = conv_w.shape[2]
    rows = _all_gather_rows("gather_conv_w", w["conv_w"].reshape(1, 3 * cw))
    conv_full = jnp.concatenate([rows[2 * s, 0].reshape(3, cw) for s in range(N_CHIPS)], axis=1)

    pending = []

    def grads_ready(names, gs):
        n = len(gs)
        lands = [lax.empty((N_PEERS, g.shape[1] // 2, g.shape[2]), BF16) for g in gs]
        send, recv, arrays, token = _start_copies("partials_start_" + names[0], list(gs) + lands, n * N_PEERS, _partial_copies(n))
        pending.append((names, send, recv, arrays))
        return token[0:1, 0:1]

    dx, small, lb = _local_step(x[0], loss_target[0], g_mix + w_token[0:1, 0:1], g_q, g_k, lb_logits, g_hg_out, g_ffn,
                                conv_full, conv_b, w_in_full, later_weights, grads_ready)

    mine = {}
    after = dx
    for names, send, recv, arrays in pending:
        n = len(names)
        arrays = _wait_copies("partials_wait_" + names[0], send, recv, arrays, after, _partial_copies(n))
        for k, name in enumerate(names):
            mine[name] = _sum_peers("sum_" + name, arrays[k], arrays[n + k], shard, core)
        after = mine[names[-1]]
    theirs = _to_sibling("reduced_to_sibling", [mine[n] for n in BIG])
    grads = {n: _join_halves(mine[n], b, core) for n, b in zip(BIG, theirs)}

    names = ("g_mix", "g_q", "g_k", "lb", "g_hg_out", "g_ffn", "conv_b", "sq")
    packed = jnp.concatenate([small[n] for n in names] + [small["conv_w"].reshape(1, -1)], axis=1)
    total = _sum_devices(_all_gather_rows("gather_small_grads", packed))
    off = 0
    red = {}
    for n in names:
        ln = small[n].shape[1]
        red[n] = total[:, off:off + ln]
        off += ln
    conv_all = total[:, off:].reshape(3, -1)
    loss = 0.5 * jnp.sum(red["sq"]) / d
    grads["conv_w"] = lax.dynamic_slice_in_dim(conv_all, (2 * cx + cy) * cw, cw, axis=1)
    grads["lb_logits"] = _lb_logits_grad(red["lb"], lb)
    for n in ("g_mix", "g_q", "g_k", "g_hg_out", "g_ffn", "conv_b"):
        grads[n] = red[n]

    delta, new_m, new_v = {}, {}, {}
    for n in ORDER:
        delta[n], new_m[n], new_v[n] = _adamw("adamw_" + n, w[n], grads[n], mom[n], var[n])

    def shaped(a, like):
        return a.reshape(like.shape)

    ref_w = dict(g_mix=g_mix, w_in=w_in, g_q=g_q, g_k=g_k, lb_logits=lb_logits, g_hg_out=g_hg_out, p_a=p_a, p_b=p_b, w_o=w_o,
                 g_ffn=g_ffn, w_up=w_up, conv_w=conv_w, conv_b=conv_b, w_down=w_down)
    outs = [loss, dx[None]]
    for group in (grads, delta, new_m, new_v):
        outs += [shaped(group[n], ref_w[n]) for n in ORDER]
    return tuple(outs)
```

```python
import functools

import jax
import jax.numpy as jnp
from jax import lax
from jax.experimental import pallas as pl
from jax.experimental.pallas import tpu as pltpu

F32 = jnp.float32
BF16 = jnp.bfloat16
HEAD = 128
EPS = 1e-6
N_CHIPS = 4
HG_CHUNK = 64
SUB = 8
ADAM_LR, ADAM_B1, ADAM_B2, ADAM_EPS, ADAM_WD, ADAM_STEP = 0.001, 0.9, 0.999, 1e-08, 0.01, 10
VMEM_LIMIT = 56 * 1024 * 1024
MESH = pl.DeviceIdType.MESH

NN = (((1,), (0,)), ((), ()))
NT = (((1,), (1,)), ((), ()))
TN = (((0,), (0,)), ((), ()))


def _cparams(*sem):
    return pltpu.CompilerParams(dimension_semantics=sem if sem else None, vmem_limit_bytes=VMEM_LIMIT)


def _pick(n, cands):
    for c in cands:
        if c <= n and n % c == 0:
            return c
    return n


def _sig(x):
    return 0.5 * jnp.tanh(0.5 * x) + 0.5


def _dot(a, b, dims):
    return lax.dot_general(a, b, dims, preferred_element_type=F32)


def _split(x):
    hi = x.astype(BF16)
    lo = (x - hi.astype(F32)).astype(BF16)
    return hi, lo


def _tri(n, kind):
    r = lax.broadcasted_iota(jnp.int32, (n, n), 0)
    c = lax.broadcasted_iota(jnp.int32, (n, n), 1)
    m = {"ge": c >= r, "gt": c > r, "le": c <= r, "lt": c < r}[kind]
    return jnp.where(m, 1.0, 0.0).astype(BF16)


TILE_M = (1024, 512, 256, 128)
TILE_N = (1408, 1024, 512, 256, 128)
TILE_K = (2816, 2048, 1408, 1024, 512, 256, 128)


def _matmul(name, a, b, mode, out_dtype, add=None, out_shards=None):
    if mode == "nn":
        m, k = a.shape
        g, _, ns = b.shape
        n = g * ns
        ns_out = n // out_shards if out_shards else ns
        tm, tn, tk = _pick(m, TILE_M), _pick(min(ns, ns_out), TILE_N), _pick(k, TILE_K)
        nps = ns // tn
        grid = (m // tm, n // tn, k // tk)
        a_spec = pl.BlockSpec((tm, tk), lambda i, j, kk: (i, kk))
        b_spec = pl.BlockSpec((None, tk, tn), lambda i, j, kk: (j // nps, kk, j % nps))
        if out_shards:
            npo = ns_out // tn
            o_spec = pl.BlockSpec((None, tm, tn), lambda i, j, kk: (j // npo, i, j % npo))
            o_shape = jax.ShapeDtypeStruct((out_shards, m, ns_out), out_dtype)
        else:
            o_spec = pl.BlockSpec((tm, tn), lambda i, j, kk: (i, j))
            o_shape = jax.ShapeDtypeStruct((m, n), out_dtype)
        dims = NN
    elif mode == "nt":
        m, k = a.shape
        g, n, ks = b.shape
        tm, tn, tk = _pick(m, TILE_M), _pick(n, TILE_N), _pick(ks, TILE_K)
        kps = ks // tk
        grid = (m // tm, n // tn, k // tk)
        a_spec = pl.BlockSpec((tm, tk), lambda i, j, kk: (i, kk))
        b_spec = pl.BlockSpec((None, tn, tk), lambda i, j, kk: (kk // kps, j, kk % kps))
        o_spec = pl.BlockSpec((tm, tn), lambda i, j, kk: (i, j))
        o_shape = jax.ShapeDtypeStruct((m, n), out_dtype)
        dims = NT
    else:
        raise ValueError(mode)
    nk = grid[2]

    def body(*refs):
        a_ref, b_ref = refs[0], refs[1]
        add_ref = refs[2] if add is not None else None
        o_ref = refs[2 + (add is not None)]

        def finish(r):
            if add is not None:
                r = r + add_ref[...]
            o_ref[...] = r.astype(o_ref.dtype)

        part = _dot(a_ref[...].astype(BF16), b_ref[...].astype(BF16), dims)
        if nk == 1:
            finish(part)
            return
        acc = refs[-1]
        kk = pl.program_id(2)

        @pl.when(kk == 0)
        def _():
            acc[...] = part

        @pl.when(kk > 0)
        def _():
            acc[...] += part

        @pl.when(kk == nk - 1)
        def _():
            finish(acc[...])

    in_specs = [a_spec, b_spec]
    args = [a, b]
    if add is not None:
        in_specs.append(o_spec)
        args.append(add)
    return pl.pallas_call(
        body, name=name, grid=grid, in_specs=in_specs, out_specs=o_spec, out_shape=o_shape,
        scratch_shapes=[pltpu.VMEM((tm, tn), F32)] if nk > 1 else [],
        compiler_params=_cparams("parallel", "parallel", "arbitrary"),
    )(*args)


def _matmul_tn(name, a, b, g, out_dtype):
    k, m = a.shape
    _, n = b.shape
    ns = n // g
    tm, tn, tk = _pick(m, (2048, 1408) + TILE_M), _pick(ns, TILE_N), _pick(k, (1024, 512, 256, 128))
    nps = ns // tn
    nk = k // tk

    def body(a_ref, b_ref, o_ref, acc):
        kk = pl.program_id(2)
        part = _dot(a_ref[...].astype(BF16), b_ref[...].astype(BF16), TN)

        @pl.when(kk == 0)
        def _():
            acc[...] = part

        @pl.when(kk > 0)
        def _():
            acc[...] += part

        @pl.when(kk == nk - 1)
        def _():
            o_ref[...] = acc[...].astype(o_ref.dtype)

    return pl.pallas_call(
        body, name=name, grid=(m // tm, n // tn, nk),
        in_specs=[pl.BlockSpec((tk, tm), lambda i, j, kk: (kk, i)), pl.BlockSpec((tk, tn), lambda i, j, kk: (kk, j))],
        out_specs=pl.BlockSpec((None, tm, tn), lambda i, j, kk: (j // nps, i, j % nps)),
        out_shape=jax.ShapeDtypeStruct((g, m, ns), out_dtype),
        scratch_shapes=[pltpu.VMEM((tm, tn), F32)],
        compiler_params=_cparams("parallel", "parallel", "arbitrary"),
    )(a, b)


def _rmsnorm_fwd(name, x, g):
    s, d = x.shape
    tr = _pick(s, (256, 128))

    def body(x_ref, g_ref, u_ref, ut_ref):
        xv = x_ref[...]
        r = lax.rsqrt(jnp.mean(xv * xv, axis=-1, keepdims=True) + EPS)
        u = xv * r * g_ref[...]
        u_ref[...] = u.astype(BF16)
        ut_ref[...] = u.T.astype(BF16)

    return pl.pallas_call(
        body, name=name, grid=(s // tr,),
        in_specs=[pl.BlockSpec((tr, d), lambda i: (i, 0)), pl.BlockSpec((1, d), lambda i: (0, 0))],
        out_specs=[pl.BlockSpec((tr, d), lambda i: (i, 0)), pl.BlockSpec((d, tr), lambda i: (0, i))],
        out_shape=[jax.ShapeDtypeStruct((s, d), BF16), jax.ShapeDtypeStruct((d, s), BF16)],
        compiler_params=_cparams("parallel"),
    )(x, g)


def _rmsnorm_bwd(name, x, g, du, extra):
    s, d = x.shape
    tr = _pick(s, (256, 128, 64, 32, 16, 8))

    def body(x_ref, g_ref, du_ref, e_ref, dx_ref, dg_ref):
        i = pl.program_id(0)
        xv = x_ref[...]
        r = lax.rsqrt(jnp.mean(xv * xv, axis=-1, keepdims=True) + EPS)
        n = xv * r
        dy = du_ref[...]
        a = dy * g_ref[...]
        dx = r * (a - n * jnp.mean(a * n, axis=-1, keepdims=True))
        dx_ref[...] = e_ref[...] + dx

        @pl.when(i == 0)
        def _():
            dg_ref[...] = jnp.zeros_like(dg_ref)

        dg_ref[...] += jnp.sum((dy * n).reshape(tr // SUB, SUB, d), axis=0)

    return pl.pallas_call(
        body, name=name, grid=(s // tr,),
        in_specs=[pl.BlockSpec((tr, d), lambda i: (i, 0)), pl.BlockSpec((1, d), lambda i: (0, 0)),
                  pl.BlockSpec((tr, d), lambda i: (i, 0)), pl.BlockSpec((tr, d), lambda i: (i, 0))],
        out_specs=[pl.BlockSpec((tr, d), lambda i: (i, 0)), pl.BlockSpec((SUB, d), lambda i: (0, 0))],
        out_shape=[jax.ShapeDtypeStruct((s, d), F32), jax.ShapeDtypeStruct((SUB, d), F32)],
        compiler_params=_cparams("arbitrary"),
    )(x, g, du, extra)


def _head_norm(x, g):
    r = lax.rsqrt(jnp.mean(x * x, axis=-1, keepdims=True) + EPS)
    return x * r * g


def _qk_norm_fwd(proj, g_q, g_k, nh, blk):
    s = proj.shape[0]

    def body(q_ref, k_ref, v_ref, gq_ref, gk_ref, qn_ref, kn_ref, vb_ref, kt_ref, vt_ref):
        qn_ref[...] = _head_norm(q_ref[...], gq_ref[...]).astype(BF16)
        kn = _head_norm(k_ref[...], gk_ref[...])
        kn_ref[...] = kn.astype(BF16)
        kt_ref[...] = kn.T.astype(BF16)
        v = v_ref[...]
        vb_ref[...] = v.astype(BF16)
        vt_ref[...] = v.T.astype(BF16)

    col = lambda base: pl.BlockSpec((blk, HEAD), lambda i, h: (i, base + h))
    gs = pl.BlockSpec((1, HEAD), lambda i, h: (0, 0))
    o = pl.BlockSpec((blk, HEAD), lambda i, h: (i, h))
    t = pl.BlockSpec((None, HEAD, blk), lambda i, h: (i, h, 0))
    sh = jax.ShapeDtypeStruct((s, nh * HEAD), BF16)
    tsh = jax.ShapeDtypeStruct((s // blk, nh * HEAD, blk), BF16)
    return pl.pallas_call(
        body, name="qk_norm_fwd", grid=(s // blk, nh),
        in_specs=[col(0), col(nh), col(2 * nh), gs, gs], out_specs=[o, o, o, t, t], out_shape=[sh, sh, sh, tsh, tsh],
        compiler_params=_cparams("parallel", "parallel"),
    )(proj, proj, proj, g_q, g_k)


def _qk_norm_bwd(proj, g_q, g_k, dqn, dkn, nh):
    s = proj.shape[0]
    tr = _pick(s, (512, 256, 128, 64, 32, 16, 8))

    def one(x, g, dy):
        r = lax.rsqrt(jnp.mean(x * x, axis=-1, keepdims=True) + EPS)
        n = x * r
        a = dy * g
        dx = r * (a - n * jnp.mean(a * n, axis=-1, keepdims=True))
        return dx, jnp.sum((dy * n).reshape(tr // SUB, SUB, HEAD), axis=0)

    def body(q_ref, k_ref, gq_ref, gk_ref, dqn_ref, dkn_ref, dq_ref, dk_ref, dgq_ref, dgk_ref):
        first = (pl.program_id(0) == 0) & (pl.program_id(1) == 0)

        @pl.when(first)
        def _():
            dgq_ref[...] = jnp.zeros_like(dgq_ref)
            dgk_ref[...] = jnp.zeros_like(dgk_ref)

        dq, pq = one(q_ref[...], gq_ref[...], dqn_ref[...])
        dk, pk = one(k_ref[...], gk_ref[...], dkn_ref[...])
        dq_ref[...] = dq.astype(BF16)
        dk_ref[...] = dk.astype(BF16)
        dgq_ref[...] += pq
        dgk_ref[...] += pk

    col = lambda base: pl.BlockSpec((tr, HEAD), lambda i, h: (i, base + h))
    gs = pl.BlockSpec((1, HEAD), lambda i, h: (0, 0))
    o = pl.BlockSpec((tr, HEAD), lambda i, h: (i, h))
    part = pl.BlockSpec((SUB, HEAD), lambda i, h: (0, 0))
    sh = jax.ShapeDtypeStruct((s, nh * HEAD), BF16)
    psh = jax.ShapeDtypeStruct((SUB, HEAD), F32)
    return pl.pallas_call(
        body, name="qk_norm_bwd", grid=(s // tr, nh),
        in_specs=[col(0), col(nh), gs, gs, o, o], out_specs=[o, o, part, part], out_shape=[sh, sh, psh, psh],
        compiler_params=_cparams("arbitrary", "arbitrary"),
    )(proj, proj, g_q, g_k, dqn, dkn)


def _sb_consts(blk, hp):
    upper = _tri(blk, "ge")
    row = lax.broadcasted_iota(jnp.int32, (blk, hp * blk), 0)
    col = lax.broadcasted_iota(jnp.int32, (blk, hp * blk), 1)
    strict = row < col
    for h in range(1, hp):
        strict = strict & ((col < h * blk) | (row < col - h * blk))
    return jnp.concatenate([upper, upper], axis=1), strict


def _sb_log_keep(zt, strict):
    l = jnp.minimum(-zt, 0.0) - jnp.log(1.0 + jnp.exp(-jnp.abs(zt)))
    return l if strict is None else jnp.where(strict, l, 0.0)


SB_GROUP = 4
SB_GROUP_BWD = 4


def _sb_heads(nh):
    return 2 if nh % 2 == 0 else 1


def _sb_attn_fwd(qn, kn, vt, nh, blk):
    s = qn.shape[0]
    nb = s // blk
    scale = HEAD ** -0.5
    hp = _sb_heads(nh)

    def body(q_ref, k_ref, vt_ref, y_ref, c_ref):
        qi = pl.program_id(1)
        suffix, strict = _sb_consts(blk, hp)
        qs = [q_ref[:, h * HEAD:(h + 1) * HEAD] for h in range(hp)]

        def logits(kb_i):
            off = pl.multiple_of(kb_i * blk, blk)
            return jnp.concatenate(
                [_dot(k_ref[pl.ds(off, blk), h * HEAD:(h + 1) * HEAD], qs[h], NT) for h in range(hp)], axis=1) * scale

        def sums(zt, mask):
            l = _sb_log_keep(zt, mask)
            parts = []
            for h in range(hp):
                hi, lo = _split(l[:, h * blk:(h + 1) * blk])
                parts.append(_dot(suffix, jnp.concatenate([hi, lo], axis=0), NN))
            return jnp.concatenate(parts, axis=1)

        def weights(kb_i, zt, cum, cr, mask):
            for h in range(hp):
                c_ref[h, kb_i] = cr[:, h * blk:(h + 1) * blk]
            wt = jnp.exp(zt + cum + cr)
            if mask is not None:
                wt = jnp.where(mask, wt, 0.0)
            return wt.astype(BF16), cr + cum[0:1, :]

        def add_values(kb_i, wt, accs):
            return tuple(
                accs[h] + _dot(vt_ref[kb_i, h * HEAD:(h + 1) * HEAD, :], wt[:, h * blk:(h + 1) * blk], NN)
                for h in range(hp))

        def group(kbs, masks, accs, cr):
            zts = [logits(k) for k in kbs]
            cums = [sums(zt, m) for zt, m in zip(zts, masks)]
            for k, zt, cum, m in zip(kbs, zts, cums, masks):
                wt, cr = weights(k, zt, cum, cr, m)
                accs = add_values(k, wt, accs)
            return accs, cr

        accs = tuple(jnp.zeros((HEAD, blk), F32) for _ in range(hp))
        accs, cr = group([qi], [strict], accs, jnp.zeros((1, hp * blk), F32))
        n_groups = qi // SB_GROUP

        def many(g, st):
            top = qi - 1 - g * SB_GROUP
            return group([top - j for j in range(SB_GROUP)], [None] * SB_GROUP, *st)

        def one(r, st):
            return group([qi - 1 - n_groups * SB_GROUP - r], [None], *st)

        st = lax.fori_loop(0, n_groups, many, (accs, cr))
        accs, _ = lax.fori_loop(0, qi - n_groups * SB_GROUP, one, st)
        for h in range(hp):
            y_ref[:, h * HEAD:(h + 1) * HEAD] = accs[h].T.astype(y_ref.dtype)

    qs_ = pl.BlockSpec((blk, hp * HEAD), lambda h, i: (i, h))
    full = pl.BlockSpec((s, hp * HEAD), lambda h, i: (0, h))
    return pl.pallas_call(
        body, name="sb_attn_fwd", grid=(nh // hp, nb),
        in_specs=[qs_, full, pl.BlockSpec((nb, hp * HEAD, blk), lambda h, i: (0, h, 0))],
        out_specs=[qs_, pl.BlockSpec((hp, nb, 1, blk), lambda h, i: (h, 0, 0, i))],
        out_shape=[jax.ShapeDtypeStruct((s, nh * HEAD), BF16), jax.ShapeDtypeStruct((nh, nb, 1, s), F32)],
        compiler_params=_cparams("parallel", "arbitrary"),
    )(qn, kn, vt)


def _sb_attn_bwd(qn, kn, kt, vb, dy, carries, nh, blk):
    s = qn.shape[0]
    nb = s // blk
    scale = HEAD ** -0.5
    hp = _sb_heads(nh)

    def body(q_ref, k_ref, kt_ref, v_ref, dy_ref, c_ref, dq_ref, dk_ref, dv_ref):
        qi = pl.program_id(1)

        @pl.when(qi == 0)
        def _():
            dk_ref[...] = jnp.zeros_like(dk_ref)
            dv_ref[...] = jnp.zeros_like(dv_ref)

        suffix, strict = _sb_consts(blk, hp)
        prefix = _tri(blk, "lt")
        qs = [q_ref[:, h * HEAD:(h + 1) * HEAD] for h in range(hp)]
        dos = [dy_ref[:, h * HEAD:(h + 1) * HEAD].astype(BF16) for h in range(hp)]

        def logits(kb_i):
            off = pl.multiple_of(kb_i * blk, blk)
            return jnp.concatenate(
                [_dot(k_ref[pl.ds(off, blk), h * HEAD:(h + 1) * HEAD], qs[h], NT) for h in range(hp)], axis=1) * scale

        def group(kbs, masks, dqs, ec):
            rows = [pl.ds(pl.multiple_of(k * blk, blk), blk) for k in kbs]
            zts = [logits(k) for k in kbs]
            dws = [jnp.concatenate([_dot(v_ref[r, h * HEAD:(h + 1) * HEAD], dos[h], NT) for h in range(hp)], axis=1)
                   for r in rows]
            ls = [_sb_log_keep(zt, m) for zt, m in zip(zts, masks)]
            cums = []
            for l in ls:
                parts = []
                for h in range(hp):
                    hi, lo = _split(l[:, h * blk:(h + 1) * blk])
                    parts.append(_dot(suffix, jnp.concatenate([hi, lo], axis=0), NN))
                cums.append(jnp.concatenate(parts, axis=1))
            wts, ets, befores = [], [], []
            for k, zt, cum, dw, m in zip(kbs, zts, cums, dws, masks):
                cr = jnp.concatenate([c_ref[h, k] for h in range(hp)], axis=1)
                wt = jnp.exp(zt + cum + cr)
                if m is not None:
                    wt = jnp.where(m, wt, 0.0)
                et = wt * dw
                befores.append(_dot(prefix, et.astype(BF16), NN) + ec)
                ec = ec + jnp.sum(et, axis=0, keepdims=True)
                wts.append(wt.astype(BF16))
                ets.append(et)
            for k, r, l, et, before, wtb in zip(kbs, rows, ls, ets, befores, wts):
                dzt = ((jnp.exp(l) * (et + before) - before) * scale).astype(BF16)
                out = []
                for h in range(hp):
                    cols, part = slice(h * HEAD, (h + 1) * HEAD), slice(h * blk, (h + 1) * blk)
                    dv_ref[r, cols] += _dot(wtb[:, part], dos[h], NN)
                    dk_ref[r, cols] += _dot(dzt[:, part], qs[h], NN)
                    out.append(dqs[h] + _dot(kt_ref[k, cols, :], dzt[:, part], NN))
                dqs = tuple(out)
            return dqs, ec

        n_groups = qi // SB_GROUP_BWD

        def many(g, st):
            return group([g * SB_GROUP_BWD + j for j in range(SB_GROUP_BWD)], [None] * SB_GROUP_BWD, *st)

        def one(r, st):
            return group([n_groups * SB_GROUP_BWD + r], [None], *st)

        st = (tuple(jnp.zeros((HEAD, blk), F32) for _ in range(hp)), jnp.zeros((1, hp * blk), F32))
        st = lax.fori_loop(0, n_groups, many, st)
        st = lax.fori_loop(0, qi - n_groups * SB_GROUP_BWD, one, st)
        dqs, _ = group([qi], [strict], *st)
        for h in range(hp):
            dq_ref[:, h * HEAD:(h + 1) * HEAD] = dqs[h].T

    qs_ = pl.BlockSpec((blk, hp * HEAD), lambda h, i: (i, h))
    full = pl.BlockSpec((s, hp * HEAD), lambda h, i: (0, h), pipeline_mode=pl.Buffered(1))
    sh = jax.ShapeDtypeStruct((s, nh * HEAD), F32)
    return pl.pallas_call(
        body, name="sb_attn_bwd", grid=(nh // hp, nb),
        in_specs=[qs_, full, pl.BlockSpec((nb, hp * HEAD, blk), lambda h, i: (0, h, 0), pipeline_mode=pl.Buffered(1)),
                  full, qs_, pl.BlockSpec((hp, nb, 1, blk), lambda h, i: (h, 0, 0, i))],
        out_specs=[qs_, full, full], out_shape=[sh, sh, sh],
        compiler_params=_cparams("parallel", "arbitrary"),
    )(qn, kn, kt, vb, dy, carries)


def _lower_bound(lb_logits):
    def body(l_ref, o_ref):
        l = l_ref[...]
        m = jnp.max(l, axis=0, keepdims=True)
        e = jnp.exp(l - m)
        o_ref[...] = e[0:1, :] / jnp.sum(e, axis=0, keepdims=True)

    return pl.pallas_call(body, name="lower_bound", out_shape=jax.ShapeDtypeStruct((1, lb_logits.shape[1]), F32))(lb_logits)


def _hg_gates(hq, hf, lb):
    sq = _sig(hq)
    q = hq * sq
    sf = _sig(hf)
    f = lb + (1.0 - lb) * sf
    return q, sq, f, sf


def _hg_cum(g, c):
    hi, lo = _split(g)
    t = _tri(c, "le")
    return _dot(t, hi, NN) + _dot(t, lo, NN)


def _hg_heads(nh):
    return 4 if nh % 4 == 0 else 2 if nh % 2 == 0 else 1


def _head_cols(x, h):
    return x[:, h * HEAD:(h + 1) * HEAD]


def _per_head(x, hp, fn):
    return jnp.concatenate([fn(_head_cols(x, h), h) for h in range(hp)], axis=1)


def _head_sums(x, hp):
    return [jnp.sum(_head_cols(x, h), axis=1, keepdims=True) for h in range(hp)]


def _head_scale(cols, x, hp):
    return jnp.concatenate([cols[h] * _head_cols(x, h) for h in range(hp)], axis=1)


def _row_mask(r, width):
    return lax.broadcasted_iota(jnp.int32, (SUB, width), 0) >= r


def _hg_intra_fwd(q, k, v, b, c, hp):
    outs = []
    for bi in range(c // SUB):
        q_i, b_i = q[bi * SUB:(bi + 1) * SUB], b[bi * SUB:(bi + 1) * SUB]
        acc = jnp.zeros((SUB, hp * HEAD), F32)
        for s in range((bi + 1) * SUB):
            d = b_i - b[s:s + 1]
            if s >= bi * SUB:
                d = jnp.where(_row_mask(s - bi * SUB, hp * HEAD), d, -jnp.inf)
            acc = acc + _head_scale(_head_sums(q_i * k[s:s + 1] * jnp.exp(d), hp), v[s:s + 1], hp)
        outs.append(acc)
    return jnp.concatenate(outs, axis=0)


def _hg_intra_bwd(q, k, v, b, do, c, hp, dq_scr, dk_scr, dv_scr):
    nblk = c // SUB
    dq_scr[...] = jnp.zeros_like(dq_scr)
    for s in range(c):
        bj = s // SUB
        ks, vs, bs = k[s:s + 1], v[s:s + 1], b[s:s + 1]
        acc_k = jnp.zeros((SUB, hp * HEAD), F32)
        acc_v = jnp.zeros((SUB, hp * HEAD), F32)
        for bi in range(bj, nblk):
            sl = slice(bi * SUB, (bi + 1) * SUB)
            d = b[sl] - bs
            if bi == bj:
                d = jnp.where(_row_mask(s - bj * SUB, hp * HEAD), d, -jnp.inf)
            dec = jnp.exp(d)
            qd = q[sl] * dec
            col = _head_sums(qd * ks, hp)
            dcol = _head_sums(do[sl] * vs, hp)
            dq_scr[sl, :] += _head_scale(dcol, ks * dec, hp)
            acc_k = acc_k + _head_scale(dcol, qd, hp)
            acc_v = acc_v + _head_scale(col, do[sl], hp)
        dk_scr[s:s + 1, :] = jnp.sum(acc_k, axis=0, keepdims=True)
        dv_scr[s:s + 1, :] = jnp.sum(acc_v, axis=0, keepdims=True)


def _hgrn2_fwd(proj, lb, g_out, nh, base, c):
    s = proj.shape[0]
    nch = s // c
    hp = _hg_heads(nh)
    wide = hp * HEAD

    def body(hq_ref, hf_ref, hi_ref, og_ref, lb_ref, g_ref, o_ref, y_ref, st_ref, st):
        @pl.when(pl.program_id(1) == 0)
        def _():
            st[...] = jnp.zeros_like(st)

        st_in = [st[h] for h in range(hp)]
        for h in range(hp):
            st_ref[h] = st_in[h]
        q, _, f, _ = _hg_gates(hq_ref[...], hf_ref[...], lb_ref[...])
        k = 1.0 - f
        v = hi_ref[...]
        b = _hg_cum(jnp.log(f), c)
        bl = b[c - 1:c, :]
        qe = (q * jnp.exp(b)).astype(BF16)
        o = _per_head(qe, hp, lambda x, h: _dot(x, st_in[h].astype(BF16), NT)) + _hg_intra_fwd(q, k, v, b, c, hp)
        kd = (k * jnp.exp(bl - b)).astype(BF16)
        vb = v.astype(BF16)
        keep = jnp.exp(bl)
        for h in range(hp):
            st[h] = st_in[h] * _head_cols(keep, h) + _dot(_head_cols(vb, h), _head_cols(kd, h), TN)
        o_ref[...] = o
        og = og_ref[...]
        gout = g_ref[...]
        y_ref[...] = (_per_head(o, hp, lambda x, h: _head_norm(x, gout)) * (og * _sig(og))).astype(BF16)

    col = lambda j: pl.BlockSpec((c, wide), lambda g, i: (i, (base + j * nh) // hp + g))
    row = pl.BlockSpec((1, wide), lambda g, i: (0, g))
    gs = pl.BlockSpec((1, HEAD), lambda g, i: (0, 0))
    o = pl.BlockSpec((c, wide), lambda g, i: (i, g))
    return pl.pallas_call(
        body, name="hgrn2_fwd", grid=(nh // hp, nch),
        in_specs=[col(0), col(1), col(2), col(3), row, gs],
        out_specs=[o, o, pl.BlockSpec((hp, None, HEAD, HEAD), lambda g, i: (g, i, 0, 0))],
        out_shape=[jax.ShapeDtypeStruct((s, nh * HEAD), F32), jax.ShapeDtypeStruct((s, nh * HEAD), BF16),
                   jax.ShapeDtypeStruct((nh, nch, HEAD, HEAD), F32)],
        scratch_shapes=[pltpu.VMEM((hp, HEAD, HEAD), F32)],
        compiler_params=_cparams("parallel", "arbitrary"),
    )(proj, proj, proj, proj, lb, g_out)


def _hgrn2_bwd(proj, lb, g_out, o_pre, states, dy, nh, base, c):
    s = proj.shape[0]
    nch = s // c
    hp = _hg_heads(nh)
    wide = hp * HEAD

    def fold(x):
        return jnp.sum(x.reshape(c // SUB, SUB, x.shape[1]), axis=0)

    def body(hq_ref, hf_ref, hi_ref, og_ref, lb_ref, g_ref, o_ref, st_ref, se_ref, dy_ref,
             dhq_ref, dhf_ref, dhi_ref, dog_ref, dg_ref, dlb_ref, dst, dq_scr, dk_scr, dv_scr):
        g, i = pl.program_id(0), pl.program_id(1)

        @pl.when(i == 0)
        def _():
            dst[...] = jnp.zeros_like(dst)
            dlb_ref[...] = jnp.zeros_like(dlb_ref)

        @pl.when((i == 0) & (g == 0))
        def _():
            dg_ref[...] = jnp.zeros_like(dg_ref)

        lbv = lb_ref[...]
        hq, hf = hq_ref[...], hf_ref[...]
        q, sq, f, sf = _hg_gates(hq, hf, lbv)
        k = 1.0 - f
        v = hi_ref[...]
        b = _hg_cum(jnp.log(f), c)
        bl = b[c - 1:c, :]
        eb = jnp.exp(b)
        ebl = jnp.exp(bl - b)

        o = o_ref[...]
        gout = g_ref[...]
        og = og_ref[...]
        sg = _sig(og)
        r = _per_head(o, hp, lambda x, h: jnp.broadcast_to(
            lax.rsqrt(jnp.mean(x * x, axis=-1, keepdims=True) + EPS), x.shape))
        gw = jnp.concatenate([gout] * hp, axis=1)
        n = o * r
        dyv = dy_ref[...]
        dn = dyv * (og * sg)
        dog_ref[...] = (dyv * n * gw * (sg * (1.0 + og * (1.0 - sg)))).astype(BF16)
        dnn = fold(dn * n)
        part = _head_cols(dnn, 0)
        for h in range(1, hp):
            part = part + _head_cols(dnn, h)
        dg_ref[...] += part
        a = dn * gw
        an = a * n
        do = r * (a - n * _per_head(an, hp, lambda x, h: jnp.broadcast_to(jnp.mean(x, axis=-1, keepdims=True), x.shape)))

        st_in = [st_ref[h].astype(BF16) for h in range(hp)]
        dstv = [dst[h] for h in range(hp)]
        dstb = [d.astype(BF16) for d in dstv]
        dob = do.astype(BF16)
        vb = v.astype(BF16)
        kdb = (k * ebl).astype(BF16)
        qeb = (q * eb).astype(BF16)
        _hg_intra_bwd(q, k, v, b, do, c, hp, dq_scr, dk_scr, dv_scr)
        dq = dq_scr[...] + eb * _per_head(dob, hp, lambda x, h: _dot(x, st_in[h], NN))
        dk = dk_scr[...] + ebl * _per_head(vb, hp, lambda x, h: _dot(x, dstb[h], NN))
        dv = dv_scr[...] + _per_head(kdb, hp, lambda x, h: _dot(x, dstb[h], NT))
        keep = jnp.exp(bl)
        for h in range(hp):
            dst[h] = dstv[h] * _head_cols(keep, h) + _dot(_head_cols(dob, h), _head_cols(qeb, h), TN)

        hi_, lo_ = _split(q * dq - k * dk)
        rev = _tri(c, "ge")
        later = jnp.concatenate([jnp.sum(dstv[h] * se_ref[h], axis=0, keepdims=True) for h in range(hp)], axis=1)
        dg = _dot(rev, hi_, NN) + _dot(rev, lo_, NN) + jnp.where(i > 0, later, 0.0)
        df = dg / f - dk
        dhq_ref[...] = (dq * (sq * (1.0 + hq * (1.0 - sq)))).astype(BF16)
        dhf_ref[...] = (df * (1.0 - lbv) * sf * (1.0 - sf)).astype(BF16)
        dhi_ref[...] = dv.astype(BF16)
        dlb_ref[...] += fold(df * (1.0 - sf))

    rv = lambda i: nch - 1 - i
    col = lambda j: pl.BlockSpec((c, wide), lambda g, i: (rv(i), (base + j * nh) // hp + g))
    row = pl.BlockSpec((1, wide), lambda g, i: (0, g))
    gs = pl.BlockSpec((1, HEAD), lambda g, i: (0, 0))
    o = pl.BlockSpec((c, wide), lambda g, i: (rv(i), g))
    st = pl.BlockSpec((hp, None, HEAD, HEAD), lambda g, i: (g, rv(i), 0, 0))
    se = pl.BlockSpec((hp, None, HEAD, HEAD), lambda g, i: (g, jnp.minimum(rv(i) + 1, nch - 1), 0, 0))
    sh = jax.ShapeDtypeStruct((s, nh * HEAD), BF16)
    return pl.pallas_call(
        body, name="hgrn2_bwd", grid=(nh // hp, nch),
        in_specs=[col(0), col(1), col(2), col(3), row, gs, o, st, se, o],
        out_specs=[o, o, o, o, pl.BlockSpec((SUB, HEAD), lambda g, i: (0, 0)), pl.BlockSpec((SUB, wide), lambda g, i: (0, g))],
        out_shape=[sh, sh, sh, sh, jax.ShapeDtypeStruct((SUB, HEAD), F32), jax.ShapeDtypeStruct((SUB, nh * HEAD), F32)],
        scratch_shapes=[pltpu.VMEM((hp, HEAD, HEAD), F32), pltpu.VMEM((c, wide), F32), pltpu.VMEM((c, wide), F32),
                        pltpu.VMEM((c, wide), F32)],
        compiler_params=_cparams("arbitrary", "arbitrary"),
    )(proj, proj, proj, proj, lb, g_out, o_pre, states, states, dy)


def _merge_tiles(s, d, gate_col):
    tr = _pick(s, (256, 128, 64, 32, 16, 8))
    tc = 128
    for cand in (512, 256):
        if d % cand == 0 and gate_col % cand == 0:
            tc = cand
            break
    return tr, tc


def _merge_fwd(proj, ya, yb, gate_col):
    s, d = ya.shape
    tr, tc = _merge_tiles(s, d, gate_col)
    ga0, gb0 = gate_col // tc, (gate_col + d) // tc

    def body(ga_ref, gb_ref, ya_ref, yb_ref, m_ref):
        m_ref[...] = (_sig(ga_ref[...]) * ya_ref[...] + _sig(gb_ref[...]) * yb_ref[...]).astype(BF16)

    o = pl.BlockSpec((tr, tc), lambda i, j: (i, j))
    return pl.pallas_call(
        body, name="merge_fwd", grid=(s // tr, d // tc),
        in_specs=[pl.BlockSpec((tr, tc), lambda i, j: (i, ga0 + j)), pl.BlockSpec((tr, tc), lambda i, j: (i, gb0 + j)), o, o],
        out_specs=o, out_shape=jax.ShapeDtypeStruct((s, d), BF16),
        compiler_params=_cparams("parallel", "parallel"),
    )(proj, proj, ya, yb)


def _merge_bwd(proj, ya, yb, dm, gate_col):
    s, d = ya.shape
    tr, tc = _merge_tiles(s, d, gate_col)
    ga0, gb0 = gate_col // tc, (gate_col + d) // tc

    def body(ga_ref, gb_ref, ya_ref, yb_ref, dm_ref, dya_ref, dyb_ref, dga_ref, dgb_ref):
        dmv = dm_ref[...]
        sa, sb = _sig(ga_ref[...]), _sig(gb_ref[...])
        dya_ref[...] = (dmv * sa).astype(BF16)
        dyb_ref[...] = (dmv * sb).astype(BF16)
        dga_ref[...] = (dmv * ya_ref[...] * sa * (1.0 - sa)).astype(BF16)
        dgb_ref[...] = (dmv * yb_ref[...] * sb * (1.0 - sb)).astype(BF16)

    o = pl.BlockSpec((tr, tc), lambda i, j: (i, j))
    sh = jax.ShapeDtypeStruct((s, d), BF16)
    return pl.pallas_call(
        body, name="merge_bwd", grid=(s // tr, d // tc),
        in_specs=[pl.BlockSpec((tr, tc), lambda i, j: (i, ga0 + j)), pl.BlockSpec((tr, tc), lambda i, j: (i, gb0 + j)), o, o, o],
        out_specs=[o, o, o, o], out_shape=[sh, sh, sh, sh],
        compiler_params=_cparams("parallel", "parallel"),
    )(proj, proj, ya, yb, dm)


CONV_ROWS = 512


def _conv_ext(ref, i, rows, s, before, after):
    parts = []
    if before:
        p = ref[pl.ds(pl.multiple_of(jnp.maximum(i * rows - before, 0), SUB), before), :]
        parts.append(jnp.where(i > 0, p, 0.0))
    parts.append(ref[pl.ds(pl.multiple_of(i * rows, SUB), rows), :])
    if after:
        nxt = ref[pl.ds(pl.multiple_of(jnp.minimum((i + 1) * rows, s - after), SUB), after), :]
        parts.append(jnp.where((i + 1) * rows < s, nxt, 0.0))
    return jnp.concatenate(parts, axis=0)


def _conv3(ext, w, bias):
    x1 = pltpu.roll(ext, 1, 0)
    x2 = pltpu.roll(ext, 2, 0)
    return bias + w[0:1, :] * x2 + w[1:2, :] * x1 + w[2:3, :] * ext, x1, x2


def _convffn_fwd(up, conv_w, conv_b, dff):
    s = up.shape[0]
    tc = HEAD
    nf = dff // tc
    rows = _pick(s, (CONV_ROWS, 256, 128, 64, 32, 16, 8))

    def body(ug_ref, uv_ref, wg_ref, wv_ref, bg_ref, bv_ref, a_ref):
        wg, wv, bg, bv = wg_ref[...], wv_ref[...], bg_ref[...], bv_ref[...]

        def step(i, _):
            g = _conv3(_conv_ext(ug_ref, i, rows, s, SUB, 0), wg, bg)[0][SUB:]
            v = _conv3(_conv_ext(uv_ref, i, rows, s, SUB, 0), wv, bv)[0][SUB:]
            a_ref[pl.ds(pl.multiple_of(i * rows, SUB), rows), :] = (g * _sig(g) * v).astype(BF16)
            return 0

        lax.fori_loop(0, s // rows, step, 0)

    cg = lambda r: pl.BlockSpec((r, tc), lambda j: (0, j))
    cv = lambda r: pl.BlockSpec((r, tc), lambda j: (0, nf + j))
    return pl.pallas_call(
        body, name="convffn_fwd", grid=(nf,),
        in_specs=[cg(s), cv(s), cg(3), cv(3), cg(1), cv(1)], out_specs=cg(s),
        out_shape=jax.ShapeDtypeStruct((s, dff), BF16),
        compiler_params=_cparams("parallel"),
    )(up, up, conv_w, conv_w, conv_b, conv_b)


def _convffn_bwd(up, conv_w, conv_b, dact, dff):
    s = up.shape[0]
    tc = HEAD
    nf = dff // tc
    rows = _pick(s, (CONV_ROWS, 256, 128, 64, 32, 16, 8))
    n_ext = rows + SUB

    def body(ug_ref, uv_ref, wg_ref, wv_ref, bg_ref, bv_ref, da_ref,
             dug_ref, duv_ref, dwg_ref, dwv_ref, dbg_ref, dbv_ref):
        wg, wv, bg, bv = wg_ref[...], wv_ref[...], bg_ref[...], bv_ref[...]

        def fold(x):
            return jnp.sum(x.reshape(rows // SUB, SUB, tc), axis=0)

        def one(ext, x1, x2, d_ext, w):
            d1 = pltpu.roll(d_ext, n_ext - 1, 0)[:rows]
            d2 = pltpu.roll(d_ext, n_ext - 2, 0)[:rows]
            dc = d_ext[:rows]
            du = w[2:3, :] * dc + w[1:2, :] * d1 + w[0:1, :] * d2
            sl = slice(SUB, SUB + rows)
            return du, (fold(dc * x2[sl]), fold(dc * x1[sl]), fold(dc * ext[sl]), fold(dc))

        def step(i, acc):
            eg = _conv_ext(ug_ref, i, rows, s, SUB, SUB)
            ev = _conv_ext(uv_ref, i, rows, s, SUB, SUB)
            g, g1, g2 = _conv3(eg, wg, bg)
            v, v1, v2 = _conv3(ev, wv, bv)
            g, v = g[SUB:], v[SUB:]
            da = _conv_ext(da_ref, i, rows, s, 0, SUB)
            sg = _sig(g)
            dg = da * v * (sg * (1.0 + g * (1.0 - sg)))
            dv = da * (g * sg)
            dug, pg = one(eg, g1, g2, dg, wg)
            duv, pv = one(ev, v1, v2, dv, wv)
            at = pl.ds(pl.multiple_of(i * rows, SUB), rows)
            dug_ref[at, :] = dug.astype(BF16)
            duv_ref[at, :] = duv.astype(BF16)
            return tuple(a + p for a, p in zip(acc, pg + pv))

        zero = jnp.zeros((SUB, tc), F32)
        acc = lax.fori_loop(0, s // rows, step, (zero,) * 8)
        red = [jnp.sum(a, axis=0, keepdims=True) for a in acc]
        for j in range(3):
            dwg_ref[j:j + 1, :] = red[j]
            dwv_ref[j:j + 1, :] = red[4 + j]
        dbg_ref[...] = red[3]
        dbv_ref[...] = red[7]

    cg = lambda r: pl.BlockSpec((r, tc), lambda j: (0, j))
    cv = lambda r: pl.BlockSpec((r, tc), lambda j: (0, nf + j))
    outs = pl.pallas_call(
        body, name="convffn_bwd", grid=(nf,),
        in_specs=[cg(s), cv(s), cg(3), cv(3), cg(1), cv(1), cg(s)],
        out_specs=[cg(s), cg(s), cg(3), cg(3), cg(1), cg(1)],
        out_shape=[jax.ShapeDtypeStruct((s, dff), BF16), jax.ShapeDtypeStruct((s, dff), BF16),
                   jax.ShapeDtypeStruct((3, dff), F32), jax.ShapeDtypeStruct((3, dff), F32),
                   jax.ShapeDtypeStruct((1, dff), F32), jax.ShapeDtypeStruct((1, dff), F32)],
        compiler_params=_cparams("parallel"),
    )(up, up, conv_w, conv_w, conv_b, conv_b, dact)
    return outs


def _loss_head(out, target):
    s, d = out.shape
    tr = _pick(s, (256, 128, 64, 32, 16, 8))

    def body(o_ref, t_ref, d_ref, l_ref):
        @pl.when(pl.program_id(0) == 0)
        def _():
            l_ref[...] = jnp.zeros_like(l_ref)

        err = o_ref[...] - t_ref[...]
        d_ref[...] = err * (1.0 / d)
        sq = jnp.sum((err * err).reshape(tr // SUB, SUB, d), axis=0)
        part = sq[:, 0:HEAD]
        for j in range(1, d // HEAD):
            part = part + sq[:, j * HEAD:(j + 1) * HEAD]
        l_ref[...] += part

    blk = pl.BlockSpec((tr, d), lambda i: (i, 0))
    return pl.pallas_call(
        body, name="loss_head", grid=(s // tr,), in_specs=[blk, blk],
        out_specs=[blk, pl.BlockSpec((SUB, HEAD), lambda i: (0, 0))],
        out_shape=[jax.ShapeDtypeStruct((s, d), F32), jax.ShapeDtypeStruct((SUB, HEAD), F32)],
        compiler_params=_cparams("arbitrary"),
    )(out, target)


def _sum_rows(name, parts):
    def body(p_ref, o_ref):
        o_ref[...] = jnp.sum(p_ref[...], axis=0, keepdims=True)

    return pl.pallas_call(body, name=name, out_shape=jax.ShapeDtypeStruct((1, parts.shape[1]), F32))(parts)


def _local_step(x, target, g_mix, g_q, g_k, lb_logits, g_hg_out, g_ffn, conv_w, conv_b, w_in, later_weights, grads_ready):
    s, d = x.shape
    nh = lb_logits.shape[1] // HEAD
    wid = nh * HEAD
    blk = _pick(s, (256, 128))
    chunk = _pick(s, (HG_CHUNK,))
    gate_col = 7 * wid

    u, u_t = _rmsnorm_fwd("rmsnorm_mix", x, g_mix)
    proj = _matmul("in_proj", u, w_in, "nn", F32)
    qn, kn, vb, kt, vt = _qk_norm_fwd(proj, g_q, g_k, nh, blk)
    y_a, carries = _sb_attn_fwd(qn, kn, vt, nh, blk)
    lb = _lower_bound(lb_logits)
    o_pre, y_b, states = _hgrn2_fwd(proj, lb, g_hg_out, nh, 3 * nh, chunk)
    later = later_weights(o_pre)
    p_a, p_b, w_up = later["p_a"], later["p_b"], later["w_up"]
    w_o = later["w_o"].reshape(1, d, d)
    dff = later["w_down"].shape[1] * N_CHIPS
    w_down = later["w_down"].reshape(1, dff, d)
    ya_p = _matmul("proj_a", y_a, p_a, "nn", F32)
    yb_p = _matmul("proj_b", y_b, p_b, "nn", F32)
    m = _merge_fwd(proj, ya_p, yb_p, gate_col)
    h = _matmul("out_proj", m, w_o, "nn", F32, add=x)
    u2, u2_t = _rmsnorm_fwd("rmsnorm_ffn", h, g_ffn)
    up = _matmul("up_proj", u2, w_up, "nn", F32)
    act = _convffn_fwd(up, conv_w, conv_b, dff)
    out = _matmul("down_proj", act, w_down, "nn", F32, add=h)
    dout, sq = _loss_head(out, target)

    dact = _matmul("d_act", dout, w_down, "nt", F32)
    g_w_down = _matmul_tn("g_w_down", act, dout, 1, BF16).reshape(N_CHIPS, dff // N_CHIPS, d)
    dup_g, dup_v, dcw_g, dcw_v, dcb_g, dcb_v = _convffn_bwd(up, conv_w, conv_b, dact, dff)
    dup = jnp.concatenate([dup_g, dup_v], axis=1)
    g_w_up = _matmul("g_w_up", u2_t, dup[None], "nn", BF16, out_shards=N_CHIPS)
    sent = grads_ready(("w_down", "w_up"), [g_w_down, g_w_up])
    du2 = _matmul("d_u2", dup, w_up, "nt", F32)
    dh, pg_ffn = _rmsnorm_bwd("rmsnorm_ffn_bwd", h, g_ffn + sent, du2, dout)
    dm = _matmul("d_m", dh, w_o, "nt", F32)
    g_w_o = _matmul_tn("g_w_o", m, dh, 1, BF16).reshape(N_CHIPS, d // N_CHIPS, d)
    dya_p, dyb_p, dga, dgb = _merge_bwd(proj, ya_p, yb_p, dm, gate_col)
    g_p_a = _matmul_tn("g_p_a", y_a, dya_p, N_CHIPS, BF16)
    g_p_b = _matmul_tn("g_p_b", y_b, dyb_p, N_CHIPS, BF16)
    sent = grads_ready(("w_o", "p_a", "p_b"), [g_w_o, g_p_a, g_p_b])
    dy_a = _matmul("d_y_a", dya_p, p_a, "nt", F32)
    dy_b = _matmul("d_y_b", dyb_p, p_b, "nt", F32)
    dhq, dhf, dhi, dog, pg_hg, p_lb = _hgrn2_bwd(proj, lb, g_hg_out + sent, o_pre, states, dy_b, nh, 3 * nh, chunk)
    dqn, dkn, dv = _sb_attn_bwd(qn, kn, kt, vb, dy_a, carries, nh, blk)
    dq, dk, pg_q, pg_k = _qk_norm_bwd(proj, g_q, g_k, dqn, dkn, nh)
    dproj = jnp.concatenate([dq, dk, dv.astype(BF16), dhq, dhf, dhi, dog, dga, dgb], axis=1)
    g_w_in = _matmul("g_w_in", u_t, dproj[None], "nn", BF16, out_shards=N_CHIPS)
    sent = grads_ready(("w_in",), [g_w_in])
    du = _matmul("d_u", dproj, w_in, "nt", F32)
    dx, pg_mix = _rmsnorm_bwd("rmsnorm_mix_bwd", x, g_mix + sent, du, dh)

    small = dict(
        g_mix=_sum_rows("sum_g_mix", pg_mix), g_q=_sum_rows("sum_g_q", pg_q), g_k=_sum_rows("sum_g_k", pg_k),
        lb=_sum_rows("sum_lb", p_lb), g_hg_out=_sum_rows("sum_g_hg", pg_hg), g_ffn=_sum_rows("sum_g_ffn", pg_ffn),
        conv_w=jnp.concatenate([dcw_g, dcw_v], axis=1), conv_b=jnp.concatenate([dcb_g, dcb_v], axis=1),
        sq=_sum_rows("sum_sq", sq),
    )
    return dx, small, lb


ANY = pl.BlockSpec(memory_space=pl.ANY)


def _place():
    x, y, c = lax.axis_index("x"), lax.axis_index("y"), lax.axis_index("c")
    chips = [(1 - x, y), (x, 1 - y), (1 - x, 1 - y)]
    return x, y, c, chips


def _remote(src, dst, send_sem, recv_sem, to):
    return pltpu.make_async_remote_copy(src_ref=src, dst_ref=dst, send_sem=send_sem, recv_sem=recv_sem,
                                        device_id=to, device_id_type=MESH)


def _cast_bf16(name, w):
    r, c = w.shape
    tr = _pick(r, (256, 128, 64, 32, 16))

    def body(w_ref, o_ref):
        o_ref[...] = w_ref[...].astype(BF16)

    return pl.pallas_call(
        body, name=name, grid=(r // tr,), in_specs=[pl.BlockSpec((tr, c), lambda i: (i, 0))],
        out_specs=pl.BlockSpec((tr, c), lambda i: (i, 0)), out_shape=jax.ShapeDtypeStruct((r, c), BF16),
        compiler_params=_cparams("parallel"),
    )(w)


def _gather_weights(shards):
    n = len(shards)

    def body(*refs):
        ins, outs = refs[:n], refs[n:2 * n]
        send, recv, local = refs[2 * n:]
        x, y, c, chips = _place()
        mine = 2 * x + y
        sends, owns = [], []
        for k in range(n):
            half = ins[k].shape[0] // 2
            rows = pl.ds(c * half, half)
            own = pltpu.make_async_copy(ins[k], outs[k].at[mine], local.at[k])
            own.start()
            owns.append(own)
            for j, (px, py) in enumerate(chips):
                cp = _remote(ins[k].at[rows], outs[k].at[mine, rows], send.at[k, j], recv.at[k, j], (px, py, c))
                cp.start()
                sends.append(cp)
        for k in range(n):
            half = ins[k].shape[0] // 2
            rows = pl.ds(c * half, half)
            for j, (px, py) in enumerate(chips):
                part = outs[k].at[2 * px + py, rows]
                _remote(part, part, send.at[k, j], recv.at[k, j], (px, py, c)).wait_recv()
                fw = _remote(part, part, send.at[k, 3 + j], recv.at[k, 3 + j], (x, y, 1 - c))
                fw.start()
                sends.append(fw)
        for k in range(n):
            half = ins[k].shape[0] // 2
            other = pl.ds((1 - c) * half, half)
            for j, (px, py) in enumerate(chips):
                part = outs[k].at[2 * px + py, other]
                _remote(part, part, send.at[k, 3 + j], recv.at[k, 3 + j], (x, y, 1 - c)).wait_recv()
        for cp in sends:
            cp.wait_send()
        for cp in owns:
            cp.wait()

    return pl.pallas_call(
        body, name="gather_weights", in_specs=[ANY] * n, out_specs=[ANY] * n,
        out_shape=[jax.ShapeDtypeStruct((N_CHIPS,) + w.shape, w.dtype) for w in shards],
        scratch_shapes=[pltpu.SemaphoreType.DMA((n, 6)), pltpu.SemaphoreType.DMA((n, 6)), pltpu.SemaphoreType.DMA((n,))],
    )(*shards)


def _to_sibling(name, srcs):
    n = len(srcs)

    def body(*refs):
        ins, outs = refs[:n], refs[n:2 * n]
        send, recv = refs[2 * n:]
        x, y, c, _ = _place()
        cps = []
        for k in range(n):
            cp = _remote(ins[k], outs[k], send.at[k], recv.at[k], (x, y, 1 - c))
            cp.start()
            cps.append(cp)
        for cp in cps:
            cp.wait_recv()
        for cp in cps:
            cp.wait_send()

    return pl.pallas_call(
        body, name=name, in_specs=[ANY] * n, out_specs=[ANY] * n,
        out_shape=[jax.ShapeDtypeStruct(a.shape, a.dtype) for a in srcs],
        scratch_shapes=[pltpu.SemaphoreType.DMA((n,)), pltpu.SemaphoreType.DMA((n,))],
    )(*srcs)


HBM = pl.BlockSpec(memory_space=pltpu.HBM)
SEM = pl.BlockSpec(memory_space=pltpu.SEMAPHORE)
SIDE = pltpu.SideEffectType.DATAFLOW_SIDE_EFFECTING
N_PEERS = 7


def _peer(r):
    x, y, c = lax.axis_index("x"), lax.axis_index("y"), lax.axis_index("c")
    return (1 - x if r & 4 else x), (1 - y if r & 2 else y), (1 - c if r & 1 else c)


def _partial_copy(src, land, send, recv, k, r):
    px, py, pc = _peer(r)
    half = src.shape[1] // 2
    sem = k * N_PEERS + r - 1
    return _remote(src.at[2 * px + py, pl.ds(pc * half, half)], land.at[r - 1], send.at[sem], recv.at[sem], (px, py, pc))


def _shard_copy(full, send, recv, k, r):
    x, y, c = lax.axis_index("x"), lax.axis_index("y"), lax.axis_index("c")
    half = full.shape[1] // 2
    part = full.at[2 * x + y, pl.ds(c * half, half)]
    sem = k * (N_PEERS - 1) + r - 2
    return _remote(part, part, send.at[sem], recv.at[sem], _peer(r))


def _start_copies(name, arrays, n_sems, copies):
    n = len(arrays)

    def body(*refs):
        send, recv, token = refs[n], refs[n + 1], refs[-1]
        for cp in copies(refs[:n], send, recv):
            cp.start()
        token[...] = jnp.zeros_like(token)

    sem = pltpu.SemaphoreType.DMA((n_sems,))
    outs = pl.pallas_call(
        body, name=name, in_specs=[HBM] * n,
        out_specs=[SEM, SEM] + [HBM] * n + [pl.BlockSpec(memory_space=pltpu.VMEM)],
        out_shape=[sem, sem] + [pltpu.HBM(a.shape, a.dtype) for a in arrays] + [jax.ShapeDtypeStruct((SUB, HEAD), F32)],
        input_output_aliases={i: 2 + i for i in range(n)},
        compiler_params=pltpu.CompilerParams(has_side_effects=SIDE),
    )(*[pltpu.with_memory_space_constraint(a, pltpu.HBM) for a in arrays])
    return outs[0], outs[1], list(outs[2:2 + n]), outs[-1]


def _wait_copies(name, send, recv, arrays, after, copies):
    n = len(arrays)

    def body(*refs):
        for cp in copies(refs[:n], refs[n], refs[n + 1]):
            cp.wait_send()
            cp.wait_recv()

    return list(pl.pallas_call(
        body, name=name, in_specs=[HBM] * n + [SEM, SEM, ANY], out_specs=[HBM] * n,
        out_shape=[pltpu.HBM(a.shape, a.dtype) for a in arrays],
        input_output_aliases={i: i for i in range(n)},
        compiler_params=pltpu.CompilerParams(has_side_effects=SIDE),
    )(*arrays, send, recv, after))


def _partial_copies(n):
    def copies(refs, send, recv):
        return [_partial_copy(refs[k], refs[n + k], send, recv, k, r) for k in range(n) for r in range(1, N_PEERS + 1)]
    return copies


def _shard_copies(n):
    def copies(refs, send, recv):
        return [_shard_copy(refs[k], send, recv, k, r) for k in range(n) for r in range(2, N_PEERS + 1)]
    return copies


def _cast_place(name, w, shard):
    r, c = w.shape
    tr = _pick(r, (256, 128, 64, 32, 16))

    def body(s_ref, w_ref, o_ref):
        o_ref[...] = w_ref[...].astype(BF16)

    return pl.pallas_call(
        body, name=name,
        grid_spec=pltpu.PrefetchScalarGridSpec(
            num_scalar_prefetch=1, grid=(r // tr,), in_specs=[pl.BlockSpec((tr, c), lambda i, sr: (i, 0))],
            out_specs=pl.BlockSpec((None, tr, c), lambda i, sr: (sr[0], i, 0))),
        out_shape=jax.ShapeDtypeStruct((N_CHIPS, r, c), BF16),
        compiler_params=_cparams("parallel"),
    )(shard, w)


def _sum_peers(name, g, land, shard, core):
    _, r, cols = g.shape
    half = r // 2
    tr = _pick(half, (128, 64, 32, 16))
    nt = half // tr

    def body(s_ref, c_ref, g_ref, l_ref, o_ref):
        acc = g_ref[...].astype(F32)
        for j in range(N_PEERS):
            acc = acc + l_ref[j].astype(F32)
        o_ref[...] = acc

    return pl.pallas_call(
        body, name=name,
        grid_spec=pltpu.PrefetchScalarGridSpec(
            num_scalar_prefetch=2, grid=(nt,),
            in_specs=[pl.BlockSpec((None, tr, cols), lambda i, sr, cr: (sr[0], cr[0] * nt + i, 0)),
                      pl.BlockSpec((N_PEERS, tr, cols), lambda i, sr, cr: (0, i, 0))],
            out_specs=pl.BlockSpec((tr, cols), lambda i, sr, cr: (i, 0))),
        out_shape=jax.ShapeDtypeStruct((half, cols), F32),
        compiler_params=_cparams("parallel"),
    )(shard, core, g, land)


def _join_halves(mine, got, c):
    half, cols = mine.shape
    tr = _pick(half, (256, 128, 64, 32, 16, 8))
    nt = half // tr

    def body(c_ref, a_ref, b_ref, o_ref):
        i = pl.program_id(0)
        own = (i // nt) == c_ref[0]

        @pl.when(own)
        def _():
            o_ref[...] = a_ref[...]

        @pl.when(jnp.logical_not(own))
        def _():
            o_ref[...] = b_ref[...]

    blk = pl.BlockSpec((tr, cols), lambda i, cr: (i % nt, 0))
    return pl.pallas_call(
        body, name="join_halves",
        grid_spec=pltpu.PrefetchScalarGridSpec(num_scalar_prefetch=1, grid=(2 * nt,), in_specs=[blk, blk],
                                               out_specs=pl.BlockSpec((tr, cols), lambda i, cr: (i, 0))),
        out_shape=jax.ShapeDtypeStruct((2 * half, cols), F32),
        compiler_params=_cparams("parallel"),
    )(c, mine, got)


def _all_gather_rows(name, row):
    p = row.shape[1]

    def body(in_ref, out_ref, send, recv, local):
        x, y, c, _ = _place()
        me = 4 * x + 2 * y + c
        own = pltpu.make_async_copy(in_ref, out_ref.at[me], local)
        own.start()
        cps = []
        for k in range(1, 8):
            px, py, pc = x ^ (k >> 2), y ^ ((k >> 1) & 1), c ^ (k & 1)
            cp = _remote(in_ref, out_ref.at[me], send.at[k - 1], recv.at[k - 1], (px, py, pc))
            cp.start()
            cps.append(cp)
        for cp in cps:
            cp.wait_recv()
        for cp in cps:
            cp.wait_send()
        own.wait()

    return pl.pallas_call(
        body, name=name, in_specs=[ANY], out_specs=ANY,
        out_shape=jax.ShapeDtypeStruct((8, 1, p), F32),
        scratch_shapes=[pltpu.SemaphoreType.DMA((7,)), pltpu.SemaphoreType.DMA((7,)), pltpu.SemaphoreType.DMA],
    )(row)


def _sum_devices(rows):
    def body(r_ref, o_ref):
        acc = r_ref[0]
        for k in range(1, 8):
            acc = acc + r_ref[k]
        o_ref[...] = acc

    return pl.pallas_call(body, name="sum_devices", out_shape=jax.ShapeDtypeStruct(rows.shape[1:], F32))(rows)


def _adamw(name, w, g, m, v):
    r, c = w.shape
    tr = _pick(r, (128, 64, 32, 16, 8))
    bc1 = 1.0 - ADAM_B1 ** ADAM_STEP
    bc2 = 1.0 - ADAM_B2 ** ADAM_STEP

    def body(w_ref, g_ref, m_ref, v_ref, d_ref, nm_ref, nv_ref):
        gv = g_ref[...]
        nm = ADAM_B1 * m_ref[...] + (1.0 - ADAM_B1) * gv
        nv = ADAM_B2 * v_ref[...] + (1.0 - ADAM_B2) * (gv * gv)
        d_ref[...] = -ADAM_LR * ((nm / bc1) / (jnp.sqrt(nv / bc2) + ADAM_EPS) + ADAM_WD * w_ref[...])
        nm_ref[...] = nm
        nv_ref[...] = nv

    blk = pl.BlockSpec((tr, c), lambda i: (i, 0))
    sh = jax.ShapeDtypeStruct((r, c), F32)
    return pl.pallas_call(
        body, name=name, grid=(r // tr,), in_specs=[blk] * 4, out_specs=[blk] * 3, out_shape=[sh] * 3,
        compiler_params=_cparams("parallel"),
    )(w, g, m, v)


def _lb_logits_grad(dlb, lb):
    def body(d_ref, lb_ref, o_ref):
        lbv = lb_ref[...]
        t = d_ref[...] * lbv * (1.0 - lbv)
        o_ref[0:1, :] = t
        o_ref[1:2, :] = -t

    return pl.pallas_call(body, name="lb_logits_grad", out_shape=jax.ShapeDtypeStruct((2, dlb.shape[1]), F32))(dlb, lb)


BIG = ("w_in", "p_a", "p_b", "w_o", "w_up", "w_down")
SMALL = ("g_mix", "g_q", "g_k", "lb_logits", "g_hg_out", "g_ffn", "conv_w", "conv_b")
ORDER = ("g_mix", "w_in", "g_q", "g_k", "lb_logits", "g_hg_out", "p_a", "p_b", "w_o", "g_ffn", "w_up", "conv_w", "conv_b", "w_down")


def kernel(x, g_mix, w_in, g_q, g_k, lb_logits, g_hg_out, p_a, p_b, w_o, g_ffn, w_up, conv_w, conv_b, w_down, loss_target, m_g_mix, m_w_in, m_g_q, m_g_k, m_lb_logits, m_g_hg_out, m_p_a, m_p_b, m_w_o, m_g_ffn, m_w_up, m_conv_w, m_conv_b, m_w_down, v_g_mix, v_w_in, v_g_q, v_g_k, v_lb_logits, v_g_hg_out, v_p_a, v_p_b, v_w_o, v_g_ffn, v_w_up, v_conv_w, v_conv_b, v_w_down):
    assert lb_logits.shape[0] == 2, "the lower bound is the first row of a two-row softmax"
    w = dict(g_mix=g_mix, w_in=w_in[0], g_q=g_q, g_k=g_k, lb_logits=lb_logits, g_hg_out=g_hg_out, p_a=p_a[0], p_b=p_b[0],
             w_o=w_o[0], g_ffn=g_ffn, w_up=w_up[0], conv_w=conv_w[0], conv_b=conv_b, w_down=w_down[0])
    mom = dict(g_mix=m_g_mix, w_in=m_w_in[0], g_q=m_g_q, g_k=m_g_k, lb_logits=m_lb_logits, g_hg_out=m_g_hg_out, p_a=m_p_a[0],
               p_b=m_p_b[0], w_o=m_w_o[0], g_ffn=m_g_ffn, w_up=m_w_up[0], conv_w=m_conv_w[0], conv_b=m_conv_b, w_down=m_w_down[0])
    var = dict(g_mix=v_g_mix, w_in=v_w_in[0], g_q=v_g_q, g_k=v_g_k, lb_logits=v_lb_logits, g_hg_out=v_g_hg_out, p_a=v_p_a[0],
               p_b=v_p_b[0], w_o=v_w_o[0], g_ffn=v_g_ffn, w_up=v_w_up[0], conv_w=v_conv_w[0], conv_b=v_conv_b, w_down=v_w_down[0])
    d = x.shape[2]
    cx, cy, cc = lax.axis_index("x"), lax.axis_index("y"), lax.axis_index("c")
    shard = (2 * cx + cy).astype(jnp.int32).reshape(1)
    core = cc.astype(jnp.int32).reshape(1)

    w_in_full = _gather_weights([_cast_bf16("cast_w_in", w["w_in"])])[0]
    later = ("p_a", "p_b", "w_o", "w_up", "w_down")
    n_later = len(later)
    fulls = [_cast_place("cast_" + n, w[n], shard) for n in later]
    w_send, w_recv, fulls, w_token = _start_copies("weights_start", fulls + [w_in_full], n_later * (N_PEERS - 1),
                                                   _shard_copies(n_later))
    w_in_full = fulls.pop()

    def later_weights(after):
        return dict(zip(later, _wait_copies("weights_wait", w_send, w_recv, fulls, after, _shard_copies(n_later))))

    cw = conv_w.shape[2]
    rows = _all_gather_rows("gather_conv_w", w["conv_w"].reshape(1, 3 * cw))
    conv_full = jnp.concatenate([rows[2 * s, 0].reshape(3, cw) for s in range(N_CHIPS)], axis=1)

    pending = []

    def grads_ready(names, gs):
        n = len(gs)
        lands = [lax.empty((N_PEERS, g.shape[1] // 2, g.shape[2]), BF16) for g in gs]
        send, recv, arrays, token = _start_copies("partials_start_" + names[0], list(gs) + lands, n * N_PEERS, _partial_copies(n))
        pending.append((names, send, recv, arrays))
        return token[0:1, 0:1]

    dx, small, lb = _local_step(x[0], loss_target[0], g_mix + w_token[0:1, 0:1], g_q, g_k, lb_logits, g_hg_out, g_ffn,
                                conv_full, conv_b, w_in_full, later_weights, grads_ready)

    mine = {}
    after = dx
    for names, send, recv, arrays in pending:
        n = len(names)
        arrays = _wait_copies("partials_wait_" + names[0], send, recv, arrays, after, _partial_copies(n))
        for k, name in enumerate(names):
            mine[name] = _sum_peers("sum_" + name, arrays[k], arrays[n + k], shard, core)
        after = mine[names[-1]]
    theirs = _to_sibling("reduced_to_sibling", [mine[n] for n in BIG])
    grads = {n: _join_halves(mine[n], b, core) for n, b in zip(BIG, theirs)}

    names = ("g_mix", "g_q", "g_k", "lb", "g_hg_out", "g_ffn", "conv_b", "sq")
    packed = jnp.concatenate([small[n] for n in names] + [small["conv_w"].reshape(1, -1)], axis=1)
    total = _sum_devices(_all_gather_rows("gather_small_grads", packed))
    off = 0
    red = {}
    for n in names:
        ln = small[n].shape[1]
        red[n] = total[:, off:off + ln]
        off += ln
    conv_all = total[:, off:].reshape(3, -1)
    loss = 0.5 * jnp.sum(red["sq"]) / d
    grads["conv_w"] = lax.dynamic_slice_in_dim(conv_all, (2 * cx + cy) * cw, cw, axis=1)
    grads["lb_logits"] = _lb_logits_grad(red["lb"], lb)
    for n in ("g_mix", "g_q", "g_k", "g_hg_out", "g_ffn", "conv_b"):
        grads[n] = red[n]

    delta, new_m, new_v = {}, {}, {}
    for n in ORDER:
        delta[n], new_m[n], new_v[n] = _adamw("adamw_" + n, w[n], grads[n], mom[n], var[n])

    def shaped(a, like):
        return a.reshape(like.shape)

    ref_w = dict(g_mix=g_mix, w_in=w_in, g_q=g_q, g_k=g_k, lb_logits=lb_logits, g_hg_out=g_hg_out, p_a=p_a, p_b=p_b, w_o=w_o,
                 g_ffn=g_ffn, w_up=w_up, conv_w=conv_w, conv_b=conv_b, w_down=w_down)
    outs = [loss, dx[None]]
    for group in (grads, delta, new_m, new_v):
        outs += [shaped(group[n], ref_w[n]) for n in ORDER]
    return tuple(outs)
```

```python
import functools

import jax
import jax.numpy as jnp
from jax import lax
from jax.experimental import pallas as pl
from jax.experimental.pallas import tpu as pltpu

F32 = jnp.float32
BF16 = jnp.bfloat16
HEAD = 128
EPS = 1e-6
N_CHIPS = 4
HG_CHUNK = 32
SUB = 8
ADAM_LR, ADAM_B1, ADAM_B2, ADAM_EPS, ADAM_WD, ADAM_STEP = 0.001, 0.9, 0.999, 1e-08, 0.01, 10
VMEM_LIMIT = 56 * 1024 * 1024
MESH = pl.DeviceIdType.MESH

NN = (((1,), (0,)), ((), ()))
NT = (((1,), (1,)), ((), ()))
TN = (((0,), (0,)), ((), ()))


def _cparams(*sem):
    return pltpu.CompilerParams(dimension_semantics=sem if sem else None, vmem_limit_bytes=VMEM_LIMIT)


def _pick(n, cands):
    for c in cands:
        if c <= n and n % c == 0:
            return c
    return n


def _sig(x):
    return 0.5 * jnp.tanh(0.5 * x) + 0.5


def _dot(a, b, dims):
    return lax.dot_general(a, b, dims, preferred_element_type=F32)


def _split(x):
    hi = x.astype(BF16)
    lo = (x - hi.astype(F32)).astype(BF16)
    return hi, lo


def _tri(n, kind):
    r = lax.broadcasted_iota(jnp.int32, (n, n), 0)
    c = lax.broadcasted_iota(jnp.int32, (n, n), 1)
    m = {"ge": c >= r, "gt": c > r, "le": c <= r, "lt": c < r}[kind]
    return jnp.where(m, 1.0, 0.0).astype(BF16)


TILE_M = (1024, 512, 256, 128)
TILE_N = (1408, 1024, 512, 256, 128)
TILE_K = (2816, 2048, 1408, 1024, 512, 256, 128)


def _matmul(name, a, b, mode, out_dtype, add=None, out_shards=None):
    if mode == "nn":
        m, k = a.shape
        g, _, ns = b.shape
        n = g * ns
        ns_out = n // out_shards if out_shards else ns
        tm, tn, tk = _pick(m, TILE_M), _pick(min(ns, ns_out), TILE_N), _pick(k, TILE_K)
        nps = ns // tn
        grid = (m // tm, n // tn, k // tk)
        a_spec = pl.BlockSpec((tm, tk), lambda i, j, kk: (i, kk))
        b_spec = pl.BlockSpec((None, tk, tn), lambda i, j, kk: (j // nps, kk, j % nps))
        if out_shards:
            npo = ns_out // tn
            o_spec = pl.BlockSpec((None, tm, tn), lambda i, j, kk: (j // npo, i, j % npo))
            o_shape = jax.ShapeDtypeStruct((out_shards, m, ns_out), out_dtype)
        else:
            o_spec = pl.BlockSpec((tm, tn), lambda i, j, kk: (i, j))
            o_shape = jax.ShapeDtypeStruct((m, n), out_dtype)
        dims = NN
    elif mode == "nt":
        m, k = a.shape
        g, n, ks = b.shape
        tm, tn, tk = _pick(m, TILE_M), _pick(n, TILE_N), _pick(ks, TILE_K)
        kps = ks // tk
        grid = (m // tm, n // tn, k // tk)
        a_spec = pl.BlockSpec((tm, tk), lambda i, j, kk: (i, kk))
        b_spec = pl.BlockSpec((None, tn, tk), lambda i, j, kk: (kk // kps, j, kk % kps))
        o_spec = pl.BlockSpec((tm, tn), lambda i, j, kk: (i, j))
        o_shape = jax.ShapeDtypeStruct((m, n), out_dtype)
        dims = NT
    else:
        raise ValueError(mode)
    nk = grid[2]

    def body(*refs):
        a_ref, b_ref = refs[0], refs[1]
        add_ref = refs[2] if add is not None else None
        o_ref = refs[2 + (add is not None)]

        def finish(r):
            if add is not None:
                r = r + add_ref[...]
            o_ref[...] = r.astype(o_ref.dtype)

        part = _dot(a_ref[...].astype(BF16), b_ref[...].astype(BF16), dims)
        if nk == 1:
            finish(part)
            return
        acc = refs[-1]
        kk = pl.program_id(2)

        @pl.when(kk == 0)
        def _():
            acc[...] = part

        @pl.when(kk > 0)
        def _():
            acc[...] += part

        @pl.when(kk == nk - 1)
        def _():
            finish(acc[...])

    in_specs = [a_spec, b_spec]
    args = [a, b]
    if add is not None:
        in_specs.append(o_spec)
        args.append(add)
    return pl.pallas_call(
        body, name=name, grid=grid, in_specs=in_specs, out_specs=o_spec, out_shape=o_shape,
        scratch_shapes=[pltpu.VMEM((tm, tn), F32)] if nk > 1 else [],
        compiler_params=_cparams("parallel", "parallel", "arbitrary"),
    )(*args)


def _matmul_tn(name, a, b, g, out_dtype):
    k, m = a.shape
    _, n = b.shape
    ns = n // g
    tm, tn, tk = _pick(m, (2048, 1408) + TILE_M), _pick(ns, TILE_N), _pick(k, (1024, 512, 256, 128))
    nps = ns // tn
    nk = k // tk

    def body(a_ref, b_ref, o_ref, acc):
        kk = pl.program_id(2)
        part = _dot(a_ref[...].astype(BF16), b_ref[...].astype(BF16), TN)

        @pl.when(kk == 0)
        def _():
            acc[...] = part

        @pl.when(kk > 0)
        def _():
            acc[...] += part

        @pl.when(kk == nk - 1)
        def _():
            o_ref[...] = acc[...].astype(o_ref.dtype)

    return pl.pallas_call(
        body, name=name, grid=(m // tm, n // tn, nk),
        in_specs=[pl.BlockSpec((tk, tm), lambda i, j, kk: (kk, i)), pl.BlockSpec((tk, tn), lambda i, j, kk: (kk, j))],
        out_specs=pl.BlockSpec((None, tm, tn), lambda i, j, kk: (j // nps, i, j % nps)),
        out_shape=jax.ShapeDtypeStruct((g, m, ns), out_dtype),
        scratch_shapes=[pltpu.VMEM((tm, tn), F32)],
        compiler_params=_cparams("parallel", "parallel", "arbitrary"),
    )(a, b)


def _rmsnorm_fwd(name, x, g):
    s, d = x.shape
    tr = _pick(s, (256, 128))

    def body(x_ref, g_ref, u_ref, ut_ref):
        xv = x_ref[...]
        r = lax.rsqrt(jnp.mean(xv * xv, axis=-1, keepdims=True) + EPS)
        u = xv * r * g_ref[...]
        u_ref[...] = u.astype(BF16)
        ut_ref[...] = u.T.astype(BF16)

    return pl.pallas_call(
        body, name=name, grid=(s // tr,),
        in_specs=[pl.BlockSpec((tr, d), lambda i: (i, 0)), pl.BlockSpec((1, d), lambda i: (0, 0))],
        out_specs=[pl.BlockSpec((tr, d), lambda i: (i, 0)), pl.BlockSpec((d, tr), lambda i: (0, i))],
        out_shape=[jax.ShapeDtypeStruct((s, d), BF16), jax.ShapeDtypeStruct((d, s), BF16)],
        compiler_params=_cparams("parallel"),
    )(x, g)


def _rmsnorm_bwd(name, x, g, du, extra):
    s, d = x.shape
    tr = _pick(s, (256, 128, 64, 32, 16, 8))

    def body(x_ref, g_ref, du_ref, e_ref, dx_ref, dg_ref):
        i = pl.program_id(0)
        xv = x_ref[...]
        r = lax.rsqrt(jnp.mean(xv * xv, axis=-1, keepdims=True) + EPS)
        n = xv * r
        dy = du_ref[...]
        a = dy * g_ref[...]
        dx = r * (a - n * jnp.mean(a * n, axis=-1, keepdims=True))
        dx_ref[...] = e_ref[...] + dx

        @pl.when(i == 0)
        def _():
            dg_ref[...] = jnp.zeros_like(dg_ref)

        dg_ref[...] += jnp.sum((dy * n).reshape(tr // SUB, SUB, d), axis=0)

    return pl.pallas_call(
        body, name=name, grid=(s // tr,),
        in_specs=[pl.BlockSpec((tr, d), lambda i: (i, 0)), pl.BlockSpec((1, d), lambda i: (0, 0)),
                  pl.BlockSpec((tr, d), lambda i: (i, 0)), pl.BlockSpec((tr, d), lambda i: (i, 0))],
        out_specs=[pl.BlockSpec((tr, d), lambda i: (i, 0)), pl.BlockSpec((SUB, d), lambda i: (0, 0))],
        out_shape=[jax.ShapeDtypeStruct((s, d), F32), jax.ShapeDtypeStruct((SUB, d), F32)],
        compiler_params=_cparams("arbitrary"),
    )(x, g, du, extra)


def _head_norm(x, g):
    r = lax.rsqrt(jnp.mean(x * x, axis=-1, keepdims=True) + EPS)
    return x * r * g


def _qk_norm_fwd(proj, g_q, g_k, nh, blk):
    s = proj.shape[0]

    def body(q_ref, k_ref, v_ref, gq_ref, gk_ref, qn_ref, kn_ref, vb_ref, kt_ref, vt_ref):
        qn_ref[...] = _head_norm(q_ref[...], gq_ref[...]).astype(BF16)
        kn = _head_norm(k_ref[...], gk_ref[...])
        kn_ref[...] = kn.astype(BF16)
        kt_ref[...] = kn.T.astype(BF16)
        v = v_ref[...]
        vb_ref[...] = v.astype(BF16)
        vt_ref[...] = v.T.astype(BF16)

    col = lambda base: pl.BlockSpec((blk, HEAD), lambda i, h: (i, base + h))
    gs = pl.BlockSpec((1, HEAD), lambda i, h: (0, 0))
    o = pl.BlockSpec((blk, HEAD), lambda i, h: (i, h))
    t = pl.BlockSpec((None, HEAD, blk), lambda i, h: (i, h, 0))
    sh = jax.ShapeDtypeStruct((s, nh * HEAD), BF16)
    tsh = jax.ShapeDtypeStruct((s // blk, nh * HEAD, blk), BF16)
    return pl.pallas_call(
        body, name="qk_norm_fwd", grid=(s // blk, nh),
        in_specs=[col(0), col(nh), col(2 * nh), gs, gs], out_specs=[o, o, o, t, t], out_shape=[sh, sh, sh, tsh, tsh],
        compiler_params=_cparams("parallel", "parallel"),
    )(proj, proj, proj, g_q, g_k)


def _qk_norm_bwd(proj, g_q, g_k, dqn, dkn, nh):
    s = proj.shape[0]
    tr = _pick(s, (512, 256, 128, 64, 32, 16, 8))

    def one(x, g, dy):
        r = lax.rsqrt(jnp.mean(x * x, axis=-1, keepdims=True) + EPS)
        n = x * r
        a = dy * g
        dx = r * (a - n * jnp.mean(a * n, axis=-1, keepdims=True))
        return dx, jnp.sum((dy * n).reshape(tr // SUB, SUB, HEAD), axis=0)

    def body(q_ref, k_ref, gq_ref, gk_ref, dqn_ref, dkn_ref, dq_ref, dk_ref, dgq_ref, dgk_ref):
        first = (pl.program_id(0) == 0) & (pl.program_id(1) == 0)

        @pl.when(first)
        def _():
            dgq_ref[...] = jnp.zeros_like(dgq_ref)
            dgk_ref[...] = jnp.zeros_like(dgk_ref)

        dq, pq = one(q_ref[...], gq_ref[...], dqn_ref[...])
        dk, pk = one(k_ref[...], gk_ref[...], dkn_ref[...])
        dq_ref[...] = dq.astype(BF16)
        dk_ref[...] = dk.astype(BF16)
        dgq_ref[...] += pq
        dgk_ref[...] += pk

    col = lambda base: pl.BlockSpec((tr, HEAD), lambda i, h: (i, base + h))
    gs = pl.BlockSpec((1, HEAD), lambda i, h: (0, 0))
    o = pl.BlockSpec((tr, HEAD), lambda i, h: (i, h))
    part = pl.BlockSpec((SUB, HEAD), lambda i, h: (0, 0))
    sh = jax.ShapeDtypeStruct((s, nh * HEAD), BF16)
    psh = jax.ShapeDtypeStruct((SUB, HEAD), F32)
    return pl.pallas_call(
        body, name="qk_norm_bwd", grid=(s // tr, nh),
        in_specs=[col(0), col(nh), gs, gs, o, o], out_specs=[o, o, part, part], out_shape=[sh, sh, psh, psh],
        compiler_params=_cparams("arbitrary", "arbitrary"),
    )(proj, proj, g_q, g_k, dqn, dkn)


def _sb_consts(blk, hp):
    upper = _tri(blk, "ge")
    row = lax.broadcasted_iota(jnp.int32, (blk, hp * blk), 0)
    col = lax.broadcasted_iota(jnp.int32, (blk, hp * blk), 1)
    strict = row < col
    for h in range(1, hp):
        strict = strict & ((col < h * blk) | (row < col - h * blk))
    return jnp.concatenate([upper, upper], axis=1), strict


def _sb_log_keep(zt, strict):
    l = jnp.minimum(-zt, 0.0) - jnp.log(1.0 + jnp.exp(-jnp.abs(zt)))
    return l if strict is None else jnp.where(strict, l, 0.0)


SB_GROUP = 4
SB_GROUP_BWD = 4


def _sb_heads(nh):
    return 2 if nh % 2 == 0 else 1


def _sb_attn_fwd(qn, kn, vt, nh, blk):
    s = qn.shape[0]
    nb = s // blk
    scale = HEAD ** -0.5
    hp = _sb_heads(nh)

    def body(q_ref, k_ref, vt_ref, y_ref, c_ref):
        qi = pl.program_id(1)
        suffix, strict = _sb_consts(blk, hp)
        qs = [q_ref[:, h * HEAD:(h + 1) * HEAD] for h in range(hp)]

        def logits(kb_i):
            off = pl.multiple_of(kb_i * blk, blk)
            return jnp.concatenate(
                [_dot(k_ref[pl.ds(off, blk), h * HEAD:(h + 1) * HEAD], qs[h], NT) for h in range(hp)], axis=1) * scale

        def sums(zt, mask):
            l = _sb_log_keep(zt, mask)
            parts = []
            for h in range(hp):
                hi, lo = _split(l[:, h * blk:(h + 1) * blk])
                parts.append(_dot(suffix, jnp.concatenate([hi, lo], axis=0), NN))
            return jnp.concatenate(parts, axis=1)

        def weights(kb_i, zt, cum, cr, mask):
            for h in range(hp):
                c_ref[h, kb_i] = cr[:, h * blk:(h + 1) * blk]
            wt = jnp.exp(zt + cum + cr)
            if mask is not None:
                wt = jnp.where(mask, wt, 0.0)
            return wt.astype(BF16), cr + cum[0:1, :]

        def add_values(kb_i, wt, accs):
            return tuple(
                accs[h] + _dot(vt_ref[kb_i, h * HEAD:(h + 1) * HEAD, :], wt[:, h * blk:(h + 1) * blk], NN)
                for h in range(hp))

        def group(kbs, masks, accs, cr):
            zts = [logits(k) for k in kbs]
            cums = [sums(zt, m) for zt, m in zip(zts, masks)]
            for k, zt, cum, m in zip(kbs, zts, cums, masks):
                wt, cr = weights(k, zt, cum, cr, m)
                accs = add_values(k, wt, accs)
            return accs, cr

        accs = tuple(jnp.zeros((HEAD, blk), F32) for _ in range(hp))
        accs, cr = group([qi], [strict], accs, jnp.zeros((1, hp * blk), F32))
        n_groups = qi // SB_GROUP

        def many(g, st):
            top = qi - 1 - g * SB_GROUP
            return group([top - j for j in range(SB_GROUP)], [None] * SB_GROUP, *st)

        def one(r, st):
            return group([qi - 1 - n_groups * SB_GROUP - r], [None], *st)

        st = lax.fori_loop(0, n_groups, many, (accs, cr))
        accs, _ = lax.fori_loop(0, qi - n_groups * SB_GROUP, one, st)
        for h in range(hp):
            y_ref[:, h * HEAD:(h + 1) * HEAD] = accs[h].T.astype(y_ref.dtype)

    qs_ = pl.BlockSpec((blk, hp * HEAD), lambda h, i: (i, h))
    full = pl.BlockSpec((s, hp * HEAD), lambda h, i: (0, h))
    return pl.pallas_call(
        body, name="sb_attn_fwd", grid=(nh // hp, nb),
        in_specs=[qs_, full, pl.BlockSpec((nb, hp * HEAD, blk), lambda h, i: (0, h, 0))],
        out_specs=[qs_, pl.BlockSpec((hp, nb, 1, blk), lambda h, i: (h, 0, 0, i))],
        out_shape=[jax.ShapeDtypeStruct((s, nh * HEAD), BF16), jax.ShapeDtypeStruct((nh, nb, 1, s), F32)],
        compiler_params=_cparams("parallel", "arbitrary"),
    )(qn, kn, vt)


def _sb_attn_bwd(qn, kn, kt, vb, dy, carries, nh, blk):
    s = qn.shape[0]
    nb = s // blk
    scale = HEAD ** -0.5
    hp = _sb_heads(nh)

    def body(q_ref, k_ref, kt_ref, v_ref, dy_ref, c_ref, dq_ref, dk_ref, dv_ref):
        qi = pl.program_id(1)

        @pl.when(qi == 0)
        def _():
            dk_ref[...] = jnp.zeros_like(dk_ref)
            dv_ref[...] = jnp.zeros_like(dv_ref)

        suffix, strict = _sb_consts(blk, hp)
        prefix = _tri(blk, "lt")
        qs = [q_ref[:, h * HEAD:(h + 1) * HEAD] for h in range(hp)]
        dos = [dy_ref[:, h * HEAD:(h + 1) * HEAD].astype(BF16) for h in range(hp)]

        def logits(kb_i):
            off = pl.multiple_of(kb_i * blk, blk)
            return jnp.concatenate(
                [_dot(k_ref[pl.ds(off, blk), h * HEAD:(h + 1) * HEAD], qs[h], NT) for h in range(hp)], axis=1) * scale

        def group(kbs, masks, dqs, ec):
            rows = [pl.ds(pl.multiple_of(k * blk, blk), blk) for k in kbs]
            zts = [logits(k) for k in kbs]
            dws = [jnp.concatenate([_dot(v_ref[r, h * HEAD:(h + 1) * HEAD], dos[h], NT) for h in range(hp)], axis=1)
                   for r in rows]
            ls = [_sb_log_keep(zt, m) for zt, m in zip(zts, masks)]
            cums = []
            for l in ls:
                parts = []
                for h in range(hp):
                    hi, lo = _split(l[:, h * blk:(h + 1) * blk])
                    parts.append(_dot(suffix, jnp.concatenate([hi, lo], axis=0), NN))
                cums.append(jnp.concatenate(parts, axis=1))
            wts, ets, befores = [], [], []
            for k, zt, cum, dw, m in zip(kbs, zts, cums, dws, masks):
                cr = jnp.concatenate([c_ref[h, k] for h in range(hp)], axis=1)
                wt = jnp.exp(zt + cum + cr)
                if m is not None:
                    wt = jnp.where(m, wt, 0.0)
                et = wt * dw
                befores.append(_dot(prefix, et.astype(BF16), NN) + ec)
                ec = ec + jnp.sum(et, axis=0, keepdims=True)
                wts.append(wt.astype(BF16))
                ets.append(et)
            for k, r, l, et, before, wtb in zip(kbs, rows, ls, ets, befores, wts):
                dzt = ((jnp.exp(l) * (et + before) - before) * scale).astype(BF16)
                out = []
                for h in range(hp):
                    cols, part = slice(h * HEAD, (h + 1) * HEAD), slice(h * blk, (h + 1) * blk)
                    dv_ref[r, cols] += _dot(wtb[:, part], dos[h], NN)
                    dk_ref[r, cols] += _dot(dzt[:, part], qs[h], NN)
                    out.append(dqs[h] + _dot(kt_ref[k, cols, :], dzt[:, part], NN))
                dqs = tuple(out)
            return dqs, ec

        n_groups = qi // SB_GROUP_BWD

        def many(g, st):
            return group([g * SB_GROUP_BWD + j for j in range(SB_GROUP_BWD)], [None] * SB_GROUP_BWD, *st)

        def one(r, st):
            return group([n_groups * SB_GROUP_BWD + r], [None], *st)

        st = (tuple(jnp.zeros((HEAD, blk), F32) for _ in range(hp)), jnp.zeros((1, hp * blk), F32))
        st = lax.fori_loop(0, n_groups, many, st)
        st = lax.fori_loop(0, qi - n_groups * SB_GROUP_BWD, one, st)
        dqs, _ = group([qi], [strict], *st)
        for h in range(hp):
            dq_ref[:, h * HEAD:(h + 1) * HEAD] = dqs[h].T

    qs_ = pl.BlockSpec((blk, hp * HEAD), lambda h, i: (i, h))
    full = pl.BlockSpec((s, hp * HEAD), lambda h, i: (0, h), pipeline_mode=pl.Buffered(1))
    sh = jax.ShapeDtypeStruct((s, nh * HEAD), F32)
    return pl.pallas_call(
        body, name="sb_attn_bwd", grid=(nh // hp, nb),
        in_specs=[qs_, full, pl.BlockSpec((nb, hp * HEAD, blk), lambda h, i: (0, h, 0), pipeline_mode=pl.Buffered(1)),
                  full, qs_, pl.BlockSpec((hp, nb, 1, blk), lambda h, i: (h, 0, 0, i))],
        out_specs=[qs_, full, full], out_shape=[sh, sh, sh],
        compiler_params=_cparams("parallel", "arbitrary"),
    )(qn, kn, kt, vb, dy, carries)


def _lower_bound(lb_logits):
    def body(l_ref, o_ref):
        l = l_ref[...]
        m = jnp.max(l, axis=0, keepdims=True)
        e = jnp.exp(l - m)
        o_ref[...] = e[0:1, :] / jnp.sum(e, axis=0, keepdims=True)

    return pl.pallas_call(body, name="lower_bound", out_shape=jax.ShapeDtypeStruct((1, lb_logits.shape[1]), F32))(lb_logits)


def _hg_gates(hq, hf, lb):
    sq = _sig(hq)
    q = hq * sq
    sf = _sig(hf)
    f = lb + (1.0 - lb) * sf
    return q, sq, f, sf


def _hg_cum(g, c):
    hi, lo = _split(g)
    t = _tri(c, "le")
    return _dot(t, hi, NN) + _dot(t, lo, NN)


def _hg_heads(nh):
    return 8 if nh % 8 == 0 else 4 if nh % 4 == 0 else 2 if nh % 2 == 0 else 1


def _head_cols(x, h):
    return x[:, h * HEAD:(h + 1) * HEAD]


def _per_head(x, hp, fn):
    return jnp.concatenate([fn(_head_cols(x, h), h) for h in range(hp)], axis=1)


def _head_sums(x, hp):
    return [jnp.sum(_head_cols(x, h), axis=1, keepdims=True) for h in range(hp)]


def _head_scale(cols, x, hp):
    return jnp.concatenate([cols[h] * _head_cols(x, h) for h in range(hp)], axis=1)


def _row_mask(r, width):
    return lax.broadcasted_iota(jnp.int32, (SUB, width), 0) >= r


def _hg_intra_fwd(q, k, v, b, c, hp):
    outs = []
    for bi in range(c // SUB):
        q_i, b_i = q[bi * SUB:(bi + 1) * SUB], b[bi * SUB:(bi + 1) * SUB]
        acc = jnp.zeros((SUB, hp * HEAD), F32)
        for s in range((bi + 1) * SUB):
            d = b_i - b[s:s + 1]
            if s >= bi * SUB:
                d = jnp.where(_row_mask(s - bi * SUB, hp * HEAD), d, -jnp.inf)
            acc = acc + _head_scale(_head_sums(q_i * k[s:s + 1] * jnp.exp(d), hp), v[s:s + 1], hp)
        outs.append(acc)
    return jnp.concatenate(outs, axis=0)


def _hg_intra_bwd(q, k, v, b, do, c, hp, dq_scr, dk_scr, dv_scr):
    nblk = c // SUB
    dq_scr[...] = jnp.zeros_like(dq_scr)
    for s in range(c):
        bj = s // SUB
        ks, vs, bs = k[s:s + 1], v[s:s + 1], b[s:s + 1]
        acc_k = jnp.zeros((SUB, hp * HEAD), F32)
        acc_v = jnp.zeros((SUB, hp * HEAD), F32)
        for bi in range(bj, nblk):
            sl = slice(bi * SUB, (bi + 1) * SUB)
            d = b[sl] - bs
            if bi == bj:
                d = jnp.where(_row_mask(s - bj * SUB, hp * HEAD), d, -jnp.inf)
            dec = jnp.exp(d)
            qd = q[sl] * dec
            col = _head_sums(qd * ks, hp)
            dcol = _head_sums(do[sl] * vs, hp)
            dq_scr[sl, :] += _head_scale(dcol, ks * dec, hp)
            acc_k = acc_k + _head_scale(dcol, qd, hp)
            acc_v = acc_v + _head_scale(col, do[sl], hp)
        dk_scr[s:s + 1, :] = jnp.sum(acc_k, axis=0, keepdims=True)
        dv_scr[s:s + 1, :] = jnp.sum(acc_v, axis=0, keepdims=True)


def _hgrn2_fwd(proj, lb, g_out, nh, base, c):
    s = proj.shape[0]
    nch = s // c
    hp = _hg_heads(nh)
    wide = hp * HEAD

    def body(hq_ref, hf_ref, hi_ref, og_ref, lb_ref, g_ref, o_ref, y_ref, st_ref, st):
        @pl.when(pl.program_id(1) == 0)
        def _():
            st[...] = jnp.zeros_like(st)

        st_in = [st[h] for h in range(hp)]
        for h in range(hp):
            st_ref[h] = st_in[h]
        q, _, f, _ = _hg_gates(hq_ref[...], hf_ref[...], lb_ref[...])
        k = 1.0 - f
        v = hi_ref[...]
        b = _hg_cum(jnp.log(f), c)
        bl = b[c - 1:c, :]
        qe = (q * jnp.exp(b)).astype(BF16)
        o = _per_head(qe, hp, lambda x, h: _dot(x, st_in[h].astype(BF16), NT)) + _hg_intra_fwd(q, k, v, b, c, hp)
        kd = (k * jnp.exp(bl - b)).astype(BF16)
        vb = v.astype(BF16)
        keep = jnp.exp(bl)
        for h in range(hp):
            st[h] = st_in[h] * _head_cols(keep, h) + _dot(_head_cols(vb, h), _head_cols(kd, h), TN)
        o_ref[...] = o
        og = og_ref[...]
        gout = g_ref[...]
        y_ref[...] = (_per_head(o, hp, lambda x, h: _head_norm(x, gout)) * (og * _sig(og))).astype(BF16)

    col = lambda j: pl.BlockSpec((c, wide), lambda g, i: (i, (base + j * nh) // hp + g))
    row = pl.BlockSpec((1, wide), lambda g, i: (0, g))
    gs = pl.BlockSpec((1, HEAD), lambda g, i: (0, 0))
    o = pl.BlockSpec((c, wide), lambda g, i: (i, g))
    return pl.pallas_call(
        body, name="hgrn2_fwd", grid=(nh // hp, nch),
        in_specs=[col(0), col(1), col(2), col(3), row, gs],
        out_specs=[o, o, pl.BlockSpec((hp, None, HEAD, HEAD), lambda g, i: (g, i, 0, 0))],
        out_shape=[jax.ShapeDtypeStruct((s, nh * HEAD), F32), jax.ShapeDtypeStruct((s, nh * HEAD), BF16),
                   jax.ShapeDtypeStruct((nh, nch, HEAD, HEAD), F32)],
        scratch_shapes=[pltpu.VMEM((hp, HEAD, HEAD), F32)],
        compiler_params=_cparams("parallel", "arbitrary"),
    )(proj, proj, proj, proj, lb, g_out)


def _hgrn2_bwd(proj, lb, g_out, o_pre, states, dy, nh, base, c):
    s = proj.shape[0]
    nch = s // c
    hp = _hg_heads(nh)
    wide = hp * HEAD

    def fold(x):
        return jnp.sum(x.reshape(c // SUB, SUB, x.shape[1]), axis=0)

    def body(hq_ref, hf_ref, hi_ref, og_ref, lb_ref, g_ref, o_ref, st_ref, se_ref, dy_ref,
             dhq_ref, dhf_ref, dhi_ref, dog_ref, dg_ref, dlb_ref, dst, dq_scr, dk_scr, dv_scr):
        g, i = pl.program_id(0), pl.program_id(1)

        @pl.when(i == 0)
        def _():
            dst[...] = jnp.zeros_like(dst)
            dlb_ref[...] = jnp.zeros_like(dlb_ref)

        @pl.when((i == 0) & (g == 0))
        def _():
            dg_ref[...] = jnp.zeros_like(dg_ref)

        lbv = lb_ref[...]
        hq, hf = hq_ref[...], hf_ref[...]
        q, sq, f, sf = _hg_gates(hq, hf, lbv)
        k = 1.0 - f
        v = hi_ref[...]
        b = _hg_cum(jnp.log(f), c)
        bl = b[c - 1:c, :]
        eb = jnp.exp(b)
        ebl = jnp.exp(bl - b)

        o = o_ref[...]
        gout = g_ref[...]
        og = og_ref[...]
        sg = _sig(og)
        r = _per_head(o, hp, lambda x, h: jnp.broadcast_to(
            lax.rsqrt(jnp.mean(x * x, axis=-1, keepdims=True) + EPS), x.shape))
        gw = jnp.concatenate([gout] * hp, axis=1)
        n = o * r
        dyv = dy_ref[...]
        dn = dyv * (og * sg)
        dog_ref[...] = (dyv * n * gw * (sg * (1.0 + og * (1.0 - sg)))).astype(BF16)
        dnn = fold(dn * n)
        part = _head_cols(dnn, 0)
        for h in range(1, hp):
            part = part + _head_cols(dnn, h)
        dg_ref[...] += part
        a = dn * gw
        an = a * n
        do = r * (a - n * _per_head(an, hp, lambda x, h: jnp.broadcast_to(jnp.mean(x, axis=-1, keepdims=True), x.shape)))

        st_in = [st_ref[h].astype(BF16) for h in range(hp)]
        dstv = [dst[h] for h in range(hp)]
        dstb = [d.astype(BF16) for d in dstv]
        dob = do.astype(BF16)
        vb = v.astype(BF16)
        kdb = (k * ebl).astype(BF16)
        qeb = (q * eb).astype(BF16)
        _hg_intra_bwd(q, k, v, b, do, c, hp, dq_scr, dk_scr, dv_scr)
        dq = dq_scr[...] + eb * _per_head(dob, hp, lambda x, h: _dot(x, st_in[h], NN))
        dk = dk_scr[...] + ebl * _per_head(vb, hp, lambda x, h: _dot(x, dstb[h], NN))
        dv = dv_scr[...] + _per_head(kdb, hp, lambda x, h: _dot(x, dstb[h], NT))
        keep = jnp.exp(bl)
        for h in range(hp):
            dst[h] = dstv[h] * _head_cols(keep, h) + _dot(_head_cols(dob, h), _head_cols(qeb, h), TN)

        hi_, lo_ = _split(q * dq - k * dk)
        rev = _tri(c, "ge")
        later = jnp.concatenate([jnp.sum(dstv[h] * se_ref[h], axis=0, keepdims=True) for h in range(hp)], axis=1)
        dg = _dot(rev, hi_, NN) + _dot(rev, lo_, NN) + jnp.where(i > 0, later, 0.0)
        df = dg / f - dk
        dhq_ref[...] = (dq * (sq * (1.0 + hq * (1.0 - sq)))).astype(BF16)
        dhf_ref[...] = (df * (1.0 - lbv) * sf * (1.0 - sf)).astype(BF16)
        dhi_ref[...] = dv.astype(BF16)
        dlb_ref[...] += fold(df * (1.0 - sf))

    rv = lambda i: nch - 1 - i
    col = lambda j: pl.BlockSpec((c, wide), lambda g, i: (rv(i), (base + j * nh) // hp + g))
    row = pl.BlockSpec((1, wide), lambda g, i: (0, g))
    gs = pl.BlockSpec((1, HEAD), lambda g, i: (0, 0))
    o = pl.BlockSpec((c, wide), lambda g, i: (rv(i), g))
    st = pl.BlockSpec((hp, None, HEAD, HEAD), lambda g, i: (g, rv(i), 0, 0))
    se = pl.BlockSpec((hp, None, HEAD, HEAD), lambda g, i: (g, jnp.minimum(rv(i) + 1, nch - 1), 0, 0))
    sh = jax.ShapeDtypeStruct((s, nh * HEAD), BF16)
    return pl.pallas_call(
        body, name="hgrn2_bwd", grid=(nh // hp, nch),
        in_specs=[col(0), col(1), col(2), col(3), row, gs, o, st, se, o],
        out_specs=[o, o, o, o, pl.BlockSpec((SUB, HEAD), lambda g, i: (0, 0)), pl.BlockSpec((SUB, wide), lambda g, i: (0, g))],
        out_shape=[sh, sh, sh, sh, jax.ShapeDtypeStruct((SUB, HEAD), F32), jax.ShapeDtypeStruct((SUB, nh * HEAD), F32)],
        scratch_shapes=[pltpu.VMEM((hp, HEAD, HEAD), F32), pltpu.VMEM((c, wide), F32), pltpu.VMEM((c, wide), F32),
                        pltpu.VMEM((c, wide), F32)],
        compiler_params=_cparams("arbitrary", "arbitrary"),
    )(proj, proj, proj, proj, lb, g_out, o_pre, states, states, dy)


def _merge_tiles(s, d, gate_col):
    tr = _pick(s, (256, 128, 64, 32, 16, 8))
    tc = 128
    for cand in (512, 256):
        if d % cand == 0 and gate_col % cand == 0:
            tc = cand
            break
    return tr, tc


def _merge_fwd(proj, ya, yb, gate_col):
    s, d = ya.shape
    tr, tc = _merge_tiles(s, d, gate_col)
    ga0, gb0 = gate_col // tc, (gate_col + d) // tc

    def body(ga_ref, gb_ref, ya_ref, yb_ref, m_ref):
        m_ref[...] = (_sig(ga_ref[...]) * ya_ref[...] + _sig(gb_ref[...]) * yb_ref[...]).astype(BF16)

    o = pl.BlockSpec((tr, tc), lambda i, j: (i, j))
    return pl.pallas_call(
        body, name="merge_fwd", grid=(s // tr, d // tc),
        in_specs=[pl.BlockSpec((tr, tc), lambda i, j: (i, ga0 + j)), pl.BlockSpec((tr, tc), lambda i, j: (i, gb0 + j)), o, o],
        out_specs=o, out_shape=jax.ShapeDtypeStruct((s, d), BF16),
        compiler_params=_cparams("parallel", "parallel"),
    )(proj, proj, ya, yb)


def _merge_bwd(proj, ya, yb, dm, gate_col):
    s, d = ya.shape
    tr, tc = _merge_tiles(s, d, gate_col)
    ga0, gb0 = gate_col // tc, (gate_col + d) // tc

    def body(ga_ref, gb_ref, ya_ref, yb_ref, dm_ref, dya_ref, dyb_ref, dga_ref, dgb_ref):
        dmv = dm_ref[...]
        sa, sb = _sig(ga_ref[...]), _sig(gb_ref[...])
        dya_ref[...] = (dmv * sa).astype(BF16)
        dyb_ref[...] = (dmv * sb).astype(BF16)
        dga_ref[...] = (dmv * ya_ref[...] * sa * (1.0 - sa)).astype(BF16)
        dgb_ref[...] = (dmv * yb_ref[...] * sb * (1.0 - sb)).astype(BF16)

    o = pl.BlockSpec((tr, tc), lambda i, j: (i, j))
    sh = jax.ShapeDtypeStruct((s, d), BF16)
    return pl.pallas_call(
        body, name="merge_bwd", grid=(s // tr, d // tc),
        in_specs=[pl.BlockSpec((tr, tc), lambda i, j: (i, ga0 + j)), pl.BlockSpec((tr, tc), lambda i, j: (i, gb0 + j)), o, o, o],
        out_specs=[o, o, o, o], out_shape=[sh, sh, sh, sh],
        compiler_params=_cparams("parallel", "parallel"),
    )(proj, proj, ya, yb, dm)


CONV_ROWS = 512


def _conv_ext(ref, i, rows, s, before, after):
    parts = []
    if before:
        p = ref[pl.ds(pl.multiple_of(jnp.maximum(i * rows - before, 0), SUB), before), :]
        parts.append(jnp.where(i > 0, p, 0.0))
    parts.append(ref[pl.ds(pl.multiple_of(i * rows, SUB), rows), :])
    if after:
        nxt = ref[pl.ds(pl.multiple_of(jnp.minimum((i + 1) * rows, s - after), SUB), after), :]
        parts.append(jnp.where((i + 1) * rows < s, nxt, 0.0))
    return jnp.concatenate(parts, axis=0)


def _conv3(ext, w, bias):
    x1 = pltpu.roll(ext, 1, 0)
    x2 = pltpu.roll(ext, 2, 0)
    return bias + w[0:1, :] * x2 + w[1:2, :] * x1 + w[2:3, :] * ext, x1, x2


def _convffn_fwd(up, conv_w, conv_b, dff):
    s = up.shape[0]
    tc = HEAD
    nf = dff // tc
    rows = _pick(s, (CONV_ROWS, 256, 128, 64, 32, 16, 8))

    def body(ug_ref, uv_ref, wg_ref, wv_ref, bg_ref, bv_ref, a_ref):
        wg, wv, bg, bv = wg_ref[...], wv_ref[...], bg_ref[...], bv_ref[...]

        def step(i, _):
            g = _conv3(_conv_ext(ug_ref, i, rows, s, SUB, 0), wg, bg)[0][SUB:]
            v = _conv3(_conv_ext(uv_ref, i, rows, s, SUB, 0), wv, bv)[0][SUB:]
            a_ref[pl.ds(pl.multiple_of(i * rows, SUB), rows), :] = (g * _sig(g) * v).astype(BF16)
            return 0

        lax.fori_loop(0, s // rows, step, 0)

    cg = lambda r: pl.BlockSpec((r, tc), lambda j: (0, j))
    cv = lambda r: pl.BlockSpec((r, tc), lambda j: (0, nf + j))
    return pl.pallas_call(
        body, name="convffn_fwd", grid=(nf,),
        in_specs=[cg(s), cv(s), cg(3), cv(3), cg(1), cv(1)], out_specs=cg(s),
        out_shape=jax.ShapeDtypeStruct((s, dff), BF16),
        compiler_params=_cparams("parallel"),
    )(up, up, conv_w, conv_w, conv_b, conv_b)


def _convffn_bwd(up, conv_w, conv_b, dact, dff):
    s = up.shape[0]
    tc = HEAD
    nf = dff // tc
    rows = _pick(s, (CONV_ROWS, 256, 128, 64, 32, 16, 8))
    n_ext = rows + SUB

    def body(ug_ref, uv_ref, wg_ref, wv_ref, bg_ref, bv_ref, da_ref,
             dug_ref, duv_ref, dwg_ref, dwv_ref, dbg_ref, dbv_ref):
        wg, wv, bg, bv = wg_ref[...], wv_ref[...], bg_ref[...], bv_ref[...]

        def fold(x):
            return jnp.sum(x.reshape(rows // SUB, SUB, tc), axis=0)

        def one(ext, x1, x2, d_ext, w):
            d1 = pltpu.roll(d_ext, n_ext - 1, 0)[:rows]
            d2 = pltpu.roll(d_ext, n_ext - 2, 0)[:rows]
            dc = d_ext[:rows]
            du = w[2:3, :] * dc + w[1:2, :] * d1 + w[0:1, :] * d2
            sl = slice(SUB, SUB + rows)
            return du, (fold(dc * x2[sl]), fold(dc * x1[sl]), fold(dc * ext[sl]), fold(dc))

        def step(i, acc):
            eg = _conv_ext(ug_ref, i, rows, s, SUB, SUB)
            ev = _conv_ext(uv_ref, i, rows, s, SUB, SUB)
            g, g1, g2 = _conv3(eg, wg, bg)
            v, v1, v2 = _conv3(ev, wv, bv)
            g, v = g[SUB:], v[SUB:]
            da = _conv_ext(da_ref, i, rows, s, 0, SUB)
            sg = _sig(g)
            dg = da * v * (sg * (1.0 + g * (1.0 - sg)))
            dv = da * (g * sg)
            dug, pg = one(eg, g1, g2, dg, wg)
            duv, pv = one(ev, v1, v2, dv, wv)
            at = pl.ds(pl.multiple_of(i * rows, SUB), rows)
            dug_ref[at, :] = dug.astype(BF16)
            duv_ref[at, :] = duv.astype(BF16)
            return tuple(a + p for a, p in zip(acc, pg + pv))

        zero = jnp.zeros((SUB, tc), F32)
        acc = lax.fori_loop(0, s // rows, step, (zero,) * 8)
        red = [jnp.sum(a, axis=0, keepdims=True) for a in acc]
        for j in range(3):
            dwg_ref[j:j + 1, :] = red[j]
            dwv_ref[j:j + 1, :] = red[4 + j]
        dbg_ref[...] = red[3]
        dbv_ref[...] = red[7]

    cg = lambda r: pl.BlockSpec((r, tc), lambda j: (0, j))
    cv = lambda r: pl.BlockSpec((r, tc), lambda j: (0, nf + j))
    outs = pl.pallas_call(
        body, name="convffn_bwd", grid=(nf,),
        in_specs=[cg(s), cv(s), cg(3), cv(3), cg(1), cv(1), cg(s)],
        out_specs=[cg(s), cg(s), cg(3), cg(3), cg(1), cg(1)],
        out_shape=[jax.ShapeDtypeStruct((s, dff), BF16), jax.ShapeDtypeStruct((s, dff), BF16),
                   jax.ShapeDtypeStruct((3, dff), F32), jax.ShapeDtypeStruct((3, dff), F32),
                   jax.ShapeDtypeStruct((1, dff), F32), jax.ShapeDtypeStruct((1, dff), F32)],
        compiler_params=_cparams("parallel"),
    )(up, up, conv_w, conv_w, conv_b, conv_b, dact)
    return outs


def _loss_head(out, target):
    s, d = out.shape
    tr = _pick(s, (256, 128, 64, 32, 16, 8))

    def body(o_ref, t_ref, d_ref, l_ref):
        @pl.when(pl.program_id(0) == 0)
        def _():
            l_ref[...] = jnp.zeros_like(l_ref)

        err = o_ref[...] - t_ref[...]
        d_ref[...] = err * (1.0 / d)
        sq = jnp.sum((err * err).reshape(tr // SUB, SUB, d), axis=0)
        part = sq[:, 0:HEAD]
        for j in range(1, d // HEAD):
            part = part + sq[:, j * HEAD:(j + 1) * HEAD]
        l_ref[...] += part

    blk = pl.BlockSpec((tr, d), lambda i: (i, 0))
    return pl.pallas_call(
        body, name="loss_head", grid=(s // tr,), in_specs=[blk, blk],
        out_specs=[blk, pl.BlockSpec((SUB, HEAD), lambda i: (0, 0))],
        out_shape=[jax.ShapeDtypeStruct((s, d), F32), jax.ShapeDtypeStruct((SUB, HEAD), F32)],
        compiler_params=_cparams("arbitrary"),
    )(out, target)


def _sum_rows(name, parts):
    def body(p_ref, o_ref):
        o_ref[...] = jnp.sum(p_ref[...], axis=0, keepdims=True)

    return pl.pallas_call(body, name=name, out_shape=jax.ShapeDtypeStruct((1, parts.shape[1]), F32))(parts)


def _local_step(x, target, g_mix, g_q, g_k, lb_logits, g_hg_out, g_ffn, conv_w, conv_b, w_in, later_weights, grads_ready):
    s, d = x.shape
    nh = lb_logits.shape[1] // HEAD
    wid = nh * HEAD
    blk = _pick(s, (256, 128))
    chunk = _pick(s, (HG_CHUNK,))
    gate_col = 7 * wid

    u, u_t = _rmsnorm_fwd("rmsnorm_mix", x, g_mix)
    proj = _matmul("in_proj", u, w_in, "nn", F32)
    qn, kn, vb, kt, vt = _qk_norm_fwd(proj, g_q, g_k, nh, blk)
    y_a, carries = _sb_attn_fwd(qn, kn, vt, nh, blk)
    lb = _lower_bound(lb_logits)
    o_pre, y_b, states = _hgrn2_fwd(proj, lb, g_hg_out, nh, 3 * nh, chunk)
    later = later_weights(o_pre)
    p_a, p_b, w_up = later["p_a"], later["p_b"], later["w_up"]
    w_o = later["w_o"].reshape(1, d, d)
    dff = later["w_down"].shape[1] * N_CHIPS
    w_down = later["w_down"].reshape(1, dff, d)
    ya_p = _matmul("proj_a", y_a, p_a, "nn", F32)
    yb_p = _matmul("proj_b", y_b, p_b, "nn", F32)
    m = _merge_fwd(proj, ya_p, yb_p, gate_col)
    h = _matmul("out_proj", m, w_o, "nn", F32, add=x)
    u2, u2_t = _rmsnorm_fwd("rmsnorm_ffn", h, g_ffn)
    up = _matmul("up_proj", u2, w_up, "nn", F32)
    act = _convffn_fwd(up, conv_w, conv_b, dff)
    out = _matmul("down_proj", act, w_down, "nn", F32, add=h)
    dout, sq = _loss_head(out, target)

    dact = _matmul("d_act", dout, w_down, "nt", F32)
    g_w_down = _matmul_tn("g_w_down", act, dout, 1, BF16).reshape(N_CHIPS, dff // N_CHIPS, d)
    dup_g, dup_v, dcw_g, dcw_v, dcb_g, dcb_v = _convffn_bwd(up, conv_w, conv_b, dact, dff)
    dup = jnp.concatenate([dup_g, dup_v], axis=1)
    g_w_up = _matmul("g_w_up", u2_t, dup[None], "nn", BF16, out_shards=N_CHIPS)
    sent = grads_ready(("w_down", "w_up"), [g_w_down, g_w_up])
    du2 = _matmul("d_u2", dup, w_up, "nt", F32)
    dh, pg_ffn = _rmsnorm_bwd("rmsnorm_ffn_bwd", h, g_ffn + sent, du2, dout)
    dm = _matmul("d_m", dh, w_o, "nt", F32)
    g_w_o = _matmul_tn("g_w_o", m, dh, 1, BF16).reshape(N_CHIPS, d // N_CHIPS, d)
    dya_p, dyb_p, dga, dgb = _merge_bwd(proj, ya_p, yb_p, dm, gate_col)
    g_p_a = _matmul_tn("g_p_a", y_a, dya_p, N_CHIPS, BF16)
    g_p_b = _matmul_tn("g_p_b", y_b, dyb_p, N_CHIPS, BF16)
    sent = grads_ready(("w_o", "p_a", "p_b"), [g_w_o, g_p_a, g_p_b])
    dy_a = _matmul("d_y_a", dya_p, p_a, "nt", F32)
    dy_b = _matmul("d_y_b", dyb_p, p_b, "nt", F32)
    dhq, dhf, dhi, dog, pg_hg, p_lb = _hgrn2_bwd(proj, lb, g_hg_out + sent, o_pre, states, dy_b, nh, 3 * nh, chunk)
    dqn, dkn, dv = _sb_attn_bwd(qn, kn, kt, vb, dy_a, carries, nh, blk)
    dq, dk, pg_q, pg_k = _qk_norm_bwd(proj, g_q, g_k, dqn, dkn, nh)
    dproj = jnp.concatenate([dq, dk, dv.astype(BF16), dhq, dhf, dhi, dog, dga, dgb], axis=1)
    g_w_in = _matmul("g_w_in", u_t, dproj[None], "nn", BF16, out_shards=N_CHIPS)
    sent = grads_ready(("w_in",), [g_w_in])
    du = _matmul("d_u", dproj, w_in, "nt", F32)
    dx, pg_mix = _rmsnorm_bwd("rmsnorm_mix_bwd", x, g_mix + sent, du, dh)

    small = dict(
        g_mix=_sum_rows("sum_g_mix", pg_mix), g_q=_sum_rows("sum_g_q", pg_q), g_k=_sum_rows("sum_g_k", pg_k),
        lb=_sum_rows("sum_lb", p_lb), g_hg_out=_sum_rows("sum_g_hg", pg_hg), g_ffn=_sum_rows("sum_g_ffn", pg_ffn),
        conv_w=jnp.concatenate([dcw_g, dcw_v], axis=1), conv_b=jnp.concatenate([dcb_g, dcb_v], axis=1),
        sq=_sum_rows("sum_sq", sq),
    )
    return dx, small, lb


ANY = pl.BlockSpec(memory_space=pl.ANY)


def _place():
    x, y, c = lax.axis_index("x"), lax.axis_index("y"), lax.axis_index("c")
    chips = [(1 - x, y), (x, 1 - y), (1 - x, 1 - y)]
    return x, y, c, chips


def _remote(src, dst, send_sem, recv_sem, to):
    return pltpu.make_async_remote_copy(src_ref=src, dst_ref=dst, send_sem=send_sem, recv_sem=recv_sem,
                                        device_id=to, device_id_type=MESH)


def _cast_bf16(name, w):
    r, c = w.shape
    tr = _pick(r, (256, 128, 64, 32, 16))

    def body(w_ref, o_ref):
        o_ref[...] = w_ref[...].astype(BF16)

    return pl.pallas_call(
        body, name=name, grid=(r // tr,), in_specs=[pl.BlockSpec((tr, c), lambda i: (i, 0))],
        out_specs=pl.BlockSpec((tr, c), lambda i: (i, 0)), out_shape=jax.ShapeDtypeStruct((r, c), BF16),
        compiler_params=_cparams("parallel"),
    )(w)


def _gather_weights(shards):
    n = len(shards)

    def body(*refs):
        ins, outs = refs[:n], refs[n:2 * n]
        send, recv, local = refs[2 * n:]
        x, y, c, chips = _place()
        mine = 2 * x + y
        sends, owns = [], []
        for k in range(n):
            half = ins[k].shape[0] // 2
            rows = pl.ds(c * half, half)
            own = pltpu.make_async_copy(ins[k], outs[k].at[mine], local.at[k])
            own.start()
            owns.append(own)
            for j, (px, py) in enumerate(chips):
                cp = _remote(ins[k].at[rows], outs[k].at[mine, rows], send.at[k, j], recv.at[k, j], (px, py, c))
                cp.start()
                sends.append(cp)
        for k in range(n):
            half = ins[k].shape[0] // 2
            rows = pl.ds(c * half, half)
            for j, (px, py) in enumerate(chips):
                part = outs[k].at[2 * px + py, rows]
                _remote(part, part, send.at[k, j], recv.at[k, j], (px, py, c)).wait_recv()
                fw = _remote(part, part, send.at[k, 3 + j], recv.at[k, 3 + j], (x, y, 1 - c))
                fw.start()
                sends.append(fw)
        for k in range(n):
            half = ins[k].shape[0] // 2
            other = pl.ds((1 - c) * half, half)
            for j, (px, py) in enumerate(chips):
                part = outs[k].at[2 * px + py, other]
                _remote(part, part, send.at[k, 3 + j], recv.at[k, 3 + j], (x, y, 1 - c)).wait_recv()
        for cp in sends:
            cp.wait_send()
        for cp in owns:
            cp.wait()

    return pl.pallas_call(
        body, name="gather_weights", in_specs=[ANY] * n, out_specs=[ANY] * n,
        out_shape=[jax.ShapeDtypeStruct((N_CHIPS,) + w.shape, w.dtype) for w in shards],
        scratch_shapes=[pltpu.SemaphoreType.DMA((n, 6)), pltpu.SemaphoreType.DMA((n, 6)), pltpu.SemaphoreType.DMA((n,))],
    )(*shards)


def _to_sibling(name, srcs):
    n = len(srcs)

    def body(*refs):
        ins, outs = refs[:n], refs[n:2 * n]
        send, recv = refs[2 * n:]
        x, y, c, _ = _place()
        cps = []
        for k in range(n):
            cp = _remote(ins[k], outs[k], send.at[k], recv.at[k], (x, y, 1 - c))
            cp.start()
            cps.append(cp)
        for cp in cps:
            cp.wait_recv()
        for cp in cps:
            cp.wait_send()

    return pl.pallas_call(
        body, name=name, in_specs=[ANY] * n, out_specs=[ANY] * n,
        out_shape=[jax.ShapeDtypeStruct(a.shape, a.dtype) for a in srcs],
        scratch_shapes=[pltpu.SemaphoreType.DMA((n,)), pltpu.SemaphoreType.DMA((n,))],
    )(*srcs)


HBM = pl.BlockSpec(memory_space=pltpu.HBM)
SEM = pl.BlockSpec(memory_space=pltpu.SEMAPHORE)
SIDE = pltpu.SideEffectType.DATAFLOW_SIDE_EFFECTING
N_PEERS = 7


def _peer(r):
    x, y, c = lax.axis_index("x"), lax.axis_index("y"), lax.axis_index("c")
    return (1 - x if r & 4 else x), (1 - y if r & 2 else y), (1 - c if r & 1 else c)


def _partial_copy(src, land, send, recv, k, r):
    px, py, pc = _peer(r)
    half = src.shape[1] // 2
    sem = k * N_PEERS + r - 1
    return _remote(src.at[2 * px + py, pl.ds(pc * half, half)], land.at[r - 1], send.at[sem], recv.at[sem], (px, py, pc))


def _shard_copy(full, send, recv, k, r):
    x, y, c = lax.axis_index("x"), lax.axis_index("y"), lax.axis_index("c")
    half = full.shape[1] // 2
    part = full.at[2 * x + y, pl.ds(c * half, half)]
    sem = k * (N_PEERS - 1) + r - 2
    return _remote(part, part, send.at[sem], recv.at[sem], _peer(r))


def _start_copies(name, arrays, n_sems, copies):
    n = len(arrays)

    def body(*refs):
        send, recv, token = refs[n], refs[n + 1], refs[-1]
        for cp in copies(refs[:n], send, recv):
            cp.start()
        token[...] = jnp.zeros_like(token)

    sem = pltpu.SemaphoreType.DMA((n_sems,))
    outs = pl.pallas_call(
        body, name=name, in_specs=[HBM] * n,
        out_specs=[SEM, SEM] + [HBM] * n + [pl.BlockSpec(memory_space=pltpu.VMEM)],
        out_shape=[sem, sem] + [pltpu.HBM(a.shape, a.dtype) for a in arrays] + [jax.ShapeDtypeStruct((SUB, HEAD), F32)],
        input_output_aliases={i: 2 + i for i in range(n)},
        compiler_params=pltpu.CompilerParams(has_side_effects=SIDE),
    )(*[pltpu.with_memory_space_constraint(a, pltpu.HBM) for a in arrays])
    return outs[0], outs[1], list(outs[2:2 + n]), outs[-1]


def _wait_copies(name, send, recv, arrays, after, copies):
    n = len(arrays)

    def body(*refs):
        for cp in copies(refs[:n], refs[n], refs[n + 1]):
            cp.wait_send()
            cp.wait_recv()

    return list(pl.pallas_call(
        body, name=name, in_specs=[HBM] * n + [SEM, SEM, ANY], out_specs=[HBM] * n,
        out_shape=[pltpu.HBM(a.shape, a.dtype) for a in arrays],
        input_output_aliases={i: i for i in range(n)},
        compiler_params=pltpu.CompilerParams(has_side_effects=SIDE),
    )(*arrays, send, recv, after))


def _partial_copies(n):
    def copies(refs, send, recv):
        return [_partial_copy(refs[k], refs[n + k], send, recv, k, r) for k in range(n) for r in range(1, N_PEERS + 1)]
    return copies


def _shard_copies(n):
    def copies(refs, send, recv):
        return [_shard_copy(refs[k], send, recv, k, r) for k in range(n) for r in range(2, N_PEERS + 1)]
    return copies


def _cast_place(name, w, shard):
    r, c = w.shape
    tr = _pick(r, (256, 128, 64, 32, 16))

    def body(s_ref, w_ref, o_ref):
        o_ref[...] = w_ref[...].astype(BF16)

    return pl.pallas_call(
        body, name=name,
        grid_spec=pltpu.PrefetchScalarGridSpec(
            num_scalar_prefetch=1, grid=(r // tr,), in_specs=[pl.BlockSpec((tr, c), lambda i, sr: (i, 0))],
            out_specs=pl.BlockSpec((None, tr, c), lambda i, sr: (sr[0], i, 0))),
        out_shape=jax.ShapeDtypeStruct((N_CHIPS, r, c), BF16),
        compiler_params=_cparams("parallel"),
    )(shard, w)


def _sum_peers(name, g, land, shard, core):
    _, r, cols = g.shape
    half = r // 2
    tr = _pick(half, (128, 64, 32, 16))
    nt = half // tr

    def body(s_ref, c_ref, g_ref, l_ref, o_ref):
        acc = g_ref[...].astype(F32)
        for j in range(N_PEERS):
            acc = acc + l_ref[j].astype(F32)
        o_ref[...] = acc

    return pl.pallas_call(
        body, name=name,
        grid_spec=pltpu.PrefetchScalarGridSpec(
            num_scalar_prefetch=2, grid=(nt,),
            in_specs=[pl.BlockSpec((None, tr, cols), lambda i, sr, cr: (sr[0], cr[0] * nt + i, 0)),
                      pl.BlockSpec((N_PEERS, tr, cols), lambda i, sr, cr: (0, i, 0))],
            out_specs=pl.BlockSpec((tr, cols), lambda i, sr, cr: (i, 0))),
        out_shape=jax.ShapeDtypeStruct((half, cols), F32),
        compiler_params=_cparams("parallel"),
    )(shard, core, g, land)


def _join_halves(mine, got, c):
    half, cols = mine.shape
    tr = _pick(half, (256, 128, 64, 32, 16, 8))
    nt = half // tr

    def body(c_ref, a_ref, b_ref, o_ref):
        i = pl.program_id(0)
        own = (i // nt) == c_ref[0]

        @pl.when(own)
        def _():
            o_ref[...] = a_ref[...]

        @pl.when(jnp.logical_not(own))
        def _():
            o_ref[...] = b_ref[...]

    blk = pl.BlockSpec((tr, cols), lambda i, cr: (i % nt, 0))
    return pl.pallas_call(
        body, name="join_halves",
        grid_spec=pltpu.PrefetchScalarGridSpec(num_scalar_prefetch=1, grid=(2 * nt,), in_specs=[blk, blk],
                                               out_specs=pl.BlockSpec((tr, cols), lambda i, cr: (i, 0))),
        out_shape=jax.ShapeDtypeStruct((2 * half, cols), F32),
        compiler_params=_cparams("parallel"),
    )(c, mine, got)


def _all_gather_rows(name, row):
    p = row.shape[1]

    def body(in_ref, out_ref, send, recv, local):
        x, y, c, _ = _place()
        me = 4 * x + 2 * y + c
        own = pltpu.make_async_copy(in_ref, out_ref.at[me], local)
        own.start()
        cps = []
        for k in range(1, 8):
            px, py, pc = x ^ (k >> 2), y ^ ((k >> 1) & 1), c ^ (k & 1)
            cp = _remote(in_ref, out_ref.at[me], send.at[k - 1], recv.at[k - 1], (px, py, pc))
            cp.start()
            cps.append(cp)
        for cp in cps:
            cp.wait_recv()
        for cp in cps:
            cp.wait_send()
        own.wait()

    return pl.pallas_call(
        body, name=name, in_specs=[ANY], out_specs=ANY,
        out_shape=jax.ShapeDtypeStruct((8, 1, p), F32),
        scratch_shapes=[pltpu.SemaphoreType.DMA((7,)), pltpu.SemaphoreType.DMA((7,)), pltpu.SemaphoreType.DMA],
    )(row)


def _sum_devices(rows):
    def body(r_ref, o_ref):
        acc = r_ref[0]
        for k in range(1, 8):
            acc = acc + r_ref[k]
        o_ref[...] = acc

    return pl.pallas_call(body, name="sum_devices", out_shape=jax.ShapeDtypeStruct(rows.shape[1:], F32))(rows)


def _adamw(name, w, g, m, v):
    r, c = w.shape
    tr = _pick(r, (128, 64, 32, 16, 8))
    bc1 = 1.0 - ADAM_B1 ** ADAM_STEP
    bc2 = 1.0 - ADAM_B2 ** ADAM_STEP

    def body(w_ref, g_ref, m_ref, v_ref, d_ref, nm_ref, nv_ref):
        gv = g_ref[...]
        nm = ADAM_B1 * m_ref[...] + (1.0 - ADAM_B1) * gv
        nv = ADAM_B2 * v_ref[...] + (1.0 - ADAM_B2) * (gv * gv)
        d_ref[...] = -ADAM_LR * ((nm / bc1) / (jnp.sqrt(nv / bc2) + ADAM_EPS) + ADAM_WD * w_ref[...])
        nm_ref[...] = nm
        nv_ref[...] = nv

    blk = pl.BlockSpec((tr, c), lambda i: (i, 0))
    sh = jax.ShapeDtypeStruct((r, c), F32)
    return pl.pallas_call(
        body, name=name, grid=(r // tr,), in_specs=[blk] * 4, out_specs=[blk] * 3, out_shape=[sh] * 3,
        compiler_params=_cparams("parallel"),
    )(w, g, m, v)


def _lb_logits_grad(dlb, lb):
    def body(d_ref, lb_ref, o_ref):
        lbv = lb_ref[...]
        t = d_ref[...] * lbv * (1.0 - lbv)
        o_ref[0:1, :] = t
        o_ref[1:2, :] = -t

    return pl.pallas_call(body, name="lb_logits_grad", out_shape=jax.ShapeDtypeStruct((2, dlb.shape[1]), F32))(dlb, lb)


BIG = ("w_in", "p_a", "p_b", "w_o", "w_up", "w_down")
SMALL = ("g_mix", "g_q", "g_k", "lb_logits", "g_hg_out", "g_ffn", "conv_w", "conv_b")
ORDER = ("g_mix", "w_in", "g_q", "g_k", "lb_logits", "g_hg_out", "p_a", "p_b", "w_o", "g_ffn", "w_up", "conv_w", "conv_b", "w_down")


def kernel(x, g_mix, w_in, g_q, g_k, lb_logits, g_hg_out, p_a, p_b, w_o, g_ffn, w_up, conv_w, conv_b, w_down, loss_target, m_g_mix, m_w_in, m_g_q, m_g_k, m_lb_logits, m_g_hg_out, m_p_a, m_p_b, m_w_o, m_g_ffn, m_w_up, m_conv_w, m_conv_b, m_w_down, v_g_mix, v_w_in, v_g_q, v_g_k, v_lb_logits, v_g_hg_out, v_p_a, v_p_b, v_w_o, v_g_ffn, v_w_up, v_conv_w, v_conv_b, v_w_down):
    assert lb_logits.shape[0] == 2, "the lower bound is the first row of a two-row softmax"
    w = dict(g_mix=g_mix, w_in=w_in[0], g_q=g_q, g_k=g_k, lb_logits=lb_logits, g_hg_out=g_hg_out, p_a=p_a[0], p_b=p_b[0],
             w_o=w_o[0], g_ffn=g_ffn, w_up=w_up[0], conv_w=conv_w[0], conv_b=conv_b, w_down=w_down[0])
    mom = dict(g_mix=m_g_mix, w_in=m_w_in[0], g_q=m_g_q, g_k=m_g_k, lb_logits=m_lb_logits, g_hg_out=m_g_hg_out, p_a=m_p_a[0],
               p_b=m_p_b[0], w_o=m_w_o[0], g_ffn=m_g_ffn, w_up=m_w_up[0], conv_w=m_conv_w[0], conv_b=m_conv_b, w_down=m_w_down[0])
    var = dict(g_mix=v_g_mix, w_in=v_w_in[0], g_q=v_g_q, g_k=v_g_k, lb_logits=v_lb_logits, g_hg_out=v_g_hg_out, p_a=v_p_a[0],
               p_b=v_p_b[0], w_o=v_w_o[0], g_ffn=v_g_ffn, w_up=v_w_up[0], conv_w=v_conv_w[0], conv_b=v_conv_b, w_down=v_w_down[0])
    d = x.shape[2]
    cx, cy, cc = lax.axis_index("x"), lax.axis_index("y"), lax.axis_index("c")
    shard = (2 * cx + cy).astype(jnp.int32).reshape(1)
    core = cc.astype(jnp.int32).reshape(1)

    w_in_full = _gather_weights([_cast_bf16("cast_w_in", w["w_in"])])[0]
    later = ("p_a", "p_b", "w_o", "w_up", "w_down")
    n_later = len(later)
    fulls = [_cast_place("cast_" + n, w[n], shard) for n in later]
    w_send, w_recv, fulls, w_token = _start_copies("weights_start", fulls + [w_in_full], n_later * (N_PEERS - 1),
                                                   _shard_copies(n_later))
    w_in_full = fulls.pop()

    def later_weights(after):
        return dict(zip(later, _wait_copies("weights_wait", w_send, w_recv, fulls, after, _shard_copies(n_later))))

    cw = conv_w.shape[2]
    rows = _all_gather_rows("gather_conv_w", w["conv_w"].reshape(1, 3 * cw))
    conv_full = jnp.concatenate([rows[2 * s, 0].reshape(3, cw) for s in range(N_CHIPS)], axis=1)

    pending = []

    def grads_ready(names, gs):
        n = len(gs)
        lands = [lax.empty((N_PEERS, g.shape[1] // 2, g.shape[2]), BF16) for g in gs]
        send, recv, arrays, token = _start_copies("partials_start_" + names[0], list(gs) + lands, n * N_PEERS, _partial_copies(n))
        pending.append((names, send, recv, arrays))
        return token[0:1, 0:1]

    dx, small, lb = _local_step(x[0], loss_target[0], g_mix + w_token[0:1, 0:1], g_q, g_k, lb_logits, g_hg_out, g_ffn,
                                conv_full, conv_b, w_in_full, later_weights, grads_ready)

    mine = {}
    after = dx
    for names, send, recv, arrays in pending:
        n = len(names)
        arrays = _wait_copies("partials_wait_" + names[0], send, recv, arrays, after, _partial_copies(n))
        for k, name in enumerate(names):
            mine[name] = _sum_peers("sum_" + name, arrays[k], arrays[n + k], shard, core)
        after = mine[names[-1]]
    theirs = _to_sibling("reduced_to_sibling", [mine[n] for n in BIG])
    grads = {n: _join_halves(mine[n], b, core) for n, b in zip(BIG, theirs)}

    names = ("g_mix", "g_q", "g_k", "lb", "g_hg_out", "g_ffn", "conv_b", "sq")
    packed = jnp.concatenate([small[n] for n in names] + [small["conv_w"].reshape(1, -1)], axis=1)
    total = _sum_devices(_all_gather_rows("gather_small_grads", packed))
    off = 0
    red = {}
    for n in names:
        ln = small[n].shape[1]
        red[n] = total[:, off:off + ln]
        off += ln
    conv_all = total[:, off:].reshape(3, -1)
    loss = 0.5 * jnp.sum(red["sq"]) / d
    grads["conv_w"] = lax.dynamic_slice_in_dim(conv_all, (2 * cx + cy) * cw, cw, axis=1)
    grads["lb_logits"] = _lb_logits_grad(red["lb"], lb)
    for n in ("g_mix", "g_q", "g_k", "g_hg_out", "g_ffn", "conv_b"):
        grads[n] = red[n]

    delta, new_m, new_v = {}, {}, {}
    for n in ORDER:
        delta[n], new_m[n], new_v[n] = _adamw("adamw_" + n, w[n], grads[n], mom[n], var[n])

    def shaped(a, like):
        return a.reshape(like.shape)

    ref_w = dict(g_mix=g_mix, w_in=w_in, g_q=g_q, g_k=g_k, lb_logits=lb_logits, g_hg_out=g_hg_out, p_a=p_a, p_b=p_b, w_o=w_o,
                 g_ffn=g_ffn, w_up=w_up, conv_w=conv_w, conv_b=conv_b, w_down=w_down)
    outs = [loss, dx[None]]
    for group in (grads, delta, new_m, new_v):
        outs += [shaped(group[n], ref_w[n]) for n in ORDER]
    return tuple(outs)
```

```python
import functools

import jax
import jax.numpy as jnp
from jax import lax
from jax.experimental import pallas as pl
from jax.experimental.pallas import tpu as pltpu

F32 = jnp.float32
BF16 = jnp.bfloat16
HEAD = 128
EPS = 1e-6
N_CHIPS = 4
HG_CHUNK = 32
SUB = 8
ADAM_LR, ADAM_B1, ADAM_B2, ADAM_EPS, ADAM_WD, ADAM_STEP = 0.001, 0.9, 0.999, 1e-08, 0.01, 10
VMEM_LIMIT = 56 * 1024 * 1024
MESH = pl.DeviceIdType.MESH

NN = (((1,), (0,)), ((), ()))
NT = (((1,), (1,)), ((), ()))
TN = (((0,), (0,)), ((), ()))


def _cparams(*sem):
    return pltpu.CompilerParams(dimension_semantics=sem if sem else None, vmem_limit_bytes=VMEM_LIMIT)


def _pick(n, cands):
    for c in cands:
        if c <= n and n % c == 0:
            return c
    return n


def _sig(x):
    return 0.5 * jnp.tanh(0.5 * x) + 0.5


def _dot(a, b, dims):
    return lax.dot_general(a, b, dims, preferred_element_type=F32)


def _split(x):
    hi = x.astype(BF16)
    lo = (x - hi.astype(F32)).astype(BF16)
    return hi, lo


def _tri(n, kind):
    r = lax.broadcasted_iota(jnp.int32, (n, n), 0)
    c = lax.broadcasted_iota(jnp.int32, (n, n), 1)
    m = {"ge": c >= r, "gt": c > r, "le": c <= r, "lt": c < r}[kind]
    return jnp.where(m, 1.0, 0.0).astype(BF16)


TILE_M = (1024, 512, 256, 128)
TILE_N = (1408, 1024, 512, 256, 128)
TILE_K = (2816, 2048, 1408, 1024, 512, 256, 128)


def _matmul(name, a, b, mode, out_dtype, add=None, out_shards=None):
    if mode == "nn":
        m, k = a.shape
        g, _, ns = b.shape
        n = g * ns
        ns_out = n // out_shards if out_shards else ns
        tm, tn, tk = _pick(m, TILE_M), _pick(min(ns, ns_out), TILE_N), _pick(k, TILE_K)
        nps = ns // tn
        grid = (m // tm, n // tn, k // tk)
        a_spec = pl.BlockSpec((tm, tk), lambda i, j, kk: (i, kk))
        b_spec = pl.BlockSpec((None, tk, tn), lambda i, j, kk: (j // nps, kk, j % nps))
        if out_shards:
            npo = ns_out // tn
            o_spec = pl.BlockSpec((None, tm, tn), lambda i, j, kk: (j // npo, i, j % npo))
            o_shape = jax.ShapeDtypeStruct((out_shards, m, ns_out), out_dtype)
        else:
            o_spec = pl.BlockSpec((tm, tn), lambda i, j, kk: (i, j))
            o_shape = jax.ShapeDtypeStruct((m, n), out_dtype)
        dims = NN
    elif mode == "nt":
        m, k = a.shape
        g, n, ks = b.shape
        tm, tn, tk = _pick(m, TILE_M), _pick(n, TILE_N), _pick(ks, TILE_K)
        kps = ks // tk
        grid = (m // tm, n // tn, k // tk)
        a_spec = pl.BlockSpec((tm, tk), lambda i, j, kk: (i, kk))
        b_spec = pl.BlockSpec((None, tn, tk), lambda i, j, kk: (kk // kps, j, kk % kps))
        o_spec = pl.BlockSpec((tm, tn), lambda i, j, kk: (i, j))
        o_shape = jax.ShapeDtypeStruct((m, n), out_dtype)
        dims = NT
    else:
        raise ValueError(mode)
    nk = grid[2]

    def body(*refs):
        a_ref, b_ref = refs[0], refs[1]
        add_ref = refs[2] if add is not None else None
        o_ref = refs[2 + (add is not None)]

        def finish(r):
            if add is not None:
                r = r + add_ref[...]
            o_ref[...] = r.astype(o_ref.dtype)

        part = _dot(a_ref[...].astype(BF16), b_ref[...].astype(BF16), dims)
        if nk == 1:
            finish(part)
            return
        acc = refs[-1]
        kk = pl.program_id(2)

        @pl.when(kk == 0)
        def _():
            acc[...] = part

        @pl.when(kk > 0)
        def _():
            acc[...] += part

        @pl.when(kk == nk - 1)
        def _():
            finish(acc[...])

    in_specs = [a_spec, b_spec]
    args = [a, b]
    if add is not None:
        in_specs.append(o_spec)
        args.append(add)
    return pl.pallas_call(
        body, name=name, grid=grid, in_specs=in_specs, out_specs=o_spec, out_shape=o_shape,
        scratch_shapes=[pltpu.VMEM((tm, tn), F32)] if nk > 1 else [],
        compiler_params=_cparams("parallel", "parallel", "arbitrary"),
    )(*args)


def _matmul_tn(name, a, b, g, out_dtype):
    k, m = a.shape
    _, n = b.shape
    ns = n // g
    tm, tn, tk = _pick(m, (2048, 1408) + TILE_M), _pick(ns, TILE_N), _pick(k, (1024, 512, 256, 128))
    nps = ns // tn
    nk = k // tk

    def body(a_ref, b_ref, o_ref, acc):
        kk = pl.program_id(2)
        part = _dot(a_ref[...].astype(BF16), b_ref[...].astype(BF16), TN)

        @pl.when(kk == 0)
        def _():
            acc[...] = part

        @pl.when(kk > 0)
        def _():
            acc[...] += part

        @pl.when(kk == nk - 1)
        def _():
            o_ref[...] = acc[...].astype(o_ref.dtype)

    return pl.pallas_call(
        body, name=name, grid=(m // tm, n // tn, nk),
        in_specs=[pl.BlockSpec((tk, tm), lambda i, j, kk: (kk, i)), pl.BlockSpec((tk, tn), lambda i, j, kk: (kk, j))],
        out_specs=pl.BlockSpec((None, tm, tn), lambda i, j, kk: (j // nps, i, j % nps)),
        out_shape=jax.ShapeDtypeStruct((g, m, ns), out_dtype),
        scratch_shapes=[pltpu.VMEM((tm, tn), F32)],
        compiler_params=_cparams("parallel", "parallel", "arbitrary"),
    )(a, b)


def _rmsnorm_fwd(name, x, g):
    s, d = x.shape
    tr = _pick(s, (256, 128))

    def body(x_ref, g_ref, u_ref, ut_ref):
        xv = x_ref[...]
        r = lax.rsqrt(jnp.mean(xv * xv, axis=-1, keepdims=True) + EPS)
        u = xv * r * g_ref[...]
        u_ref[...] = u.astype(BF16)
        ut_ref[...] = u.T.astype(BF16)

    return pl.pallas_call(
        body, name=name, grid=(s // tr,),
        in_specs=[pl.BlockSpec((tr, d), lambda i: (i, 0)), pl.BlockSpec((1, d), lambda i: (0, 0))],
        out_specs=[pl.BlockSpec((tr, d), lambda i: (i, 0)), pl.BlockSpec((d, tr), lambda i: (0, i))],
        out_shape=[jax.ShapeDtypeStruct((s, d), BF16), jax.ShapeDtypeStruct((d, s), BF16)],
        compiler_params=_cparams("parallel"),
    )(x, g)


def _rmsnorm_bwd(name, x, g, du, extra):
    s, d = x.shape
    tr = _pick(s, (256, 128, 64, 32, 16, 8))

    def body(x_ref, g_ref, du_ref, e_ref, dx_ref, dg_ref):
        i = pl.program_id(0)
        xv = x_ref[...]
        r = lax.rsqrt(jnp.mean(xv * xv, axis=-1, keepdims=True) + EPS)
        n = xv * r
        dy = du_ref[...]
        a = dy * g_ref[...]
        dx = r * (a - n * jnp.mean(a * n, axis=-1, keepdims=True))
        dx_ref[...] = e_ref[...] + dx

        @pl.when(i == 0)
        def _():
            dg_ref[...] = jnp.zeros_like(dg_ref)

        dg_ref[...] += jnp.sum((dy * n).reshape(tr // SUB, SUB, d), axis=0)

    return pl.pallas_call(
        body, name=name, grid=(s // tr,),
        in_specs=[pl.BlockSpec((tr, d), lambda i: (i, 0)), pl.BlockSpec((1, d), lambda i: (0, 0)),
                  pl.BlockSpec((tr, d), lambda i: (i, 0)), pl.BlockSpec((tr, d), lambda i: (i, 0))],
        out_specs=[pl.BlockSpec((tr, d), lambda i: (i, 0)), pl.BlockSpec((SUB, d), lambda i: (0, 0))],
        out_shape=[jax.ShapeDtypeStruct((s, d), F32), jax.ShapeDtypeStruct((SUB, d), F32)],
        compiler_params=_cparams("arbitrary"),
    )(x, g, du, extra)


def _head_norm(x, g):
    r = lax.rsqrt(jnp.mean(x * x, axis=-1, keepdims=True) + EPS)
    return x * r * g


def _qk_norm_fwd(proj, g_q, g_k, nh, blk):
    s = proj.shape[0]

    def body(q_ref, k_ref, v_ref, gq_ref, gk_ref, qn_ref, kn_ref, vb_ref, kt_ref, vt_ref):
        qn_ref[...] = _head_norm(q_ref[...], gq_ref[...]).astype(BF16)
        kn = _head_norm(k_ref[...], gk_ref[...])
        kn_ref[...] = kn.astype(BF16)
        kt_ref[...] = kn.T.astype(BF16)
        v = v_ref[...]
        vb_ref[...] = v.astype(BF16)
        vt_ref[...] = v.T.astype(BF16)

    col = lambda base: pl.BlockSpec((blk, HEAD), lambda i, h: (i, base + h))
    gs = pl.BlockSpec((1, HEAD), lambda i, h: (0, 0))
    o = pl.BlockSpec((blk, HEAD), lambda i, h: (i, h))
    t = pl.BlockSpec((None, HEAD, blk), lambda i, h: (i, h, 0))
    sh = jax.ShapeDtypeStruct((s, nh * HEAD), BF16)
    tsh = jax.ShapeDtypeStruct((s // blk, nh * HEAD, blk), BF16)
    return pl.pallas_call(
        body, name="qk_norm_fwd", grid=(s // blk, nh),
        in_specs=[col(0), col(nh), col(2 * nh), gs, gs], out_specs=[o, o, o, t, t], out_shape=[sh, sh, sh, tsh, tsh],
        compiler_params=_cparams("parallel", "parallel"),
    )(proj, proj, proj, g_q, g_k)


def _qk_norm_bwd(proj, g_q, g_k, dqn, dkn, nh):
    s = proj.shape[0]
    tr = _pick(s, (512, 256, 128, 64, 32, 16, 8))

    def one(x, g, dy):
        r = lax.rsqrt(jnp.mean(x * x, axis=-1, keepdims=True) + EPS)
        n = x * r
        a = dy * g
        dx = r * (a - n * jnp.mean(a * n, axis=-1, keepdims=True))
        return dx, jnp.sum((dy * n).reshape(tr // SUB, SUB, HEAD), axis=0)

    def body(q_ref, k_ref, gq_ref, gk_ref, dqn_ref, dkn_ref, dq_ref, dk_ref, dgq_ref, dgk_ref):
        first = (pl.program_id(0) == 0) & (pl.program_id(1) == 0)

        @pl.when(first)
        def _():
            dgq_ref[...] = jnp.zeros_like(dgq_ref)
            dgk_ref[...] = jnp.zeros_like(dgk_ref)

        dq, pq = one(q_ref[...], gq_ref[...], dqn_ref[...])
        dk, pk = one(k_ref[...], gk_ref[...], dkn_ref[...])
        dq_ref[...] = dq.astype(BF16)
        dk_ref[...] = dk.astype(BF16)
        dgq_ref[...] += pq
        dgk_ref[...] += pk

    col = lambda base: pl.BlockSpec((tr, HEAD), lambda i, h: (i, base + h))
    gs = pl.BlockSpec((1, HEAD), lambda i, h: (0, 0))
    o = pl.BlockSpec((tr, HEAD), lambda i, h: (i, h))
    part = pl.BlockSpec((SUB, HEAD), lambda i, h: (0, 0))
    sh = jax.ShapeDtypeStruct((s, nh * HEAD), BF16)
    psh = jax.ShapeDtypeStruct((SUB, HEAD), F32)
    return pl.pallas_call(
        body, name="qk_norm_bwd", grid=(s // tr, nh),
        in_specs=[col(0), col(nh), gs, gs, o, o], out_specs=[o, o, part, part], out_shape=[sh, sh, psh, psh],
        compiler_params=_cparams("arbitrary", "arbitrary"),
    )(proj, proj, g_q, g_k, dqn, dkn)


def _sb_consts(blk, hp):
    upper = _tri(blk, "ge")
    row = lax.broadcasted_iota(jnp.int32, (blk, hp * blk), 0)
    col = lax.broadcasted_iota(jnp.int32, (blk, hp * blk), 1)
    strict = row < col
    for h in range(1, hp):
        strict = strict & ((col < h * blk) | (row < col - h * blk))
    return upper, strict


def _sb_log_keep(zt, strict):
    l = jnp.minimum(-zt, 0.0) - jnp.log(1.0 + jnp.exp(-jnp.abs(zt)))
    return l if strict is None else jnp.where(strict, l, 0.0)


SB_GROUP = 4
SB_GROUP_BWD = 4


def _sb_heads(nh):
    return 2 if nh % 2 == 0 else 1


def _sb_attn_fwd(qn, kn, vt, nh, blk):
    s = qn.shape[0]
    nb = s // blk
    scale = HEAD ** -0.5
    hp = 4 if nh % 4 == 0 else _sb_heads(nh)

    def body(q_ref, k_ref, vt_ref, y_ref, c_ref):
        qi = pl.program_id(1)
        suffix, strict = _sb_consts(blk, hp)
        qs = [q_ref[:, h * HEAD:(h + 1) * HEAD] for h in range(hp)]

        def logits(kb_i):
            off = pl.multiple_of(kb_i * blk, blk)
            return jnp.concatenate(
                [_dot(k_ref[pl.ds(off, blk), h * HEAD:(h + 1) * HEAD], qs[h], NT) for h in range(hp)], axis=1) * scale

        def sums(zt, mask):
            return _dot(suffix, _sb_log_keep(zt, mask).astype(BF16), NN)

        def weights(kb_i, zt, cum, cr, mask):
            for h in range(hp):
                c_ref[h, kb_i] = cr[:, h * blk:(h + 1) * blk]
            wt = jnp.exp(zt + cum + cr)
            if mask is not None:
                wt = jnp.where(mask, wt, 0.0)
            return wt.astype(BF16), cr + cum[0:1, :]

        def add_values(kb_i, wt, accs):
            return tuple(
                accs[h] + _dot(vt_ref[kb_i, h * HEAD:(h + 1) * HEAD, :], wt[:, h * blk:(h + 1) * blk], NN)
                for h in range(hp))

        def group(kbs, masks, accs, cr):
            zts = [logits(k) for k in kbs]
            cums = [sums(zt, m) for zt, m in zip(zts, masks)]
            for k, zt, cum, m in zip(kbs, zts, cums, masks):
                wt, cr = weights(k, zt, cum, cr, m)
                accs = add_values(k, wt, accs)
            return accs, cr

        accs = tuple(jnp.zeros((HEAD, blk), F32) for _ in range(hp))
        accs, cr = group([qi], [strict], accs, jnp.zeros((1, hp * blk), F32))
        n_groups = qi // SB_GROUP

        def many(g, st):
            top = qi - 1 - g * SB_GROUP
            return group([top - j for j in range(SB_GROUP)], [None] * SB_GROUP, *st)

        def one(r, st):
            return group([qi - 1 - n_groups * SB_GROUP - r], [None], *st)

        st = lax.fori_loop(0, n_groups, many, (accs, cr))
        accs, _ = lax.fori_loop(0, qi - n_groups * SB_GROUP, one, st)
        for h in range(hp):
            y_ref[:, h * HEAD:(h + 1) * HEAD] = accs[h].T.astype(y_ref.dtype)

    qs_ = pl.BlockSpec((blk, hp * HEAD), lambda h, i: (i, h))
    full = pl.BlockSpec((s, hp * HEAD), lambda h, i: (0, h), pipeline_mode=pl.Buffered(1))
    return pl.pallas_call(
        body, name="sb_attn_fwd", grid=(nh // hp, nb),
        in_specs=[qs_, full, pl.BlockSpec((nb, hp * HEAD, blk), lambda h, i: (0, h, 0), pipeline_mode=pl.Buffered(1))],
        out_specs=[qs_, pl.BlockSpec((hp, nb, 1, blk), lambda h, i: (h, 0, 0, i))],
        out_shape=[jax.ShapeDtypeStruct((s, nh * HEAD), BF16), jax.ShapeDtypeStruct((nh, nb, 1, s), F32)],
        compiler_params=_cparams("parallel", "arbitrary"),
    )(qn, kn, vt)


def _sb_attn_bwd(qn, kn, kt, vb, dy, carries, nh, blk):
    s = qn.shape[0]
    nb = s // blk
    scale = HEAD ** -0.5
    hp = _sb_heads(nh)

    def body(q_ref, k_ref, kt_ref, v_ref, dy_ref, c_ref, dq_ref, dk_ref, dv_ref):
        qi = pl.program_id(1)

        @pl.when(qi == 0)
        def _():
            dk_ref[...] = jnp.zeros_like(dk_ref)
            dv_ref[...] = jnp.zeros_like(dv_ref)

        suffix, strict = _sb_consts(blk, hp)
        prefix = _tri(blk, "lt")
        qs = [q_ref[:, h * HEAD:(h + 1) * HEAD] for h in range(hp)]
        dos = [dy_ref[:, h * HEAD:(h + 1) * HEAD].astype(BF16) for h in range(hp)]

        def logits(kb_i):
            off = pl.multiple_of(kb_i * blk, blk)
            return jnp.concatenate(
                [_dot(k_ref[pl.ds(off, blk), h * HEAD:(h + 1) * HEAD], qs[h], NT) for h in range(hp)], axis=1) * scale

        def group(kbs, masks, dqs, ec):
            rows = [pl.ds(pl.multiple_of(k * blk, blk), blk) for k in kbs]
            zts = [logits(k) for k in kbs]
            dws = [jnp.concatenate([_dot(v_ref[r, h * HEAD:(h + 1) * HEAD], dos[h], NT) for h in range(hp)], axis=1)
                   for r in rows]
            ls = [_sb_log_keep(zt, m) for zt, m in zip(zts, masks)]
            cums = [_dot(suffix, l.astype(BF16), NN) for l in ls]
            wts, ets, befores = [], [], []
            for k, zt, cum, dw, m in zip(kbs, zts, cums, dws, masks):
                cr = jnp.concatenate([c_ref[h, k] for h in range(hp)], axis=1)
                wt = jnp.exp(zt + cum + cr)
                if m is not None:
                    wt = jnp.where(m, wt, 0.0)
                et = wt * dw
                befores.append(_dot(prefix, et.astype(BF16), NN) + ec)
                ec = ec + jnp.sum(et, axis=0, keepdims=True)
                wts.append(wt.astype(BF16))
                ets.append(et)
            for k, r, l, et, before, wtb in zip(kbs, rows, ls, ets, befores, wts):
                dzt = ((jnp.exp(l) * (et + before) - before) * scale).astype(BF16)
                out = []
                for h in range(hp):
                    cols, part = slice(h * HEAD, (h + 1) * HEAD), slice(h * blk, (h + 1) * blk)
                    dv_ref[r, cols] += _dot(wtb[:, part], dos[h], NN)
                    dk_ref[r, cols] += _dot(dzt[:, part], qs[h], NN)
                    out.append(dqs[h] + _dot(kt_ref[k, cols, :], dzt[:, part], NN))
                dqs = tuple(out)
            return dqs, ec

        n_groups = qi // SB_GROUP_BWD

        def many(g, st):
            return group([g * SB_GROUP_BWD + j for j in range(SB_GROUP_BWD)], [None] * SB_GROUP_BWD, *st)

        def one(r, st):
            return group([n_groups * SB_GROUP_BWD + r], [None], *st)

        st = (tuple(jnp.zeros((HEAD, blk), F32) for _ in range(hp)), jnp.zeros((1, hp * blk), F32))
        st = lax.fori_loop(0, n_groups, many, st)
        st = lax.fori_loop(0, qi - n_groups * SB_GROUP_BWD, one, st)
        dqs, _ = group([qi], [strict], *st)
        for h in range(hp):
            dq_ref[:, h * HEAD:(h + 1) * HEAD] = dqs[h].T

    qs_ = pl.BlockSpec((blk, hp * HEAD), lambda h, i: (i, h))
    full = pl.BlockSpec((s, hp * HEAD), lambda h, i: (0, h), pipeline_mode=pl.Buffered(1))
    sh = jax.ShapeDtypeStruct((s, nh * HEAD), F32)
    return pl.pallas_call(
        body, name="sb_attn_bwd", grid=(nh // hp, nb),
        in_specs=[qs_, full, pl.BlockSpec((nb, hp * HEAD, blk), lambda h, i: (0, h, 0), pipeline_mode=pl.Buffered(1)),
                  full, qs_, pl.BlockSpec((hp, nb, 1, blk), lambda h, i: (h, 0, 0, i))],
        out_specs=[qs_, full, full], out_shape=[sh, sh, sh],
        compiler_params=_cparams("parallel", "arbitrary"),
    )(qn, kn, kt, vb, dy, carries)


def _lower_bound(lb_logits):
    def body(l_ref, o_ref):
        l = l_ref[...]
        m = jnp.max(l, axis=0, keepdims=True)
        e = jnp.exp(l - m)
        o_ref[...] = e[0:1, :] / jnp.sum(e, axis=0, keepdims=True)

    return pl.pallas_call(body, name="lower_bound", out_shape=jax.ShapeDtypeStruct((1, lb_logits.shape[1]), F32))(lb_logits)


def _hg_gates(hq, hf, lb):
    sq = _sig(hq)
    q = hq * sq
    sf = _sig(hf)
    f = lb + (1.0 - lb) * sf
    return q, sq, f, sf


def _hg_cum(g, c):
    hi, lo = _split(g)
    t = _tri(c, "le")
    return _dot(t, hi, NN) + _dot(t, lo, NN)


def _hg_heads(nh):
    return 8 if nh % 8 == 0 else 4 if nh % 4 == 0 else 2 if nh % 2 == 0 else 1


def _head_cols(x, h):
    return x[:, h * HEAD:(h + 1) * HEAD]


def _per_head(x, hp, fn):
    return jnp.concatenate([fn(_head_cols(x, h), h) for h in range(hp)], axis=1)


def _head_sums(x, hp):
    return [jnp.sum(_head_cols(x, h), axis=1, keepdims=True) for h in range(hp)]


def _head_scale(cols, x, hp):
    return jnp.concatenate([cols[h] * _head_cols(x, h) for h in range(hp)], axis=1)


def _row_mask(r, width):
    return lax.broadcasted_iota(jnp.int32, (SUB, width), 0) >= r


def _hg_intra_fwd(q, k, v, b, c, hp):
    outs = []
    for bi in range(c // SUB):
        q_i, b_i = q[bi * SUB:(bi + 1) * SUB], b[bi * SUB:(bi + 1) * SUB]
        acc = jnp.zeros((SUB, hp * HEAD), F32)
        for s in range((bi + 1) * SUB):
            d = b_i - b[s:s + 1]
            if s >= bi * SUB:
                d = jnp.where(_row_mask(s - bi * SUB, hp * HEAD), d, -jnp.inf)
            acc = acc + _head_scale(_head_sums(q_i * k[s:s + 1] * jnp.exp(d), hp), v[s:s + 1], hp)
        outs.append(acc)
    return jnp.concatenate(outs, axis=0)


def _hg_intra_bwd(q, k, v, b, do, c, hp, dq_scr, dk_scr, dv_scr):
    nblk = c // SUB
    dq_scr[...] = jnp.zeros_like(dq_scr)
    for s in range(c):
        bj = s // SUB
        ks, vs, bs = k[s:s + 1], v[s:s + 1], b[s:s + 1]
        acc_k = jnp.zeros((SUB, hp * HEAD), F32)
        acc_v = jnp.zeros((SUB, hp * HEAD), F32)
        for bi in range(bj, nblk):
            sl = slice(bi * SUB, (bi + 1) * SUB)
            d = b[sl] - bs
            if bi == bj:
                d = jnp.where(_row_mask(s - bj * SUB, hp * HEAD), d, -jnp.inf)
            dec = jnp.exp(d)
            qd = q[sl] * dec
            col = _head_sums(qd * ks, hp)
            dcol = _head_sums(do[sl] * vs, hp)
            dq_scr[sl, :] += _head_scale(dcol, ks * dec, hp)
            acc_k = acc_k + _head_scale(dcol, qd, hp)
            acc_v = acc_v + _head_scale(col, do[sl], hp)
        dk_scr[s:s + 1, :] = jnp.sum(acc_k, axis=0, keepdims=True)
        dv_scr[s:s + 1, :] = jnp.sum(acc_v, axis=0, keepdims=True)


def _hgrn2_fwd(proj, lb, g_out, nh, base, c):
    s = proj.shape[0]
    nch = s // c
    hp = _hg_heads(nh)
    wide = hp * HEAD

    def body(hq_ref, hf_ref, hi_ref, og_ref, lb_ref, g_ref, o_ref, y_ref, st_ref, st):
        @pl.when(pl.program_id(1) == 0)
        def _():
            st[...] = jnp.zeros_like(st)

        st_in = [st[h] for h in range(hp)]
        for h in range(hp):
            st_ref[h] = st_in[h]
        q, _, f, _ = _hg_gates(hq_ref[...], hf_ref[...], lb_ref[...])
        k = 1.0 - f
        v = hi_ref[...]
        b = _hg_cum(jnp.log(f), c)
        bl = b[c - 1:c, :]
        qe = (q * jnp.exp(b)).astype(BF16)
        o = _per_head(qe, hp, lambda x, h: _dot(x, st_in[h].astype(BF16), NT)) + _hg_intra_fwd(q, k, v, b, c, hp)
        kd = (k * jnp.exp(bl - b)).astype(BF16)
        vb = v.astype(BF16)
        keep = jnp.exp(bl)
        for h in range(hp):
            st[h] = st_in[h] * _head_cols(keep, h) + _dot(_head_cols(vb, h), _head_cols(kd, h), TN)
        o_ref[...] = o
        og = og_ref[...]
        gout = g_ref[...]
        y_ref[...] = (_per_head(o, hp, lambda x, h: _head_norm(x, gout)) * (og * _sig(og))).astype(BF16)

    col = lambda j: pl.BlockSpec((c, wide), lambda g, i: (i, (base + j * nh) // hp + g))
    row = pl.BlockSpec((1, wide), lambda g, i: (0, g))
    gs = pl.BlockSpec((1, HEAD), lambda g, i: (0, 0))
    o = pl.BlockSpec((c, wide), lambda g, i: (i, g))
    return pl.pallas_call(
        body, name="hgrn2_fwd", grid=(nh // hp, nch),
        in_specs=[col(0), col(1), col(2), col(3), row, gs],
        out_specs=[o, o, pl.BlockSpec((hp, None, HEAD, HEAD), lambda g, i: (g, i, 0, 0))],
        out_shape=[jax.ShapeDtypeStruct((s, nh * HEAD), F32), jax.ShapeDtypeStruct((s, nh * HEAD), BF16),
                   jax.ShapeDtypeStruct((nh, nch, HEAD, HEAD), F32)],
        scratch_shapes=[pltpu.VMEM((hp, HEAD, HEAD), F32)],
        compiler_params=_cparams("parallel", "arbitrary"),
    )(proj, proj, proj, proj, lb, g_out)


def _hgrn2_bwd(proj, lb, g_out, o_pre, states, dy, nh, base, c):
    s = proj.shape[0]
    nch = s // c
    hp = _hg_heads(nh)
    wide = hp * HEAD

    def fold(x):
        return jnp.sum(x.reshape(c // SUB, SUB, x.shape[1]), axis=0)

    def body(hq_ref, hf_ref, hi_ref, og_ref, lb_ref, g_ref, o_ref, st_ref, se_ref, dy_ref,
             dhq_ref, dhf_ref, dhi_ref, dog_ref, dg_ref, dlb_ref, dst, dq_scr, dk_scr, dv_scr):
        g, i = pl.program_id(0), pl.program_id(1)

        @pl.when(i == 0)
        def _():
            dst[...] = jnp.zeros_like(dst)
            dlb_ref[...] = jnp.zeros_like(dlb_ref)

        @pl.when((i == 0) & (g == 0))
        def _():
            dg_ref[...] = jnp.zeros_like(dg_ref)

        lbv = lb_ref[...]
        hq, hf = hq_ref[...], hf_ref[...]
        q, sq, f, sf = _hg_gates(hq, hf, lbv)
        k = 1.0 - f
        v = hi_ref[...]
        b = _hg_cum(jnp.log(f), c)
        bl = b[c - 1:c, :]
        eb = jnp.exp(b)
        ebl = jnp.exp(bl - b)

        o = o_ref[...]
        gout = g_ref[...]
        og = og_ref[...]
        sg = _sig(og)
        r = _per_head(o, hp, lambda x, h: jnp.broadcast_to(
            lax.rsqrt(jnp.mean(x * x, axis=-1, keepdims=True) + EPS), x.shape))
        gw = jnp.concatenate([gout] * hp, axis=1)
        n = o * r
        dyv = dy_ref[...]
        dn = dyv * (og * sg)
        dog_ref[...] = (dyv * n * gw * (sg * (1.0 + og * (1.0 - sg)))).astype(BF16)
        dnn = fold(dn * n)
        part = _head_cols(dnn, 0)
        for h in range(1, hp):
            part = part + _head_cols(dnn, h)
        dg_ref[...] += part
        a = dn * gw
        an = a * n
        do = r * (a - n * _per_head(an, hp, lambda x, h: jnp.broadcast_to(jnp.mean(x, axis=-1, keepdims=True), x.shape)))

        st_in = [st_ref[h].astype(BF16) for h in range(hp)]
        dstv = [dst[h] for h in range(hp)]
        dstb = [d.astype(BF16) for d in dstv]
        dob = do.astype(BF16)
        vb = v.astype(BF16)
        kdb = (k * ebl).astype(BF16)
        qeb = (q * eb).astype(BF16)
        _hg_intra_bwd(q, k, v, b, do, c, hp, dq_scr, dk_scr, dv_scr)
        dq = dq_scr[...] + eb * _per_head(dob, hp, lambda x, h: _dot(x, st_in[h], NN))
        dk = dk_scr[...] + ebl * _per_head(vb, hp, lambda x, h: _dot(x, dstb[h], NN))
        dv = dv_scr[...] + _per_head(kdb, hp, lambda x, h: _dot(x, dstb[h], NT))
        keep = jnp.exp(bl)
        for h in range(hp):
            dst[h] = dstv[h] * _head_cols(keep, h) + _dot(_head_cols(dob, h), _head_cols(qeb, h), TN)

        hi_, lo_ = _split(q * dq - k * dk)
        rev = _tri(c, "ge")
        later = jnp.concatenate([jnp.sum(dstv[h] * se_ref[h], axis=0, keepdims=True) for h in range(hp)], axis=1)
        dg = _dot(rev, hi_, NN) + _dot(rev, lo_, NN) + jnp.where(i > 0, later, 0.0)
        df = dg / f - dk
        dhq_ref[...] = (dq * (sq * (1.0 + hq * (1.0 - sq)))).astype(BF16)
        dhf_ref[...] = (df * (1.0 - lbv) * sf * (1.0 - sf)).astype(BF16)
        dhi_ref[...] = dv.astype(BF16)
        dlb_ref[...] += fold(df * (1.0 - sf))

    rv = lambda i: nch - 1 - i
    col = lambda j: pl.BlockSpec((c, wide), lambda g, i: (rv(i), (base + j * nh) // hp + g))
    row = pl.BlockSpec((1, wide), lambda g, i: (0, g))
    gs = pl.BlockSpec((1, HEAD), lambda g, i: (0, 0))
    o = pl.BlockSpec((c, wide), lambda g, i: (rv(i), g))
    st = pl.BlockSpec((hp, None, HEAD, HEAD), lambda g, i: (g, rv(i), 0, 0))
    se = pl.BlockSpec((hp, None, HEAD, HEAD), lambda g, i: (g, jnp.minimum(rv(i) + 1, nch - 1), 0, 0))
    sh = jax.ShapeDtypeStruct((s, nh * HEAD), BF16)
    return pl.pallas_call(
        body, name="hgrn2_bwd", grid=(nh // hp, nch),
        in_specs=[col(0), col(1), col(2), col(3), row, gs, o, st, se, o],
        out_specs=[o, o, o, o, pl.BlockSpec((SUB, HEAD), lambda g, i: (0, 0)), pl.BlockSpec((SUB, wide), lambda g, i: (0, g))],
        out_shape=[sh, sh, sh, sh, jax.ShapeDtypeStruct((SUB, HEAD), F32), jax.ShapeDtypeStruct((SUB, nh * HEAD), F32)],
        scratch_shapes=[pltpu.VMEM((hp, HEAD, HEAD), F32), pltpu.VMEM((c, wide), F32), pltpu.VMEM((c, wide), F32),
                        pltpu.VMEM((c, wide), F32)],
        compiler_params=_cparams("arbitrary", "arbitrary"),
    )(proj, proj, proj, proj, lb, g_out, o_pre, states, states, dy)


def _merge_tiles(s, d, gate_col):
    tr = _pick(s, (256, 128, 64, 32, 16, 8))
    tc = 128
    for cand in (512, 256):
        if d % cand == 0 and gate_col % cand == 0:
            tc = cand
            break
    return tr, tc


def _merge_fwd(proj, ya, yb, gate_col):
    s, d = ya.shape
    tr, tc = _merge_tiles(s, d, gate_col)
    ga0, gb0 = gate_col // tc, (gate_col + d) // tc

    def body(ga_ref, gb_ref, ya_ref, yb_ref, m_ref):
        m_ref[...] = (_sig(ga_ref[...]) * ya_ref[...] + _sig(gb_ref[...]) * yb_ref[...]).astype(BF16)

    o = pl.BlockSpec((tr, tc), lambda i, j: (i, j))
    return pl.pallas_call(
        body, name="merge_fwd", grid=(s // tr, d // tc),
        in_specs=[pl.BlockSpec((tr, tc), lambda i, j: (i, ga0 + j)), pl.BlockSpec((tr, tc), lambda i, j: (i, gb0 + j)), o, o],
        out_specs=o, out_shape=jax.ShapeDtypeStruct((s, d), BF16),
        compiler_params=_cparams("parallel", "parallel"),
    )(proj, proj, ya, yb)


def _merge_bwd(proj, ya, yb, dm, gate_col):
    s, d = ya.shape
    tr, tc = _merge_tiles(s, d, gate_col)
    ga0, gb0 = gate_col // tc, (gate_col + d) // tc

    def body(ga_ref, gb_ref, ya_ref, yb_ref, dm_ref, dya_ref, dyb_ref, dga_ref, dgb_ref):
        dmv = dm_ref[...]
        sa, sb = _sig(ga_ref[...]), _sig(gb_ref[...])
        dya_ref[...] = (dmv * sa).astype(BF16)
        dyb_ref[...] = (dmv * sb).astype(BF16)
        dga_ref[...] = (dmv * ya_ref[...] * sa * (1.0 - sa)).astype(BF16)
        dgb_ref[...] = (dmv * yb_ref[...] * sb * (1.0 - sb)).astype(BF16)

    o = pl.BlockSpec((tr, tc), lambda i, j: (i, j))
    sh = jax.ShapeDtypeStruct((s, d), BF16)
    return pl.pallas_call(
        body, name="merge_bwd", grid=(s // tr, d // tc),
        in_specs=[pl.BlockSpec((tr, tc), lambda i, j: (i, ga0 + j)), pl.BlockSpec((tr, tc), lambda i, j: (i, gb0 + j)), o, o, o],
        out_specs=[o, o, o, o], out_shape=[sh, sh, sh, sh],
        compiler_params=_cparams("parallel", "parallel"),
    )(proj, proj, ya, yb, dm)


CONV_ROWS = 512


def _conv_ext(ref, i, rows, s, before, after):
    parts = []
    if before:
        p = ref[pl.ds(pl.multiple_of(jnp.maximum(i * rows - before, 0), SUB), before), :]
        parts.append(jnp.where(i > 0, p, 0.0))
    parts.append(ref[pl.ds(pl.multiple_of(i * rows, SUB), rows), :])
    if after:
        nxt = ref[pl.ds(pl.multiple_of(jnp.minimum((i + 1) * rows, s - after), SUB), after), :]
        parts.append(jnp.where((i + 1) * rows < s, nxt, 0.0))
    return jnp.concatenate(parts, axis=0)


def _conv3(ext, w, bias):
    x1 = pltpu.roll(ext, 1, 0)
    x2 = pltpu.roll(ext, 2, 0)
    return bias + w[0:1, :] * x2 + w[1:2, :] * x1 + w[2:3, :] * ext, x1, x2


def _convffn_fwd(up, conv_w, conv_b, dff):
    s = up.shape[0]
    tc = HEAD
    nf = dff // tc
    rows = _pick(s, (CONV_ROWS, 256, 128, 64, 32, 16, 8))

    def body(ug_ref, uv_ref, wg_ref, wv_ref, bg_ref, bv_ref, a_ref):
        wg, wv, bg, bv = wg_ref[...], wv_ref[...], bg_ref[...], bv_ref[...]

        def step(i, _):
            g = _conv3(_conv_ext(ug_ref, i, rows, s, SUB, 0), wg, bg)[0][SUB:]
            v = _conv3(_conv_ext(uv_ref, i, rows, s, SUB, 0), wv, bv)[0][SUB:]
            a_ref[pl.ds(pl.multiple_of(i * rows, SUB), rows), :] = (g * _sig(g) * v).astype(BF16)
            return 0

        lax.fori_loop(0, s // rows, step, 0)

    cg = lambda r: pl.BlockSpec((r, tc), lambda j: (0, j))
    cv = lambda r: pl.BlockSpec((r, tc), lambda j: (0, nf + j))
    return pl.pallas_call(
        body, name="convffn_fwd", grid=(nf,),
        in_specs=[cg(s), cv(s), cg(3), cv(3), cg(1), cv(1)], out_specs=cg(s),
        out_shape=jax.ShapeDtypeStruct((s, dff), BF16),
        compiler_params=_cparams("parallel"),
    )(up, up, conv_w, conv_w, conv_b, conv_b)


def _convffn_bwd(up, conv_w, conv_b, dact, dff):
    s = up.shape[0]
    tc = HEAD
    nf = dff // tc
    rows = _pick(s, (CONV_ROWS, 256, 128, 64, 32, 16, 8))
    n_ext = rows + SUB

    def body(ug_ref, uv_ref, wg_ref, wv_ref, bg_ref, bv_ref, da_ref,
             dug_ref, duv_ref, dwg_ref, dwv_ref, dbg_ref, dbv_ref):
        wg, wv, bg, bv = wg_ref[...], wv_ref[...], bg_ref[...], bv_ref[...]

        def fold(x):
            return jnp.sum(x.reshape(rows // SUB, SUB, tc), axis=0)

        def one(ext, x1, x2, d_ext, w):
            d1 = pltpu.roll(d_ext, n_ext - 1, 0)[:rows]
            d2 = pltpu.roll(d_ext, n_ext - 2, 0)[:rows]
            dc = d_ext[:rows]
            du = w[2:3, :] * dc + w[1:2, :] * d1 + w[0:1, :] * d2
            sl = slice(SUB, SUB + rows)
            return du, (fold(dc * x2[sl]), fold(dc * x1[sl]), fold(dc * ext[sl]), fold(dc))

        def step(i, acc):
            eg = _conv_ext(ug_ref, i, rows, s, SUB, SUB)
            ev = _conv_ext(uv_ref, i, rows, s, SUB, SUB)
            g, g1, g2 = _conv3(eg, wg, bg)
            v, v1, v2 = _conv3(ev, wv, bv)
            g, v = g[SUB:], v[SUB:]
            da = _conv_ext(da_ref, i, rows, s, 0, SUB)
            sg = _sig(g)
            dg = da * v * (sg * (1.0 + g * (1.0 - sg)))
            dv = da * (g * sg)
            dug, pg = one(eg, g1, g2, dg, wg)
            duv, pv = one(ev, v1, v2, dv, wv)
            at = pl.ds(pl.multiple_of(i * rows, SUB), rows)
            dug_ref[at, :] = dug.astype(BF16)
            duv_ref[at, :] = duv.astype(BF16)
            return tuple(a + p for a, p in zip(acc, pg + pv))

        zero = jnp.zeros((SUB, tc), F32)
        acc = lax.fori_loop(0, s // rows, step, (zero,) * 8)
        red = [jnp.sum(a, axis=0, keepdims=True) for a in acc]
        for j in range(3):
            dwg_ref[j:j + 1, :] = red[j]
            dwv_ref[j:j + 1, :] = red[4 + j]
        dbg_ref[...] = red[3]
        dbv_ref[...] = red[7]

    cg = lambda r: pl.BlockSpec((r, tc), lambda j: (0, j))
    cv = lambda r: pl.BlockSpec((r, tc), lambda j: (0, nf + j))
    outs = pl.pallas_call(
        body, name="convffn_bwd", grid=(nf,),
        in_specs=[cg(s), cv(s), cg(3), cv(3), cg(1), cv(1), cg(s)],
        out_specs=[cg(s), cg(s), cg(3), cg(3), cg(1), cg(1)],
        out_shape=[jax.ShapeDtypeStruct((s, dff), BF16), jax.ShapeDtypeStruct((s, dff), BF16),
                   jax.ShapeDtypeStruct((3, dff), F32), jax.ShapeDtypeStruct((3, dff), F32),
                   jax.ShapeDtypeStruct((1, dff), F32), jax.ShapeDtypeStruct((1, dff), F32)],
        compiler_params=_cparams("parallel"),
    )(up, up, conv_w, conv_w, conv_b, conv_b, dact)
    return outs


def _loss_head(out, target):
    s, d = out.shape
    tr = _pick(s, (256, 128, 64, 32, 16, 8))

    def body(o_ref, t_ref, d_ref, l_ref):
        @pl.when(pl.program_id(0) == 0)
        def _():
            l_ref[...] = jnp.zeros_like(l_ref)

        err = o_ref[...] - t_ref[...]
        d_ref[...] = err * (1.0 / d)
        sq = jnp.sum((err * err).reshape(tr // SUB, SUB, d), axis=0)
        part = sq[:, 0:HEAD]
        for j in range(1, d // HEAD):
            part = part + sq[:, j * HEAD:(j + 1) * HEAD]
        l_ref[...] += part

    blk = pl.BlockSpec((tr, d), lambda i: (i, 0))
    return pl.pallas_call(
        body, name="loss_head", grid=(s // tr,), in_specs=[blk, blk],
        out_specs=[blk, pl.BlockSpec((SUB, HEAD), lambda i: (0, 0))],
        out_shape=[jax.ShapeDtypeStruct((s, d), F32), jax.ShapeDtypeStruct((SUB, HEAD), F32)],
        compiler_params=_cparams("arbitrary"),
    )(out, target)


def _sum_rows(name, parts):
    def body(p_ref, o_ref):
        o_ref[...] = jnp.sum(p_ref[...], axis=0, keepdims=True)

    return pl.pallas_call(body, name=name, out_shape=jax.ShapeDtypeStruct((1, parts.shape[1]), F32))(parts)


def _local_step(x, target, g_mix, g_q, g_k, lb_logits, g_hg_out, g_ffn, conv_w, conv_b, w_in, later_weights, grads_ready):
    s, d = x.shape
    nh = lb_logits.shape[1] // HEAD
    wid = nh * HEAD
    blk = _pick(s, (256, 128))
    chunk = _pick(s, (HG_CHUNK,))
    gate_col = 7 * wid

    u, u_t = _rmsnorm_fwd("rmsnorm_mix", x, g_mix)
    proj = _matmul("in_proj", u, w_in, "nn", F32)
    qn, kn, vb, kt, vt = _qk_norm_fwd(proj, g_q, g_k, nh, blk)
    y_a, carries = _sb_attn_fwd(qn, kn, vt, nh, blk)
    lb = _lower_bound(lb_logits)
    o_pre, y_b, states = _hgrn2_fwd(proj, lb, g_hg_out, nh, 3 * nh, chunk)
    later = later_weights(o_pre)
    p_a, p_b, w_up = later["p_a"], later["p_b"], later["w_up"]
    w_o = later["w_o"].reshape(1, d, d)
    dff = later["w_down"].shape[1] * N_CHIPS
    w_down = later["w_down"].reshape(1, dff, d)
    ya_p = _matmul("proj_a", y_a, p_a, "nn", F32)
    yb_p = _matmul("proj_b", y_b, p_b, "nn", F32)
    m = _merge_fwd(proj, ya_p, yb_p, gate_col)
    h = _matmul("out_proj", m, w_o, "nn", F32, add=x)
    u2, u2_t = _rmsnorm_fwd("rmsnorm_ffn", h, g_ffn)
    up = _matmul("up_proj", u2, w_up, "nn", F32)
    act = _convffn_fwd(up, conv_w, conv_b, dff)
    out = _matmul("down_proj", act, w_down, "nn", F32, add=h)
    dout, sq = _loss_head(out, target)

    dact = _matmul("d_act", dout, w_down, "nt", F32)
    g_w_down = _matmul_tn("g_w_down", act, dout, 1, BF16).reshape(N_CHIPS, dff // N_CHIPS, d)
    dup_g, dup_v, dcw_g, dcw_v, dcb_g, dcb_v = _convffn_bwd(up, conv_w, conv_b, dact, dff)
    dup = jnp.concatenate([dup_g, dup_v], axis=1)
    g_w_up = _matmul("g_w_up", u2_t, dup[None], "nn", BF16, out_shards=N_CHIPS)
    sent = grads_ready(("w_down", "w_up"), [g_w_down, g_w_up])
    du2 = _matmul("d_u2", dup, w_up, "nt", F32)
    dh, pg_ffn = _rmsnorm_bwd("rmsnorm_ffn_bwd", h, g_ffn + sent, du2, dout)
    dm = _matmul("d_m", dh, w_o, "nt", F32)
    g_w_o = _matmul_tn("g_w_o", m, dh, 1, BF16).reshape(N_CHIPS, d // N_CHIPS, d)
    dya_p, dyb_p, dga, dgb = _merge_bwd(proj, ya_p, yb_p, dm, gate_col)
    g_p_a = _matmul_tn("g_p_a", y_a, dya_p, N_CHIPS, BF16)
    g_p_b = _matmul_tn("g_p_b", y_b, dyb_p, N_CHIPS, BF16)
    sent = grads_ready(("w_o", "p_a", "p_b"), [g_w_o, g_p_a, g_p_b])
    dy_a = _matmul("d_y_a", dya_p, p_a, "nt", F32)
    dy_b = _matmul("d_y_b", dyb_p, p_b, "nt", F32)
    dhq, dhf, dhi, dog, pg_hg, p_lb = _hgrn2_bwd(proj, lb, g_hg_out + sent, o_pre, states, dy_b, nh, 3 * nh, chunk)
    dqn, dkn, dv = _sb_attn_bwd(qn, kn, kt, vb, dy_a, carries, nh, blk)
    dq, dk, pg_q, pg_k = _qk_norm_bwd(proj, g_q, g_k, dqn, dkn, nh)
    dproj = jnp.concatenate([dq, dk, dv.astype(BF16), dhq, dhf, dhi, dog, dga, dgb], axis=1)
    g_w_in = _matmul("g_w_in", u_t, dproj[None], "nn", BF16, out_shards=N_CHIPS)
    sent = grads_ready(("w_in",), [g_w_in])
    du = _matmul("d_u", dproj, w_in, "nt", F32)
    dx, pg_mix = _rmsnorm_bwd("rmsnorm_mix_bwd", x, g_mix + sent, du, dh)

    small = dict(
        g_mix=_sum_rows("sum_g_mix", pg_mix), g_q=_sum_rows("sum_g_q", pg_q), g_k=_sum_rows("sum_g_k", pg_k),
        lb=_sum_rows("sum_lb", p_lb), g_hg_out=_sum_rows("sum_g_hg", pg_hg), g_ffn=_sum_rows("sum_g_ffn", pg_ffn),
        conv_w=jnp.concatenate([dcw_g, dcw_v], axis=1), conv_b=jnp.concatenate([dcb_g, dcb_v], axis=1),
        sq=_sum_rows("sum_sq", sq),
    )
    return dx, small, lb


ANY = pl.BlockSpec(memory_space=pl.ANY)


def _place():
    x, y, c = lax.axis_index("x"), lax.axis_index("y"), lax.axis_index("c")
    chips = [(1 - x, y), (x, 1 - y), (1 - x, 1 - y)]
    return x, y, c, chips


def _remote(src, dst, send_sem, recv_sem, to):
    return pltpu.make_async_remote_copy(src_ref=src, dst_ref=dst, send_sem=send_sem, recv_sem=recv_sem,
                                        device_id=to, device_id_type=MESH)


def _cast_bf16(name, w):
    r, c = w.shape
    tr = _pick(r, (256, 128, 64, 32, 16))

    def body(w_ref, o_ref):
        o_ref[...] = w_ref[...].astype(BF16)

    return pl.pallas_call(
        body, name=name, grid=(r // tr,), in_specs=[pl.BlockSpec((tr, c), lambda i: (i, 0))],
        out_specs=pl.BlockSpec((tr, c), lambda i: (i, 0)), out_shape=jax.ShapeDtypeStruct((r, c), BF16),
        compiler_params=_cparams("parallel"),
    )(w)


def _gather_weights(shards):
    n = len(shards)

    def body(*refs):
        ins, outs = refs[:n], refs[n:2 * n]
        send, recv, local = refs[2 * n:]
        x, y, c, chips = _place()
        mine = 2 * x + y
        sends, owns = [], []
        for k in range(n):
            half = ins[k].shape[0] // 2
            rows = pl.ds(c * half, half)
            own = pltpu.make_async_copy(ins[k], outs[k].at[mine], local.at[k])
            own.start()
            owns.append(own)
            for j, (px, py) in enumerate(chips):
                cp = _remote(ins[k].at[rows], outs[k].at[mine, rows], send.at[k, j], recv.at[k, j], (px, py, c))
                cp.start()
                sends.append(cp)
        for k in range(n):
            half = ins[k].shape[0] // 2
            rows = pl.ds(c * half, half)
            for j, (px, py) in enumerate(chips):
                part = outs[k].at[2 * px + py, rows]
                _remote(part, part, send.at[k, j], recv.at[k, j], (px, py, c)).wait_recv()
                fw = _remote(part, part, send.at[k, 3 + j], recv.at[k, 3 + j], (x, y, 1 - c))
                fw.start()
                sends.append(fw)
        for k in range(n):
            half = ins[k].shape[0] // 2
            other = pl.ds((1 - c) * half, half)
            for j, (px, py) in enumerate(chips):
                part = outs[k].at[2 * px + py, other]
                _remote(part, part, send.at[k, 3 + j], recv.at[k, 3 + j], (x, y, 1 - c)).wait_recv()
        for cp in sends:
            cp.wait_send()
        for cp in owns:
            cp.wait()

    return pl.pallas_call(
        body, name="gather_weights", in_specs=[ANY] * n, out_specs=[ANY] * n,
        out_shape=[jax.ShapeDtypeStruct((N_CHIPS,) + w.shape, w.dtype) for w in shards],
        scratch_shapes=[pltpu.SemaphoreType.DMA((n, 6)), pltpu.SemaphoreType.DMA((n, 6)), pltpu.SemaphoreType.DMA((n,))],
    )(*shards)


def _to_sibling(name, srcs):
    n = len(srcs)

    def body(*refs):
        ins, outs = refs[:n], refs[n:2 * n]
        send, recv = refs[2 * n:]
        x, y, c, _ = _place()
        cps = []
        for k in range(n):
            cp = _remote(ins[k], outs[k], send.at[k], recv.at[k], (x, y, 1 - c))
            cp.start()
            cps.append(cp)
        for cp in cps:
            cp.wait_recv()
        for cp in cps:
            cp.wait_send()

    return pl.pallas_call(
        body, name=name, in_specs=[ANY] * n, out_specs=[ANY] * n,
        out_shape=[jax.ShapeDtypeStruct(a.shape, a.dtype) for a in srcs],
        scratch_shapes=[pltpu.SemaphoreType.DMA((n,)), pltpu.SemaphoreType.DMA((n,))],
    )(*srcs)


HBM = pl.BlockSpec(memory_space=pltpu.HBM)
SEM = pl.BlockSpec(memory_space=pltpu.SEMAPHORE)
SIDE = pltpu.SideEffectType.DATAFLOW_SIDE_EFFECTING
N_PEERS = 7


def _peer(r):
    x, y, c = lax.axis_index("x"), lax.axis_index("y"), lax.axis_index("c")
    return (1 - x if r & 4 else x), (1 - y if r & 2 else y), (1 - c if r & 1 else c)


def _partial_copy(src, land, send, recv, k, r):
    px, py, pc = _peer(r)
    half = src.shape[1] // 2
    sem = k * N_PEERS + r - 1
    return _remote(src.at[2 * px + py, pl.ds(pc * half, half)], land.at[r - 1], send.at[sem], recv.at[sem], (px, py, pc))


def _shard_copy(full, send, recv, k, r):
    x, y, c = lax.axis_index("x"), lax.axis_index("y"), lax.axis_index("c")
    half = full.shape[1] // 2
    part = full.at[2 * x + y, pl.ds(c * half, half)]
    sem = k * (N_PEERS - 1) + r - 2
    return _remote(part, part, send.at[sem], recv.at[sem], _peer(r))


def _start_copies(name, arrays, n_sems, copies):
    n = len(arrays)

    def body(*refs):
        send, recv, token = refs[n], refs[n + 1], refs[-1]
        for cp in copies(refs[:n], send, recv):
            cp.start()
        token[...] = jnp.zeros_like(token)

    sem = pltpu.SemaphoreType.DMA((n_sems,))
    outs = pl.pallas_call(
        body, name=name, in_specs=[HBM] * n,
        out_specs=[SEM, SEM] + [HBM] * n + [pl.BlockSpec(memory_space=pltpu.VMEM)],
        out_shape=[sem, sem] + [pltpu.HBM(a.shape, a.dtype) for a in arrays] + [jax.ShapeDtypeStruct((SUB, HEAD), F32)],
        input_output_aliases={i: 2 + i for i in range(n)},
        compiler_params=pltpu.CompilerParams(has_side_effects=SIDE),
    )(*[pltpu.with_memory_space_constraint(a, pltpu.HBM) for a in arrays])
    return outs[0], outs[1], list(outs[2:2 + n]), outs[-1]


def _wait_copies(name, send, recv, arrays, after, copies):
    n = len(arrays)

    def body(*refs):
        for cp in copies(refs[:n], refs[n], refs[n + 1]):
            cp.wait_send()
            cp.wait_recv()

    return list(pl.pallas_call(
        body, name=name, in_specs=[HBM] * n + [SEM, SEM, ANY], out_specs=[HBM] * n,
        out_shape=[pltpu.HBM(a.shape, a.dtype) for a in arrays],
        input_output_aliases={i: i for i in range(n)},
        compiler_params=pltpu.CompilerParams(has_side_effects=SIDE),
    )(*arrays, send, recv, after))


def _partial_copies(n):
    def copies(refs, send, recv):
        return [_partial_copy(refs[k], refs[n + k], send, recv, k, r) for k in range(n) for r in range(1, N_PEERS + 1)]
    return copies


def _shard_copies(n):
    def copies(refs, send, recv):
        return [_shard_copy(refs[k], send, recv, k, r) for k in range(n) for r in range(2, N_PEERS + 1)]
    return copies


def _cast_place(name, w, shard):
    r, c = w.shape
    tr = _pick(r, (256, 128, 64, 32, 16))

    def body(s_ref, w_ref, o_ref):
        o_ref[...] = w_ref[...].astype(BF16)

    return pl.pallas_call(
        body, name=name,
        grid_spec=pltpu.PrefetchScalarGridSpec(
            num_scalar_prefetch=1, grid=(r // tr,), in_specs=[pl.BlockSpec((tr, c), lambda i, sr: (i, 0))],
            out_specs=pl.BlockSpec((None, tr, c), lambda i, sr: (sr[0], i, 0))),
        out_shape=jax.ShapeDtypeStruct((N_CHIPS, r, c), BF16),
        compiler_params=_cparams("parallel"),
    )(shard, w)


def _sum_peers(name, g, land, shard, core):
    _, r, cols = g.shape
    half = r // 2
    tr = _pick(half, (128, 64, 32, 16))
    nt = half // tr

    def body(s_ref, c_ref, g_ref, l_ref, o_ref):
        acc = g_ref[...].astype(F32)
        for j in range(N_PEERS):
            acc = acc + l_ref[j].astype(F32)
        o_ref[...] = acc

    return pl.pallas_call(
        body, name=name,
        grid_spec=pltpu.PrefetchScalarGridSpec(
            num_scalar_prefetch=2, grid=(nt,),
            in_specs=[pl.BlockSpec((None, tr, cols), lambda i, sr, cr: (sr[0], cr[0] * nt + i, 0)),
                      pl.BlockSpec((N_PEERS, tr, cols), lambda i, sr, cr: (0, i, 0))],
            out_specs=pl.BlockSpec((tr, cols), lambda i, sr, cr: (i, 0))),
        out_shape=jax.ShapeDtypeStruct((half, cols), F32),
        compiler_params=_cparams("parallel"),
    )(shard, core, g, land)


def _join_halves(mine, got, c):
    half, cols = mine.shape
    tr = _pick(half, (256, 128, 64, 32, 16, 8))
    nt = half // tr

    def body(c_ref, a_ref, b_ref, o_ref):
        i = pl.program_id(0)
        own = (i // nt) == c_ref[0]

        @pl.when(own)
        def _():
            o_ref[...] = a_ref[...]

        @pl.when(jnp.logical_not(own))
        def _():
            o_ref[...] = b_ref[...]

    blk = pl.BlockSpec((tr, cols), lambda i, cr: (i % nt, 0))
    return pl.pallas_call(
        body, name="join_halves",
        grid_spec=pltpu.PrefetchScalarGridSpec(num_scalar_prefetch=1, grid=(2 * nt,), in_specs=[blk, blk],
                                               out_specs=pl.BlockSpec((tr, cols), lambda i, cr: (i, 0))),
        out_shape=jax.ShapeDtypeStruct((2 * half, cols), F32),
        compiler_params=_cparams("parallel"),
    )(c, mine, got)


def _all_gather_rows(name, row):
    p = row.shape[1]

    def body(in_ref, out_ref, send, recv, local):
        x, y, c, _ = _place()
        me = 4 * x + 2 * y + c
        own = pltpu.make_async_copy(in_ref, out_ref.at[me], local)
        own.start()
        cps = []
        for k in range(1, 8):
            px, py, pc = x ^ (k >> 2), y ^ ((k >> 1) & 1), c ^ (k & 1)
            cp = _remote(in_ref, out_ref.at[me], send.at[k - 1], recv.at[k - 1], (px, py, pc))
            cp.start()
            cps.append(cp)
        for cp in cps:
            cp.wait_recv()
        for cp in cps:
            cp.wait_send()
        own.wait()

    return pl.pallas_call(
        body, name=name, in_specs=[ANY], out_specs=ANY,
        out_shape=jax.ShapeDtypeStruct((8, 1, p), F32),
        scratch_shapes=[pltpu.SemaphoreType.DMA((7,)), pltpu.SemaphoreType.DMA((7,)), pltpu.SemaphoreType.DMA],
    )(row)


def _sum_devices(rows):
    def body(r_ref, o_ref):
        acc = r_ref[0]
        for k in range(1, 8):
            acc = acc + r_ref[k]
        o_ref[...] = acc

    return pl.pallas_call(body, name="sum_devices", out_shape=jax.ShapeDtypeStruct(rows.shape[1:], F32))(rows)


def _adamw(name, w, g, m, v):
    r, c = w.shape
    tr = _pick(r, (128, 64, 32, 16, 8))
    bc1 = 1.0 - ADAM_B1 ** ADAM_STEP
    bc2 = 1.0 - ADAM_B2 ** ADAM_STEP

    def body(w_ref, g_ref, m_ref, v_ref, d_ref, nm_ref, nv_ref):
        gv = g_ref[...]
        nm = ADAM_B1 * m_ref[...] + (1.0 - ADAM_B1) * gv
        nv = ADAM_B2 * v_ref[...] + (1.0 - ADAM_B2) * (gv * gv)
        d_ref[...] = -ADAM_LR * ((nm / bc1) / (jnp.sqrt(nv / bc2) + ADAM_EPS) + ADAM_WD * w_ref[...])
        nm_ref[...] = nm
        nv_ref[...] = nv

    blk = pl.BlockSpec((tr, c), lambda i: (i, 0))
    sh = jax.ShapeDtypeStruct((r, c), F32)
    return pl.pallas_call(
        body, name=name, grid=(r // tr,), in_specs=[blk] * 4, out_specs=[blk] * 3, out_shape=[sh] * 3,
        compiler_params=_cparams("parallel"),
    )(w, g, m, v)


def _lb_logits_grad(dlb, lb):
    def body(d_ref, lb_ref, o_ref):
        lbv = lb_ref[...]
        t = d_ref[...] * lbv * (1.0 - lbv)
        o_ref[0:1, :] = t
        o_ref[1:2, :] = -t

    return pl.pallas_call(body, name="lb_logits_grad", out_shape=jax.ShapeDtypeStruct((2, dlb.shape[1]), F32))(dlb, lb)


BIG = ("w_in", "p_a", "p_b", "w_o", "w_up", "w_down")
SMALL = ("g_mix", "g_q", "g_k", "lb_logits", "g_hg_out", "g_ffn", "conv_w", "conv_b")
ORDER = ("g_mix", "w_in", "g_q", "g_k", "lb_logits", "g_hg_out", "p_a", "p_b", "w_o", "g_ffn", "w_up", "conv_w", "conv_b", "w_down")


def kernel(x, g_mix, w_in, g_q, g_k, lb_logits, g_hg_out, p_a, p_b, w_o, g_ffn, w_up, conv_w, conv_b, w_down, loss_target, m_g_mix, m_w_in, m_g_q, m_g_k, m_lb_logits, m_g_hg_out, m_p_a, m_p_b, m_w_o, m_g_ffn, m_w_up, m_conv_w, m_conv_b, m_w_down, v_g_mix, v_w_in, v_g_q, v_g_k, v_lb_logits, v_g_hg_out, v_p_a, v_p_b, v_w_o, v_g_ffn, v_w_up, v_conv_w, v_conv_b, v_w_down):
    assert lb_logits.shape[0] == 2, "the lower bound is the first row of a two-row softmax"
    w = dict(g_mix=g_mix, w_in=w_in[0], g_q=g_q, g_k=g_k, lb_logits=lb_logits, g_hg_out=g_hg_out, p_a=p_a[0], p_b=p_b[0],
             w_o=w_o[0], g_ffn=g_ffn, w_up=w_up[0], conv_w=conv_w[0], conv_b=conv_b, w_down=w_down[0])
    mom = dict(g_mix=m_g_mix, w_in=m_w_in[0], g_q=m_g_q, g_k=m_g_k, lb_logits=m_lb_logits, g_hg_out=m_g_hg_out, p_a=m_p_a[0],
               p_b=m_p_b[0], w_o=m_w_o[0], g_ffn=m_g_ffn, w_up=m_w_up[0], conv_w=m_conv_w[0], conv_b=m_conv_b, w_down=m_w_down[0])
    var = dict(g_mix=v_g_mix, w_in=v_w_in[0], g_q=v_g_q, g_k=v_g_k, lb_logits=v_lb_logits, g_hg_out=v_g_hg_out, p_a=v_p_a[0],
               p_b=v_p_b[0], w_o=v_w_o[0], g_ffn=v_g_ffn, w_up=v_w_up[0], conv_w=v_conv_w[0], conv_b=v_conv_b, w_down=v_w_down[0])
    d = x.shape[2]
    cx, cy, cc = lax.axis_index("x"), lax.axis_index("y"), lax.axis_index("c")
    shard = (2 * cx + cy).astype(jnp.int32).reshape(1)
    core = cc.astype(jnp.int32).reshape(1)

    w_in_full = _gather_weights([_cast_bf16("cast_w_in", w["w_in"])])[0]
    later = ("p_a", "p_b", "w_o", "w_up", "w_down")
    n_later = len(later)
    fulls = [_cast_place("cast_" + n, w[n], shard) for n in later]
    w_send, w_recv, fulls, w_token = _start_copies("weights_start", fulls + [w_in_full], n_later * (N_PEERS - 1),
                                                   _shard_copies(n_later))
    w_in_full = fulls.pop()

    def later_weights(after):
        return dict(zip(later, _wait_copies("weights_wait", w_send, w_recv, fulls, after, _shard_copies(n_later))))

    cw = conv_w.shape[2]
    rows = _all_gather_rows("gather_conv_w", w["conv_w"].reshape(1, 3 * cw))
    conv_full = jnp.concatenate([rows[2 * s, 0].reshape(3, cw) for s in range(N_CHIPS)], axis=1)

    pending = []

    def grads_ready(names, gs):
        n = len(gs)
        lands = [lax.empty((N_PEERS, g.shape[1] // 2, g.shape[2]), BF16) for g in gs]
        send, recv, arrays, token = _start_copies("partials_start_" + names[0], list(gs) + lands, n * N_PEERS, _partial_copies(n))
        pending.append((names, send, recv, arrays))
        return token[0:1, 0:1]

    dx, small, lb = _local_step(x[0], loss_target[0], g_mix + w_token[0:1, 0:1], g_q, g_k, lb_logits, g_hg_out, g_ffn,
                                conv_full, conv_b, w_in_full, later_weights, grads_ready)

    mine = {}
    after = dx
    for names, send, recv, arrays in pending:
        n = len(names)
        arrays = _wait_copies("partials_wait_" + names[0], send, recv, arrays, after, _partial_copies(n))
        for k, name in enumerate(names):
            mine[name] = _sum_peers("sum_" + name, arrays[k], arrays[n + k], shard, core)
        after = mine[names[-1]]
    theirs = _to_sibling("reduced_to_sibling", [mine[n] for n in BIG])
    grads = {n: _join_halves(mine[n], b, core) for n, b in zip(BIG, theirs)}

    names = ("g_mix", "g_q", "g_k", "lb", "g_hg_out", "g_ffn", "conv_b", "sq")
    packed = jnp.concatenate([small[n] for n in names] + [small["conv_w"].reshape(1, -1)], axis=1)
    total = _sum_devices(_all_gather_rows("gather_small_grads", packed))
    off = 0
    red = {}
    for n in names:
        ln = small[n].shape[1]
        red[n] = total[:, off:off + ln]
        off += ln
    conv_all = total[:, off:].reshape(3, -1)
    loss = 0.5 * jnp.sum(red["sq"]) / d
    grads["conv_w"] = lax.dynamic_slice_in_dim(conv_all, (2 * cx + cy) * cw, cw, axis=1)
    grads["lb_logits"] = _lb_logits_grad(red["lb"], lb)
    for n in ("g_mix", "g_q", "g_k", "g_hg_out", "g_ffn", "conv_b"):
        grads[n] = red[n]

    delta, new_m, new_v = {}, {}, {}
    for n in ORDER:
        delta[n], new_m[n], new_v[n] = _adamw("adamw_" + n, w[n], grads[n], mom[n], var[n])

    def shaped(a, like):
        return a.reshape(like.shape)

    ref_w = dict(g_mix=g_mix, w_in=w_in, g_q=g_q, g_k=g_k, lb_logits=lb_logits, g_hg_out=g_hg_out, p_a=p_a, p_b=p_b, w_o=w_o,
                 g_ffn=g_ffn, w_up=w_up, conv_w=conv_w, conv_b=conv_b, w_down=w_down)
    outs = [loss, dx[None]]
    for group in (grads, delta, new_m, new_v):
        outs += [shaped(group[n], ref_w[n]) for n in ORDER]
    return tuple(outs)
```

```python
import functools

import jax
import jax.numpy as jnp
from jax import lax
from jax.experimental import pallas as pl
from jax.experimental.pallas import tpu as pltpu

F32 = jnp.float32
BF16 = jnp.bfloat16
HEAD = 128
EPS = 1e-6
N_CHIPS = 4
HG_CHUNK = 32
SUB = 8
ADAM_LR, ADAM_B1, ADAM_B2, ADAM_EPS, ADAM_WD, ADAM_STEP = 0.001, 0.9, 0.999, 1e-08, 0.01, 10
VMEM_LIMIT = 56 * 1024 * 1024
MESH = pl.DeviceIdType.MESH

NN = (((1,), (0,)), ((), ()))
NT = (((1,), (1,)), ((), ()))
TN = (((0,), (0,)), ((), ()))


def _cparams(*sem):
    return pltpu.CompilerParams(dimension_semantics=sem if sem else None, vmem_limit_bytes=VMEM_LIMIT)


def _pick(n, cands):
    for c in cands:
        if c <= n and n % c == 0:
            return c
    return n


def _sig(x):
    return 0.5 * jnp.tanh(0.5 * x) + 0.5


def _dot(a, b, dims):
    return lax.dot_general(a, b, dims, preferred_element_type=F32)


def _split(x):
    hi = x.astype(BF16)
    lo = (x - hi.astype(F32)).astype(BF16)
    return hi, lo


def _tri(n, kind):
    r = lax.broadcasted_iota(jnp.int32, (n, n), 0)
    c = lax.broadcasted_iota(jnp.int32, (n, n), 1)
    m = {"ge": c >= r, "gt": c > r, "le": c <= r, "lt": c < r}[kind]
    return jnp.where(m, 1.0, 0.0).astype(BF16)


TILE_M = (1024, 512, 256, 128)
TILE_N = (1408, 1024, 512, 256, 128)
TILE_K = (2816, 2048, 1408, 1024, 512, 256, 128)


def _matmul(name, a, b, mode, out_dtype, add=None, out_shards=None):
    if mode == "nn":
        m, k = a.shape
        g, _, ns = b.shape
        n = g * ns
        ns_out = n // out_shards if out_shards else ns
        tm, tn, tk = _pick(m, TILE_M), _pick(min(ns, ns_out), TILE_N), _pick(k, TILE_K)
        nps = ns // tn
        grid = (m // tm, n // tn, k // tk)
        a_spec = pl.BlockSpec((tm, tk), lambda i, j, kk: (i, kk))
        b_spec = pl.BlockSpec((None, tk, tn), lambda i, j, kk: (j // nps, kk, j % nps))
        if out_shards:
            npo = ns_out // tn
            o_spec = pl.BlockSpec((None, tm, tn), lambda i, j, kk: (j // npo, i, j % npo))
            o_shape = jax.ShapeDtypeStruct((out_shards, m, ns_out), out_dtype)
        else:
            o_spec = pl.BlockSpec((tm, tn), lambda i, j, kk: (i, j))
            o_shape = jax.ShapeDtypeStruct((m, n), out_dtype)
        dims = NN
    elif mode == "nt":
        m, k = a.shape
        g, n, ks = b.shape
        tm, tn, tk = _pick(m, TILE_M), _pick(n, TILE_N), _pick(ks, TILE_K)
        kps = ks // tk
        grid = (m // tm, n // tn, k // tk)
        a_spec = pl.BlockSpec((tm, tk), lambda i, j, kk: (i, kk))
        b_spec = pl.BlockSpec((None, tn, tk), lambda i, j, kk: (kk // kps, j, kk % kps))
        o_spec = pl.BlockSpec((tm, tn), lambda i, j, kk: (i, j))
        o_shape = jax.ShapeDtypeStruct((m, n), out_dtype)
        dims = NT
    else:
        raise ValueError(mode)
    nk = grid[2]

    def body(*refs):
        a_ref, b_ref = refs[0], refs[1]
        add_ref = refs[2] if add is not None else None
        o_ref = refs[2 + (add is not None)]

        def finish(r):
            if add is not None:
                r = r + add_ref[...]
            o_ref[...] = r.astype(o_ref.dtype)

        part = _dot(a_ref[...].astype(BF16), b_ref[...].astype(BF16), dims)
        if nk == 1:
            finish(part)
            return
        acc = refs[-1]
        kk = pl.program_id(2)

        @pl.when(kk == 0)
        def _():
            acc[...] = part

        @pl.when(kk > 0)
        def _():
            acc[...] += part

        @pl.when(kk == nk - 1)
        def _():
            finish(acc[...])

    in_specs = [a_spec, b_spec]
    args = [a, b]
    if add is not None:
        in_specs.append(o_spec)
        args.append(add)
    return pl.pallas_call(
        body, name=name, grid=grid, in_specs=in_specs, out_specs=o_spec, out_shape=o_shape,
        scratch_shapes=[pltpu.VMEM((tm, tn), F32)] if nk > 1 else [],
        compiler_params=_cparams("parallel", "parallel", "arbitrary"),
    )(*args)


def _matmul_tn(name, a, b, g, out_dtype):
    k, m = a.shape
    _, n = b.shape
    ns = n // g
    tm, tn, tk = _pick(m, (2048, 1408) + TILE_M), _pick(ns, TILE_N), _pick(k, (1024, 512, 256, 128))
    nps = ns // tn
    nk = k // tk

    def body(a_ref, b_ref, o_ref, acc):
        kk = pl.program_id(2)
        part = _dot(a_ref[...].astype(BF16), b_ref[...].astype(BF16), TN)

        @pl.when(kk == 0)
        def _():
            acc[...] = part

        @pl.when(kk > 0)
        def _():
            acc[...] += part

        @pl.when(kk == nk - 1)
        def _():
            o_ref[...] = acc[...].astype(o_ref.dtype)

    return pl.pallas_call(
        body, name=name, grid=(m // tm, n // tn, nk),
        in_specs=[pl.BlockSpec((tk, tm), lambda i, j, kk: (kk, i)), pl.BlockSpec((tk, tn), lambda i, j, kk: (kk, j))],
        out_specs=pl.BlockSpec((None, tm, tn), lambda i, j, kk: (j // nps, i, j % nps)),
        out_shape=jax.ShapeDtypeStruct((g, m, ns), out_dtype),
        scratch_shapes=[pltpu.VMEM((tm, tn), F32)],
        compiler_params=_cparams("parallel", "parallel", "arbitrary"),
    )(a, b)


def _rmsnorm_fwd(name, x, g):
    s, d = x.shape
    tr = _pick(s, (256, 128))

    def body(x_ref, g_ref, u_ref, ut_ref):
        xv = x_ref[...]
        r = lax.rsqrt(jnp.mean(xv * xv, axis=-1, keepdims=True) + EPS)
        u = xv * r * g_ref[...]
        u_ref[...] = u.astype(BF16)
        ut_ref[...] = u.T.astype(BF16)

    return pl.pallas_call(
        body, name=name, grid=(s // tr,),
        in_specs=[pl.BlockSpec((tr, d), lambda i: (i, 0)), pl.BlockSpec((1, d), lambda i: (0, 0))],
        out_specs=[pl.BlockSpec((tr, d), lambda i: (i, 0)), pl.BlockSpec((d, tr), lambda i: (0, i))],
        out_shape=[jax.ShapeDtypeStruct((s, d), BF16), jax.ShapeDtypeStruct((d, s), BF16)],
        compiler_params=_cparams("parallel"),
    )(x, g)


def _rmsnorm_bwd(name, x, g, du, extra):
    s, d = x.shape
    tr = _pick(s, (256, 128, 64, 32, 16, 8))

    def body(x_ref, g_ref, du_ref, e_ref, dx_ref, dg_ref):
        i = pl.program_id(0)
        xv = x_ref[...]
        r = lax.rsqrt(jnp.mean(xv * xv, axis=-1, keepdims=True) + EPS)
        n = xv * r
        dy = du_ref[...]
        a = dy * g_ref[...]
        dx = r * (a - n * jnp.mean(a * n, axis=-1, keepdims=True))
        dx_ref[...] = e_ref[...] + dx

        @pl.when(i == 0)
        def _():
            dg_ref[...] = jnp.zeros_like(dg_ref)

        dg_ref[...] += jnp.sum((dy * n).reshape(tr // SUB, SUB, d), axis=0)

    return pl.pallas_call(
        body, name=name, grid=(s // tr,),
        in_specs=[pl.BlockSpec((tr, d), lambda i: (i, 0)), pl.BlockSpec((1, d), lambda i: (0, 0)),
                  pl.BlockSpec((tr, d), lambda i: (i, 0)), pl.BlockSpec((tr, d), lambda i: (i, 0))],
        out_specs=[pl.BlockSpec((tr, d), lambda i: (i, 0)), pl.BlockSpec((SUB, d), lambda i: (0, 0))],
        out_shape=[jax.ShapeDtypeStruct((s, d), F32), jax.ShapeDtypeStruct((SUB, d), F32)],
        compiler_params=_cparams("arbitrary"),
    )(x, g, du, extra)


def _head_norm(x, g):
    r = lax.rsqrt(jnp.mean(x * x, axis=-1, keepdims=True) + EPS)
    return x * r * g


def _qk_norm_fwd(proj, g_q, g_k, nh, blk):
    s = proj.shape[0]

    def body(q_ref, k_ref, v_ref, gq_ref, gk_ref, qn_ref, kn_ref, vb_ref, kt_ref, vt_ref):
        qn_ref[...] = _head_norm(q_ref[...], gq_ref[...]).astype(BF16)
        kn = _head_norm(k_ref[...], gk_ref[...])
        kn_ref[...] = kn.astype(BF16)
        kt_ref[...] = kn.T.astype(BF16)
        v = v_ref[...]
        vb_ref[...] = v.astype(BF16)
        vt_ref[...] = v.T.astype(BF16)

    col = lambda base: pl.BlockSpec((blk, HEAD), lambda i, h: (i, base + h))
    gs = pl.BlockSpec((1, HEAD), lambda i, h: (0, 0))
    o = pl.BlockSpec((blk, HEAD), lambda i, h: (i, h))
    t = pl.BlockSpec((None, HEAD, blk), lambda i, h: (i, h, 0))
    sh = jax.ShapeDtypeStruct((s, nh * HEAD), BF16)
    tsh = jax.ShapeDtypeStruct((s // blk, nh * HEAD, blk), BF16)
    return pl.pallas_call(
        body, name="qk_norm_fwd", grid=(s // blk, nh),
        in_specs=[col(0), col(nh), col(2 * nh), gs, gs], out_specs=[o, o, o, t, t], out_shape=[sh, sh, sh, tsh, tsh],
        compiler_params=_cparams("parallel", "parallel"),
    )(proj, proj, proj, g_q, g_k)


def _qk_norm_bwd(proj, g_q, g_k, dqn, dkn, nh):
    s = proj.shape[0]
    tr = _pick(s, (512, 256, 128, 64, 32, 16, 8))

    def one(x, g, dy):
        r = lax.rsqrt(jnp.mean(x * x, axis=-1, keepdims=True) + EPS)
        n = x * r
        a = dy * g
        dx = r * (a - n * jnp.mean(a * n, axis=-1, keepdims=True))
        return dx, jnp.sum((dy * n).reshape(tr // SUB, SUB, HEAD), axis=0)

    def body(q_ref, k_ref, gq_ref, gk_ref, dqn_ref, dkn_ref, dq_ref, dk_ref, dgq_ref, dgk_ref):
        first = (pl.program_id(0) == 0) & (pl.program_id(1) == 0)

        @pl.when(first)
        def _():
            dgq_ref[...] = jnp.zeros_like(dgq_ref)
            dgk_ref[...] = jnp.zeros_like(dgk_ref)

        dq, pq = one(q_ref[...], gq_ref[...], dqn_ref[...])
        dk, pk = one(k_ref[...], gk_ref[...], dkn_ref[...])
        dq_ref[...] = dq.astype(BF16)
        dk_ref[...] = dk.astype(BF16)
        dgq_ref[...] += pq
        dgk_ref[...] += pk

    col = lambda base: pl.BlockSpec((tr, HEAD), lambda i, h: (i, base + h))
    gs = pl.BlockSpec((1, HEAD), lambda i, h: (0, 0))
    o = pl.BlockSpec((tr, HEAD), lambda i, h: (i, h))
    part = pl.BlockSpec((SUB, HEAD), lambda i, h: (0, 0))
    sh = jax.ShapeDtypeStruct((s, nh * HEAD), BF16)
    psh = jax.ShapeDtypeStruct((SUB, HEAD), F32)
    return pl.pallas_call(
        body, name="qk_norm_bwd", grid=(s // tr, nh),
        in_specs=[col(0), col(nh), gs, gs, o, o], out_specs=[o, o, part, part], out_shape=[sh, sh, psh, psh],
        compiler_params=_cparams("arbitrary", "arbitrary"),
    )(proj, proj, g_q, g_k, dqn, dkn)


def _sb_consts(blk, hp):
    upper = _tri(blk, "ge")
    row = lax.broadcasted_iota(jnp.int32, (blk, hp * blk), 0)
    col = lax.broadcasted_iota(jnp.int32, (blk, hp * blk), 1)
    strict = row < col
    for h in range(1, hp):
        strict = strict & ((col < h * blk) | (row < col - h * blk))
    return upper, strict


def _sb_log_keep(zt, strict):
    l = jnp.minimum(-zt, 0.0) - jnp.log(1.0 + jnp.exp(-jnp.abs(zt)))
    return l if strict is None else jnp.where(strict, l, 0.0)


SB_GROUP = 4
SB_GROUP_BWD = 4


def _sb_heads(nh):
    return 2 if nh % 2 == 0 else 1


def _sb_attn_fwd(qn, kn, vt, nh, blk):
    s = qn.shape[0]
    nb = s // blk
    scale = HEAD ** -0.5
    hp = 4 if nh % 4 == 0 else _sb_heads(nh)

    def body(q_ref, k_ref, vt_ref, y_ref, c_ref):
        qi = pl.program_id(1)
        suffix, strict = _sb_consts(blk, hp)
        qs = [q_ref[:, h * HEAD:(h + 1) * HEAD] for h in range(hp)]

        def logits(kb_i):
            off = pl.multiple_of(kb_i * blk, blk)
            return jnp.concatenate(
                [_dot(k_ref[pl.ds(off, blk), h * HEAD:(h + 1) * HEAD], qs[h], NT) for h in range(hp)], axis=1) * scale

        def sums(zt, mask):
            return _dot(suffix, _sb_log_keep(zt, mask).astype(BF16), NN)

        def weights(kb_i, zt, cum, cr, mask):
            for h in range(hp):
                c_ref[h, kb_i] = cr[:, h * blk:(h + 1) * blk]
            wt = jnp.exp(zt + cum + cr)
            if mask is not None:
                wt = jnp.where(mask, wt, 0.0)
            return wt.astype(BF16), cr + cum[0:1, :]

        def add_values(kb_i, wt, accs):
            return tuple(
                accs[h] + _dot(vt_ref[kb_i, h * HEAD:(h + 1) * HEAD, :], wt[:, h * blk:(h + 1) * blk], NN)
                for h in range(hp))

        def group(kbs, masks, accs, cr):
            zts = [logits(k) for k in kbs]
            cums = [sums(zt, m) for zt, m in zip(zts, masks)]
            for k, zt, cum, m in zip(kbs, zts, cums, masks):
                wt, cr = weights(k, zt, cum, cr, m)
                accs = add_values(k, wt, accs)
            return accs, cr

        accs = tuple(jnp.zeros((HEAD, blk), F32) for _ in range(hp))
        accs, cr = group([qi], [strict], accs, jnp.zeros((1, hp * blk), F32))
        n_groups = qi // SB_GROUP

        def many(g, st):
            top = qi - 1 - g * SB_GROUP
            return group([top - j for j in range(SB_GROUP)], [None] * SB_GROUP, *st)

        def one(r, st):
            return group([qi - 1 - n_groups * SB_GROUP - r], [None], *st)

        st = lax.fori_loop(0, n_groups, many, (accs, cr))
        accs, _ = lax.fori_loop(0, qi - n_groups * SB_GROUP, one, st)
        for h in range(hp):
            y_ref[:, h * HEAD:(h + 1) * HEAD] = accs[h].T.astype(y_ref.dtype)

    qs_ = pl.BlockSpec((blk, hp * HEAD), lambda h, i: (i, h))
    full = pl.BlockSpec((s, hp * HEAD), lambda h, i: (0, h), pipeline_mode=pl.Buffered(1))
    return pl.pallas_call(
        body, name="sb_attn_fwd", grid=(nh // hp, nb),
        in_specs=[qs_, full, pl.BlockSpec((nb, hp * HEAD, blk), lambda h, i: (0, h, 0), pipeline_mode=pl.Buffered(1))],
        out_specs=[qs_, pl.BlockSpec((hp, nb, 1, blk), lambda h, i: (h, 0, 0, i))],
        out_shape=[jax.ShapeDtypeStruct((s, nh * HEAD), BF16), jax.ShapeDtypeStruct((nh, nb, 1, s), F32)],
        compiler_params=_cparams("parallel", "arbitrary"),
    )(qn, kn, vt)


def _sb_attn_bwd(qn, kn, kt, vb, dy, carries, nh, blk):
    s = qn.shape[0]
    nb = s // blk
    scale = HEAD ** -0.5
    hp = _sb_heads(nh)

    def body(q_ref, k_ref, kt_ref, v_ref, dy_ref, c_ref, dq_ref, dk_ref, dv_ref):
        qi = pl.program_id(1)

        @pl.when(qi == 0)
        def _():
            dk_ref[...] = jnp.zeros_like(dk_ref)
            dv_ref[...] = jnp.zeros_like(dv_ref)

        suffix, strict = _sb_consts(blk, hp)
        prefix = _tri(blk, "lt")
        qs = [q_ref[:, h * HEAD:(h + 1) * HEAD] for h in range(hp)]
        dos = [dy_ref[:, h * HEAD:(h + 1) * HEAD].astype(BF16) for h in range(hp)]

        def logits(kb_i):
            off = pl.multiple_of(kb_i * blk, blk)
            return jnp.concatenate(
                [_dot(k_ref[pl.ds(off, blk), h * HEAD:(h + 1) * HEAD], qs[h], NT) for h in range(hp)], axis=1) * scale

        def group(kbs, masks, dqs, ec):
            rows = [pl.ds(pl.multiple_of(k * blk, blk), blk) for k in kbs]
            zts = [logits(k) for k in kbs]
            dws = [jnp.concatenate([_dot(v_ref[r, h * HEAD:(h + 1) * HEAD], dos[h], NT) for h in range(hp)], axis=1)
                   for r in rows]
            ls = [_sb_log_keep(zt, m) for zt, m in zip(zts, masks)]
            cums = [_dot(suffix, l.astype(BF16), NN) for l in ls]
            wts, ets, befores = [], [], []
            for k, zt, cum, dw, m in zip(kbs, zts, cums, dws, masks):
                cr = jnp.concatenate([c_ref[h, k] for h in range(hp)], axis=1)
                wt = jnp.exp(zt + cum + cr)
                if m is not None:
                    wt = jnp.where(m, wt, 0.0)
                et = wt * dw
                befores.append(_dot(prefix, et.astype(BF16), NN) + ec)
                ec = ec + jnp.sum(et, axis=0, keepdims=True)
                wts.append(wt.astype(BF16))
                ets.append(et)
            for k, r, l, et, before, wtb in zip(kbs, rows, ls, ets, befores, wts):
                dzt = ((jnp.exp(l) * (et + before) - before) * scale).astype(BF16)
                out = []
                for h in range(hp):
                    cols, part = slice(h * HEAD, (h + 1) * HEAD), slice(h * blk, (h + 1) * blk)
                    dv_ref[r, cols] += _dot(wtb[:, part], dos[h], NN)
                    dk_ref[r, cols] += _dot(dzt[:, part], qs[h], NN)
                    out.append(dqs[h] + _dot(kt_ref[k, cols, :], dzt[:, part], NN))
                dqs = tuple(out)
            return dqs, ec

        n_groups = qi // SB_GROUP_BWD

        def many(g, st):
            return group([g * SB_GROUP_BWD + j for j in range(SB_GROUP_BWD)], [None] * SB_GROUP_BWD, *st)

        def one(r, st):
            return group([n_groups * SB_GROUP_BWD + r], [None], *st)

        st = (tuple(jnp.zeros((HEAD, blk), F32) for _ in range(hp)), jnp.zeros((1, hp * blk), F32))
        st = lax.fori_loop(0, n_groups, many, st)
        st = lax.fori_loop(0, qi - n_groups * SB_GROUP_BWD, one, st)
        dqs, _ = group([qi], [strict], *st)
        for h in range(hp):
            dq_ref[:, h * HEAD:(h + 1) * HEAD] = dqs[h].T

    qs_ = pl.BlockSpec((blk, hp * HEAD), lambda h, i: (i, h))
    full = pl.BlockSpec((s, hp * HEAD), lambda h, i: (0, h), pipeline_mode=pl.Buffered(1))
    sh = jax.ShapeDtypeStruct((s, nh * HEAD), F32)
    return pl.pallas_call(
        body, name="sb_attn_bwd", grid=(nh // hp, nb),
        in_specs=[qs_, full, pl.BlockSpec((nb, hp * HEAD, blk), lambda h, i: (0, h, 0), pipeline_mode=pl.Buffered(1)),
                  full, qs_, pl.BlockSpec((hp, nb, 1, blk), lambda h, i: (h, 0, 0, i))],
        out_specs=[qs_, full, full], out_shape=[sh, sh, sh],
        compiler_params=_cparams("parallel", "arbitrary"),
    )(qn, kn, kt, vb, dy, carries)


def _lower_bound(lb_logits):
    def body(l_ref, o_ref):
        l = l_ref[...]
        m = jnp.max(l, axis=0, keepdims=True)
        e = jnp.exp(l - m)
        o_ref[...] = e[0:1, :] / jnp.sum(e, axis=0, keepdims=True)

    return pl.pallas_call(body, name="lower_bound", out_shape=jax.ShapeDtypeStruct((1, lb_logits.shape[1]), F32))(lb_logits)


def _hg_gates(hq, hf, lb):
    sq = _sig(hq)
    q = hq * sq
    sf = _sig(hf)
    f = lb + (1.0 - lb) * sf
    return q, sq, f, sf


def _hg_cum(g, c):
    hi, lo = _split(g)
    t = _tri(c, "le")
    return _dot(t, hi, NN) + _dot(t, lo, NN)


def _hg_heads(nh):
    return 8 if nh % 8 == 0 else 4 if nh % 4 == 0 else 2 if nh % 2 == 0 else 1


def _head_cols(x, h):
    return x[:, h * HEAD:(h + 1) * HEAD]


def _per_head(x, hp, fn):
    return jnp.concatenate([fn(_head_cols(x, h), h) for h in range(hp)], axis=1)


def _head_sums(x, hp):
    return [jnp.sum(_head_cols(x, h), axis=1, keepdims=True) for h in range(hp)]


def _head_scale(cols, x, hp):
    return jnp.concatenate([cols[h] * _head_cols(x, h) for h in range(hp)], axis=1)


def _row_mask(r, width):
    return lax.broadcasted_iota(jnp.int32, (SUB, width), 0) >= r


def _hg_intra_fwd(q, k, v, b, c, hp):
    outs = []
    for bi in range(c // SUB):
        q_i, b_i = q[bi * SUB:(bi + 1) * SUB], b[bi * SUB:(bi + 1) * SUB]
        acc = jnp.zeros((SUB, hp * HEAD), F32)
        for s in range((bi + 1) * SUB):
            d = b_i - b[s:s + 1]
            if s >= bi * SUB:
                d = jnp.where(_row_mask(s - bi * SUB, hp * HEAD), d, -jnp.inf)
            acc = acc + _head_scale(_head_sums(q_i * k[s:s + 1] * jnp.exp(d), hp), v[s:s + 1], hp)
        outs.append(acc)
    return jnp.concatenate(outs, axis=0)


def _hg_intra_bwd(q, k, v, b, do, c, hp, dq_scr, dk_scr, dv_scr):
    nblk = c // SUB
    dq_scr[...] = jnp.zeros_like(dq_scr)
    for s in range(c):
        bj = s // SUB
        ks, vs, bs = k[s:s + 1], v[s:s + 1], b[s:s + 1]
        acc_k = jnp.zeros((SUB, hp * HEAD), F32)
        acc_v = jnp.zeros((SUB, hp * HEAD), F32)
        for bi in range(bj, nblk):
            sl = slice(bi * SUB, (bi + 1) * SUB)
            d = b[sl] - bs
            if bi == bj:
                d = jnp.where(_row_mask(s - bj * SUB, hp * HEAD), d, -jnp.inf)
            dec = jnp.exp(d)
            qd = q[sl] * dec
            col = _head_sums(qd * ks, hp)
            dcol = _head_sums(do[sl] * vs, hp)
            dq_scr[sl, :] += _head_scale(dcol, ks * dec, hp)
            acc_k = acc_k + _head_scale(dcol, qd, hp)
            acc_v = acc_v + _head_scale(col, do[sl], hp)
        dk_scr[s:s + 1, :] = jnp.sum(acc_k, axis=0, keepdims=True)
        dv_scr[s:s + 1, :] = jnp.sum(acc_v, axis=0, keepdims=True)


def _hgrn2_fwd(proj, lb, g_out, nh, base, c):
    s = proj.shape[0]
    nch = s // c
    hp = _hg_heads(nh)
    wide = hp * HEAD

    def body(hq_ref, hf_ref, hi_ref, og_ref, lb_ref, g_ref, o_ref, y_ref, st_ref, st):
        @pl.when(pl.program_id(1) == 0)
        def _():
            st[...] = jnp.zeros_like(st)

        st_in = [st[h] for h in range(hp)]
        for h in range(hp):
            st_ref[h] = st_in[h]
        q, _, f, _ = _hg_gates(hq_ref[...], hf_ref[...], lb_ref[...])
        k = 1.0 - f
        v = hi_ref[...]
        b = _hg_cum(jnp.log(f), c)
        bl = b[c - 1:c, :]
        qe = (q * jnp.exp(b)).astype(BF16)
        o = _per_head(qe, hp, lambda x, h: _dot(x, st_in[h].astype(BF16), NT)) + _hg_intra_fwd(q, k, v, b, c, hp)
        kd = (k * jnp.exp(bl - b)).astype(BF16)
        vb = v.astype(BF16)
        keep = jnp.exp(bl)
        for h in range(hp):
            st[h] = st_in[h] * _head_cols(keep, h) + _dot(_head_cols(vb, h), _head_cols(kd, h), TN)
        o_ref[...] = o
        og = og_ref[...]
        gout = g_ref[...]
        y_ref[...] = (_per_head(o, hp, lambda x, h: _head_norm(x, gout)) * (og * _sig(og))).astype(BF16)

    col = lambda j: pl.BlockSpec((c, wide), lambda g, i: (i, (base + j * nh) // hp + g))
    row = pl.BlockSpec((1, wide), lambda g, i: (0, g))
    gs = pl.BlockSpec((1, HEAD), lambda g, i: (0, 0))
    o = pl.BlockSpec((c, wide), lambda g, i: (i, g))
    return pl.pallas_call(
        body, name="hgrn2_fwd", grid=(nh // hp, nch),
        in_specs=[col(0), col(1), col(2), col(3), row, gs],
        out_specs=[o, o, pl.BlockSpec((hp, None, HEAD, HEAD), lambda g, i: (g, i, 0, 0))],
        out_shape=[jax.ShapeDtypeStruct((s, nh * HEAD), F32), jax.ShapeDtypeStruct((s, nh * HEAD), BF16),
                   jax.ShapeDtypeStruct((nh, nch, HEAD, HEAD), F32)],
        scratch_shapes=[pltpu.VMEM((hp, HEAD, HEAD), F32)],
        compiler_params=_cparams("parallel", "arbitrary"),
    )(proj, proj, proj, proj, lb, g_out)


def _hgrn2_bwd(proj, lb, g_out, o_pre, states, dy, nh, base, c):
    s = proj.shape[0]
    nch = s // c
    hp = _hg_heads(nh)
    wide = hp * HEAD

    def fold(x):
        return jnp.sum(x.reshape(c // SUB, SUB, x.shape[1]), axis=0)

    def body(hq_ref, hf_ref, hi_ref, og_ref, lb_ref, g_ref, o_ref, st_ref, se_ref, dy_ref,
             dhq_ref, dhf_ref, dhi_ref, dog_ref, dg_ref, dlb_ref, dst, dq_scr, dk_scr, dv_scr):
        g, i = pl.program_id(0), pl.program_id(1)

        @pl.when(i == 0)
        def _():
            dst[...] = jnp.zeros_like(dst)
            dlb_ref[...] = jnp.zeros_like(dlb_ref)

        @pl.when((i == 0) & (g == 0))
        def _():
            dg_ref[...] = jnp.zeros_like(dg_ref)

        lbv = lb_ref[...]
        hq, hf = hq_ref[...], hf_ref[...]
        q, sq, f, sf = _hg_gates(hq, hf, lbv)
        k = 1.0 - f
        v = hi_ref[...]
        b = _hg_cum(jnp.log(f), c)
        bl = b[c - 1:c, :]
        eb = jnp.exp(b)
        ebl = jnp.exp(bl - b)

        o = o_ref[...]
        gout = g_ref[...]
        og = og_ref[...]
        sg = _sig(og)
        r = _per_head(o, hp, lambda x, h: jnp.broadcast_to(
            lax.rsqrt(jnp.mean(x * x, axis=-1, keepdims=True) + EPS), x.shape))
        gw = jnp.concatenate([gout] * hp, axis=1)
        n = o * r
        dyv = dy_ref[...]
        dn = dyv * (og * sg)
        dog_ref[...] = (dyv * n * gw * (sg * (1.0 + og * (1.0 - sg)))).astype(BF16)
        dnn = fold(dn * n)
        part = _head_cols(dnn, 0)
        for h in range(1, hp):
            part = part + _head_cols(dnn, h)
        dg_ref[...] += part
        a = dn * gw
        an = a * n
        do = r * (a - n * _per_head(an, hp, lambda x, h: jnp.broadcast_to(jnp.mean(x, axis=-1, keepdims=True), x.shape)))

        st_in = [st_ref[h].astype(BF16) for h in range(hp)]
        dstv = [dst[h] for h in range(hp)]
        dstb = [d.astype(BF16) for d in dstv]
        dob = do.astype(BF16)
        vb = v.astype(BF16)
        kdb = (k * ebl).astype(BF16)
        qeb = (q * eb).astype(BF16)
        _hg_intra_bwd(q, k, v, b, do, c, hp, dq_scr, dk_scr, dv_scr)
        dq = dq_scr[...] + eb * _per_head(dob, hp, lambda x, h: _dot(x, st_in[h], NN))
        dk = dk_scr[...] + ebl * _per_head(vb, hp, lambda x, h: _dot(x, dstb[h], NN))
        dv = dv_scr[...] + _per_head(kdb, hp, lambda x, h: _dot(x, dstb[h], NT))
        keep = jnp.exp(bl)
        for h in range(hp):
            dst[h] = dstv[h] * _head_cols(keep, h) + _dot(_head_cols(dob, h), _head_cols(qeb, h), TN)

        hi_, lo_ = _split(q * dq - k * dk)
        rev = _tri(c, "ge")
        later = jnp.concatenate([jnp.sum(dstv[h] * se_ref[h], axis=0, keepdims=True) for h in range(hp)], axis=1)
        dg = _dot(rev, hi_, NN) + _dot(rev, lo_, NN) + jnp.where(i > 0, later, 0.0)
        df = dg / f - dk
        dhq_ref[...] = (dq * (sq * (1.0 + hq * (1.0 - sq)))).astype(BF16)
        dhf_ref[...] = (df * (1.0 - lbv) * sf * (1.0 - sf)).astype(BF16)
        dhi_ref[...] = dv.astype(BF16)
        dlb_ref[...] += fold(df * (1.0 - sf))

    rv = lambda i: nch - 1 - i
    col = lambda j: pl.BlockSpec((c, wide), lambda g, i: (rv(i), (base + j * nh) // hp + g))
    row = pl.BlockSpec((1, wide), lambda g, i: (0, g))
    gs = pl.BlockSpec((1, HEAD), lambda g, i: (0, 0))
    o = pl.BlockSpec((c, wide), lambda g, i: (rv(i), g))
    st = pl.BlockSpec((hp, None, HEAD, HEAD), lambda g, i: (g, rv(i), 0, 0))
    se = pl.BlockSpec((hp, None, HEAD, HEAD), lambda g, i: (g, jnp.minimum(rv(i) + 1, nch - 1), 0, 0))
    sh = jax.ShapeDtypeStruct((s, nh * HEAD), BF16)
    return pl.pallas_call(
        body, name="hgrn2_bwd", grid=(nh // hp, nch),
        in_specs=[col(0), col(1), col(2), col(3), row, gs, o, st, se, o],
        out_specs=[o, o, o, o, pl.BlockSpec((SUB, HEAD), lambda g, i: (0, 0)), pl.BlockSpec((SUB, wide), lambda g, i: (0, g))],
        out_shape=[sh, sh, sh, sh, jax.ShapeDtypeStruct((SUB, HEAD), F32), jax.ShapeDtypeStruct((SUB, nh * HEAD), F32)],
        scratch_shapes=[pltpu.VMEM((hp, HEAD, HEAD), F32), pltpu.VMEM((c, wide), F32), pltpu.VMEM((c, wide), F32),
                        pltpu.VMEM((c, wide), F32)],
        compiler_params=_cparams("arbitrary", "arbitrary"),
    )(proj, proj, proj, proj, lb, g_out, o_pre, states, states, dy)


def _merge_tiles(s, d, gate_col):
    tr = _pick(s, (256, 128, 64, 32, 16, 8))
    tc = 128
    for cand in (512, 256):
        if d % cand == 0 and gate_col % cand == 0:
            tc = cand
            break
    return tr, tc


def _merge_fwd(proj, ya, yb, gate_col):
    s, d = ya.shape
    tr, tc = _merge_tiles(s, d, gate_col)
    ga0, gb0 = gate_col // tc, (gate_col + d) // tc

    def body(ga_ref, gb_ref, ya_ref, yb_ref, m_ref):
        m_ref[...] = (_sig(ga_ref[...]) * ya_ref[...] + _sig(gb_ref[...]) * yb_ref[...]).astype(BF16)

    o = pl.BlockSpec((tr, tc), lambda i, j: (i, j))
    return pl.pallas_call(
        body, name="merge_fwd", grid=(s // tr, d // tc),
        in_specs=[pl.BlockSpec((tr, tc), lambda i, j: (i, ga0 + j)), pl.BlockSpec((tr, tc), lambda i, j: (i, gb0 + j)), o, o],
        out_specs=o, out_shape=jax.ShapeDtypeStruct((s, d), BF16),
        compiler_params=_cparams("parallel", "parallel"),
    )(proj, proj, ya, yb)


def _merge_bwd(proj, ya, yb, dm, gate_col):
    s, d = ya.shape
    tr, tc = _merge_tiles(s, d, gate_col)
    ga0, gb0 = gate_col // tc, (gate_col + d) // tc

    def body(ga_ref, gb_ref, ya_ref, yb_ref, dm_ref, dya_ref, dyb_ref, dga_ref, dgb_ref):
        dmv = dm_ref[...]
        sa, sb = _sig(ga_ref[...]), _sig(gb_ref[...])
        dya_ref[...] = (dmv * sa).astype(BF16)
        dyb_ref[...] = (dmv * sb).astype(BF16)
        dga_ref[...] = (dmv * ya_ref[...] * sa * (1.0 - sa)).astype(BF16)
        dgb_ref[...] = (dmv * yb_ref[...] * sb * (1.0 - sb)).astype(BF16)

    o = pl.BlockSpec((tr, tc), lambda i, j: (i, j))
    sh = jax.ShapeDtypeStruct((s, d), BF16)
    return pl.pallas_call(
        body, name="merge_bwd", grid=(s // tr, d // tc),
        in_specs=[pl.BlockSpec((tr, tc), lambda i, j: (i, ga0 + j)), pl.BlockSpec((tr, tc), lambda i, j: (i, gb0 + j)), o, o, o],
        out_specs=[o, o, o, o], out_shape=[sh, sh, sh, sh],
        compiler_params=_cparams("parallel", "parallel"),
    )(proj, proj, ya, yb, dm)


CONV_ROWS = 512


def _conv_ext(ref, i, rows, s, before, after):
    parts = []
    if before:
        p = ref[pl.ds(pl.multiple_of(jnp.maximum(i * rows - before, 0), SUB), before), :]
        parts.append(jnp.where(i > 0, p, 0.0))
    parts.append(ref[pl.ds(pl.multiple_of(i * rows, SUB), rows), :])
    if after:
        nxt = ref[pl.ds(pl.multiple_of(jnp.minimum((i + 1) * rows, s - after), SUB), after), :]
        parts.append(jnp.where((i + 1) * rows < s, nxt, 0.0))
    return jnp.concatenate(parts, axis=0)


def _conv3(ext, w, bias):
    x1 = pltpu.roll(ext, 1, 0)
    x2 = pltpu.roll(ext, 2, 0)
    return bias + w[0:1, :] * x2 + w[1:2, :] * x1 + w[2:3, :] * ext, x1, x2


def _convffn_fwd(up, conv_w, conv_b, dff):
    s = up.shape[0]
    tc = HEAD
    nf = dff // tc
    rows = _pick(s, (CONV_ROWS, 256, 128, 64, 32, 16, 8))

    def body(ug_ref, uv_ref, wg_ref, wv_ref, bg_ref, bv_ref, a_ref):
        wg, wv, bg, bv = wg_ref[...], wv_ref[...], bg_ref[...], bv_ref[...]

        def step(i, _):
            g = _conv3(_conv_ext(ug_ref, i, rows, s, SUB, 0), wg, bg)[0][SUB:]
            v = _conv3(_conv_ext(uv_ref, i, rows, s, SUB, 0), wv, bv)[0][SUB:]
            a_ref[pl.ds(pl.multiple_of(i * rows, SUB), rows), :] = (g * _sig(g) * v).astype(BF16)
            return 0

        lax.fori_loop(0, s // rows, step, 0)

    cg = lambda r: pl.BlockSpec((r, tc), lambda j: (0, j))
    cv = lambda r: pl.BlockSpec((r, tc), lambda j: (0, nf + j))
    return pl.pallas_call(
        body, name="convffn_fwd", grid=(nf,),
        in_specs=[cg(s), cv(s), cg(3), cv(3), cg(1), cv(1)], out_specs=cg(s),
        out_shape=jax.ShapeDtypeStruct((s, dff), BF16),
        compiler_params=_cparams("parallel"),
    )(up, up, conv_w, conv_w, conv_b, conv_b)


def _convffn_bwd(up, conv_w, conv_b, dact, dff):
    s = up.shape[0]
    tc = HEAD
    nf = dff // tc
    rows = _pick(s, (CONV_ROWS, 256, 128, 64, 32, 16, 8))
    n_ext = rows + SUB

    def body(ug_ref, uv_ref, wg_ref, wv_ref, bg_ref, bv_ref, da_ref,
             dug_ref, duv_ref, dwg_ref, dwv_ref, dbg_ref, dbv_ref):
        wg, wv, bg, bv = wg_ref[...], wv_ref[...], bg_ref[...], bv_ref[...]

        def fold(x):
            return jnp.sum(x.reshape(rows // SUB, SUB, tc), axis=0)

        def one(ext, x1, x2, d_ext, w):
            d1 = pltpu.roll(d_ext, n_ext - 1, 0)[:rows]
            d2 = pltpu.roll(d_ext, n_ext - 2, 0)[:rows]
            dc = d_ext[:rows]
            du = w[2:3, :] * dc + w[1:2, :] * d1 + w[0:1, :] * d2
            sl = slice(SUB, SUB + rows)
            return du, (fold(dc * x2[sl]), fold(dc * x1[sl]), fold(dc * ext[sl]), fold(dc))

        def step(i, acc):
            eg = _conv_ext(ug_ref, i, rows, s, SUB, SUB)
            ev = _conv_ext(uv_ref, i, rows, s, SUB, SUB)
            g, g1, g2 = _conv3(eg, wg, bg)
            v, v1, v2 = _conv3(ev, wv, bv)
            g, v = g[SUB:], v[SUB:]
            da = _conv_ext(da_ref, i, rows, s, 0, SUB)
            sg = _sig(g)
            dg = da * v * (sg * (1.0 + g * (1.0 - sg)))
            dv = da * (g * sg)
            dug, pg = one(eg, g1, g2, dg, wg)
            duv, pv = one(ev, v1, v2, dv, wv)
            at = pl.ds(pl.multiple_of(i * rows, SUB), rows)
            dug_ref[at, :] = dug.astype(BF16)
            duv_ref[at, :] = duv.astype(BF16)
            return tuple(a + p for a, p in zip(acc, pg + pv))

        zero = jnp.zeros((SUB, tc), F32)
        acc = lax.fori_loop(0, s // rows, step, (zero,) * 8)
        red = [jnp.sum(a, axis=0, keepdims=True) for a in acc]
        for j in range(3):
            dwg_ref[j:j + 1, :] = red[j]
            dwv_ref[j:j + 1, :] = red[4 + j]
        dbg_ref[...] = red[3]
        dbv_ref[...] = red[7]

    cg = lambda r: pl.BlockSpec((r, tc), lambda j: (0, j))
    cv = lambda r: pl.BlockSpec((r, tc), lambda j: (0, nf + j))
    outs = pl.pallas_call(
        body, name="convffn_bwd", grid=(nf,),
        in_specs=[cg(s), cv(s), cg(3), cv(3), cg(1), cv(1), cg(s)],
        out_specs=[cg(s), cg(s), cg(3), cg(3), cg(1), cg(1)],
        out_shape=[jax.ShapeDtypeStruct((s, dff), BF16), jax.ShapeDtypeStruct((s, dff), BF16),
                   jax.ShapeDtypeStruct((3, dff), F32), jax.ShapeDtypeStruct((3, dff), F32),
                   jax.ShapeDtypeStruct((1, dff), F32), jax.ShapeDtypeStruct((1, dff), F32)],
        compiler_params=_cparams("parallel"),
    )(up, up, conv_w, conv_w, conv_b, conv_b, dact)
    return outs


def _loss_head(out, target):
    s, d = out.shape
    tr = _pick(s, (256, 128, 64, 32, 16, 8))

    def body(o_ref, t_ref, d_ref, l_ref):
        @pl.when(pl.program_id(0) == 0)
        def _():
            l_ref[...] = jnp.zeros_like(l_ref)

        err = o_ref[...] - t_ref[...]
        d_ref[...] = err * (1.0 / d)
        sq = jnp.sum((err * err).reshape(tr // SUB, SUB, d), axis=0)
        part = sq[:, 0:HEAD]
        for j in range(1, d // HEAD):
            part = part + sq[:, j * HEAD:(j + 1) * HEAD]
        l_ref[...] += part

    blk = pl.BlockSpec((tr, d), lambda i: (i, 0))
    return pl.pallas_call(
        body, name="loss_head", grid=(s // tr,), in_specs=[blk, blk],
        out_specs=[blk, pl.BlockSpec((SUB, HEAD), lambda i: (0, 0))],
        out_shape=[jax.ShapeDtypeStruct((s, d), F32), jax.ShapeDtypeStruct((SUB, HEAD), F32)],
        compiler_params=_cparams("arbitrary"),
    )(out, target)


def _sum_rows(name, parts):
    def body(p_ref, o_ref):
        o_ref[...] = jnp.sum(p_ref[...], axis=0, keepdims=True)

    return pl.pallas_call(body, name=name, out_shape=jax.ShapeDtypeStruct((1, parts.shape[1]), F32))(parts)


def _local_step(x, target, g_mix, g_q, g_k, lb_logits, g_hg_out, g_ffn, conv_w, conv_b, first_weight, later_weights,
                grads_ready):
    s, d = x.shape
    nh = lb_logits.shape[1] // HEAD
    wid = nh * HEAD
    blk = _pick(s, (256, 128))
    chunk = _pick(s, (HG_CHUNK,))
    gate_col = 7 * wid

    u, u_t = _rmsnorm_fwd("rmsnorm_mix", x, g_mix)
    w_in = first_weight(u)
    proj = _matmul("in_proj", u, w_in, "nn", F32)
    qn, kn, vb, kt, vt = _qk_norm_fwd(proj, g_q, g_k, nh, blk)
    y_a, carries = _sb_attn_fwd(qn, kn, vt, nh, blk)
    lb = _lower_bound(lb_logits)
    o_pre, y_b, states = _hgrn2_fwd(proj, lb, g_hg_out, nh, 3 * nh, chunk)
    later = later_weights(o_pre)
    p_a, p_b, w_up = later["p_a"], later["p_b"], later["w_up"]
    w_o = later["w_o"].reshape(1, d, d)
    dff = later["w_down"].shape[1] * N_CHIPS
    w_down = later["w_down"].reshape(1, dff, d)
    ya_p = _matmul("proj_a", y_a, p_a, "nn", F32)
    yb_p = _matmul("proj_b", y_b, p_b, "nn", F32)
    m = _merge_fwd(proj, ya_p, yb_p, gate_col)
    h = _matmul("out_proj", m, w_o, "nn", F32, add=x)
    u2, u2_t = _rmsnorm_fwd("rmsnorm_ffn", h, g_ffn)
    up = _matmul("up_proj", u2, w_up, "nn", F32)
    act = _convffn_fwd(up, conv_w, conv_b, dff)
    out = _matmul("down_proj", act, w_down, "nn", F32, add=h)
    dout, sq = _loss_head(out, target)

    dact = _matmul("d_act", dout, w_down, "nt", F32)
    g_w_down = _matmul_tn("g_w_down", act, dout, 1, BF16).reshape(N_CHIPS, dff // N_CHIPS, d)
    dup_g, dup_v, dcw_g, dcw_v, dcb_g, dcb_v = _convffn_bwd(up, conv_w, conv_b, dact, dff)
    dup = jnp.concatenate([dup_g, dup_v], axis=1)
    g_w_up = _matmul("g_w_up", u2_t, dup[None], "nn", BF16, out_shards=N_CHIPS)
    sent = grads_ready(("w_down", "w_up"), [g_w_down, g_w_up])
    du2 = _matmul("d_u2", dup, w_up, "nt", F32)
    dh, pg_ffn = _rmsnorm_bwd("rmsnorm_ffn_bwd", h, g_ffn + sent, du2, dout)
    dm = _matmul("d_m", dh, w_o, "nt", F32)
    g_w_o = _matmul_tn("g_w_o", m, dh, 1, BF16).reshape(N_CHIPS, d // N_CHIPS, d)
    dya_p, dyb_p, dga, dgb = _merge_bwd(proj, ya_p, yb_p, dm, gate_col)
    g_p_a = _matmul_tn("g_p_a", y_a, dya_p, N_CHIPS, BF16)
    g_p_b = _matmul_tn("g_p_b", y_b, dyb_p, N_CHIPS, BF16)
    sent = grads_ready(("w_o", "p_a", "p_b"), [g_w_o, g_p_a, g_p_b])
    dy_a = _matmul("d_y_a", dya_p, p_a, "nt", F32)
    dy_b = _matmul("d_y_b", dyb_p, p_b, "nt", F32)
    dhq, dhf, dhi, dog, pg_hg, p_lb = _hgrn2_bwd(proj, lb, g_hg_out + sent, o_pre, states, dy_b, nh, 3 * nh, chunk)
    dqn, dkn, dv = _sb_attn_bwd(qn, kn, kt, vb, dy_a, carries, nh, blk)
    dq, dk, pg_q, pg_k = _qk_norm_bwd(proj, g_q, g_k, dqn, dkn, nh)
    dproj = jnp.concatenate([dq, dk, dv.astype(BF16), dhq, dhf, dhi, dog, dga, dgb], axis=1)
    g_w_in = _matmul("g_w_in", u_t, dproj[None], "nn", BF16, out_shards=N_CHIPS)
    sent = grads_ready(("w_in",), [g_w_in])
    du = _matmul("d_u", dproj, w_in, "nt", F32)
    dx, pg_mix = _rmsnorm_bwd("rmsnorm_mix_bwd", x, g_mix + sent, du, dh)

    small = dict(
        g_mix=_sum_rows("sum_g_mix", pg_mix), g_q=_sum_rows("sum_g_q", pg_q), g_k=_sum_rows("sum_g_k", pg_k),
        lb=_sum_rows("sum_lb", p_lb), g_hg_out=_sum_rows("sum_g_hg", pg_hg), g_ffn=_sum_rows("sum_g_ffn", pg_ffn),
        conv_w=jnp.concatenate([dcw_g, dcw_v], axis=1), conv_b=jnp.concatenate([dcb_g, dcb_v], axis=1),
        sq=_sum_rows("sum_sq", sq),
    )
    return dx, small, lb


ANY = pl.BlockSpec(memory_space=pl.ANY)


def _place():
    x, y, c = lax.axis_index("x"), lax.axis_index("y"), lax.axis_index("c")
    chips = [(1 - x, y), (x, 1 - y), (1 - x, 1 - y)]
    return x, y, c, chips


def _remote(src, dst, send_sem, recv_sem, to):
    return pltpu.make_async_remote_copy(src_ref=src, dst_ref=dst, send_sem=send_sem, recv_sem=recv_sem,
                                        device_id=to, device_id_type=MESH)


def _to_sibling(name, srcs):
    n = len(srcs)

    def body(*refs):
        ins, outs = refs[:n], refs[n:2 * n]
        send, recv = refs[2 * n:]
        x, y, c, _ = _place()
        cps = []
        for k in range(n):
            cp = _remote(ins[k], outs[k], send.at[k], recv.at[k], (x, y, 1 - c))
            cp.start()
            cps.append(cp)
        for cp in cps:
            cp.wait_recv()
        for cp in cps:
            cp.wait_send()

    return pl.pallas_call(
        body, name=name, in_specs=[ANY] * n, out_specs=[ANY] * n,
        out_shape=[jax.ShapeDtypeStruct(a.shape, a.dtype) for a in srcs],
        scratch_shapes=[pltpu.SemaphoreType.DMA((n,)), pltpu.SemaphoreType.DMA((n,))],
    )(*srcs)


HBM = pl.BlockSpec(memory_space=pltpu.HBM)
SEM = pl.BlockSpec(memory_space=pltpu.SEMAPHORE)
SIDE = pltpu.SideEffectType.DATAFLOW_SIDE_EFFECTING
N_PEERS = 7


def _peer(r):
    x, y, c = lax.axis_index("x"), lax.axis_index("y"), lax.axis_index("c")
    return (1 - x if r & 4 else x), (1 - y if r & 2 else y), (1 - c if r & 1 else c)


def _partial_copy(src, land, send, recv, k, r):
    px, py, pc = _peer(r)
    half = src.shape[1] // 2
    sem = k * N_PEERS + r - 1
    return _remote(src.at[2 * px + py, pl.ds(pc * half, half)], land.at[r - 1], send.at[sem], recv.at[sem], (px, py, pc))


def _shard_copy(full, send, recv, k, r):
    x, y, c = lax.axis_index("x"), lax.axis_index("y"), lax.axis_index("c")
    half = full.shape[1] // 2
    part = full.at[2 * x + y, pl.ds(c * half, half)]
    sem = k * (N_PEERS - 1) + r - 2
    return _remote(part, part, send.at[sem], recv.at[sem], _peer(r))


def _start_copies(name, arrays, n_sems, copies):
    n = len(arrays)

    def body(*refs):
        send, recv, token = refs[n], refs[n + 1], refs[-1]
        for cp in copies(refs[:n], send, recv):
            cp.start()
        token[...] = jnp.zeros_like(token)

    sem = pltpu.SemaphoreType.DMA((n_sems,))
    outs = pl.pallas_call(
        body, name=name, in_specs=[HBM] * n,
        out_specs=[SEM, SEM] + [HBM] * n + [pl.BlockSpec(memory_space=pltpu.VMEM)],
        out_shape=[sem, sem] + [pltpu.HBM(a.shape, a.dtype) for a in arrays] + [jax.ShapeDtypeStruct((SUB, HEAD), F32)],
        input_output_aliases={i: 2 + i for i in range(n)},
        compiler_params=pltpu.CompilerParams(has_side_effects=SIDE),
    )(*[pltpu.with_memory_space_constraint(a, pltpu.HBM) for a in arrays])
    return outs[0], outs[1], list(outs[2:2 + n]), outs[-1]


def _wait_copies(name, send, recv, arrays, after, copies):
    n = len(arrays)

    def body(*refs):
        for cp in copies(refs[:n], refs[n], refs[n + 1]):
            cp.wait_send()
            cp.wait_recv()

    return list(pl.pallas_call(
        body, name=name, in_specs=[HBM] * n + [SEM, SEM, ANY], out_specs=[HBM] * n,
        out_shape=[pltpu.HBM(a.shape, a.dtype) for a in arrays],
        input_output_aliases={i: i for i in range(n)},
        compiler_params=pltpu.CompilerParams(has_side_effects=SIDE),
    )(*arrays, send, recv, after))


def _partial_copies(n):
    def copies(refs, send, recv):
        return [_partial_copy(refs[k], refs[n + k], send, recv, k, r) for k in range(n) for r in range(1, N_PEERS + 1)]
    return copies


def _shard_copies(n):
    def copies(refs, send, recv):
        return [_shard_copy(refs[k], send, recv, k, r) for k in range(n) for r in range(2, N_PEERS + 1)]
    return copies


def _same_core_copies(refs, send, recv):
    return [_shard_copy(refs[0], send, recv, 0, r) for r in (2, 4, 6)]


def _forward_halves(full):
    def body(in_ref, out_ref, send, recv):
        x, y, c, chips = _place()
        half = out_ref.shape[1] // 2
        cps = []
        for j, (px, py) in enumerate(chips):
            part = out_ref.at[2 * px + py, pl.ds(c * half, half)]
            cp = _remote(part, part, send.at[j], recv.at[j], (x, y, 1 - c))
            cp.start()
            cps.append(cp)
        for j, (px, py) in enumerate(chips):
            other = out_ref.at[2 * px + py, pl.ds((1 - c) * half, half)]
            _remote(other, other, send.at[j], recv.at[j], (x, y, 1 - c)).wait_recv()
        for cp in cps:
            cp.wait_send()

    return pl.pallas_call(
        body, name="w_in_forward", in_specs=[ANY], out_specs=ANY,
        out_shape=jax.ShapeDtypeStruct(full.shape, full.dtype), input_output_aliases={0: 0},
        scratch_shapes=[pltpu.SemaphoreType.DMA((3,)), pltpu.SemaphoreType.DMA((3,))],
    )(full)


def _cast_place(name, w, shard):
    r, c = w.shape
    tr = _pick(r, (256, 128, 64, 32, 16))

    def body(s_ref, w_ref, o_ref):
        o_ref[...] = w_ref[...].astype(BF16)

    return pl.pallas_call(
        body, name=name,
        grid_spec=pltpu.PrefetchScalarGridSpec(
            num_scalar_prefetch=1, grid=(r // tr,), in_specs=[pl.BlockSpec((tr, c), lambda i, sr: (i, 0))],
            out_specs=pl.BlockSpec((None, tr, c), lambda i, sr: (sr[0], i, 0))),
        out_shape=jax.ShapeDtypeStruct((N_CHIPS, r, c), BF16),
        compiler_params=_cparams("parallel"),
    )(shard, w)


def _sum_peers(name, g, land, shard, core):
    _, r, cols = g.shape
    half = r // 2
    tr = _pick(half, (128, 64, 32, 16))
    nt = half // tr

    def body(s_ref, c_ref, g_ref, l_ref, o_ref):
        acc = g_ref[...].astype(F32)
        for j in range(N_PEERS):
            acc = acc + l_ref[j].astype(F32)
        o_ref[...] = acc

    return pl.pallas_call(
        body, name=name,
        grid_spec=pltpu.PrefetchScalarGridSpec(
            num_scalar_prefetch=2, grid=(nt,),
            in_specs=[pl.BlockSpec((None, tr, cols), lambda i, sr, cr: (sr[0], cr[0] * nt + i, 0)),
                      pl.BlockSpec((N_PEERS, tr, cols), lambda i, sr, cr: (0, i, 0))],
            out_specs=pl.BlockSpec((tr, cols), lambda i, sr, cr: (i, 0))),
        out_shape=jax.ShapeDtypeStruct((half, cols), F32),
        compiler_params=_cparams("parallel"),
    )(shard, core, g, land)


def _join_halves(mine, got, c):
    half, cols = mine.shape
    tr = _pick(half, (256, 128, 64, 32, 16, 8))
    nt = half // tr

    def body(c_ref, a_ref, b_ref, o_ref):
        i = pl.program_id(0)
        own = (i // nt) == c_ref[0]

        @pl.when(own)
        def _():
            o_ref[...] = a_ref[...]

        @pl.when(jnp.logical_not(own))
        def _():
            o_ref[...] = b_ref[...]

    blk = pl.BlockSpec((tr, cols), lambda i, cr: (i % nt, 0))
    return pl.pallas_call(
        body, name="join_halves",
        grid_spec=pltpu.PrefetchScalarGridSpec(num_scalar_prefetch=1, grid=(2 * nt,), in_specs=[blk, blk],
                                               out_specs=pl.BlockSpec((tr, cols), lambda i, cr: (i, 0))),
        out_shape=jax.ShapeDtypeStruct((2 * half, cols), F32),
        compiler_params=_cparams("parallel"),
    )(c, mine, got)


def _all_gather_rows(name, row):
    p = row.shape[1]

    def body(in_ref, out_ref, send, recv, local):
        x, y, c, _ = _place()
        me = 4 * x + 2 * y + c
        own = pltpu.make_async_copy(in_ref, out_ref.at[me], local)
        own.start()
        cps = []
        for k in range(1, 8):
            px, py, pc = x ^ (k >> 2), y ^ ((k >> 1) & 1), c ^ (k & 1)
            cp = _remote(in_ref, out_ref.at[me], send.at[k - 1], recv.at[k - 1], (px, py, pc))
            cp.start()
            cps.append(cp)
        for cp in cps:
            cp.wait_recv()
        for cp in cps:
            cp.wait_send()
        own.wait()

    return pl.pallas_call(
        body, name=name, in_specs=[ANY], out_specs=ANY,
        out_shape=jax.ShapeDtypeStruct((8, 1, p), F32),
        scratch_shapes=[pltpu.SemaphoreType.DMA((7,)), pltpu.SemaphoreType.DMA((7,)), pltpu.SemaphoreType.DMA],
    )(row)


def _sum_devices(rows):
    def body(r_ref, o_ref):
        acc = r_ref[0]
        for k in range(1, 8):
            acc = acc + r_ref[k]
        o_ref[...] = acc

    return pl.pallas_call(body, name="sum_devices", out_shape=jax.ShapeDtypeStruct(rows.shape[1:], F32))(rows)


def _adamw(name, w, g, m, v):
    r, c = w.shape
    tr = _pick(r, (128, 64, 32, 16, 8))
    bc1 = 1.0 - ADAM_B1 ** ADAM_STEP
    bc2 = 1.0 - ADAM_B2 ** ADAM_STEP

    def body(w_ref, g_ref, m_ref, v_ref, d_ref, nm_ref, nv_ref):
        gv = g_ref[...]
        nm = ADAM_B1 * m_ref[...] + (1.0 - ADAM_B1) * gv
        nv = ADAM_B2 * v_ref[...] + (1.0 - ADAM_B2) * (gv * gv)
        d_ref[...] = -ADAM_LR * ((nm / bc1) / (jnp.sqrt(nv / bc2) + ADAM_EPS) + ADAM_WD * w_ref[...])
        nm_ref[...] = nm
        nv_ref[...] = nv

    blk = pl.BlockSpec((tr, c), lambda i: (i, 0))
    sh = jax.ShapeDtypeStruct((r, c), F32)
    return pl.pallas_call(
        body, name=name, grid=(r // tr,), in_specs=[blk] * 4, out_specs=[blk] * 3, out_shape=[sh] * 3,
        compiler_params=_cparams("parallel"),
    )(w, g, m, v)


def _lb_logits_grad(dlb, lb):
    def body(d_ref, lb_ref, o_ref):
        lbv = lb_ref[...]
        t = d_ref[...] * lbv * (1.0 - lbv)
        o_ref[0:1, :] = t
        o_ref[1:2, :] = -t

    return pl.pallas_call(body, name="lb_logits_grad", out_shape=jax.ShapeDtypeStruct((2, dlb.shape[1]), F32))(dlb, lb)


BIG = ("w_in", "p_a", "p_b", "w_o", "w_up", "w_down")
SMALL = ("g_mix", "g_q", "g_k", "lb_logits", "g_hg_out", "g_ffn", "conv_w", "conv_b")
ORDER = ("g_mix", "w_in", "g_q", "g_k", "lb_logits", "g_hg_out", "p_a", "p_b", "w_o", "g_ffn", "w_up", "conv_w", "conv_b", "w_down")


def kernel(x, g_mix, w_in, g_q, g_k, lb_logits, g_hg_out, p_a, p_b, w_o, g_ffn, w_up, conv_w, conv_b, w_down, loss_target, m_g_mix, m_w_in, m_g_q, m_g_k, m_lb_logits, m_g_hg_out, m_p_a, m_p_b, m_w_o, m_g_ffn, m_w_up, m_conv_w, m_conv_b, m_w_down, v_g_mix, v_w_in, v_g_q, v_g_k, v_lb_logits, v_g_hg_out, v_p_a, v_p_b, v_w_o, v_g_ffn, v_w_up, v_conv_w, v_conv_b, v_w_down):
    assert lb_logits.shape[0] == 2, "the lower bound is the first row of a two-row softmax"
    w = dict(g_mix=g_mix, w_in=w_in[0], g_q=g_q, g_k=g_k, lb_logits=lb_logits, g_hg_out=g_hg_out, p_a=p_a[0], p_b=p_b[0],
             w_o=w_o[0], g_ffn=g_ffn, w_up=w_up[0], conv_w=conv_w[0], conv_b=conv_b, w_down=w_down[0])
    mom = dict(g_mix=m_g_mix, w_in=m_w_in[0], g_q=m_g_q, g_k=m_g_k, lb_logits=m_lb_logits, g_hg_out=m_g_hg_out, p_a=m_p_a[0],
               p_b=m_p_b[0], w_o=m_w_o[0], g_ffn=m_g_ffn, w_up=m_w_up[0], conv_w=m_conv_w[0], conv_b=m_conv_b, w_down=m_w_down[0])
    var = dict(g_mix=v_g_mix, w_in=v_w_in[0], g_q=v_g_q, g_k=v_g_k, lb_logits=v_lb_logits, g_hg_out=v_g_hg_out, p_a=v_p_a[0],
               p_b=v_p_b[0], w_o=v_w_o[0], g_ffn=v_g_ffn, w_up=v_w_up[0], conv_w=v_conv_w[0], conv_b=v_conv_b, w_down=v_w_down[0])
    d = x.shape[2]
    cx, cy, cc = lax.axis_index("x"), lax.axis_index("y"), lax.axis_index("c")
    shard = (2 * cx + cy).astype(jnp.int32).reshape(1)
    core = cc.astype(jnp.int32).reshape(1)

    later = ("p_a", "p_b", "w_o", "w_up", "w_down")
    n_later = len(later)
    first = [_cast_place("cast_w_in", w["w_in"], shard)]
    f_send, f_recv, first, f_token = _start_copies("w_in_start", first, N_PEERS - 1, _same_core_copies)
    started = {}

    def first_weight(after):
        got = _wait_copies("w_in_wait", f_send, f_recv, first, after, _same_core_copies)
        fulls = [_cast_place("cast_" + n, w[n], shard) for n in later]
        send, recv, fulls, _ = _start_copies("weights_start", fulls + [_forward_halves(got[0])],
                                             n_later * (N_PEERS - 1), _shard_copies(n_later))
        started.update(send=send, recv=recv, fulls=fulls[:n_later])
        return fulls[n_later]

    def later_weights(after):
        return dict(zip(later, _wait_copies("weights_wait", started["send"], started["recv"], started["fulls"], after,
                                            _shard_copies(n_later))))

    cw = conv_w.shape[2]
    rows = _all_gather_rows("gather_conv_w", w["conv_w"].reshape(1, 3 * cw))
    conv_full = jnp.concatenate([rows[2 * s, 0].reshape(3, cw) for s in range(N_CHIPS)], axis=1)

    pending = []

    def grads_ready(names, gs):
        n = len(gs)
        lands = [lax.empty((N_PEERS, g.shape[1] // 2, g.shape[2]), BF16) for g in gs]
        send, recv, arrays, token = _start_copies("partials_start_" + names[0], list(gs) + lands, n * N_PEERS, _partial_copies(n))
        pending.append((names, send, recv, arrays))
        return token[0:1, 0:1]

    dx, small, lb = _local_step(x[0], loss_target[0], g_mix + f_token[0:1, 0:1], g_q, g_k, lb_logits, g_hg_out, g_ffn,
                                conv_full, conv_b, first_weight, later_weights, grads_ready)

    mine = {}
    after = dx
    for names, send, recv, arrays in pending:
        n = len(names)
        arrays = _wait_copies("partials_wait_" + names[0], send, recv, arrays, after, _partial_copies(n))
        for k, name in enumerate(names):
            mine[name] = _sum_peers("sum_" + name, arrays[k], arrays[n + k], shard, core)
        after = mine[names[-1]]
    theirs = _to_sibling("reduced_to_sibling", [mine[n] for n in BIG])
    grads = {n: _join_halves(mine[n], b, core) for n, b in zip(BIG, theirs)}

    names = ("g_mix", "g_q", "g_k", "lb", "g_hg_out", "g_ffn", "conv_b", "sq")
    packed = jnp.concatenate([small[n] for n in names] + [small["conv_w"].reshape(1, -1)], axis=1)
    total = _sum_devices(_all_gather_rows("gather_small_grads", packed))
    off = 0
    red = {}
    for n in names:
        ln = small[n].shape[1]
        red[n] = total[:, off:off + ln]
        off += ln
    conv_all = total[:, off:].reshape(3, -1)
    loss = 0.5 * jnp.sum(red["sq"]) / d
    grads["conv_w"] = lax.dynamic_slice_in_dim(conv_all, (2 * cx + cy) * cw, cw, axis=1)
    grads["lb_logits"] = _lb_logits_grad(red["lb"], lb)
    for n in ("g_mix", "g_q", "g_k", "g_hg_out", "g_ffn", "conv_b"):
        grads[n] = red[n]

    delta, new_m, new_v = {}, {}, {}
    for n in ORDER:
        delta[n], new_m[n], new_v[n] = _adamw("adamw_" + n, w[n], grads[n], mom[n], var[n])

    def shaped(a, like):
        return a.reshape(like.shape)

    ref_w = dict(g_mix=g_mix, w_in=w_in, g_q=g_q, g_k=g_k, lb_logits=lb_logits, g_hg_out=g_hg_out, p_a=p_a, p_b=p_b, w_o=w_o,
                 g_ffn=g_ffn, w_up=w_up, conv_w=conv_w, conv_b=conv_b, w_down=w_down)
    outs = [loss, dx[None]]
    for group in (grads, delta, new_m, new_v):
        outs += [shaped(group[n], ref_w[n]) for n in ORDER]
    return tuple(outs)
```

```python
import functools

import jax
import jax.numpy as jnp
from jax import lax
from jax.experimental import pallas as pl
from jax.experimental.pallas import tpu as pltpu

F32 = jnp.float32
BF16 = jnp.bfloat16
HEAD = 128
EPS = 1e-6
N_CHIPS = 4
HG_CHUNK = 32
SUB = 8
ADAM_LR, ADAM_B1, ADAM_B2, ADAM_EPS, ADAM_WD, ADAM_STEP = 0.001, 0.9, 0.999, 1e-08, 0.01, 10
VMEM_LIMIT = 56 * 1024 * 1024
MESH = pl.DeviceIdType.MESH

NN = (((1,), (0,)), ((), ()))
NT = (((1,), (1,)), ((), ()))
TN = (((0,), (0,)), ((), ()))


def _cparams(*sem):
    return pltpu.CompilerParams(dimension_semantics=sem if sem else None, vmem_limit_bytes=VMEM_LIMIT)


def _pick(n, cands):
    for c in cands:
        if c <= n and n % c == 0:
            return c
    return n


def _sig(x):
    return 0.5 * jnp.tanh(0.5 * x) + 0.5


def _dot(a, b, dims):
    return lax.dot_general(a, b, dims, preferred_element_type=F32)


def _split(x):
    hi = x.astype(BF16)
    lo = (x - hi.astype(F32)).astype(BF16)
    return hi, lo


def _tri(n, kind):
    r = lax.broadcasted_iota(jnp.int32, (n, n), 0)
    c = lax.broadcasted_iota(jnp.int32, (n, n), 1)
    m = {"ge": c >= r, "gt": c > r, "le": c <= r, "lt": c < r}[kind]
    return jnp.where(m, 1.0, 0.0).astype(BF16)


TILE_M = (1024, 512, 256, 128)
TILE_N = (1408, 1024, 512, 256, 128)
TILE_K = (2816, 2048, 1408, 1024, 512, 256, 128)


def _matmul(name, a, b, mode, out_dtype, add=None, out_shards=None):
    if mode == "nn":
        m, k = a.shape
        g, _, ns = b.shape
        n = g * ns
        ns_out = n // out_shards if out_shards else ns
        tm, tn, tk = _pick(m, TILE_M), _pick(min(ns, ns_out), TILE_N), _pick(k, TILE_K)
        nps = ns // tn
        grid = (m // tm, n // tn, k // tk)
        a_spec = pl.BlockSpec((tm, tk), lambda i, j, kk: (i, kk))
        b_spec = pl.BlockSpec((None, tk, tn), lambda i, j, kk: (j // nps, kk, j % nps))
        if out_shards:
            npo = ns_out // tn
            o_spec = pl.BlockSpec((None, tm, tn), lambda i, j, kk: (j // npo, i, j % npo))
            o_shape = jax.ShapeDtypeStruct((out_shards, m, ns_out), out_dtype)
        else:
            o_spec = pl.BlockSpec((tm, tn), lambda i, j, kk: (i, j))
            o_shape = jax.ShapeDtypeStruct((m, n), out_dtype)
        dims = NN
    elif mode == "nt":
        m, k = a.shape
        g, n, ks = b.shape
        tm, tn, tk = _pick(m, TILE_M), _pick(n, TILE_N), _pick(ks, TILE_K)
        kps = ks // tk
        grid = (m // tm, n // tn, k // tk)
        a_spec = pl.BlockSpec((tm, tk), lambda i, j, kk: (i, kk))
        b_spec = pl.BlockSpec((None, tn, tk), lambda i, j, kk: (kk // kps, j, kk % kps))
        o_spec = pl.BlockSpec((tm, tn), lambda i, j, kk: (i, j))
        o_shape = jax.ShapeDtypeStruct((m, n), out_dtype)
        dims = NT
    else:
        raise ValueError(mode)
    nk = grid[2]

    def body(*refs):
        a_ref, b_ref = refs[0], refs[1]
        add_ref = refs[2] if add is not None else None
        o_ref = refs[2 + (add is not None)]

        def finish(r):
            if add is not None:
                r = r + add_ref[...]
            o_ref[...] = r.astype(o_ref.dtype)

        part = _dot(a_ref[...].astype(BF16), b_ref[...].astype(BF16), dims)
        if nk == 1:
            finish(part)
            return
        acc = refs[-1]
        kk = pl.program_id(2)

        @pl.when(kk == 0)
        def _():
            acc[...] = part

        @pl.when(kk > 0)
        def _():
            acc[...] += part

        @pl.when(kk == nk - 1)
        def _():
            finish(acc[...])

    in_specs = [a_spec, b_spec]
    args = [a, b]
    if add is not None:
        in_specs.append(o_spec)
        args.append(add)
    return pl.pallas_call(
        body, name=name, grid=grid, in_specs=in_specs, out_specs=o_spec, out_shape=o_shape,
        scratch_shapes=[pltpu.VMEM((tm, tn), F32)] if nk > 1 else [],
        compiler_params=_cparams("parallel", "parallel", "arbitrary"),
    )(*args)


def _matmul_tn(name, a, b, g, out_dtype):
    k, m = a.shape
    _, n = b.shape
    ns = n // g
    tm, tn, tk = _pick(m, (2048, 1408) + TILE_M), _pick(ns, TILE_N), _pick(k, (1024, 512, 256, 128))
    nps = ns // tn
    nk = k // tk

    def body(a_ref, b_ref, o_ref, acc):
        kk = pl.program_id(2)
        part = _dot(a_ref[...].astype(BF16), b_ref[...].astype(BF16), TN)

        @pl.when(kk == 0)
        def _():
            acc[...] = part

        @pl.when(kk > 0)
        def _():
            acc[...] += part

        @pl.when(kk == nk - 1)
        def _():
            o_ref[...] = acc[...].astype(o_ref.dtype)

    return pl.pallas_call(
        body, name=name, grid=(m // tm, n // tn, nk),
        in_specs=[pl.BlockSpec((tk, tm), lambda i, j, kk: (kk, i)), pl.BlockSpec((tk, tn), lambda i, j, kk: (kk, j))],
        out_specs=pl.BlockSpec((None, tm, tn), lambda i, j, kk: (j // nps, i, j % nps)),
        out_shape=jax.ShapeDtypeStruct((g, m, ns), out_dtype),
        scratch_shapes=[pltpu.VMEM((tm, tn), F32)],
        compiler_params=_cparams("parallel", "parallel", "arbitrary"),
    )(a, b)


def _rmsnorm_fwd(name, x, g):
    s, d = x.shape
    tr = _pick(s, (256, 128))

    def body(x_ref, g_ref, u_ref, ut_ref):
        xv = x_ref[...]
        r = lax.rsqrt(jnp.mean(xv * xv, axis=-1, keepdims=True) + EPS)
        u = xv * r * g_ref[...]
        u_ref[...] = u.astype(BF16)
        ut_ref[...] = u.T.astype(BF16)

    return pl.pallas_call(
        body, name=name, grid=(s // tr,),
        in_specs=[pl.BlockSpec((tr, d), lambda i: (i, 0)), pl.BlockSpec((1, d), lambda i: (0, 0))],
        out_specs=[pl.BlockSpec((tr, d), lambda i: (i, 0)), pl.BlockSpec((d, tr), lambda i: (0, i))],
        out_shape=[jax.ShapeDtypeStruct((s, d), BF16), jax.ShapeDtypeStruct((d, s), BF16)],
        compiler_params=_cparams("parallel"),
    )(x, g)


def _rmsnorm_bwd(name, x, g, du, extra):
    s, d = x.shape
    tr = _pick(s, (256, 128, 64, 32, 16, 8))

    def body(x_ref, g_ref, du_ref, e_ref, dx_ref, dg_ref):
        i = pl.program_id(0)
        xv = x_ref[...]
        r = lax.rsqrt(jnp.mean(xv * xv, axis=-1, keepdims=True) + EPS)
        n = xv * r
        dy = du_ref[...]
        a = dy * g_ref[...]
        dx = r * (a - n * jnp.mean(a * n, axis=-1, keepdims=True))
        dx_ref[...] = e_ref[...] + dx

        @pl.when(i == 0)
        def _():
            dg_ref[...] = jnp.zeros_like(dg_ref)

        dg_ref[...] += jnp.sum((dy * n).reshape(tr // SUB, SUB, d), axis=0)

    return pl.pallas_call(
        body, name=name, grid=(s // tr,),
        in_specs=[pl.BlockSpec((tr, d), lambda i: (i, 0)), pl.BlockSpec((1, d), lambda i: (0, 0)),
                  pl.BlockSpec((tr, d), lambda i: (i, 0)), pl.BlockSpec((tr, d), lambda i: (i, 0))],
        out_specs=[pl.BlockSpec((tr, d), lambda i: (i, 0)), pl.BlockSpec((SUB, d), lambda i: (0, 0))],
        out_shape=[jax.ShapeDtypeStruct((s, d), F32), jax.ShapeDtypeStruct((SUB, d), F32)],
        compiler_params=_cparams("arbitrary"),
    )(x, g, du, extra)


def _head_norm(x, g):
    r = lax.rsqrt(jnp.mean(x * x, axis=-1, keepdims=True) + EPS)
    return x * r * g


def _qk_norm_fwd(proj, g_q, g_k, nh, blk):
    s = proj.shape[0]

    def body(q_ref, k_ref, v_ref, gq_ref, gk_ref, qn_ref, kn_ref, vb_ref, kt_ref, vt_ref):
        qn_ref[...] = _head_norm(q_ref[...], gq_ref[...]).astype(BF16)
        kn = _head_norm(k_ref[...], gk_ref[...])
        kn_ref[...] = kn.astype(BF16)
        kt_ref[...] = kn.T.astype(BF16)
        v = v_ref[...]
        vb_ref[...] = v.astype(BF16)
        vt_ref[...] = v.T.astype(BF16)

    col = lambda base: pl.BlockSpec((blk, HEAD), lambda i, h: (i, base + h))
    gs = pl.BlockSpec((1, HEAD), lambda i, h: (0, 0))
    o = pl.BlockSpec((blk, HEAD), lambda i, h: (i, h))
    t = pl.BlockSpec((None, HEAD, blk), lambda i, h: (i, h, 0))
    sh = jax.ShapeDtypeStruct((s, nh * HEAD), BF16)
    tsh = jax.ShapeDtypeStruct((s // blk, nh * HEAD, blk), BF16)
    return pl.pallas_call(
        body, name="qk_norm_fwd", grid=(s // blk, nh),
        in_specs=[col(0), col(nh), col(2 * nh), gs, gs], out_specs=[o, o, o, t, t], out_shape=[sh, sh, sh, tsh, tsh],
        compiler_params=_cparams("parallel", "parallel"),
    )(proj, proj, proj, g_q, g_k)


def _qk_norm_bwd(proj, g_q, g_k, dqn, dkn, nh):
    s = proj.shape[0]
    tr = _pick(s, (512, 256, 128, 64, 32, 16, 8))

    def one(x, g, dy):
        r = lax.rsqrt(jnp.mean(x * x, axis=-1, keepdims=True) + EPS)
        n = x * r
        a = dy * g
        dx = r * (a - n * jnp.mean(a * n, axis=-1, keepdims=True))
        return dx, jnp.sum((dy * n).reshape(tr // SUB, SUB, HEAD), axis=0)

    def body(q_ref, k_ref, gq_ref, gk_ref, dqn_ref, dkn_ref, dq_ref, dk_ref, dgq_ref, dgk_ref):
        first = (pl.program_id(0) == 0) & (pl.program_id(1) == 0)

        @pl.when(first)
        def _():
            dgq_ref[...] = jnp.zeros_like(dgq_ref)
            dgk_ref[...] = jnp.zeros_like(dgk_ref)

        dq, pq = one(q_ref[...], gq_ref[...], dqn_ref[...])
        dk, pk = one(k_ref[...], gk_ref[...], dkn_ref[...])
        dq_ref[...] = dq.astype(BF16)
        dk_ref[...] = dk.astype(BF16)
        dgq_ref[...] += pq
        dgk_ref[...] += pk

    col = lambda base: pl.BlockSpec((tr, HEAD), lambda i, h: (i, base + h))
    gs = pl.BlockSpec((1, HEAD), lambda i, h: (0, 0))
    o = pl.BlockSpec((tr, HEAD), lambda i, h: (i, h))
    part = pl.BlockSpec((SUB, HEAD), lambda i, h: (0, 0))
    sh = jax.ShapeDtypeStruct((s, nh * HEAD), BF16)
    psh = jax.ShapeDtypeStruct((SUB, HEAD), F32)
    return pl.pallas_call(
        body, name="qk_norm_bwd", grid=(s // tr, nh),
        in_specs=[col(0), col(nh), gs, gs, o, o], out_specs=[o, o, part, part], out_shape=[sh, sh, psh, psh],
        compiler_params=_cparams("arbitrary", "arbitrary"),
    )(proj, proj, g_q, g_k, dqn, dkn)


def _sb_consts(blk, hp):
    upper = _tri(blk, "ge")
    row = lax.broadcasted_iota(jnp.int32, (blk, hp * blk), 0)
    col = lax.broadcasted_iota(jnp.int32, (blk, hp * blk), 1)
    strict = row < col
    for h in range(1, hp):
        strict = strict & ((col < h * blk) | (row < col - h * blk))
    return upper, strict


def _sb_log_keep(zt, strict):
    l = jnp.minimum(-zt, 0.0) - jnp.log(1.0 + jnp.exp(-jnp.abs(zt)))
    return l if strict is None else jnp.where(strict, l, 0.0)


SB_GROUP = 4
SB_GROUP_BWD = 4


def _sb_heads(nh):
    return 2 if nh % 2 == 0 else 1


def _sb_attn_fwd(qn, kn, vt, nh, blk):
    s = qn.shape[0]
    nb = s // blk
    scale = HEAD ** -0.5
    hp = 4 if nh % 4 == 0 else _sb_heads(nh)

    def body(q_ref, k_ref, vt_ref, y_ref, c_ref):
        qi = pl.program_id(1)
        suffix, strict = _sb_consts(blk, hp)
        qs = [q_ref[:, h * HEAD:(h + 1) * HEAD] for h in range(hp)]

        def logits(kb_i):
            off = pl.multiple_of(kb_i * blk, blk)
            return jnp.concatenate(
                [_dot(k_ref[pl.ds(off, blk), h * HEAD:(h + 1) * HEAD], qs[h], NT) for h in range(hp)], axis=1) * scale

        def sums(zt, mask):
            return _dot(suffix, _sb_log_keep(zt, mask).astype(BF16), NN)

        def weights(kb_i, zt, cum, cr, mask):
            for h in range(hp):
                c_ref[h, kb_i] = cr[:, h * blk:(h + 1) * blk]
            wt = jnp.exp(zt + cum + cr)
            if mask is not None:
                wt = jnp.where(mask, wt, 0.0)
            return wt.astype(BF16), cr + cum[0:1, :]

        def add_values(kb_i, wt, accs):
            return tuple(
                accs[h] + _dot(vt_ref[kb_i, h * HEAD:(h + 1) * HEAD, :], wt[:, h * blk:(h + 1) * blk], NN)
                for h in range(hp))

        def group(kbs, masks, accs, cr):
            zts = [logits(k) for k in kbs]
            cums = [sums(zt, m) for zt, m in zip(zts, masks)]
            for k, zt, cum, m in zip(kbs, zts, cums, masks):
                wt, cr = weights(k, zt, cum, cr, m)
                accs = add_values(k, wt, accs)
            return accs, cr

        accs = tuple(jnp.zeros((HEAD, blk), F32) for _ in range(hp))
        accs, cr = group([qi], [strict], accs, jnp.zeros((1, hp * blk), F32))
        n_groups = qi // SB_GROUP

        def many(g, st):
            top = qi - 1 - g * SB_GROUP
            return group([top - j for j in range(SB_GROUP)], [None] * SB_GROUP, *st)

        def one(r, st):
            return group([qi - 1 - n_groups * SB_GROUP - r], [None], *st)

        st = lax.fori_loop(0, n_groups, many, (accs, cr))
        accs, _ = lax.fori_loop(0, qi - n_groups * SB_GROUP, one, st)
        for h in range(hp):
            y_ref[:, h * HEAD:(h + 1) * HEAD] = accs[h].T.astype(y_ref.dtype)

    qs_ = pl.BlockSpec((blk, hp * HEAD), lambda h, i: (i, h))
    full = pl.BlockSpec((s, hp * HEAD), lambda h, i: (0, h), pipeline_mode=pl.Buffered(1))
    return pl.pallas_call(
        body, name="sb_attn_fwd", grid=(nh // hp, nb),
        in_specs=[qs_, full, pl.BlockSpec((nb, hp * HEAD, blk), lambda h, i: (0, h, 0), pipeline_mode=pl.Buffered(1))],
        out_specs=[qs_, pl.BlockSpec((hp, nb, 1, blk), lambda h, i: (h, 0, 0, i))],
        out_shape=[jax.ShapeDtypeStruct((s, nh * HEAD), BF16), jax.ShapeDtypeStruct((nh, nb, 1, s), F32)],
        compiler_params=_cparams("parallel", "arbitrary"),
    )(qn, kn, vt)


def _sb_attn_bwd(qn, kn, kt, vb, dy, carries, nh, blk):
    s = qn.shape[0]
    nb = s // blk
    scale = HEAD ** -0.5
    hp = _sb_heads(nh)

    def body(q_ref, k_ref, kt_ref, v_ref, dy_ref, c_ref, dq_ref, dk_ref, dv_ref):
        qi = pl.program_id(1)

        @pl.when(qi == 0)
        def _():
            dk_ref[...] = jnp.zeros_like(dk_ref)
            dv_ref[...] = jnp.zeros_like(dv_ref)

        suffix, strict = _sb_consts(blk, hp)
        prefix = _tri(blk, "lt")
        qs = [q_ref[:, h * HEAD:(h + 1) * HEAD] for h in range(hp)]
        dos = [dy_ref[:, h * HEAD:(h + 1) * HEAD].astype(BF16) for h in range(hp)]

        def logits(kb_i):
            off = pl.multiple_of(kb_i * blk, blk)
            return jnp.concatenate(
                [_dot(k_ref[pl.ds(off, blk), h * HEAD:(h + 1) * HEAD], qs[h], NT) for h in range(hp)], axis=1) * scale

        def group(kbs, masks, dqs, ec):
            rows = [pl.ds(pl.multiple_of(k * blk, blk), blk) for k in kbs]
            zts = [logits(k) for k in kbs]
            dws = [jnp.concatenate([_dot(v_ref[r, h * HEAD:(h + 1) * HEAD], dos[h], NT) for h in range(hp)], axis=1)
                   for r in rows]
            ls = [_sb_log_keep(zt, m) for zt, m in zip(zts, masks)]
            cums = [_dot(suffix, l.astype(BF16), NN) for l in ls]
            wts, ets, befores = [], [], []
            for k, zt, cum, dw, m in zip(kbs, zts, cums, dws, masks):
                cr = jnp.concatenate([c_ref[h, k] for h in range(hp)], axis=1)
                wt = jnp.exp(zt + cum + cr)
                if m is not None:
                    wt = jnp.where(m, wt, 0.0)
                et = wt * dw
                befores.append(_dot(prefix, et.astype(BF16), NN) + ec)
                ec = ec + jnp.sum(et, axis=0, keepdims=True)
                wts.append(wt.astype(BF16))
                ets.append(et)
            for k, r, l, et, before, wtb in zip(kbs, rows, ls, ets, befores, wts):
                dzt = ((jnp.exp(l) * (et + before) - before) * scale).astype(BF16)
                out = []
                for h in range(hp):
                    cols, part = slice(h * HEAD, (h + 1) * HEAD), slice(h * blk, (h + 1) * blk)
                    dv_ref[r, cols] += _dot(wtb[:, part], dos[h], NN)
                    dk_ref[r, cols] += _dot(dzt[:, part], qs[h], NN)
                    out.append(dqs[h] + _dot(kt_ref[k, cols, :], dzt[:, part], NN))
                dqs = tuple(out)
            return dqs, ec

        n_groups = qi // SB_GROUP_BWD

        def many(g, st):
            return group([g * SB_GROUP_BWD + j for j in range(SB_GROUP_BWD)], [None] * SB_GROUP_BWD, *st)

        def one(r, st):
            return group([n_groups * SB_GROUP_BWD + r], [None], *st)

        st = (tuple(jnp.zeros((HEAD, blk), F32) for _ in range(hp)), jnp.zeros((1, hp * blk), F32))
        st = lax.fori_loop(0, n_groups, many, st)
        st = lax.fori_loop(0, qi - n_groups * SB_GROUP_BWD, one, st)
        dqs, _ = group([qi], [strict], *st)
        for h in range(hp):
            dq_ref[:, h * HEAD:(h + 1) * HEAD] = dqs[h].T

    qs_ = pl.BlockSpec((blk, hp * HEAD), lambda h, i: (i, h))
    full = pl.BlockSpec((s, hp * HEAD), lambda h, i: (0, h), pipeline_mode=pl.Buffered(1))
    sh = jax.ShapeDtypeStruct((s, nh * HEAD), F32)
    return pl.pallas_call(
        body, name="sb_attn_bwd", grid=(nh // hp, nb),
        in_specs=[qs_, full, pl.BlockSpec((nb, hp * HEAD, blk), lambda h, i: (0, h, 0), pipeline_mode=pl.Buffered(1)),
                  full, qs_, pl.BlockSpec((hp, nb, 1, blk), lambda h, i: (h, 0, 0, i))],
        out_specs=[qs_, full, full], out_shape=[sh, sh, sh],
        compiler_params=_cparams("parallel", "arbitrary"),
    )(qn, kn, kt, vb, dy, carries)


def _lower_bound(lb_logits):
    def body(l_ref, o_ref):
        l = l_ref[...]
        m = jnp.max(l, axis=0, keepdims=True)
        e = jnp.exp(l - m)
        o_ref[...] = e[0:1, :] / jnp.sum(e, axis=0, keepdims=True)

    return pl.pallas_call(body, name="lower_bound", out_shape=jax.ShapeDtypeStruct((1, lb_logits.shape[1]), F32))(lb_logits)


def _hg_gates(hq, hf, lb):
    sq = _sig(hq)
    q = hq * sq
    sf = _sig(hf)
    f = lb + (1.0 - lb) * sf
    return q, sq, f, sf


def _hg_cum(g, c):
    hi, lo = _split(g)
    t = _tri(c, "le")
    return _dot(t, hi, NN) + _dot(t, lo, NN)


def _hg_heads(nh):
    return 8 if nh % 8 == 0 else 4 if nh % 4 == 0 else 2 if nh % 2 == 0 else 1


def _head_cols(x, h):
    return x[:, h * HEAD:(h + 1) * HEAD]


def _per_head(x, hp, fn):
    return jnp.concatenate([fn(_head_cols(x, h), h) for h in range(hp)], axis=1)


def _head_sums(x, hp):
    return [jnp.sum(_head_cols(x, h), axis=1, keepdims=True) for h in range(hp)]


def _head_scale(cols, x, hp):
    return jnp.concatenate([cols[h] * _head_cols(x, h) for h in range(hp)], axis=1)


def _row_mask(r, width):
    return lax.broadcasted_iota(jnp.int32, (SUB, width), 0) >= r


def _hg_intra_fwd(q, k, v, b, c, hp):
    outs = []
    for bi in range(c // SUB):
        q_i, b_i = q[bi * SUB:(bi + 1) * SUB], b[bi * SUB:(bi + 1) * SUB]
        acc = jnp.zeros((SUB, hp * HEAD), F32)
        for s in range((bi + 1) * SUB):
            d = b_i - b[s:s + 1]
            if s >= bi * SUB:
                d = jnp.where(_row_mask(s - bi * SUB, hp * HEAD), d, -jnp.inf)
            acc = acc + _head_scale(_head_sums(q_i * k[s:s + 1] * jnp.exp(d), hp), v[s:s + 1], hp)
        outs.append(acc)
    return jnp.concatenate(outs, axis=0)


def _hg_intra_bwd(q, k, v, b, do, c, hp, dq_scr, dk_scr, dv_scr):
    nblk = c // SUB
    dq_scr[...] = jnp.zeros_like(dq_scr)
    for s in range(c):
        bj = s // SUB
        ks, vs, bs = k[s:s + 1], v[s:s + 1], b[s:s + 1]
        acc_k = jnp.zeros((SUB, hp * HEAD), F32)
        acc_v = jnp.zeros((SUB, hp * HEAD), F32)
        for bi in range(bj, nblk):
            sl = slice(bi * SUB, (bi + 1) * SUB)
            d = b[sl] - bs
            if bi == bj:
                d = jnp.where(_row_mask(s - bj * SUB, hp * HEAD), d, -jnp.inf)
            dec = jnp.exp(d)
            qd = q[sl] * dec
            col = _head_sums(qd * ks, hp)
            dcol = _head_sums(do[sl] * vs, hp)
            dq_scr[sl, :] += _head_scale(dcol, ks * dec, hp)
            acc_k = acc_k + _head_scale(dcol, qd, hp)
            acc_v = acc_v + _head_scale(col, do[sl], hp)
        dk_scr[s:s + 1, :] = jnp.sum(acc_k, axis=0, keepdims=True)
        dv_scr[s:s + 1, :] = jnp.sum(acc_v, axis=0, keepdims=True)


def _hgrn2_fwd(proj, lb, g_out, nh, base, c):
    s = proj.shape[0]
    nch = s // c
    hp = _hg_heads(nh)
    wide = hp * HEAD

    def body(hq_ref, hf_ref, hi_ref, og_ref, lb_ref, g_ref, o_ref, y_ref, st_ref, st):
        @pl.when(pl.program_id(1) == 0)
        def _():
            st[...] = jnp.zeros_like(st)

        st_in = [st[h] for h in range(hp)]
        for h in range(hp):
            st_ref[h] = st_in[h]
        q, _, f, _ = _hg_gates(hq_ref[...], hf_ref[...], lb_ref[...])
        k = 1.0 - f
        v = hi_ref[...]
        b = _hg_cum(jnp.log(f), c)
        bl = b[c - 1:c, :]
        qe = (q * jnp.exp(b)).astype(BF16)
        o = _per_head(qe, hp, lambda x, h: _dot(x, st_in[h].astype(BF16), NT)) + _hg_intra_fwd(q, k, v, b, c, hp)
        kd = (k * jnp.exp(bl - b)).astype(BF16)
        vb = v.astype(BF16)
        keep = jnp.exp(bl)
        for h in range(hp):
            st[h] = st_in[h] * _head_cols(keep, h) + _dot(_head_cols(vb, h), _head_cols(kd, h), TN)
        o_ref[...] = o
        og = og_ref[...]
        gout = g_ref[...]
        y_ref[...] = (_per_head(o, hp, lambda x, h: _head_norm(x, gout)) * (og * _sig(og))).astype(BF16)

    col = lambda j: pl.BlockSpec((c, wide), lambda g, i: (i, (base + j * nh) // hp + g))
    row = pl.BlockSpec((1, wide), lambda g, i: (0, g))
    gs = pl.BlockSpec((1, HEAD), lambda g, i: (0, 0))
    o = pl.BlockSpec((c, wide), lambda g, i: (i, g))
    return pl.pallas_call(
        body, name="hgrn2_fwd", grid=(nh // hp, nch),
        in_specs=[col(0), col(1), col(2), col(3), row, gs],
        out_specs=[o, o, pl.BlockSpec((hp, None, HEAD, HEAD), lambda g, i: (g, i, 0, 0))],
        out_shape=[jax.ShapeDtypeStruct((s, nh * HEAD), F32), jax.ShapeDtypeStruct((s, nh * HEAD), BF16),
                   jax.ShapeDtypeStruct((nh, nch, HEAD, HEAD), F32)],
        scratch_shapes=[pltpu.VMEM((hp, HEAD, HEAD), F32)],
        compiler_params=_cparams("parallel", "arbitrary"),
    )(proj, proj, proj, proj, lb, g_out)


def _hgrn2_bwd(proj, lb, g_out, o_pre, states, dy, nh, base, c):
    s = proj.shape[0]
    nch = s // c
    hp = _hg_heads(nh)
    wide = hp * HEAD

    def fold(x):
        return jnp.sum(x.reshape(c // SUB, SUB, x.shape[1]), axis=0)

    def body(hq_ref, hf_ref, hi_ref, og_ref, lb_ref, g_ref, o_ref, st_ref, se_ref, dy_ref,
             dhq_ref, dhf_ref, dhi_ref, dog_ref, dg_ref, dlb_ref, dst, dq_scr, dk_scr, dv_scr):
        g, i = pl.program_id(0), pl.program_id(1)

        @pl.when(i == 0)
        def _():
            dst[...] = jnp.zeros_like(dst)
            dlb_ref[...] = jnp.zeros_like(dlb_ref)

        @pl.when((i == 0) & (g == 0))
        def _():
            dg_ref[...] = jnp.zeros_like(dg_ref)

        lbv = lb_ref[...]
        hq, hf = hq_ref[...], hf_ref[...]
        q, sq, f, sf = _hg_gates(hq, hf, lbv)
        k = 1.0 - f
        v = hi_ref[...]
        b = _hg_cum(jnp.log(f), c)
        bl = b[c - 1:c, :]
        eb = jnp.exp(b)
        ebl = jnp.exp(bl - b)

        o = o_ref[...]
        gout = g_ref[...]
        og = og_ref[...]
        sg = _sig(og)
        r = _per_head(o, hp, lambda x, h: jnp.broadcast_to(
            lax.rsqrt(jnp.mean(x * x, axis=-1, keepdims=True) + EPS), x.shape))
        gw = jnp.concatenate([gout] * hp, axis=1)
        n = o * r
        dyv = dy_ref[...]
        dn = dyv * (og * sg)
        dog_ref[...] = (dyv * n * gw * (sg * (1.0 + og * (1.0 - sg)))).astype(BF16)
        dnn = fold(dn * n)
        part = _head_cols(dnn, 0)
        for h in range(1, hp):
            part = part + _head_cols(dnn, h)
        dg_ref[...] += part
        a = dn * gw
        an = a * n
        do = r * (a - n * _per_head(an, hp, lambda x, h: jnp.broadcast_to(jnp.mean(x, axis=-1, keepdims=True), x.shape)))

        st_in = [st_ref[h].astype(BF16) for h in range(hp)]
        dstv = [dst[h] for h in range(hp)]
        dstb = [d.astype(BF16) for d in dstv]
        dob = do.astype(BF16)
        vb = v.astype(BF16)
        kdb = (k * ebl).astype(BF16)
        qeb = (q * eb).astype(BF16)
        _hg_intra_bwd(q, k, v, b, do, c, hp, dq_scr, dk_scr, dv_scr)
        dq = dq_scr[...] + eb * _per_head(dob, hp, lambda x, h: _dot(x, st_in[h], NN))
        dk = dk_scr[...] + ebl * _per_head(vb, hp, lambda x, h: _dot(x, dstb[h], NN))
        dv = dv_scr[...] + _per_head(kdb, hp, lambda x, h: _dot(x, dstb[h], NT))
        keep = jnp.exp(bl)
        for h in range(hp):
            dst[h] = dstv[h] * _head_cols(keep, h) + _dot(_head_cols(dob, h), _head_cols(qeb, h), TN)

        hi_, lo_ = _split(q * dq - k * dk)
        rev = _tri(c, "ge")
        later = jnp.concatenate([jnp.sum(dstv[h] * se_ref[h], axis=0, keepdims=True) for h in range(hp)], axis=1)
        dg = _dot(rev, hi_, NN) + _dot(rev, lo_, NN) + jnp.where(i > 0, later, 0.0)
        df = dg / f - dk
        dhq_ref[...] = (dq * (sq * (1.0 + hq * (1.0 - sq)))).astype(BF16)
        dhf_ref[...] = (df * (1.0 - lbv) * sf * (1.0 - sf)).astype(BF16)
        dhi_ref[...] = dv.astype(BF16)
        dlb_ref[...] += fold(df * (1.0 - sf))

    rv = lambda i: nch - 1 - i
    col = lambda j: pl.BlockSpec((c, wide), lambda g, i: (rv(i), (base + j * nh) // hp + g))
    row = pl.BlockSpec((1, wide), lambda g, i: (0, g))
    gs = pl.BlockSpec((1, HEAD), lambda g, i: (0, 0))
    o = pl.BlockSpec((c, wide), lambda g, i: (rv(i), g))
    st = pl.BlockSpec((hp, None, HEAD, HEAD), lambda g, i: (g, rv(i), 0, 0))
    se = pl.BlockSpec((hp, None, HEAD, HEAD), lambda g, i: (g, jnp.minimum(rv(i) + 1, nch - 1), 0, 0))
    sh = jax.ShapeDtypeStruct((s, nh * HEAD), BF16)
    return pl.pallas_call(
        body, name="hgrn2_bwd", grid=(nh // hp, nch),
        in_specs=[col(0), col(1), col(2), col(3), row, gs, o, st, se, o],
        out_specs=[o, o, o, o, pl.BlockSpec((SUB, HEAD), lambda g, i: (0, 0)), pl.BlockSpec((SUB, wide), lambda g, i: (0, g))],
        out_shape=[sh, sh, sh, sh, jax.ShapeDtypeStruct((SUB, HEAD), F32), jax.ShapeDtypeStruct((SUB, nh * HEAD), F32)],
        scratch_shapes=[pltpu.VMEM((hp, HEAD, HEAD), F32), pltpu.VMEM((c, wide), F32), pltpu.VMEM((c, wide), F32),
                        pltpu.VMEM((c, wide), F32)],
        compiler_params=_cparams("arbitrary", "arbitrary"),
    )(proj, proj, proj, proj, lb, g_out, o_pre, states, states, dy)


def _merge_tiles(s, d, gate_col):
    tr = _pick(s, (256, 128, 64, 32, 16, 8))
    tc = 128
    for cand in (512, 256):
        if d % cand == 0 and gate_col % cand == 0:
            tc = cand
            break
    return tr, tc


def _merge_fwd(proj, ya, yb, gate_col):
    s, d = ya.shape
    tr, tc = _merge_tiles(s, d, gate_col)
    ga0, gb0 = gate_col // tc, (gate_col + d) // tc

    def body(ga_ref, gb_ref, ya_ref, yb_ref, m_ref):
        m_ref[...] = (_sig(ga_ref[...]) * ya_ref[...] + _sig(gb_ref[...]) * yb_ref[...]).astype(BF16)

    o = pl.BlockSpec((tr, tc), lambda i, j: (i, j))
    return pl.pallas_call(
        body, name="merge_fwd", grid=(s // tr, d // tc),
        in_specs=[pl.BlockSpec((tr, tc), lambda i, j: (i, ga0 + j)), pl.BlockSpec((tr, tc), lambda i, j: (i, gb0 + j)), o, o],
        out_specs=o, out_shape=jax.ShapeDtypeStruct((s, d), BF16),
        compiler_params=_cparams("parallel", "parallel"),
    )(proj, proj, ya, yb)


def _merge_bwd(proj, ya, yb, dm, gate_col):
    s, d = ya.shape
    tr, tc = _merge_tiles(s, d, gate_col)
    ga0, gb0 = gate_col // tc, (gate_col + d) // tc

    def body(ga_ref, gb_ref, ya_ref, yb_ref, dm_ref, dya_ref, dyb_ref, dga_ref, dgb_ref):
        dmv = dm_ref[...]
        sa, sb = _sig(ga_ref[...]), _sig(gb_ref[...])
        dya_ref[...] = (dmv * sa).astype(BF16)
        dyb_ref[...] = (dmv * sb).astype(BF16)
        dga_ref[...] = (dmv * ya_ref[...] * sa * (1.0 - sa)).astype(BF16)
        dgb_ref[...] = (dmv * yb_ref[...] * sb * (1.0 - sb)).astype(BF16)

    o = pl.BlockSpec((tr, tc), lambda i, j: (i, j))
    sh = jax.ShapeDtypeStruct((s, d), BF16)
    return pl.pallas_call(
        body, name="merge_bwd", grid=(s // tr, d // tc),
        in_specs=[pl.BlockSpec((tr, tc), lambda i, j: (i, ga0 + j)), pl.BlockSpec((tr, tc), lambda i, j: (i, gb0 + j)), o, o, o],
        out_specs=[o, o, o, o], out_shape=[sh, sh, sh, sh],
        compiler_params=_cparams("parallel", "parallel"),
    )(proj, proj, ya, yb, dm)


CONV_ROWS = 512


def _conv_ext(ref, i, rows, s, before, after):
    parts = []
    if before:
        p = ref[pl.ds(pl.multiple_of(jnp.maximum(i * rows - before, 0), SUB), before), :]
        parts.append(jnp.where(i > 0, p, 0.0))
    parts.append(ref[pl.ds(pl.multiple_of(i * rows, SUB), rows), :])
    if after:
        nxt = ref[pl.ds(pl.multiple_of(jnp.minimum((i + 1) * rows, s - after), SUB), after), :]
        parts.append(jnp.where((i + 1) * rows < s, nxt, 0.0))
    return jnp.concatenate(parts, axis=0)


def _conv3(ext, w, bias):
    x1 = pltpu.roll(ext, 1, 0)
    x2 = pltpu.roll(ext, 2, 0)
    return bias + w[0:1, :] * x2 + w[1:2, :] * x1 + w[2:3, :] * ext, x1, x2


def _convffn_fwd(up, conv_w, conv_b, dff):
    s = up.shape[0]
    tc = HEAD
    nf = dff // tc
    rows = _pick(s, (CONV_ROWS, 256, 128, 64, 32, 16, 8))

    def body(ug_ref, uv_ref, wg_ref, wv_ref, bg_ref, bv_ref, a_ref):
        wg, wv, bg, bv = wg_ref[...], wv_ref[...], bg_ref[...], bv_ref[...]

        def step(i, _):
            g = _conv3(_conv_ext(ug_ref, i, rows, s, SUB, 0), wg, bg)[0][SUB:]
            v = _conv3(_conv_ext(uv_ref, i, rows, s, SUB, 0), wv, bv)[0][SUB:]
            a_ref[pl.ds(pl.multiple_of(i * rows, SUB), rows), :] = (g * _sig(g) * v).astype(BF16)
            return 0

        lax.fori_loop(0, s // rows, step, 0)

    cg = lambda r: pl.BlockSpec((r, tc), lambda j: (0, j))
    cv = lambda r: pl.BlockSpec((r, tc), lambda j: (0, nf + j))
    return pl.pallas_call(
        body, name="convffn_fwd", grid=(nf,),
        in_specs=[cg(s), cv(s), cg(3), cv(3), cg(1), cv(1)], out_specs=cg(s),
        out_shape=jax.ShapeDtypeStruct((s, dff), BF16),
        compiler_params=_cparams("parallel"),
    )(up, up, conv_w, conv_w, conv_b, conv_b)


def _convffn_bwd(up, conv_w, conv_b, dact, dff):
    s = up.shape[0]
    tc = HEAD
    nf = dff // tc
    rows = _pick(s, (CONV_ROWS, 256, 128, 64, 32, 16, 8))
    n_ext = rows + SUB

    def body(ug_ref, uv_ref, wg_ref, wv_ref, bg_ref, bv_ref, da_ref,
             dug_ref, duv_ref, dwg_ref, dwv_ref, dbg_ref, dbv_ref):
        wg, wv, bg, bv = wg_ref[...], wv_ref[...], bg_ref[...], bv_ref[...]

        def fold(x):
            return jnp.sum(x.reshape(rows // SUB, SUB, tc), axis=0)

        def one(ext, x1, x2, d_ext, w):
            d1 = pltpu.roll(d_ext, n_ext - 1, 0)[:rows]
            d2 = pltpu.roll(d_ext, n_ext - 2, 0)[:rows]
            dc = d_ext[:rows]
            du = w[2:3, :] * dc + w[1:2, :] * d1 + w[0:1, :] * d2
            sl = slice(SUB, SUB + rows)
            return du, (fold(dc * x2[sl]), fold(dc * x1[sl]), fold(dc * ext[sl]), fold(dc))

        def step(i, acc):
            eg = _conv_ext(ug_ref, i, rows, s, SUB, SUB)
            ev = _conv_ext(uv_ref, i, rows, s, SUB, SUB)
            g, g1, g2 = _conv3(eg, wg, bg)
            v, v1, v2 = _conv3(ev, wv, bv)
            g, v = g[SUB:], v[SUB:]
            da = _conv_ext(da_ref, i, rows, s, 0, SUB)
            sg = _sig(g)
            dg = da * v * (sg * (1.0 + g * (1.0 - sg)))
            dv = da * (g * sg)
            dug, pg = one(eg, g1, g2, dg, wg)
            duv, pv = one(ev, v1, v2, dv, wv)
            at = pl.ds(pl.multiple_of(i * rows, SUB), rows)
            dug_ref[at, :] = dug.astype(BF16)
            duv_ref[at, :] = duv.astype(BF16)
            return tuple(a + p for a, p in zip(acc, pg + pv))

        zero = jnp.zeros((SUB, tc), F32)
        acc = lax.fori_loop(0, s // rows, step, (zero,) * 8)
        red = [jnp.sum(a, axis=0, keepdims=True) for a in acc]
        for j in range(3):
            dwg_ref[j:j + 1, :] = red[j]
            dwv_ref[j:j + 1, :] = red[4 + j]
        dbg_ref[...] = red[3]
        dbv_ref[...] = red[7]

    cg = lambda r: pl.BlockSpec((r, tc), lambda j: (0, j))
    cv = lambda r: pl.BlockSpec((r, tc), lambda j: (0, nf + j))
    outs = pl.pallas_call(
        body, name="convffn_bwd", grid=(nf,),
        in_specs=[cg(s), cv(s), cg(3), cv(3), cg(1), cv(1), cg(s)],
        out_specs=[cg(s), cg(s), cg(3), cg(3), cg(1), cg(1)],
        out_shape=[jax.ShapeDtypeStruct((s, dff), BF16), jax.ShapeDtypeStruct((s, dff), BF16),
                   jax.ShapeDtypeStruct((3, dff), F32), jax.ShapeDtypeStruct((3, dff), F32),
                   jax.ShapeDtypeStruct((1, dff), F32), jax.ShapeDtypeStruct((1, dff), F32)],
        compiler_params=_cparams("parallel"),
    )(up, up, conv_w, conv_w, conv_b, conv_b, dact)
    return outs


def _loss_head(out, target):
    s, d = out.shape
    tr = _pick(s, (256, 128, 64, 32, 16, 8))

    def body(o_ref, t_ref, d_ref, l_ref):
        @pl.when(pl.program_id(0) == 0)
        def _():
            l_ref[...] = jnp.zeros_like(l_ref)

        err = o_ref[...] - t_ref[...]
        d_ref[...] = err * (1.0 / d)
        sq = jnp.sum((err * err).reshape(tr // SUB, SUB, d), axis=0)
        part = sq[:, 0:HEAD]
        for j in range(1, d // HEAD):
            part = part + sq[:, j * HEAD:(j + 1) * HEAD]
        l_ref[...] += part

    blk = pl.BlockSpec((tr, d), lambda i: (i, 0))
    return pl.pallas_call(
        body, name="loss_head", grid=(s // tr,), in_specs=[blk, blk],
        out_specs=[blk, pl.BlockSpec((SUB, HEAD), lambda i: (0, 0))],
        out_shape=[jax.ShapeDtypeStruct((s, d), F32), jax.ShapeDtypeStruct((SUB, HEAD), F32)],
        compiler_params=_cparams("arbitrary"),
    )(out, target)


def _sum_rows(name, parts):
    def body(p_ref, o_ref):
        o_ref[...] = jnp.sum(p_ref[...], axis=0, keepdims=True)

    return pl.pallas_call(body, name=name, out_shape=jax.ShapeDtypeStruct((1, parts.shape[1]), F32))(parts)


def _local_step(x, target, g_mix, g_q, g_k, lb_logits, g_hg_out, g_ffn, conv_w, conv_b, first_weight, later_weights,
                grads_ready):
    s, d = x.shape
    nh = lb_logits.shape[1] // HEAD
    wid = nh * HEAD
    blk = _pick(s, (256, 128))
    chunk = _pick(s, (HG_CHUNK,))
    gate_col = 7 * wid

    u, u_t = _rmsnorm_fwd("rmsnorm_mix", x, g_mix)
    w_in = first_weight(u)
    proj = _matmul("in_proj", u, w_in, "nn", F32)
    qn, kn, vb, kt, vt = _qk_norm_fwd(proj, g_q, g_k, nh, blk)
    y_a, carries = _sb_attn_fwd(qn, kn, vt, nh, blk)
    lb = _lower_bound(lb_logits)
    o_pre, y_b, states = _hgrn2_fwd(proj, lb, g_hg_out, nh, 3 * nh, chunk)
    later = later_weights(o_pre)
    p_a, p_b, w_up = later["p_a"], later["p_b"], later["w_up"]
    w_o = later["w_o"].reshape(1, d, d)
    dff = later["w_down"].shape[1] * N_CHIPS
    w_down = later["w_down"].reshape(1, dff, d)
    ya_p = _matmul("proj_a", y_a, p_a, "nn", F32)
    yb_p = _matmul("proj_b", y_b, p_b, "nn", F32)
    m = _merge_fwd(proj, ya_p, yb_p, gate_col)
    h = _matmul("out_proj", m, w_o, "nn", F32, add=x)
    u2, u2_t = _rmsnorm_fwd("rmsnorm_ffn", h, g_ffn)
    up = _matmul("up_proj", u2, w_up, "nn", F32)
    act = _convffn_fwd(up, conv_w, conv_b, dff)
    out = _matmul("down_proj", act, w_down, "nn", F32, add=h)
    dout, sq = _loss_head(out, target)

    dact = _matmul("d_act", dout, w_down, "nt", F32)
    g_w_down = _matmul_tn("g_w_down", act, dout, 1, BF16).reshape(N_CHIPS, dff // N_CHIPS, d)
    dup_g, dup_v, dcw_g, dcw_v, dcb_g, dcb_v = _convffn_bwd(up, conv_w, conv_b, dact, dff)
    dup = jnp.concatenate([dup_g, dup_v], axis=1)
    g_w_up = _matmul("g_w_up", u2_t, dup[None], "nn", BF16, out_shards=N_CHIPS)
    sent = grads_ready(("w_down", "w_up"), [g_w_down, g_w_up])
    du2 = _matmul("d_u2", dup, w_up, "nt", F32)
    dh, pg_ffn = _rmsnorm_bwd("rmsnorm_ffn_bwd", h, g_ffn + sent, du2, dout)
    dm = _matmul("d_m", dh, w_o, "nt", F32)
    g_w_o = _matmul_tn("g_w_o", m, dh, 1, BF16).reshape(N_CHIPS, d // N_CHIPS, d)
    dya_p, dyb_p, dga, dgb = _merge_bwd(proj, ya_p, yb_p, dm, gate_col)
    g_p_a = _matmul_tn("g_p_a", y_a, dya_p, N_CHIPS, BF16)
    g_p_b = _matmul_tn("g_p_b", y_b, dyb_p, N_CHIPS, BF16)
    sent = grads_ready(("w_o", "p_a", "p_b"), [g_w_o, g_p_a, g_p_b])
    dy_a = _matmul("d_y_a", dya_p, p_a, "nt", F32)
    dy_b = _matmul("d_y_b", dyb_p, p_b, "nt", F32)
    dhq, dhf, dhi, dog, pg_hg, p_lb = _hgrn2_bwd(proj, lb, g_hg_out + sent, o_pre, states, dy_b, nh, 3 * nh, chunk)
    dqn, dkn, dv = _sb_attn_bwd(qn, kn, kt, vb, dy_a, carries, nh, blk)
    dq, dk, pg_q, pg_k = _qk_norm_bwd(proj, g_q, g_k, dqn, dkn, nh)
    dproj = jnp.concatenate([dq, dk, dv.astype(BF16), dhq, dhf, dhi, dog, dga, dgb], axis=1)
    g_w_in = _matmul("g_w_in", u_t, dproj[None], "nn", BF16, out_shards=N_CHIPS)
    sent = grads_ready(("w_in",), [g_w_in])
    du = _matmul("d_u", dproj, w_in, "nt", F32)
    dx, pg_mix = _rmsnorm_bwd("rmsnorm_mix_bwd", x, g_mix + sent, du, dh)

    small = dict(
        g_mix=_sum_rows("sum_g_mix", pg_mix), g_q=_sum_rows("sum_g_q", pg_q), g_k=_sum_rows("sum_g_k", pg_k),
        lb=_sum_rows("sum_lb", p_lb), g_hg_out=_sum_rows("sum_g_hg", pg_hg), g_ffn=_sum_rows("sum_g_ffn", pg_ffn),
        conv_w=jnp.concatenate([dcw_g, dcw_v], axis=1), conv_b=jnp.concatenate([dcb_g, dcb_v], axis=1),
        sq=_sum_rows("sum_sq", sq),
    )
    return dx, small, lb


ANY = pl.BlockSpec(memory_space=pl.ANY)


def _place():
    x, y, c = lax.axis_index("x"), lax.axis_index("y"), lax.axis_index("c")
    chips = [(1 - x, y), (x, 1 - y), (1 - x, 1 - y)]
    return x, y, c, chips


def _remote(src, dst, send_sem, recv_sem, to):
    return pltpu.make_async_remote_copy(src_ref=src, dst_ref=dst, send_sem=send_sem, recv_sem=recv_sem,
                                        device_id=to, device_id_type=MESH)


def _to_sibling(name, srcs):
    n = len(srcs)

    def body(*refs):
        ins, outs = refs[:n], refs[n:2 * n]
        send, recv = refs[2 * n:]
        x, y, c, _ = _place()
        cps = []
        for k in range(n):
            cp = _remote(ins[k], outs[k], send.at[k], recv.at[k], (x, y, 1 - c))
            cp.start()
            cps.append(cp)
        for cp in cps:
            cp.wait_recv()
        for cp in cps:
            cp.wait_send()

    return pl.pallas_call(
        body, name=name, in_specs=[ANY] * n, out_specs=[ANY] * n,
        out_shape=[jax.ShapeDtypeStruct(a.shape, a.dtype) for a in srcs],
        scratch_shapes=[pltpu.SemaphoreType.DMA((n,)), pltpu.SemaphoreType.DMA((n,))],
    )(*srcs)


HBM = pl.BlockSpec(memory_space=pltpu.HBM)
SEM = pl.BlockSpec(memory_space=pltpu.SEMAPHORE)
SIDE = pltpu.SideEffectType.DATAFLOW_SIDE_EFFECTING
N_PEERS = 7


def _peer(r):
    x, y, c = lax.axis_index("x"), lax.axis_index("y"), lax.axis_index("c")
    return (1 - x if r & 4 else x), (1 - y if r & 2 else y), (1 - c if r & 1 else c)


def _partial_copy(src, land, send, recv, k, r):
    px, py, pc = _peer(r)
    half = src.shape[1] // 2
    sem = k * N_PEERS + r - 1
    return _remote(src.at[2 * px + py, pl.ds(pc * half, half)], land.at[r - 1], send.at[sem], recv.at[sem], (px, py, pc))


def _shard_copy(full, send, recv, k, r):
    x, y, c = lax.axis_index("x"), lax.axis_index("y"), lax.axis_index("c")
    half = full.shape[1] // 2
    part = full.at[2 * x + y, pl.ds(c * half, half)]
    sem = k * (N_PEERS - 1) + r - 2
    return _remote(part, part, send.at[sem], recv.at[sem], _peer(r))


def _start_copies(name, arrays, n_sems, copies):
    n = len(arrays)

    def body(*refs):
        send, recv, token = refs[n], refs[n + 1], refs[-1]
        for cp in copies(refs[:n], send, recv):
            cp.start()
        token[...] = jnp.zeros_like(token)

    sem = pltpu.SemaphoreType.DMA((n_sems,))
    outs = pl.pallas_call(
        body, name=name, in_specs=[HBM] * n,
        out_specs=[SEM, SEM] + [HBM] * n + [pl.BlockSpec(memory_space=pltpu.VMEM)],
        out_shape=[sem, sem] + [pltpu.HBM(a.shape, a.dtype) for a in arrays] + [jax.ShapeDtypeStruct((SUB, HEAD), F32)],
        input_output_aliases={i: 2 + i for i in range(n)},
        compiler_params=pltpu.CompilerParams(has_side_effects=SIDE),
    )(*[pltpu.with_memory_space_constraint(a, pltpu.HBM) for a in arrays])
    return outs[0], outs[1], list(outs[2:2 + n]), outs[-1]


def _wait_copies(name, send, recv, arrays, after, copies):
    n = len(arrays)

    def body(*refs):
        for cp in copies(refs[:n], refs[n], refs[n + 1]):
            cp.wait_send()
            cp.wait_recv()

    return list(pl.pallas_call(
        body, name=name, in_specs=[HBM] * n + [SEM, SEM, ANY], out_specs=[HBM] * n,
        out_shape=[pltpu.HBM(a.shape, a.dtype) for a in arrays],
        input_output_aliases={i: i for i in range(n)},
        compiler_params=pltpu.CompilerParams(has_side_effects=SIDE),
    )(*arrays, send, recv, after))


def _partial_copies(n):
    def copies(refs, send, recv):
        return [_partial_copy(refs[k], refs[n + k], send, recv, k, r) for k in range(n) for r in range(1, N_PEERS + 1)]
    return copies


def _shard_copies(n):
    def copies(refs, send, recv):
        return [_shard_copy(refs[k], send, recv, k, r) for k in range(n) for r in range(2, N_PEERS + 1)]
    return copies


def _same_core_copies(refs, send, recv):
    return [_shard_copy(refs[0], send, recv, 0, r) for r in (2, 4, 6)]


def _forward_halves(full):
    def body(in_ref, out_ref, send, recv):
        x, y, c, chips = _place()
        half = out_ref.shape[1] // 2
        cps = []
        for j, (px, py) in enumerate(chips):
            part = out_ref.at[2 * px + py, pl.ds(c * half, half)]
            cp = _remote(part, part, send.at[j], recv.at[j], (x, y, 1 - c))
            cp.start()
            cps.append(cp)
        for j, (px, py) in enumerate(chips):
            other = out_ref.at[2 * px + py, pl.ds((1 - c) * half, half)]
            _remote(other, other, send.at[j], recv.at[j], (x, y, 1 - c)).wait_recv()
        for cp in cps:
            cp.wait_send()

    return pl.pallas_call(
        body, name="w_in_forward", in_specs=[ANY], out_specs=ANY,
        out_shape=jax.ShapeDtypeStruct(full.shape, full.dtype), input_output_aliases={0: 0},
        scratch_shapes=[pltpu.SemaphoreType.DMA((3,)), pltpu.SemaphoreType.DMA((3,))],
    )(full)


def _cast_place(name, w, shard):
    r, c = w.shape
    tr = _pick(r, (256, 128, 64, 32, 16))

    def body(s_ref, w_ref, o_ref):
        o_ref[...] = w_ref[...].astype(BF16)

    return pl.pallas_call(
        body, name=name,
        grid_spec=pltpu.PrefetchScalarGridSpec(
            num_scalar_prefetch=1, grid=(r // tr,), in_specs=[pl.BlockSpec((tr, c), lambda i, sr: (i, 0))],
            out_specs=pl.BlockSpec((None, tr, c), lambda i, sr: (sr[0], i, 0))),
        out_shape=jax.ShapeDtypeStruct((N_CHIPS, r, c), BF16),
        compiler_params=_cparams("parallel"),
    )(shard, w)


def _sum_peers(name, g, land, shard, core):
    _, r, cols = g.shape
    half = r // 2
    tr = _pick(half, (128, 64, 32, 16))
    nt = half // tr

    def body(s_ref, c_ref, g_ref, l_ref, o_ref):
        acc = g_ref[...].astype(F32)
        for j in range(N_PEERS):
            acc = acc + l_ref[j].astype(F32)
        o_ref[...] = acc

    return pl.pallas_call(
        body, name=name,
        grid_spec=pltpu.PrefetchScalarGridSpec(
            num_scalar_prefetch=2, grid=(nt,),
            in_specs=[pl.BlockSpec((None, tr, cols), lambda i, sr, cr: (sr[0], cr[0] * nt + i, 0)),
                      pl.BlockSpec((N_PEERS, tr, cols), lambda i, sr, cr: (0, i, 0))],
            out_specs=pl.BlockSpec((tr, cols), lambda i, sr, cr: (i, 0))),
        out_shape=jax.ShapeDtypeStruct((half, cols), F32),
        compiler_params=_cparams("parallel"),
    )(shard, core, g, land)


def _join_halves(mine, got, c):
    half, cols = mine.shape
    tr = _pick(half, (256, 128, 64, 32, 16, 8))
    nt = half // tr

    def body(c_ref, a_ref, b_ref, o_ref):
        i = pl.program_id(0)
        own = (i // nt) == c_ref[0]

        @pl.when(own)
        def _():
            o_ref[...] = a_ref[...]

        @pl.when(jnp.logical_not(own))
        def _():
            o_ref[...] = b_ref[...]

    blk = pl.BlockSpec((tr, cols), lambda i, cr: (i % nt, 0))
    return pl.pallas_call(
        body, name="join_halves",
        grid_spec=pltpu.PrefetchScalarGridSpec(num_scalar_prefetch=1, grid=(2 * nt,), in_specs=[blk, blk],
                                               out_specs=pl.BlockSpec((tr, cols), lambda i, cr: (i, 0))),
        out_shape=jax.ShapeDtypeStruct((2 * half, cols), F32),
        compiler_params=_cparams("parallel"),
    )(c, mine, got)


def _all_gather_rows(name, row):
    p = row.shape[1]

    def body(in_ref, out_ref, send, recv, local):
        x, y, c, _ = _place()
        me = 4 * x + 2 * y + c
        own = pltpu.make_async_copy(in_ref, out_ref.at[me], local)
        own.start()
        cps = []
        for k in range(1, 8):
            px, py, pc = x ^ (k >> 2), y ^ ((k >> 1) & 1), c ^ (k & 1)
            cp = _remote(in_ref, out_ref.at[me], send.at[k - 1], recv.at[k - 1], (px, py, pc))
            cp.start()
            cps.append(cp)
        for cp in cps:
            cp.wait_recv()
        for cp in cps:
            cp.wait_send()
        own.wait()

    return pl.pallas_call(
        body, name=name, in_specs=[ANY], out_specs=ANY,
        out_shape=jax.ShapeDtypeStruct((8, 1, p), F32),
        scratch_shapes=[pltpu.SemaphoreType.DMA((7,)), pltpu.SemaphoreType.DMA((7,)), pltpu.SemaphoreType.DMA],
    )(row)


def _sum_devices(rows):
    def body(r_ref, o_ref):
        acc = r_ref[0]
        for k in range(1, 8):
            acc = acc + r_ref[k]
        o_ref[...] = acc

    return pl.pallas_call(body, name="sum_devices", out_shape=jax.ShapeDtypeStruct(rows.shape[1:], F32))(rows)


def _adamw(name, w, g, m, v):
    r, c = w.shape
    tr = _pick(r, (128, 64, 32, 16, 8))
    bc1 = 1.0 - ADAM_B1 ** ADAM_STEP
    bc2 = 1.0 - ADAM_B2 ** ADAM_STEP

    def body(w_ref, g_ref, m_ref, v_ref, d_ref, nm_ref, nv_ref):
        gv = g_ref[...]
        nm = ADAM_B1 * m_ref[...] + (1.0 - ADAM_B1) * gv
        nv = ADAM_B2 * v_ref[...] + (1.0 - ADAM_B2) * (gv * gv)
        d_ref[...] = -ADAM_LR * ((nm / bc1) / (jnp.sqrt(nv / bc2) + ADAM_EPS) + ADAM_WD * w_ref[...])
        nm_ref[...] = nm
        nv_ref[...] = nv

    blk = pl.BlockSpec((tr, c), lambda i: (i, 0))
    sh = jax.ShapeDtypeStruct((r, c), F32)
    return pl.pallas_call(
        body, name=name, grid=(r // tr,), in_specs=[blk] * 4, out_specs=[blk] * 3, out_shape=[sh] * 3,
        compiler_params=_cparams("parallel"),
    )(w, g, m, v)


def _lb_logits_grad(dlb, lb):
    def body(d_ref, lb_ref, o_ref):
        lbv = lb_ref[...]
        t = d_ref[...] * lbv * (1.0 - lbv)
        o_ref[0:1, :] = t
        o_ref[1:2, :] = -t

    return pl.pallas_call(body, name="lb_logits_grad", out_shape=jax.ShapeDtypeStruct((2, dlb.shape[1]), F32))(dlb, lb)


BIG = ("w_in", "p_a", "p_b", "w_o", "w_up", "w_down")
SMALL = ("g_mix", "g_q", "g_k", "lb_logits", "g_hg_out", "g_ffn", "conv_w", "conv_b")
ORDER = ("g_mix", "w_in", "g_q", "g_k", "lb_logits", "g_hg_out", "p_a", "p_b", "w_o", "g_ffn", "w_up", "conv_w", "conv_b", "w_down")


def kernel(x, g_mix, w_in, g_q, g_k, lb_logits, g_hg_out, p_a, p_b, w_o, g_ffn, w_up, conv_w, conv_b, w_down, loss_target, m_g_mix, m_w_in, m_g_q, m_g_k, m_lb_logits, m_g_hg_out, m_p_a, m_p_b, m_w_o, m_g_ffn, m_w_up, m_conv_w, m_conv_b, m_w_down, v_g_mix, v_w_in, v_g_q, v_g_k, v_lb_logits, v_g_hg_out, v_p_a, v_p_b, v_w_o, v_g_ffn, v_w_up, v_conv_w, v_conv_b, v_w_down):
    assert lb_logits.shape[0] == 2, "the lower bound is the first row of a two-row softmax"
    w = dict(g_mix=g_mix, w_in=w_in[0], g_q=g_q, g_k=g_k, lb_logits=lb_logits, g_hg_out=g_hg_out, p_a=p_a[0], p_b=p_b[0],
             w_o=w_o[0], g_ffn=g_ffn, w_up=w_up[0], conv_w=conv_w[0], conv_b=conv_b, w_down=w_down[0])
    mom = dict(g_mix=m_g_mix, w_in=m_w_in[0], g_q=m_g_q, g_k=m_g_k, lb_logits=m_lb_logits, g_hg_out=m_g_hg_out, p_a=m_p_a[0],
               p_b=m_p_b[0], w_o=m_w_o[0], g_ffn=m_g_ffn, w_up=m_w_up[0], conv_w=m_conv_w[0], conv_b=m_conv_b, w_down=m_w_down[0])
    var = dict(g_mix=v_g_mix, w_in=v_w_in[0], g_q=v_g_q, g_k=v_g_k, lb_logits=v_lb_logits, g_hg_out=v_g_hg_out, p_a=v_p_a[0],
               p_b=v_p_b[0], w_o=v_w_o[0], g_ffn=v_g_ffn, w_up=v_w_up[0], conv_w=v_conv_w[0], conv_b=v_conv_b, w_down=v_w_down[0])
    d = x.shape[2]
    cx, cy, cc = lax.axis_index("x"), lax.axis_index("y"), lax.axis_index("c")
    shard = (2 * cx + cy).astype(jnp.int32).reshape(1)
    core = cc.astype(jnp.int32).reshape(1)

    later = ("p_a", "p_b", "w_o", "w_up", "w_down")
    n_later = len(later)
    first = [_cast_place("cast_w_in", w["w_in"], shard)]
    f_send, f_recv, first, f_token = _start_copies("w_in_start", first, N_PEERS - 1, _same_core_copies)
    casts = [_cast_place("cast_" + n, w[n], shard) for n in later]
    started = {}

    def first_weight(after):
        got = _wait_copies("w_in_wait", f_send, f_recv, first, after, _same_core_copies)
        send, recv, fulls, _ = _start_copies("weights_start", casts + [_forward_halves(got[0])],
                                             n_later * (N_PEERS - 1), _shard_copies(n_later))
        started.update(send=send, recv=recv, fulls=fulls[:n_later])
        return fulls[n_later]

    def later_weights(after):
        return dict(zip(later, _wait_copies("weights_wait", started["send"], started["recv"], started["fulls"], after,
                                            _shard_copies(n_later))))

    cw = conv_w.shape[2]
    rows = _all_gather_rows("gather_conv_w", w["conv_w"].reshape(1, 3 * cw))
    conv_full = jnp.concatenate([rows[2 * s, 0].reshape(3, cw) for s in range(N_CHIPS)], axis=1)

    pending = []

    def grads_ready(names, gs):
        n = len(gs)
        lands = [lax.empty((N_PEERS, g.shape[1] // 2, g.shape[2]), BF16) for g in gs]
        send, recv, arrays, token = _start_copies("partials_start_" + names[0], list(gs) + lands, n * N_PEERS, _partial_copies(n))
        pending.append((names, send, recv, arrays))
        return token[0:1, 0:1]

    dx, small, lb = _local_step(x[0], loss_target[0], g_mix + f_token[0:1, 0:1], g_q, g_k, lb_logits, g_hg_out, g_ffn,
                                conv_full, conv_b, first_weight, later_weights, grads_ready)

    mine = {}
    after = dx
    for names, send, recv, arrays in pending:
        n = len(names)
        arrays = _wait_copies("partials_wait_" + names[0], send, recv, arrays, after, _partial_copies(n))
        for k, name in enumerate(names):
            mine[name] = _sum_peers("sum_" + name, arrays[k], arrays[n + k], shard, core)
        after = mine[names[-1]]
    theirs = _to_sibling("reduced_to_sibling", [mine[n] for n in BIG])
    grads = {n: _join_halves(mine[n], b, core) for n, b in zip(BIG, theirs)}

    names = ("g_mix", "g_q", "g_k", "lb", "g_hg_out", "g_ffn", "conv_b", "sq")
    packed = jnp.concatenate([small[n] for n in names] + [small["conv_w"].reshape(1, -1)], axis=1)
    total = _sum_devices(_all_gather_rows("gather_small_grads", packed))
    off = 0
    red = {}
    for n in names:
        ln = small[n].shape[1]
        red[n] = total[:, off:off + ln]
        off += ln
    conv_all = total[:, off:].reshape(3, -1)
    loss = 0.5 * jnp.sum(red["sq"]) / d
    grads["conv_w"] = lax.dynamic_slice_in_dim(conv_all, (2 * cx + cy) * cw, cw, axis=1)
    grads["lb_logits"] = _lb_logits_grad(red["lb"], lb)
    for n in ("g_mix", "g_q", "g_k", "g_hg_out", "g_ffn", "conv_b"):
        grads[n] = red[n]

    delta, new_m, new_v = {}, {}, {}
    for n in ORDER:
        delta[n], new_m[n], new_v[n] = _adamw("adamw_" + n, w[n], grads[n], mom[n], var[n])

    def shaped(a, like):
        return a.reshape(like.shape)

    ref_w = dict(g_mix=g_mix, w_in=w_in, g_q=g_q, g_k=g_k, lb_logits=lb_logits, g_hg_out=g_hg_out, p_a=p_a, p_b=p_b, w_o=w_o,
                 g_ffn=g_ffn, w_up=w_up, conv_w=conv_w, conv_b=conv_b, w_down=w_down)
    outs = [loss, dx[None]]
    for group in (grads, delta, new_m, new_v):
        outs += [shaped(group[n], ref_w[n]) for n in ORDER]
    return tuple(outs)
```
